```python
import math
import jax
import jax.numpy as jnp
from jax import lax
import numpy as np

D_MODEL = 1024
BATCH = 32
SEQ = 256
DEPTH = 2
DEC_BATCH = 2
DEC_SEQ = 1024
PAST_LEN = 512

GRID_W = 64
N_BRANCH = 4
BRANCH_W = D_MODEL // N_BRANCH
N_HEADS = 4
GLA_DK = BRANCH_W // (2 * N_HEADS)
GLA_DV = BRANCH_W // N_HEADS
GLA_KW = N_HEADS * GLA_DK
GLA_LOWRANK = 16
GLA_TAU = 16.0
DN_DK = BRANCH_W // N_HEADS
DN_DV = BRANCH_W // N_HEADS
DN_CONV = 3
HG_DK = BRANCH_W // N_HEADS
HG_DV = BRANCH_W // N_HEADS
HG_W = N_HEADS * HG_DK
DF_DH = BRANCH_W // (2 * N_HEADS)
ROPE_BASE = 10000.0
D_FF = ((8 * D_MODEL // 3 + 255) // 256) * 256
N_MOD = 9
CHUNK = 64
Q_BLOCK = 128
EPS = 1e-6
IN_SIZES = (GLA_KW, GLA_KW, BRANCH_W, BRANCH_W, 2 * GLA_LOWRANK,
            3 * BRANCH_W, 2 * N_HEADS, 2 * N_HEADS, BRANCH_W,
            HG_W, 2 * HG_W, BRANCH_W, BRANCH_W,
            BRANCH_W, BRANCH_W, BRANCH_W)
N_IN = sum(IN_SIZES)

kernel_name = 'hybrid_diffusion_prefix_step'


def rms_norm(x, w):
    xf = x.astype(jnp.float32)
    y = xf * lax.rsqrt(jnp.mean(xf * xf, axis=-1, keepdims=True) + EPS)
    return (y * w.astype(jnp.float32)).astype(x.dtype)


def l2norm(x):
    xf = x.astype(jnp.float32)
    return xf * lax.rsqrt(jnp.sum(xf * xf, axis=-1, keepdims=True) + EPS)


def split_cols(a, sizes):
    out, start = [], 0
    for s in sizes:
        out.append(a[..., start:start + s])
        start += s
    return out


def split_heads(a):
    b, t, _ = a.shape
    return a.reshape(b, t, N_HEADS, -1).transpose(0, 2, 1, 3)


def merge_heads(a):
    b, h, t, d = a.shape
    return a.transpose(0, 2, 1, 3).reshape(b, t, h * d)


def flip_t(a):
    return jnp.flip(a, axis=2)


def gated_head_norm(o, w, gate):
    return merge_heads(rms_norm(o, w)) * jax.nn.silu(gate)


def to_chunks(a):
    b, h, t = a.shape[:3]
    return jnp.moveaxis(a.reshape(b, h, t // CHUNK, CHUNK, *a.shape[3:]), 2, 0)


def from_chunks(o):
    n, b, h, c, d = o.shape
    return jnp.moveaxis(o, 0, 2).reshape(b, h, n * c, d)


def gla_chunk_scan(q, k, v, g, s0):
    f32 = jnp.float32
    mask = jnp.tril(jnp.ones((CHUNK, CHUNK), bool))[:, :, None]

    def step(s, inp):
        qc, kc, vc, gc = [a.astype(f32) for a in inp]
        b = jnp.cumsum(gc, axis=-2)
        rel = b[..., :, None, :] - b[..., None, :, :]
        dec = jnp.exp(jnp.where(mask, rel, -jnp.inf))
        attn = jnp.einsum('bhid,bhjd,bhijd->bhij', qc, kc, dec)
        o = jnp.einsum('bhid,bhde->bhie', qc * jnp.exp(b), s) + jnp.einsum('bhij,bhje->bhie', attn, vc)
        b_last = b[..., -1:, :]
        s_new = s * jnp.exp(b[..., -1, :])[..., None] + jnp.einsum('bhjd,bhje->bhde', kc * jnp.exp(b_last - b), vc)
        return s_new, o

    s_fin, o = lax.scan(step, s0.astype(f32), (to_chunks(q), to_chunks(k), to_chunks(v), to_chunks(g)))
    return from_chunks(o).astype(v.dtype), s_fin


def delta_chunk_scan(q, k, v, beta, g, s0):
    f32 = jnp.float32
    dv = v.shape[-1]
    strict = jnp.tril(jnp.ones((CHUNK, CHUNK), bool), -1)
    incl = jnp.tril(jnp.ones((CHUNK, CHUNK), bool))
    eye = jnp.eye(CHUNK, dtype=f32)

    def step(s, inp):
        qc, kc, vc, bc, gc = [a.astype(f32) for a in inp]
        d = jnp.cumsum(gc, axis=-1)
        rel = d[..., :, None] - d[..., None, :]
        dec_s = jnp.exp(jnp.where(strict, rel, -jnp.inf))
        dec_i = jnp.exp(jnp.where(incl, rel, -jnp.inf))
        a_mat = eye + bc[..., :, None] * jnp.einsum('bhid,bhjd->bhij', kc, kc) * dec_s
        rhs = jnp.concatenate([vc * bc[..., None], kc * (bc * jnp.exp(d))[..., None]], axis=-1)
        sol = lax.linalg.triangular_solve(a_mat, rhs, left_side=True, lower=True)
        u, w = sol[..., :dv], sol[..., dv:]
        v_new = u - jnp.einsum('bhid,bhde->bhie', w, s)
        qk = jnp.einsum('bhid,bhjd->bhij', qc, kc) * dec_i
        o = jnp.einsum('bhid,bhde->bhie', qc * jnp.exp(d)[..., None], s) + jnp.einsum('bhij,bhje->bhie', qk, v_new)
        d_last = d[..., -1:]
        s_new = s * jnp.exp(d_last)[..., None] + jnp.einsum('bhjd,bhje->bhde', kc * jnp.exp(d_last - d)[..., None], v_new)
        return s_new, o

    s_fin, o = lax.scan(step, s0.astype(f32), (to_chunks(q), to_chunks(k), to_chunks(v), to_chunks(beta), to_chunks(g)))
    return from_chunks(o).astype(v.dtype), s_fin


def centred_dwconv(x, w):
    pad = w.shape[0] // 2
    return lax.conv_general_dilated(x, w[:, None, :].astype(x.dtype), window_strides=(1,),
                                    padding=((pad, pad),), dimension_numbers=('NWC', 'WIO', 'NWC'),
                                    feature_group_count=x.shape[-1])


def axial_rope_tables(t_len, dh):
    f32 = jnp.float32
    rows = t_len // GRID_W
    row = jnp.repeat(jnp.arange(rows), GRID_W).astype(f32)
    col = jnp.tile(jnp.arange(GRID_W), rows).astype(f32)
    half = dh // 2
    inv = ROPE_BASE ** (-jnp.arange(0, half, 2, dtype=f32) / half)
    def angles(pos):
        a = pos[:, None] * inv[None, :]
        return jnp.concatenate([a, a], axis=-1)
    ang = jnp.concatenate([angles(row), angles(col)], axis=-1)
    return jnp.cos(ang), jnp.sin(ang)


def apply_axial_rope(x, cos, sin):
    half = x.shape[-1] // 2
    qt = half // 2
    rot = lambda a: jnp.concatenate([-a[..., qt:], a[..., :qt]], axis=-1)
    xrot = jnp.concatenate([rot(x[..., :half]), rot(x[..., half:])], axis=-1)
    return (x.astype(jnp.float32) * cos + xrot.astype(jnp.float32) * sin).astype(x.dtype)


def diff_attention(q1, q2, k1, k2, v, lam):
    b, h, tq, dh = q1.shape
    nb = tq // Q_BLOCK
    scale = dh ** -0.5
    blk = lambda a: jnp.moveaxis(a.reshape(b, h, nb, Q_BLOCK, dh), 2, 0)

    def one(qs):
        qb1, qb2 = qs
        p1 = jax.nn.softmax(jnp.einsum('bhqd,bhkd->bhqk', qb1, k1).astype(jnp.float32) * scale, axis=-1)
        p2 = jax.nn.softmax(jnp.einsum('bhqd,bhkd->bhqk', qb2, k2).astype(jnp.float32) * scale, axis=-1)
        a = (p1 - lam * p2).astype(v.dtype)
        return jnp.einsum('bhqk,bhke->bhqe', a, v)

    o = lax.map(one, (blk(q1), blk(q2)))
    return jnp.moveaxis(o, 0, 2).reshape(b, h, tq, 2 * dh)


def token_mixers(xn, p, ctx, rope):
    f32 = jnp.float32
    bsz, t_len, _ = xn.shape
    (ga_q, ga_k, ga_v, ga_r, ga_lr, dn_x, dn_b, dn_a, dn_g,
     hg_q, hg_f, hg_i, hg_g, df_q, df_k, df_v) = split_cols(xn @ p['w_in'], IN_SIZES)
    if ctx is None:
        zeros = lambda dk, dv: jnp.zeros((bsz, 2, N_HEADS, dk, dv), f32)
        s_gla, s_dn, s_hg = zeros(GLA_DK, GLA_DV), zeros(DN_DK, DN_DV), zeros(HG_DK, HG_DV)
    else:
        s_gla, s_dn, s_hg = ctx['gla'], ctx['dn'], ctx['hg']

    q = split_heads(ga_q) * GLA_DK ** -0.5
    k = split_heads(ga_k)
    v = split_heads(ga_v)
    lr = ga_lr.reshape(bsz, t_len, 2, GLA_LOWRANK)
    log_a = jax.nn.log_sigmoid(jnp.einsum('btnr,nrk->nbtk', lr, p['gla_w2']).astype(f32)
                               + p['gla_b'][:, None, None, :].astype(f32)) / GLA_TAU
    o_f, sf = gla_chunk_scan(q, k, v, split_heads(log_a[0]), s_gla[:, 0])
    o_b, sb = gla_chunk_scan(flip_t(q), flip_t(k), flip_t(v), flip_t(split_heads(log_a[1])), s_gla[:, 1])
    out_a = gated_head_norm(o_f + flip_t(o_b), p['gla_norm'], ga_r)
    st_a = jnp.stack([sf, sb], axis=1)

    cq, ck, cv = split_cols(jax.nn.silu(centred_dwconv(dn_x, p['dn_conv'])), (BRANCH_W, BRANCH_W, BRANCH_W))
    q = l2norm(split_heads(cq)) * DN_DK ** -0.5
    k = l2norm(split_heads(ck))
    v = split_heads(cv)
    beta = jax.nn.sigmoid(dn_b.astype(f32)).reshape(bsz, t_len, 2, N_HEADS).transpose(2, 0, 3, 1)
    a_in = dn_a.astype(f32).reshape(bsz, t_len, 2, N_HEADS) + p['dn_dt_bias'].astype(f32)
    g = (-jnp.exp(p['dn_a_log'].astype(f32)) * jax.nn.softplus(a_in)).transpose(2, 0, 3, 1)
    o_f, sf = delta_chunk_scan(q, k, v, beta[0], g[0], s_dn[:, 0])
    o_b, sb = delta_chunk_scan(flip_t(q), flip_t(k), flip_t(v), flip_t(beta[1]), flip_t(g[1]), s_dn[:, 1])
    out_b = gated_head_norm(o_f + flip_t(o_b), p['dn_norm'], dn_g)
    st_b = jnp.stack([sf, sb], axis=1)

    q = jax.nn.silu(split_heads(hg_q)) * HG_DK ** -0.5
    v = split_heads(hg_i)
    f = p['hg_lb'] + (1.0 - p['hg_lb']) * jax.nn.sigmoid(hg_f.astype(f32).reshape(bsz, t_len, 2, HG_W))
    log_f = jnp.log(f)
    o_f, sf = gla_chunk_scan(q, split_heads(1.0 - f[:, :, 0]), v, split_heads(log_f[:, :, 0]), s_hg[:, 0])
    o_b, sb = gla_chunk_scan(flip_t(q), flip_t(split_heads(1.0 - f[:, :, 1])), flip_t(v),
                             flip_t(split_heads(log_f[:, :, 1])), s_hg[:, 1])
    out_c = gated_head_norm(o_f + flip_t(o_b), p['hg_norm'], hg_g)
    st_c = jnp.stack([sf, sb], axis=1)

    dq = split_heads(df_q)
    df_kh = split_heads(df_k)
    df_vh = split_heads(df_v)
    q1, q2 = dq[..., :DF_DH], dq[..., DF_DH:]
    k1, k2 = df_kh[..., :DF_DH], df_kh[..., DF_DH:]
    v = df_vh
    if ctx is not None:
        cos, sin = rope
        q1, q2, k1, k2 = [apply_axial_rope(a, cos, sin) for a in (q1, q2, k1, k2)]
        ck_cache, cv_cache = ctx['diff_k'], ctx['diff_v']
        k1 = jnp.concatenate([ck_cache[..., :DF_DH], k1], axis=2)
        k2 = jnp.concatenate([ck_cache[..., DF_DH:], k2], axis=2)
        v = jnp.concatenate([cv_cache, v], axis=2)
    lv = p['diff_lambda'].astype(f32)
    lam = jnp.exp(jnp.sum(lv[0] * lv[1])) - jnp.exp(jnp.sum(lv[2] * lv[3])) + p['lam_init']
    o = diff_attention(q1, q2, k1, k2, v, lam)
    out_d = merge_heads(rms_norm(o, p['diff_norm'])) * (1.0 - p['lam_init'])

    br = jnp.stack([out_a, out_b, out_c, out_d], axis=0)
    up = jnp.einsum('nbtc,ncd->nbtd', br, p['w_branch'])
    gate = jax.nn.sigmoid(xn @ p['w_mgate']).reshape(bsz, t_len, N_BRANCH, D_MODEL)
    out = jnp.einsum('btnd,nbtd->btd', gate, up) @ p['w_out']
    if ctx is None:
        return out, (df_kh, df_vh, st_a, st_b, st_c)
    return out, None


def swiglu(x, w_up, w_down):
    gu = x @ w_up
    return (jax.nn.silu(gu[..., :D_FF]) * gu[..., D_FF:]) @ w_down


def trunk_layer(x, mod, p, ctx, rope):
    sh1, sc1, ga1, shm, scm, gam, sh2, sc2, ga2 = jnp.split(mod, N_MOD, axis=-1)
    h = rms_norm(x, p['norm'][0]) * (1.0 + sc1) + sh1
    x = x + 0.5 * ga1 * swiglu(h, p['ffn1_in'], p['ffn1_down'])
    h = rms_norm(x, p['norm'][1]) * (1.0 + scm) + shm
    mix, new_ctx = token_mixers(h, p, ctx, rope)
    x = x + gam * mix
    h = rms_norm(x, p['norm'][2]) * (1.0 + sc2) + sh2
    x = x + 0.5 * ga2 * swiglu(h, p['ffn2_in'], p['ffn2_down'])
    return x, new_ctx


def setup_inputs(seed: int = 0) -> dict:
    key = jax.random.key(seed)
    ks = jax.random.split(key, 40)
    f32 = jnp.float32
    nrm = lambda k, shape, s: jax.random.normal(k, shape, f32) * s
    H = N_HEADS
    dt = jnp.exp(jax.random.uniform(ks[20], (DEPTH, 2, H), f32, math.log(1e-3), math.log(1e-1)))
    return {
        'x_prompt': nrm(ks[0], (BATCH, SEQ, D_MODEL), 1.0),
        'x_sample': nrm(ks[1], (DEC_BATCH, DEC_SEQ, D_MODEL), 1.0),
        'cache_diff_k': nrm(ks[2], (DEC_BATCH, DEPTH, H, PAST_LEN, 2 * DF_DH), 1.0),
        'cache_diff_v': nrm(ks[3], (DEC_BATCH, DEPTH, H, PAST_LEN, 2 * DF_DH), 1.0),
        'state_gla': nrm(ks[4], (DEC_BATCH, DEPTH, 2, H, GLA_DK, GLA_DV), 0.5),
        'state_dn': nrm(ks[5], (DEC_BATCH, DEPTH, 2, H, DN_DK, DN_DV), 0.5),
        'state_hgrn': nrm(ks[6], (DEC_BATCH, DEPTH, 2, H, HG_DK, HG_DV), 0.5),
        'c': nrm(ks[7], (DEC_BATCH, D_MODEL), 1.0),
        'c_ctx': nrm(ks[8], (D_MODEL,), 1.0),
        'norm_w': 1.0 + nrm(ks[9], (DEPTH, 3, D_MODEL), 0.02),
        'w_mod': nrm(ks[10], (DEPTH, D_MODEL, N_MOD * D_MODEL), 0.5 * D_MODEL ** -0.5),
        'b_mod': nrm(ks[11], (DEPTH, N_MOD * D_MODEL), 0.02),
        'ffn1_in': nrm(ks[12], (DEPTH, D_MODEL, 2 * D_FF), D_MODEL ** -0.5),
        'ffn1_down': nrm(ks[13], (DEPTH, D_FF, D_MODEL), D_FF ** -0.5),
        'ffn2_in': nrm(ks[14], (DEPTH, D_MODEL, 2 * D_FF), D_MODEL ** -0.5),
        'ffn2_down': nrm(ks[15], (DEPTH, D_FF, D_MODEL), D_FF ** -0.5),
        'w_in': nrm(ks[16], (DEPTH, D_MODEL, N_IN), D_MODEL ** -0.5),
        'gla_w2': nrm(ks[17], (DEPTH, 2, GLA_LOWRANK, GLA_KW), GLA_LOWRANK ** -0.5),
        'gla_b': nrm(ks[18], (DEPTH, 2, GLA_KW), 0.1),
        'gla_norm': 1.0 + nrm(ks[19], (DEPTH, GLA_DV), 0.02),
        'dn_conv': nrm(ks[21], (DEPTH, DN_CONV, 3 * BRANCH_W), DN_CONV ** -0.5),
        'dn_a_log': jnp.log(jax.random.uniform(ks[22], (DEPTH, 2, H), f32, 1.0, 16.0)),
        'dn_dt_bias': dt + jnp.log(-jnp.expm1(-dt)),
        'dn_norm': 1.0 + nrm(ks[23], (DEPTH, DN_DV), 0.02),
        'hg_lb_logits': nrm(ks[24], (DEPTH, 2, HG_W), 0.5),
        'hg_norm': 1.0 + nrm(ks[25], (DEPTH, HG_DV), 0.02),
        'diff_lambda': nrm(ks[26], (DEPTH, 4, DF_DH), 0.1),
        'diff_norm': 1.0 + nrm(ks[27], (DEPTH, 2 * DF_DH), 0.02),
        'w_branch': nrm(ks[28], (DEPTH, N_BRANCH, BRANCH_W, D_MODEL), BRANCH_W ** -0.5),
        'w_mgate': nrm(ks[29], (DEPTH, D_MODEL, N_BRANCH * D_MODEL), D_MODEL ** -0.5),
        'w_out': nrm(ks[30], (DEPTH, D_MODEL, D_MODEL), D_MODEL ** -0.5),
        'final_norm': 1.0 + nrm(ks[31], (D_MODEL,), 0.02),
    }


def reference(x_prompt, x_sample, cache_diff_k, cache_diff_v, state_gla, state_dn, state_hgrn, c, c_ctx,
              norm_w, w_mod, b_mod, ffn1_in, ffn1_down, ffn2_in, ffn2_down, w_in, gla_w2, gla_b, gla_norm,
              dn_conv, dn_a_log, dn_dt_bias, dn_norm, hg_lb_logits, hg_norm, diff_lambda, diff_norm,
              w_branch, w_mgate, w_out, final_norm):
    f32 = jnp.float32
    lb_p = jax.nn.softmax(hg_lb_logits.astype(f32), axis=0)
    hg_lb = jnp.cumsum(lb_p, axis=0) - lb_p[0:1]
    rope = axial_rope_tables(x_sample.shape[1], DF_DH)
    xp, xs = x_prompt, x_sample
    new_k, new_v, new_gla, new_dn, new_hg = [], [], [], [], []
    for l in range(DEPTH):
        p = {'norm': norm_w[l], 'ffn1_in': ffn1_in[l], 'ffn1_down': ffn1_down[l],
             'ffn2_in': ffn2_in[l], 'ffn2_down': ffn2_down[l], 'w_in': w_in[l],
             'gla_w2': gla_w2[l], 'gla_b': gla_b[l], 'gla_norm': gla_norm[l],
             'dn_conv': dn_conv[l], 'dn_a_log': dn_a_log[l], 'dn_dt_bias': dn_dt_bias[l], 'dn_norm': dn_norm[l],
             'hg_lb': hg_lb[l], 'hg_norm': hg_norm[l], 'diff_lambda': diff_lambda[l], 'diff_norm': diff_norm[l],
             'w_branch': w_branch[l], 'w_mgate': w_mgate[l], 'w_out': w_out[l],
             'lam_init': 0.8 - 0.6 * math.exp(-0.3 * l)}
        mod_ctx = (jax.nn.silu(c_ctx) @ w_mod[l] + b_mod[l])[None, None, :]
        mod_lat = (jax.nn.silu(c) @ w_mod[l] + b_mod[l])[:, None, :]
        xp, (k_l, v_l, sg, sd, sh) = trunk_layer(xp, mod_ctx, p, None, None)
        new_k.append(k_l)
        new_v.append(v_l)
        new_gla.append(sg)
        new_dn.append(sd)
        new_hg.append(sh)
        ctx = {'diff_k': cache_diff_k[:, l], 'diff_v': cache_diff_v[:, l],
               'gla': state_gla[:, l], 'dn': state_dn[:, l], 'hg': state_hgrn[:, l]}
        xs, _ = trunk_layer(xs, mod_lat, p, ctx, rope)
    y_prompt = rms_norm(xp, final_norm)
    y_sample = rms_norm(xs, final_norm)
    return (y_prompt, y_sample, jnp.stack(new_k, axis=1), jnp.stack(new_v, axis=1),
            jnp.stack(new_gla, axis=1), jnp.stack(new_dn, axis=1), jnp.stack(new_hg, axis=1))
```

```python
import functools
import math

import numpy as np
import jax
import jax.numpy as jnp
from jax import lax
from jax.experimental import pallas as pl
from jax.experimental.pallas import tpu as pltpu

F32 = jnp.float32
BF = jnp.bfloat16

D_MODEL = 1024
N_CTX_SEQ = 32
T_CTX = 256
DEPTH = 2
N_LAT_SEQ = 2
T_LAT = 1024
PAST_LEN = 512
GRID_W = 64
N_HEADS = 4
BRANCH_W = 256
HEAD_V = 64
GLA_DK = 32
GLA_KW = 128
GLA_LOWRANK = 16
GLA_TAU = 16.0
DN_DK = 64
HG_DK = 64
HG_W = 256
DF_DH = 32
ROPE_BASE = 10000.0
D_FF = 2816
N_MOD = 9
CHUNK = 64
EPS = 1e-6
N_CTX_TOK = N_CTX_SEQ * T_CTX
N_TOK = N_CTX_TOK + N_LAT_SEQ * T_LAT
N_LEVELS = 6

IN_ORIG = (128, 128, 256, 256, 32, 768, 8, 8, 256, 256, 512, 256, 256, 256, 256, 256)
N_IN = sum(IN_ORIG)
N_IN_PAD = 3968
COL_SMALL_BLOCK = 30
SMALL_LR = 0
SMALL_DNB = 32
SMALL_DNA = 40

VMEM_LIMIT = 56 * 1024 * 1024

TM_FFN = 512
TF_FFN = 1408
TM_PROJ = 512
TM_MERGE = 512
TN_MOD = 2304
TQ_ATT = 256


def _silu(x):
    return x * (1.0 / (1.0 + jnp.exp(-x)))


def _sigmoid(x):
    return 1.0 / (1.0 + jnp.exp(-x))


def _softplus(x):
    return jnp.maximum(x, 0.0) + jnp.log(1.0 + jnp.exp(-jnp.abs(x)))


def _log_sigmoid(x):
    return -_softplus(-x)


def _mm(a, b):
    return jnp.dot(a.astype(BF), b.astype(BF), preferred_element_type=F32)


def _mm_nt(a, b):
    return lax.dot_general(a.astype(BF), b.astype(BF), (((1,), (1,)), ((), ())),
                           preferred_element_type=F32)


def _mm_tn(a, b):
    return lax.dot_general(a.astype(BF), b.astype(BF), (((0,), (0,)), ((), ())),
                           preferred_element_type=F32)


def _split(x, n):
    parts = []
    r = x
    for i in range(n):
        p = r.astype(BF)
        parts.append(p)
        if i + 1 < n:
            r = r - p.astype(F32)
    return parts


def _sel_l(m01, x, n=3):
    out = None
    for p in _split(x, n):
        t = jnp.dot(m01, p, preferred_element_type=F32)
        out = t if out is None else out + t
    return out


def _sel_r(x, m01, n=3):
    out = None
    for p in _split(x, n):
        t = jnp.dot(p, m01, preferred_element_type=F32)
        out = t if out is None else out + t
    return out


def _sel_tn(x, m01, n=3):
    out = None
    for p in _split(x, n):
        t = lax.dot_general(p, m01, (((0,), (0,)), ((), ())), preferred_element_type=F32)
        out = t if out is None else out + t
    return out


def _rms(x, w):
    return x * lax.rsqrt(jnp.mean(x * x, axis=-1, keepdims=True) + EPS) * w


def _head_lanes(x, h):
    blk = x[:, (h // 2) * 128:(h // 2 + 1) * 128]
    if h % 2:
        blk = pltpu.roll(blk, 64, 1)
    return blk[:, :HEAD_V]


def _stack_heads(x, hm_ref):
    return jnp.concatenate([x * hm_ref[h:h + 1, :] for h in range(N_HEADS)], axis=0)


def _head_diag(o_full, hv_ref, c):
    out = None
    for h in range(N_HEADS):
        t = o_full[h * c:(h + 1) * c, :] * hv_ref[h:h + 1, :]
        out = t if out is None else out + t
    return out


def _head_norm_gate(o, gate, nw, ones_bd):
    ms = _sel_r(o * o, ones_bd, 2) * (1.0 / HEAD_V)
    return o * lax.rsqrt(ms + EPS) * nw * _silu(gate)


def _mod_row(i, tm):
    return jnp.maximum(i * tm - (N_CTX_TOK - T_LAT), 0) // T_LAT


def _cparams(sem):
    return pltpu.CompilerParams(dimension_semantics=sem, vmem_limit_bytes=VMEM_LIMIT)


@functools.lru_cache(maxsize=None)
def _scan_consts(w, reverse):
    c = CHUNK
    idx = np.arange(c)
    i = idx[:, None]
    m = idx[None, :]
    ms, pm = [], []
    s = c // 2
    while s >= 1:
        par = idx // (2 * s)
        right = (idx % (2 * s)) >= s
        same = par[:, None] == par[None, :]
        if not reverse:
            e = (par * 2 * s + s - 1)[:, None]
            mat = np.where(right[:, None], (m > e) & (m <= i), (m > i) & (m <= e))
            p = same & right[:, None] & (~right)[None, :]
        else:
            e = (par * 2 * s + s)[:, None]
            mat = np.where(right[:, None], (m >= e) & (m < i), (m >= i) & (m < e))
            p = same & (~right)[:, None] & right[None, :]
        ms.append(mat)
        pm.append(p)
        s //= 2
    if not reverse:
        ms.append(m <= i)
        ms.append(m > i)
    else:
        ms.append(m >= i)
        ms.append(m < i)
    pm.append(i == m)
    mstack = np.concatenate(ms, axis=0).astype(np.float32)
    pmask = np.stack([np.tile(p, (N_HEADS, 1)) for p in pm]).astype(np.float32)
    return mstack, pmask


@functools.lru_cache(maxsize=None)
def _head_consts(w):
    dk = w // N_HEADS
    hm = np.zeros((N_HEADS, w), np.float32)
    hv = np.zeros((N_HEADS, BRANCH_W), np.float32)
    for h in range(N_HEADS):
        hm[h, h * dk:(h + 1) * dk] = 1.0
        hv[h, h * HEAD_V:(h + 1) * HEAD_V] = 1.0
    bd = hm.T @ hv
    ones_bd = hv.T @ hv
    return hm, hv, bd, ones_bd


@functools.lru_cache(maxsize=None)
def _dn_consts():
    c = CHUNK
    idx = np.arange(c)
    i = idx[:, None]
    j = idx[None, :]
    tri = np.stack([(j <= i), (j >= i)]).astype(np.float32)
    strict = np.stack([np.tile(j < i, (N_HEADS, 1)), np.tile(j > i, (N_HEADS, 1))]).astype(np.float32)
    eye = np.tile(np.eye(c), (N_HEADS, 1)).astype(np.float32)
    blk = np.kron(np.eye(N_HEADS), np.ones((c, c))).astype(np.float32)
    exb = np.zeros((128, 2 * BRANCH_W), np.float32)
    exa = np.zeros((128, 2 * BRANCH_W), np.float32)
    for n in range(2):
        for h in range(N_HEADS):
            lo = n * BRANCH_W + h * HEAD_V
            exb[SMALL_DNB + n * N_HEADS + h, lo:lo + HEAD_V] = 1.0
            exa[SMALL_DNA + n * N_HEADS + h, lo:lo + HEAD_V] = 1.0
    return tri, strict, eye, blk, exb, exa


@functools.lru_cache(maxsize=None)
def _att_consts():
    qm = np.zeros((2 * N_HEADS, BRANCH_W), np.float32)
    for h in range(N_HEADS):
        for mp in range(2):
            lo = h * HEAD_V + mp * DF_DH
            qm[2 * h + mp, lo:lo + DF_DH] = 1.0
    return qm


def _rope_tables():
    rows = T_LAT // GRID_W
    row = jnp.repeat(jnp.arange(rows), GRID_W).astype(F32)
    col = jnp.tile(jnp.arange(GRID_W), rows).astype(F32)
    half = DF_DH // 2
    inv = ROPE_BASE ** (-jnp.arange(0, half, 2, dtype=F32) / half)

    def angles(pos):
        a = pos[:, None] * inv[None, :]
        return jnp.concatenate([a, a], axis=-1)

    ang = jnp.concatenate([angles(row), angles(col)], axis=-1)
    reps = BRANCH_W // DF_DH
    return jnp.tile(jnp.cos(ang), (1, reps)), jnp.tile(jnp.sin(ang), (1, reps))


def _mod_kernel(c_ref, w_ref, b_ref, o_ref):
    a = _silu(c_ref[...])
    w = w_ref[0]
    out = None
    for ap in _split(a, 2):
        for wp in _split(w, 2):
            t = jnp.dot(ap, wp, preferred_element_type=F32)
            out = t if out is None else out + t
    o_ref[0] = out + b_ref[0]


def _mod_call(c_rows, w_mod, b_mod):
    n_t = (N_MOD * D_MODEL) // TN_MOD
    return pl.pallas_call(
        _mod_kernel,
        grid=(DEPTH, n_t),
        in_specs=[pl.BlockSpec((8, D_MODEL), lambda l, j: (0, 0)),
                  pl.BlockSpec((1, D_MODEL, TN_MOD), lambda l, j: (l, 0, j)),
                  pl.BlockSpec((1, 1, TN_MOD), lambda l, j: (l, 0, j))],
        out_specs=pl.BlockSpec((1, 8, TN_MOD), lambda l, j: (l, 0, j)),
        out_shape=jax.ShapeDtypeStruct((DEPTH, 8, N_MOD * D_MODEL), F32),
        compiler_params=_cparams(("arbitrary", "arbitrary")),
        name="mod_vectors",
    )(c_rows, w_mod, b_mod.reshape(DEPTH, 1, N_MOD * D_MODEL))


def _ffn_kernel(x_ref, mod_ref, nw_ref, wg_ref, wu_ref, wd_ref, fn_ref, o_ref, h_ref, acc_ref,
                *, sub, final):
    f = pl.program_id(1)

    @pl.when(f == 0)
    def _():
        x = x_ref[...]
        sh = mod_ref[0, 3 * sub:3 * sub + 1, :]
        sc = mod_ref[0, 3 * sub + 1:3 * sub + 2, :]
        h_ref[...] = (_rms(x, nw_ref[...]) * (1.0 + sc) + sh).astype(BF)
        acc_ref[...] = jnp.zeros_like(acc_ref)

    h = h_ref[...]
    g = jnp.dot(h, wg_ref[...], preferred_element_type=F32)
    u = jnp.dot(h, wu_ref[...], preferred_element_type=F32)
    acc_ref[...] += jnp.dot((_silu(g) * u).astype(BF), wd_ref[...], preferred_element_type=F32)

    @pl.when(f == pl.num_programs(1) - 1)
    def _():
        ga = mod_ref[0, 3 * sub + 2:3 * sub + 3, :]
        y = x_ref[...] + 0.5 * ga * acc_ref[...]
        if final:
            y = _rms(y, fn_ref[...])
        o_ref[...] = y


def _ffn_call(x, mod_l, nw, w_in, w_down, final_w, sub, final):
    n_f = D_FF // TF_FFN
    tm = TM_FFN
    return pl.pallas_call(
        functools.partial(_ffn_kernel, sub=sub, final=final),
        grid=(N_TOK // tm, n_f),
        in_specs=[pl.BlockSpec((tm, D_MODEL), lambda i, f: (i, 0)),
                  pl.BlockSpec((1, N_MOD, D_MODEL), lambda i, f: (_mod_row(i, tm), 0, 0)),
                  pl.BlockSpec((1, D_MODEL), lambda i, f: (0, 0)),
                  pl.BlockSpec((D_MODEL, TF_FFN), lambda i, f: (0, f)),
                  pl.BlockSpec((D_MODEL, TF_FFN), lambda i, f: (0, n_f + f)),
                  pl.BlockSpec((TF_FFN, D_MODEL), lambda i, f: (f, 0)),
                  pl.BlockSpec((1, D_MODEL), lambda i, f: (0, 0))],
        out_specs=pl.BlockSpec((tm, D_MODEL), lambda i, f: (i, 0)),
        out_shape=jax.ShapeDtypeStruct((N_TOK, D_MODEL), F32),
        scratch_shapes=[pltpu.VMEM((tm, D_MODEL), BF), pltpu.VMEM((tm, D_MODEL), F32)],
        compiler_params=_cparams(("arbitrary", "arbitrary")),
        name="swiglu_half_step",
    )(x, mod_l, nw, w_in, w_in, w_down, final_w)


def _proj_kernel(x_ref, mod_ref, nw_ref, w_ref, o_ref):
    sh = mod_ref[0, 3:4, :]
    sc = mod_ref[0, 4:5, :]
    h = (_rms(x_ref[...], nw_ref[...]) * (1.0 + sc) + sh).astype(BF)
    o_ref[...] = jnp.dot(h, w_ref[...], preferred_element_type=F32)


def _proj_call(x, mod_l, nw, w_in_p):
    tm = TM_PROJ
    return pl.pallas_call(
        _proj_kernel,
        grid=(N_TOK // tm,),
        in_specs=[pl.BlockSpec((tm, D_MODEL), lambda i: (i, 0)),
                  pl.BlockSpec((1, N_MOD, D_MODEL), lambda i: (_mod_row(i, tm), 0, 0)),
                  pl.BlockSpec((1, D_MODEL), lambda i: (0, 0)),
                  pl.BlockSpec((D_MODEL, N_IN_PAD), lambda i: (0, 0))],
        out_specs=pl.BlockSpec((tm, N_IN_PAD), lambda i: (i, 0)),
        out_shape=jax.ShapeDtypeStruct((N_TOK, N_IN_PAD), F32),
        compiler_params=_cparams(("arbitrary",)),
        name="mixer_in_proj",
    )(x, mod_l, nw, w_in_p)


def _merge_kernel(x_ref, mod_ref, nw_ref, ba_ref, bb_ref, bc_ref, bd_ref, wg_ref, wb_ref, wo_ref, o_ref):
    x = x_ref[...]
    sh = mod_ref[0, 3:4, :]
    sc = mod_ref[0, 4:5, :]
    ga = mod_ref[0, 5:6, :]
    h = (_rms(x, nw_ref[...]) * (1.0 + sc) + sh).astype(BF)
    mixed = None
    for n, b_ref in enumerate((ba_ref, bb_ref, bc_ref, bd_ref)):
        gate = _sigmoid(jnp.dot(h, wg_ref[:, n * D_MODEL:(n + 1) * D_MODEL], preferred_element_type=F32))
        up = jnp.dot(b_ref[...].astype(BF), wb_ref[n], preferred_element_type=F32)
        mixed = gate * up if mixed is None else mixed + gate * up
    out = jnp.dot(mixed.astype(BF), wo_ref[...], preferred_element_type=F32)
    o_ref[...] = x + ga * out


def _merge_call(x, mod_l, nw, branches, w_mgate, w_branch, w_out):
    tm = TM_MERGE
    bspec = pl.BlockSpec((tm, BRANCH_W), lambda i: (i, 0))
    return pl.pallas_call(
        _merge_kernel,
        grid=(N_TOK // tm,),
        in_specs=[pl.BlockSpec((tm, D_MODEL), lambda i: (i, 0)),
                  pl.BlockSpec((1, N_MOD, D_MODEL), lambda i: (_mod_row(i, tm), 0, 0)),
                  pl.BlockSpec((1, D_MODEL), lambda i: (0, 0)),
                  bspec, bspec, bspec, bspec,
                  pl.BlockSpec((D_MODEL, 4 * D_MODEL), lambda i: (0, 0)),
                  pl.BlockSpec((4, BRANCH_W, D_MODEL), lambda i: (0, 0, 0)),
                  pl.BlockSpec((D_MODEL, D_MODEL), lambda i: (0, 0))],
        out_specs=pl.BlockSpec((tm, D_MODEL), lambda i: (i, 0)),
        out_shape=jax.ShapeDtypeStruct((N_TOK, D_MODEL), F32),
        compiler_params=_cparams(("arbitrary",)),
        name="gated_merge",
    )(x, mod_l, nw, *branches, w_mgate, w_branch, w_out)


def _gated_chunk(qc, kc, vc, gc, s, ms_ref, pm_ref, hm_ref, hv_ref, bd_ref, ones_ref):
    c = CHUNK
    e = jnp.exp(_sel_l(ms_ref[...], gc))
    a = pm_ref[N_LEVELS] * _mm_nt(_stack_heads(qc, hm_ref), kc)
    for lv in range(N_LEVELS):
        el = e[lv * c:(lv + 1) * c]
        a = a + pm_ref[lv] * _mm_nt(_stack_heads(qc * el, hm_ref), kc * el)
    e_b = e[N_LEVELS * c:(N_LEVELS + 1) * c]
    e_t = e[(N_LEVELS + 1) * c:(N_LEVELS + 2) * c]
    o = _head_diag(_mm(a, vc), hv_ref, c) + _mm(qc * e_b, s)
    dec = jnp.exp(_sel_tn(gc, ones_ref[...]))
    s_new = s * dec + bd_ref[...] * _mm_tn(kc * e_t, vc)
    return o, s_new


def _gated_scan(q_ref, k_ref, v_ref, g_ref, o_ref, s0, consts, n_chunks, reverse):
    ms_ref, pm_ref, hm_ref, hv_ref, bd_ref, ones_ref = consts

    def body(t, s):
        ci = (n_chunks - 1 - t) if reverse else t
        r0 = pl.multiple_of(ci * CHUNK, CHUNK)
        rows = pl.ds(r0, CHUNK)
        o, s = _gated_chunk(q_ref[rows, :], k_ref[rows, :], v_ref[rows, :], g_ref[rows, :], s,
                            ms_ref, pm_ref, hm_ref, hv_ref, bd_ref, ones_ref)
        if reverse:
            o_ref[rows, :] += o
        else:
            o_ref[rows, :] = o
        return s

    return lax.fori_loop(0, n_chunks, body, s0)


def _write_states(st_ref, s_f, s_b, dk):
    for d, s in enumerate((s_f, s_b)):
        for h in range(N_HEADS):
            st_ref[0, d, h] = _head_lanes(s[h * dk:(h + 1) * dk, :], h)


def _gla_kernel(*refs, t_len, has_state):
    (blk_ref, small_ref, w2_ref, gb_ref, nw_ref,
     msf_ref, pmf_ref, msb_ref, pmb_ref, hm_ref, hv_ref, bd_ref, ones_ref, onesc_ref) = refs[:14]
    pos = 14
    if has_state:
        s0_ref = refs[pos]
        pos += 1
    out_ref = refs[pos]
    pos += 1
    if not has_state:
        st_ref = refs[pos]
        pos += 1
    q_s, gf_s, gb_s, o_s = refs[pos:pos + 4]

    q_s[...] = blk_ref[:, 0:128] * (GLA_DK ** -0.5)
    z = _mm(small_ref[...], w2_ref[...]) + gb_ref[...]
    g = _log_sigmoid(z) * (1.0 / GLA_TAU)
    gf_s[...] = g[:, 0:128]
    gb_s[...] = g[:, 128:256]
    k_ref = blk_ref.at[:, 128:256]
    v_ref = blk_ref.at[:, 256:512]
    n_chunks = t_len // CHUNK
    if has_state:
        s0f, s0b = s0_ref[0, 0], s0_ref[0, 1]
    else:
        s0f = s0b = jnp.zeros((GLA_KW, BRANCH_W), F32)
    cf = (msf_ref, pmf_ref, hm_ref, hv_ref, bd_ref, onesc_ref)
    cb = (msb_ref, pmb_ref, hm_ref, hv_ref, bd_ref, onesc_ref)
    s_f = _gated_scan(q_s, k_ref, v_ref, gf_s, o_s, s0f, cf, n_chunks, False)
    s_b = _gated_scan(q_s, k_ref, v_ref, gb_s, o_s, s0b, cb, n_chunks, True)
    out_ref[...] = _head_norm_gate(o_s[...], blk_ref[:, 512:768], nw_ref[...], ones_ref[...])
    if not has_state:
        _write_states(st_ref, s_f, s_b, GLA_DK)


def _const_spec(a):
    nd = a.ndim
    return pl.BlockSpec(a.shape, lambda b, _n=nd: (0,) * _n)


def _scan_const_arrays(w):
    msf, pmf = _scan_consts(w, False)
    msb, pmb = _scan_consts(w, True)
    hm, hv, bd, ones_bd = _head_consts(w)
    return (jnp.asarray(msf, BF), jnp.asarray(pmf, F32), jnp.asarray(msb, BF), jnp.asarray(pmb, F32),
            jnp.asarray(hm, F32), jnp.asarray(hv, F32), jnp.asarray(bd, F32), jnp.asarray(ones_bd, BF),
            jnp.ones((CHUNK, BRANCH_W), BF))


def _gla_call(proj, w2bd, gbias, nw, s0, t_len, n_seq, row_blk0):
    msf, pmf, msb, pmb, hm, hv, bd, ones_bd, ones_c = _scan_const_arrays(GLA_KW)
    has_state = s0 is not None
    consts = (w2bd, gbias, nw, msf, pmf, msb, pmb, hm, hv, bd, ones_bd, ones_c)
    in_specs = [pl.BlockSpec((t_len, 768), lambda b: (row_blk0 + b, 0)),
                pl.BlockSpec((t_len, 128), lambda b: (row_blk0 + b, COL_SMALL_BLOCK))]
    in_specs += [_const_spec(a) for a in consts]
    args = [proj, proj, *consts]
    if has_state:
        in_specs.append(pl.BlockSpec((1, 2, GLA_KW, BRANCH_W), lambda b: (b, 0, 0, 0)))
        args.append(s0)
    out_shape = [jax.ShapeDtypeStruct((n_seq * t_len, BRANCH_W), F32)]
    out_specs = [pl.BlockSpec((t_len, BRANCH_W), lambda b: (b, 0))]
    if not has_state:
        out_shape.append(jax.ShapeDtypeStruct((n_seq, 2, N_HEADS, GLA_DK, HEAD_V), F32))
        out_specs.append(pl.BlockSpec((1, 2, N_HEADS, GLA_DK, HEAD_V), lambda b: (b, 0, 0, 0, 0)))
    return pl.pallas_call(
        functools.partial(_gla_kernel, t_len=t_len, has_state=has_state),
        grid=(n_seq,),
        in_specs=in_specs,
        out_specs=out_specs,
        out_shape=out_shape,
        scratch_shapes=[pltpu.VMEM((t_len, GLA_KW), F32), pltpu.VMEM((t_len, GLA_KW), F32),
                        pltpu.VMEM((t_len, GLA_KW), F32), pltpu.VMEM((t_len, BRANCH_W), F32)],
        compiler_params=_cparams(("arbitrary",)),
        name="gla_mixer",
    )(*args)


def _hgrn_kernel(*refs, t_len, has_state, layer):
    (q_ref, f_ref, v_ref, gate_ref, lbl_ref, nw_ref,
     msf_ref, pmf_ref, msb_ref, pmb_ref, hm_ref, hv_ref, bd_ref, ones_ref, onesc_ref) = refs[:15]
    pos = 15
    if has_state:
        s0_ref = refs[pos]
        pos += 1
    out_ref = refs[pos]
    pos += 1
    if not has_state:
        st_ref = refs[pos]
        pos += 1
    q_s, kf_s, kb_s, gf_s, gb_s, o_s = refs[pos:pos + 6]

    lg = lbl_ref[...]
    mx = jnp.max(lg, axis=0, keepdims=True)
    ex = jnp.exp(lg - mx)
    p = ex / jnp.sum(ex, axis=0, keepdims=True)
    lb = jnp.sum(p[0:layer + 1], axis=0, keepdims=True) - p[0:1]

    q_s[...] = _silu(q_ref[...]) * (HG_DK ** -0.5)
    f = lb + (1.0 - lb) * _sigmoid(f_ref[...])
    kf_s[...] = 1.0 - f[:, 0:HG_W]
    kb_s[...] = 1.0 - f[:, HG_W:2 * HG_W]
    lf = jnp.log(f)
    gf_s[...] = lf[:, 0:HG_W]
    gb_s[...] = lf[:, HG_W:2 * HG_W]
    n_chunks = t_len // CHUNK
    if has_state:
        s0f, s0b = s0_ref[0, 0], s0_ref[0, 1]
    else:
        s0f = s0b = jnp.zeros((HG_W, BRANCH_W), F32)
    cf = (msf_ref, pmf_ref, hm_ref, hv_ref, bd_ref, onesc_ref)
    cb = (msb_ref, pmb_ref, hm_ref, hv_ref, bd_ref, onesc_ref)
    s_f = _gated_scan(q_s, kf_s, v_ref, gf_s, o_s, s0f, cf, n_chunks, False)
    s_b = _gated_scan(q_s, kb_s, v_ref, gb_s, o_s, s0b, cb, n_chunks, True)
    out_ref[...] = _head_norm_gate(o_s[...], gate_ref[...], nw_ref[...], ones_ref[...])
    if not has_state:
        _write_states(st_ref, s_f, s_b, HG_DK)


def _hgrn_call(proj, lb_logits, nw, s0, t_len, n_seq, row_blk0, layer):
    msf, pmf, msb, pmb, hm, hv, bd, ones_bd, ones_c = _scan_const_arrays(HG_W)
    has_state = s0 is not None
    consts = (lb_logits, nw, msf, pmf, msb, pmb, hm, hv, bd, ones_bd, ones_c)
    in_specs = [pl.BlockSpec((t_len, 256), lambda b: (row_blk0 + b, 7)),
                pl.BlockSpec((t_len, 512), lambda b: (row_blk0 + b, 4)),
                pl.BlockSpec((t_len, 256), lambda b: (row_blk0 + b, 10)),
                pl.BlockSpec((t_len, 256), lambda b: (row_blk0 + b, 11))]
    in_specs += [_const_spec(a) for a in consts]
    args = [proj, proj, proj, proj, *consts]
    if has_state:
        in_specs.append(pl.BlockSpec((1, 2, HG_W, BRANCH_W), lambda b: (b, 0, 0, 0)))
        args.append(s0)
    out_shape = [jax.ShapeDtypeStruct((n_seq * t_len, BRANCH_W), F32)]
    out_specs = [pl.BlockSpec((t_len, BRANCH_W), lambda b: (b, 0))]
    if not has_state:
        out_shape.append(jax.ShapeDtypeStruct((n_seq, 2, N_HEADS, HG_DK, HEAD_V), F32))
        out_specs.append(pl.BlockSpec((1, 2, N_HEADS, HG_DK, HEAD_V), lambda b: (b, 0, 0, 0, 0)))
    return pl.pallas_call(
        functools.partial(_hgrn_kernel, t_len=t_len, has_state=has_state, layer=layer),
        grid=(n_seq,),
        in_specs=in_specs,
        out_specs=out_specs,
        out_shape=out_shape,
        scratch_shapes=[pltpu.VMEM((t_len, HG_W), F32)] * 5 + [pltpu.VMEM((t_len, BRANCH_W), F32)],
        compiler_params=_cparams(("arbitrary",)),
        name="hgrn2_mixer",
    )(*args)


N_PAIR = 2
N_COMBO = N_PAIR * N_HEADS


def _dn_solve(l_ref, r_ref, reverse):
    c = CHUNK
    order = range(c - 1, -1, -1) if reverse else range(c)
    done = []
    for i in order:
        li = l_ref[pl.ds(i, N_COMBO, stride=c), :]
        acc = r_ref[pl.ds(i, N_COMBO, stride=c), :]
        for j in done:
            acc = acc - li[:, j:j + 1] * r_ref[pl.ds(j, N_COMBO, stride=c), :]
        r_ref[pl.ds(i, N_COMBO, stride=c), :] = acc
        done.append(i)


def _dn_chunk_pass(q_s, k_s, v_s, be_s, ge_s, o_s, l_s, r_s, s_pair, ci, d, t_len, consts):
    tri_ref, strict_ref, eye_ref, blk_ref, hm_ref, hv_ref, bd_ref = consts
    c = CHUNK
    lane = lax.broadcasted_iota(jnp.int32, (c, 128), 1)
    lo = lane < HEAD_V
    keep = []
    for sq in range(N_PAIR):
        r0 = pl.multiple_of(sq * t_len + ci * c, c)
        rows = pl.ds(r0, c)
        qc, kc, vc = q_s[rows, :], k_s[rows, :], v_s[rows, :]
        bexp = be_s[rows, d * BRANCH_W:(d + 1) * BRANCH_W]
        gexp = ge_s[rows, d * BRANCH_W:(d + 1) * BRANCH_W]
        dexp = _sel_l(tri_ref[d], gexp)
        dcol = jnp.concatenate([_head_lanes(dexp, h) for h in range(N_HEADS)], axis=0)
        drow = _sel_l(blk_ref[...], dcol * eye_ref[...])
        dec_s = jnp.exp(jnp.where(strict_ref[d] > 0.5, dcol - drow, -1e30))
        dec_i = dec_s + eye_ref[...]
        kst = _stack_heads(kc, hm_ref)
        kk = _mm_nt(kst, kc)
        qk = _mm_nt(_stack_heads(qc, hm_ref), kc)
        bcol = jnp.concatenate([_head_lanes(bexp, h) for h in range(N_HEADS)], axis=0)
        l_s[sq * N_HEADS * c:(sq + 1) * N_HEADS * c, :] = bcol * kk * dec_s
        ed = jnp.exp(dexp)
        vb = vc * bexp
        kb = kc * bexp * ed
        for pr in range(2):
            a = vb[:, pr * 128:(pr + 1) * 128]
            b2 = kb[:, pr * 128:(pr + 1) * 128]
            base = (sq * N_HEADS + 2 * pr) * c
            r_s[base:base + c, :] = jnp.where(lo, a, pltpu.roll(b2, 64, 1))
            r_s[base + c:base + 2 * c, :] = jnp.where(lo, pltpu.roll(a, 64, 1), b2)
        keep.append((rows, qc, kc, dexp, ed, qk * dec_i))
    _dn_solve(l_s, r_s, d == 1)
    new_states = []
    for sq in range(N_PAIR):
        rows, qc, kc, dexp, ed, qkd = keep[sq]
        s = s_pair[sq]
        us, ws = [], []
        for pr in range(2):
            base = (sq * N_HEADS + 2 * pr) * c
            s0 = r_s[base:base + c, :]
            s1 = r_s[base + c:base + 2 * c, :]
            us.append(jnp.where(lo, s0, pltpu.roll(s1, 64, 1)))
            ws.append(jnp.where(lo, pltpu.roll(s0, 64, 1), s1))
        u = jnp.concatenate(us, axis=1)
        w = jnp.concatenate(ws, axis=1)
        v_new = u - _mm(w, s)
        o = _mm(qc * ed, s) + _head_diag(_mm(qkd, v_new), hv_ref, c)
        dl = dexp[0:1, :] if d == 1 else dexp[c - 1:c, :]
        s = s * jnp.exp(dl) + bd_ref[...] * _mm_tn(kc * jnp.exp(dl - dexp), v_new)
        if d == 1:
            o_s[rows, :] += o
        else:
            o_s[rows, :] = o
        new_states.append(s)
    return tuple(new_states)


def _dn_kernel(*refs, t_len, has_state):
    (x_ref, gate_ref, small_ref, cw_ref, alog_ref, dtb_ref, nw_ref,
     tri_ref, strict_ref, eye_ref, blk_ref, exb_ref, exa_ref,
     hm_ref, hv_ref, bd_ref, ones_ref) = refs[:17]
    pos = 17
    if has_state:
        s0_ref = refs[pos]
        pos += 1
    out_ref = refs[pos]
    pos += 1
    if not has_state:
        st_ref = refs[pos]
        pos += 1
    q_s, k_s, v_s, be_s, ge_s, o_s, l_s, r_s = refs[pos:pos + 8]
    n_rows = N_PAIR * t_len

    x = x_ref[...]
    row = lax.broadcasted_iota(jnp.int32, (n_rows, 1), 0) % t_len
    x_prev = jnp.where(row == 0, 0.0, pltpu.roll(x, 1, 0))
    x_next = jnp.where(row == t_len - 1, 0.0, pltpu.roll(x, n_rows - 1, 0))
    y = _silu(x_prev * cw_ref[0:1, :] + x * cw_ref[1:2, :] + x_next * cw_ref[2:3, :])
    cq, ck = y[:, 0:256], y[:, 256:512]
    v_s[...] = y[:, 512:768]
    q_s[...] = cq * lax.rsqrt(_sel_r(cq * cq, ones_ref[...]) + EPS) * (DN_DK ** -0.5)
    k_s[...] = ck * lax.rsqrt(_sel_r(ck * ck, ones_ref[...]) + EPS)
    sm = small_ref[...]
    be_s[...] = _sel_r(_sigmoid(sm), exb_ref[...])
    ge_s[...] = _sel_r(-jnp.exp(alog_ref[...]) * _softplus(sm + dtb_ref[...]), exa_ref[...])

    n_chunks = t_len // CHUNK
    consts = (tri_ref, strict_ref, eye_ref, blk_ref, hm_ref, hv_ref, bd_ref)
    finals = []
    for d in range(2):
        if has_state:
            init = tuple(s0_ref[sq, d] for sq in range(N_PAIR))
        else:
            init = tuple(jnp.zeros((BRANCH_W, BRANCH_W), F32) for _ in range(N_PAIR))

        def body(t, s_pair, d=d):
            ci = (n_chunks - 1 - t) if d == 1 else t
            return _dn_chunk_pass(q_s, k_s, v_s, be_s, ge_s, o_s, l_s, r_s, s_pair, ci, d, t_len, consts)

        finals.append(lax.fori_loop(0, n_chunks, body, init))
    out_ref[...] = _head_norm_gate(o_s[...], gate_ref[...], nw_ref[...], ones_ref[...])
    if not has_state:
        for sq in range(N_PAIR):
            for d in range(2):
                for h in range(N_HEADS):
                    st_ref[sq, d, h] = _head_lanes(finals[d][sq][h * DN_DK:(h + 1) * DN_DK, :], h)


def _dn_call(proj, conv_w, alog_row, dtb_row, nw, s0, t_len, n_seq, row_blk0):
    tri, strict, eye, blk, exb, exa = _dn_consts()
    tri, blk, exb, exa = (jnp.asarray(a, BF) for a in (tri, blk, exb, exa))
    strict, eye = jnp.asarray(strict, F32), jnp.asarray(eye, F32)
    hm, hv, bd, ones_bd = _head_consts(BRANCH_W)
    hm, hv, bd, ones_bd = jnp.asarray(hm, F32), jnp.asarray(hv, F32), jnp.asarray(bd, F32), jnp.asarray(ones_bd, BF)
    has_state = s0 is not None
    n_rows = N_PAIR * t_len
    consts = (conv_w, alog_row, dtb_row, nw, tri, strict, eye, blk, exb, exa, hm, hv, bd, ones_bd)
    in_specs = [pl.BlockSpec((n_rows, 768), lambda b: (row_blk0 + b, 1)),
                pl.BlockSpec((n_rows, 256), lambda b: (row_blk0 + b, 6)),
                pl.BlockSpec((n_rows, 128), lambda b: (row_blk0 + b, COL_SMALL_BLOCK))]
    in_specs += [_const_spec(a) for a in consts]
    args = [proj, proj, proj, *consts]
    if has_state:
        in_specs.append(pl.BlockSpec((N_PAIR, 2, BRANCH_W, BRANCH_W), lambda b: (b, 0, 0, 0)))
        args.append(s0)
    out_shape = [jax.ShapeDtypeStruct((n_seq * t_len, BRANCH_W), F32)]
    out_specs = [pl.BlockSpec((n_rows, BRANCH_W), lambda b: (b, 0))]
    if not has_state:
        out_shape.append(jax.ShapeDtypeStruct((n_seq, 2, N_HEADS, DN_DK, HEAD_V), F32))
        out_specs.append(pl.BlockSpec((N_PAIR, 2, N_HEADS, DN_DK, HEAD_V), lambda b: (b, 0, 0, 0, 0)))
    return pl.pallas_call(
        functools.partial(_dn_kernel, t_len=t_len, has_state=has_state),
        grid=(n_seq // N_PAIR,),
        in_specs=in_specs,
        out_specs=out_specs,
        out_shape=out_shape,
        scratch_shapes=[pltpu.VMEM((n_rows, BRANCH_W), F32)] * 3
                       + [pltpu.VMEM((n_rows, 2 * BRANCH_W), F32)] * 2
                       + [pltpu.VMEM((n_rows, BRANCH_W), F32),
                          pltpu.VMEM((N_COMBO * CHUNK, CHUNK), F32),
                          pltpu.VMEM((N_COMBO * CHUNK, 128), F32)],
        compiler_params=_cparams(("arbitrary",)),
        name="deltanet_mixer",
    )(*args)


def _rope(x, cos, sin):
    lane = lax.broadcasted_iota(jnp.int32, x.shape, 1) % 16
    n = x.shape[1]
    xrot = jnp.where(lane < 8, -pltpu.roll(x, n - 8, 1), pltpu.roll(x, 8, 1))
    return x * cos + xrot * sin


def _att_kernel(*refs, t_len, lat, lam_init):
    blk_ref, lam_ref, nw_ref, qm_ref, hv_ref, ones_ref = refs[:6]
    pos = 6
    if lat:
        cos_ref, sin_ref, ck_ref, cv_ref = refs[pos:pos + 4]
        pos += 4
    out_ref = refs[pos]
    pos += 1
    if not lat:
        nk_ref, nv_ref = refs[pos:pos + 2]
        pos += 2
    if lat:
        q_s, k_s, v_s = refs[pos:pos + 3]

    lv = lam_ref[...]
    lam = (jnp.exp(jnp.sum(lv[0:1] * lv[1:2], axis=1, keepdims=True))
           - jnp.exp(jnp.sum(lv[2:3] * lv[3:4], axis=1, keepdims=True)) + lam_init)
    q = blk_ref[:, 0:256]
    k = blk_ref[:, 256:512]
    v = blk_ref[:, 512:768]
    if lat:
        cos, sin = cos_ref[...], sin_ref[...]
        q_s[...] = _rope(q, cos, sin)
        k_s[0:PAST_LEN, :] = ck_ref[0]
        k_s[PAST_LEN:PAST_LEN + t_len, :] = _rope(k, cos, sin)
        v_s[0:PAST_LEN, :] = cv_ref[0]
        v_s[PAST_LEN:PAST_LEN + t_len, :] = v
        keys = k_s[...].astype(BF)
        vals = v_s[...].astype(BF)
    else:
        keys = k.astype(BF)
        vals = v.astype(BF)
        for h in range(N_HEADS):
            nk_ref[0, h] = _head_lanes(k, h)
            nv_ref[0, h] = _head_lanes(v, h)
    tq = TQ_ATT
    scale = DF_DH ** -0.5
    for qi in range(t_len // tq):
        qt = q_s[qi * tq:(qi + 1) * tq, :] if lat else q[qi * tq:(qi + 1) * tq, :]
        qs = jnp.concatenate([qt * qm_ref[r:r + 1, :] for r in range(2 * N_HEADS)], axis=0)
        s = _mm_nt(qs, keys) * scale
        s = jnp.exp(s - jnp.max(s, axis=1, keepdims=True))
        p = s / jnp.sum(s, axis=1, keepdims=True)
        a = jnp.concatenate(
            [p[(2 * h) * tq:(2 * h + 1) * tq] - lam * p[(2 * h + 1) * tq:(2 * h + 2) * tq]
             for h in range(N_HEADS)], axis=0)
        o = _head_diag(_mm(a, vals), hv_ref, tq)
        ms = _sel_r(o * o, ones_ref[...], 2) * (1.0 / HEAD_V)
        out_ref[qi * tq:(qi + 1) * tq, :] = o * lax.rsqrt(ms + EPS) * nw_ref[...] * (1.0 - lam_init)


def _att_call(proj, lam_p, nw, cache_k, cache_v, rope, t_len, n_seq, row_blk0, lam_init):
    qm = jnp.asarray(_att_consts(), F32)
    _, hv, _, ones_bd = _head_consts(BRANCH_W)
    hv, ones_bd = jnp.asarray(hv, F32), jnp.asarray(ones_bd, BF)
    lat = cache_k is not None
    consts = (lam_p, nw, qm, hv, ones_bd)
    in_specs = [pl.BlockSpec((t_len, 768), lambda b: (row_blk0 + b, 4))]
    in_specs += [_const_spec(a) for a in consts]
    args = [proj, *consts]
    out_shape = [jax.ShapeDtypeStruct((n_seq * t_len, BRANCH_W), F32)]
    out_specs = [pl.BlockSpec((t_len, BRANCH_W), lambda b: (b, 0))]
    scratch = []
    if lat:
        cos, sin = rope
        in_specs += [_const_spec(cos), _const_spec(sin),
                     pl.BlockSpec((1, PAST_LEN, BRANCH_W), lambda b: (b, 0, 0)),
                     pl.BlockSpec((1, PAST_LEN, BRANCH_W), lambda b: (b, 0, 0))]
        args += [cos, sin, cache_k, cache_v]
        scratch = [pltpu.VMEM((t_len, BRANCH_W), F32),
                   pltpu.VMEM((PAST_LEN + t_len, BRANCH_W), F32),
                   pltpu.VMEM((PAST_LEN + t_len, BRANCH_W), F32)]
    else:
        for _ in range(2):
            out_shape.append(jax.ShapeDtypeStruct((n_seq, N_HEADS, t_len, HEAD_V), F32))
            out_specs.append(pl.BlockSpec((1, N_HEADS, t_len, HEAD_V), lambda b: (b, 0, 0, 0)))
    return pl.pallas_call(
        functools.partial(_att_kernel, t_len=t_len, lat=lat, lam_init=lam_init),
        grid=(n_seq,),
        in_specs=in_specs,
        out_specs=out_specs,
        out_shape=out_shape,
        scratch_shapes=scratch,
        compiler_params=_cparams(("arbitrary",)),
        name="diff_attention",
    )(*args)


def _block_diag_states(st, dk):
    eye = jnp.eye(N_HEADS, dtype=st.dtype)
    b = st.shape[0]
    return jnp.einsum('bnhde,hg->bnhdge', st, eye).reshape(b, 2, N_HEADS * dk, N_HEADS * HEAD_V)


def _in_perm():
    offs = np.concatenate([[0], np.cumsum(IN_ORIG)])
    seg = lambda a, b: np.arange(offs[a], offs[b])
    return np.concatenate([seg(0, 4), seg(5, 6), seg(8, 9), seg(9, 16), seg(4, 5), seg(6, 8)])


def kernel(x_prompt, x_sample, cache_diff_k, cache_diff_v, state_gla, state_dn, state_hgrn, c, c_ctx,
           norm_w, w_mod, b_mod, ffn1_in, ffn1_down, ffn2_in, ffn2_down, w_in, gla_w2, gla_b, gla_norm,
           dn_conv, dn_a_log, dn_dt_bias, dn_norm, hg_lb_logits, hg_norm, diff_lambda, diff_norm,
           w_branch, w_mgate, w_out, final_norm):
    x = jnp.concatenate([x_prompt.reshape(N_CTX_TOK, D_MODEL), x_sample.reshape(-1, D_MODEL)], axis=0)
    c_rows = jnp.concatenate([c_ctx[None, :], c, jnp.zeros((8 - 1 - N_LAT_SEQ, D_MODEL), F32)], axis=0)
    mod = _mod_call(c_rows, w_mod, b_mod).reshape(DEPTH, 8, N_MOD, D_MODEL)
    rope = _rope_tables()
    perm = _in_perm()
    lb_logits = hg_lb_logits.reshape(DEPTH, 2 * HG_W)
    lat_blk = N_CTX_TOK // T_LAT
    tile4 = lambda a: jnp.tile(a, N_HEADS)[None, :]
    fin = final_norm[None, :]
    new_k, new_v, new_gla, new_dn, new_hg = [], [], [], [], []
    for l in range(DEPTH):
        mod_l = mod[l]
        lam_init = 0.8 - 0.6 * math.exp(-0.3 * l)
        w_in_p = jnp.pad(w_in[l][:, perm], ((0, 0), (0, N_IN_PAD - N_IN))).astype(BF)
        w2bd = jnp.zeros((128, 2 * GLA_KW), F32)
        w2bd = w2bd.at[0:GLA_LOWRANK, 0:GLA_KW].set(gla_w2[l, 0])
        w2bd = w2bd.at[GLA_LOWRANK:2 * GLA_LOWRANK, GLA_KW:].set(gla_w2[l, 1]).astype(BF)
        gbias = gla_b[l].reshape(1, 2 * GLA_KW)
        alog_row = jnp.zeros((1, 128), F32).at[0, SMALL_DNA:SMALL_DNA + 8].set(dn_a_log[l].reshape(-1))
        dtb_row = jnp.zeros((1, 128), F32).at[0, SMALL_DNA:SMALL_DNA + 8].set(dn_dt_bias[l].reshape(-1))

        x = _ffn_call(x, mod_l, norm_w[l, 0][None, :], ffn1_in[l].astype(BF), ffn1_down[l].astype(BF),
                      fin, 0, False)
        proj = _proj_call(x, mod_l, norm_w[l, 1][None, :], w_in_p)

        a_c, st_a = _gla_call(proj, w2bd, gbias, tile4(gla_norm[l]), None, T_CTX, N_CTX_SEQ, 0)
        b_c, st_b = _dn_call(proj, dn_conv[l], alog_row, dtb_row, tile4(dn_norm[l]), None,
                             T_CTX, N_CTX_SEQ, 0)
        c_c, st_c = _hgrn_call(proj, lb_logits, tile4(hg_norm[l]), None, T_CTX, N_CTX_SEQ, 0, l)
        d_c, k_l, v_l = _att_call(proj, diff_lambda[l], tile4(diff_norm[l]), None, None, None,
                                  T_CTX, N_CTX_SEQ, 0, lam_init)
        ck = cache_diff_k[:, l].transpose(0, 2, 1, 3).reshape(N_LAT_SEQ, PAST_LEN, BRANCH_W)
        cv = cache_diff_v[:, l].transpose(0, 2, 1, 3).reshape(N_LAT_SEQ, PAST_LEN, BRANCH_W)
        (a_l,) = _gla_call(proj, w2bd, gbias, tile4(gla_norm[l]), _block_diag_states(state_gla[:, l], GLA_DK),
                           T_LAT, N_LAT_SEQ, lat_blk)
        (b_l,) = _dn_call(proj, dn_conv[l], alog_row, dtb_row, tile4(dn_norm[l]),
                          _block_diag_states(state_dn[:, l], DN_DK), T_LAT, N_LAT_SEQ,
                          N_CTX_TOK // (N_PAIR * T_LAT))
        (c_l,) = _hgrn_call(proj, lb_logits, tile4(hg_norm[l]), _block_diag_states(state_hgrn[:, l], HG_DK),
                            T_LAT, N_LAT_SEQ, lat_blk, l)
        (d_l,) = _att_call(proj, diff_lambda[l], tile4(diff_norm[l]), ck, cv, rope,
                           T_LAT, N_LAT_SEQ, lat_blk, lam_init)
        branches = [jnp.concatenate(p, axis=0) for p in ((a_c, a_l), (b_c, b_l), (c_c, c_l), (d_c, d_l))]

        x = _merge_call(x, mod_l, norm_w[l, 1][None, :], branches, w_mgate[l].astype(BF),
                        w_branch[l].astype(BF), w_out[l].astype(BF))
        x = _ffn_call(x, mod_l, norm_w[l, 2][None, :], ffn2_in[l].astype(BF), ffn2_down[l].astype(BF),
                      fin, 2, l == DEPTH - 1)
        new_k.append(k_l)
        new_v.append(v_l)
        new_gla.append(st_a)
        new_dn.append(st_b)
        new_hg.append(st_c)
    y_prompt = x[:N_CTX_TOK].reshape(N_CTX_SEQ, T_CTX, D_MODEL)
    y_sample = x[N_CTX_TOK:].reshape(N_LAT_SEQ, T_LAT, D_MODEL)
    return (y_prompt, y_sample, jnp.stack(new_k, axis=1), jnp.stack(new_v, axis=1),
            jnp.stack(new_gla, axis=1), jnp.stack(new_dn, axis=1), jnp.stack(new_hg, axis=1))
```

```python
import functools
import math

import numpy as np
import jax
import jax.numpy as jnp
from jax import lax
from jax.experimental import pallas as pl
from jax.experimental.pallas import tpu as pltpu

F32 = jnp.float32
BF = jnp.bfloat16

D_MODEL = 1024
N_CTX_SEQ = 32
T_CTX = 256
DEPTH = 2
N_LAT_SEQ = 2
T_LAT = 1024
PAST_LEN = 512
GRID_W = 64
N_HEADS = 4
BRANCH_W = 256
HEAD_V = 64
GLA_DK = 32
GLA_KW = 128
GLA_LOWRANK = 16
GLA_TAU = 16.0
DN_DK = 64
HG_DK = 64
HG_W = 256
DF_DH = 32
ROPE_BASE = 10000.0
D_FF = 2816
N_MOD = 9
CHUNK = 64
EPS = 1e-6
N_CTX_TOK = N_CTX_SEQ * T_CTX
N_TOK = N_CTX_TOK + N_LAT_SEQ * T_LAT
N_LEVELS = 6

IN_ORIG = (128, 128, 256, 256, 32, 768, 8, 8, 256, 256, 512, 256, 256, 256, 256, 256)
N_IN = sum(IN_ORIG)
N_IN_PAD = 3968
COL_SMALL_BLOCK = 30
SMALL_LR = 0
SMALL_DNB = 32
SMALL_DNA = 40

VMEM_LIMIT = 56 * 1024 * 1024

TM_FFN = 512
TF_FFN = 1408
TM_PROJ = 512
TM_MERGE = 512
TN_MOD = 2304
TQ_ATT = 256


def _silu(x):
    return x * (1.0 / (1.0 + jnp.exp(-x)))


def _sigmoid(x):
    return 1.0 / (1.0 + jnp.exp(-x))


def _softplus(x):
    return jnp.maximum(x, 0.0) + jnp.log(1.0 + jnp.exp(-jnp.abs(x)))


def _log_sigmoid(x):
    return -_softplus(-x)


def _mm(a, b):
    return jnp.dot(a.astype(BF), b.astype(BF), preferred_element_type=F32)


def _mm_nt(a, b):
    return lax.dot_general(a.astype(BF), b.astype(BF), (((1,), (1,)), ((), ())),
                           preferred_element_type=F32)


def _mm_tn(a, b):
    return lax.dot_general(a.astype(BF), b.astype(BF), (((0,), (0,)), ((), ())),
                           preferred_element_type=F32)


def _split(x, n):
    parts = []
    r = x
    for i in range(n):
        p = r.astype(BF)
        parts.append(p)
        if i + 1 < n:
            r = r - p.astype(F32)
    return parts


def _sel_l(m01, x, n=3):
    out = None
    for p in _split(x, n):
        t = jnp.dot(m01, p, preferred_element_type=F32)
        out = t if out is None else out + t
    return out


def _sel_r(x, m01, n=3):
    out = None
    for p in _split(x, n):
        t = jnp.dot(p, m01, preferred_element_type=F32)
        out = t if out is None else out + t
    return out


def _sel_tn(x, m01, n=3):
    out = None
    for p in _split(x, n):
        t = lax.dot_general(p, m01, (((0,), (0,)), ((), ())), preferred_element_type=F32)
        out = t if out is None else out + t
    return out


def _rms(x, w):
    return x * lax.rsqrt(jnp.mean(x * x, axis=-1, keepdims=True) + EPS) * w


def _head_lanes(x, h):
    blk = x[:, (h // 2) * 128:(h // 2 + 1) * 128]
    if h % 2:
        blk = pltpu.roll(blk, 64, 1)
    return blk[:, :HEAD_V]


def _stack_heads(x, hm_ref):
    return jnp.concatenate([x * hm_ref[h:h + 1, :] for h in range(N_HEADS)], axis=0)


def _head_diag(o_full, hv_ref, c):
    out = None
    for h in range(N_HEADS):
        t = o_full[h * c:(h + 1) * c, :] * hv_ref[h:h + 1, :]
        out = t if out is None else out + t
    return out


def _head_norm_gate(o, gate, nw, ones_bd):
    ms = _sel_r(o * o, ones_bd, 2) * (1.0 / HEAD_V)
    return o * lax.rsqrt(ms + EPS) * nw * _silu(gate)


def _mod_row(i, tm):
    return jnp.maximum(i * tm - (N_CTX_TOK - T_LAT), 0) // T_LAT


def _cparams(sem):
    return pltpu.CompilerParams(dimension_semantics=sem, vmem_limit_bytes=VMEM_LIMIT)


@functools.lru_cache(maxsize=None)
def _scan_consts(w, reverse):
    c = CHUNK
    idx = np.arange(c)
    i = idx[:, None]
    m = idx[None, :]
    ms, pm = [], []
    s = c // 2
    while s >= 1:
        par = idx // (2 * s)
        right = (idx % (2 * s)) >= s
        same = par[:, None] == par[None, :]
        if not reverse:
            e = (par * 2 * s + s - 1)[:, None]
            mat = np.where(right[:, None], (m > e) & (m <= i), (m > i) & (m <= e))
            p = same & right[:, None] & (~right)[None, :]
        else:
            e = (par * 2 * s + s)[:, None]
            mat = np.where(right[:, None], (m >= e) & (m < i), (m >= i) & (m < e))
            p = same & (~right)[:, None] & right[None, :]
        ms.append(mat)
        pm.append(p)
        s //= 2
    if not reverse:
        ms.append(m <= i)
        ms.append(m > i)
    else:
        ms.append(m >= i)
        ms.append(m < i)
    pm.append(i == m)
    mstack = np.concatenate(ms, axis=0).astype(np.float32)
    pmask = np.stack([np.tile(p, (1, N_HEADS)) for p in pm]).astype(np.float32)
    return mstack, pmask


@functools.lru_cache(maxsize=None)
def _head_consts(w):
    dk = w // N_HEADS
    hm = np.zeros((N_HEADS, w), np.float32)
    hv = np.zeros((N_HEADS, BRANCH_W), np.float32)
    for h in range(N_HEADS):
        hm[h, h * dk:(h + 1) * dk] = 1.0
        hv[h, h * HEAD_V:(h + 1) * HEAD_V] = 1.0
    bd = hm.T @ hv
    ones_bd = hv.T @ hv
    return hm, hv, bd, ones_bd


@functools.lru_cache(maxsize=None)
def _dn_consts():
    c = CHUNK
    idx = np.arange(c)
    i = idx[:, None]
    j = idx[None, :]
    tri = np.stack([(j <= i), (j >= i)]).astype(np.float32)
    strict = np.stack([np.tile(j < i, (N_HEADS, 1)), np.tile(j > i, (N_HEADS, 1))]).astype(np.float32)
    eye = np.tile(np.eye(c), (N_HEADS, 1)).astype(np.float32)
    blk = np.kron(np.eye(N_HEADS), np.ones((c, c))).astype(np.float32)
    exb = np.zeros((128, 2 * BRANCH_W), np.float32)
    exa = np.zeros((128, 2 * BRANCH_W), np.float32)
    for n in range(2):
        for h in range(N_HEADS):
            lo = n * BRANCH_W + h * HEAD_V
            exb[SMALL_DNB + n * N_HEADS + h, lo:lo + HEAD_V] = 1.0
            exa[SMALL_DNA + n * N_HEADS + h, lo:lo + HEAD_V] = 1.0
    return tri, strict, eye, blk, exb, exa


@functools.lru_cache(maxsize=None)
def _att_consts():
    qm = np.zeros((2 * N_HEADS, BRANCH_W), np.float32)
    for h in range(N_HEADS):
        for mp in range(2):
            lo = h * HEAD_V + mp * DF_DH
            qm[2 * h + mp, lo:lo + DF_DH] = 1.0
    return qm


def _rope_tables():
    rows = T_LAT // GRID_W
    row = jnp.repeat(jnp.arange(rows), GRID_W).astype(F32)
    col = jnp.tile(jnp.arange(GRID_W), rows).astype(F32)
    half = DF_DH // 2
    inv = ROPE_BASE ** (-jnp.arange(0, half, 2, dtype=F32) / half)

    def angles(pos):
        a = pos[:, None] * inv[None, :]
        return jnp.concatenate([a, a], axis=-1)

    ang = jnp.concatenate([angles(row), angles(col)], axis=-1)
    reps = BRANCH_W // DF_DH
    return jnp.tile(jnp.cos(ang), (1, reps)), jnp.tile(jnp.sin(ang), (1, reps))


def _mod_kernel(c_ref, w_ref, b_ref, o_ref):
    a = _silu(c_ref[...])
    w = w_ref[0]
    out = None
    for ap in _split(a, 2):
        for wp in _split(w, 2):
            t = jnp.dot(ap, wp, preferred_element_type=F32)
            out = t if out is None else out + t
    o_ref[0] = out + b_ref[0]


def _mod_call(c_rows, w_mod, b_mod):
    n_t = (N_MOD * D_MODEL) // TN_MOD
    return pl.pallas_call(
        _mod_kernel,
        grid=(DEPTH, n_t),
        in_specs=[pl.BlockSpec((8, D_MODEL), lambda l, j: (0, 0)),
                  pl.BlockSpec((1, D_MODEL, TN_MOD), lambda l, j: (l, 0, j)),
                  pl.BlockSpec((1, 1, TN_MOD), lambda l, j: (l, 0, j))],
        out_specs=pl.BlockSpec((1, 8, TN_MOD), lambda l, j: (l, 0, j)),
        out_shape=jax.ShapeDtypeStruct((DEPTH, 8, N_MOD * D_MODEL), F32),
        compiler_params=_cparams(("arbitrary", "arbitrary")),
        name="mod_vectors",
    )(c_rows, w_mod, b_mod.reshape(DEPTH, 1, N_MOD * D_MODEL))


def _ffn_kernel(x_ref, mod_ref, nw_ref, wg_ref, wu_ref, wd_ref, fn_ref, o_ref, h_ref, acc_ref,
                *, sub, final):
    f = pl.program_id(1)

    @pl.when(f == 0)
    def _():
        x = x_ref[...]
        sh = mod_ref[0, 3 * sub:3 * sub + 1, :]
        sc = mod_ref[0, 3 * sub + 1:3 * sub + 2, :]
        h_ref[...] = (_rms(x, nw_ref[...]) * (1.0 + sc) + sh).astype(BF)
        acc_ref[...] = jnp.zeros_like(acc_ref)

    h = h_ref[...]
    g = jnp.dot(h, wg_ref[...], preferred_element_type=F32)
    u = jnp.dot(h, wu_ref[...], preferred_element_type=F32)
    acc_ref[...] += jnp.dot((_silu(g) * u).astype(BF), wd_ref[...], preferred_element_type=F32)

    @pl.when(f == pl.num_programs(1) - 1)
    def _():
        ga = mod_ref[0, 3 * sub + 2:3 * sub + 3, :]
        y = x_ref[...] + 0.5 * ga * acc_ref[...]
        if final:
            y = _rms(y, fn_ref[...])
        o_ref[...] = y


def _ffn_call(x, mod_l, nw, w_in, w_down, final_w, sub, final):
    n_f = D_FF // TF_FFN
    tm = TM_FFN
    return pl.pallas_call(
        functools.partial(_ffn_kernel, sub=sub, final=final),
        grid=(N_TOK // tm, n_f),
        in_specs=[pl.BlockSpec((tm, D_MODEL), lambda i, f: (i, 0)),
                  pl.BlockSpec((1, N_MOD, D_MODEL), lambda i, f: (_mod_row(i, tm), 0, 0)),
                  pl.BlockSpec((1, D_MODEL), lambda i, f: (0, 0)),
                  pl.BlockSpec((D_MODEL, TF_FFN), lambda i, f: (0, f)),
                  pl.BlockSpec((D_MODEL, TF_FFN), lambda i, f: (0, n_f + f)),
                  pl.BlockSpec((TF_FFN, D_MODEL), lambda i, f: (f, 0)),
                  pl.BlockSpec((1, D_MODEL), lambda i, f: (0, 0))],
        out_specs=pl.BlockSpec((tm, D_MODEL), lambda i, f: (i, 0)),
        out_shape=jax.ShapeDtypeStruct((N_TOK, D_MODEL), F32),
        scratch_shapes=[pltpu.VMEM((tm, D_MODEL), BF), pltpu.VMEM((tm, D_MODEL), F32)],
        compiler_params=_cparams(("arbitrary", "arbitrary")),
        name="swiglu_half_step",
    )(x, mod_l, nw, w_in, w_in, w_down, final_w)


def _proj_kernel(x_ref, mod_ref, nw_ref, w_ref, o_ref):
    sh = mod_ref[0, 3:4, :]
    sc = mod_ref[0, 4:5, :]
    h = (_rms(x_ref[...], nw_ref[...]) * (1.0 + sc) + sh).astype(BF)
    o_ref[...] = jnp.dot(h, w_ref[...], preferred_element_type=F32)


def _proj_call(x, mod_l, nw, w_in_p):
    tm = TM_PROJ
    return pl.pallas_call(
        _proj_kernel,
        grid=(N_TOK // tm,),
        in_specs=[pl.BlockSpec((tm, D_MODEL), lambda i: (i, 0)),
                  pl.BlockSpec((1, N_MOD, D_MODEL), lambda i: (_mod_row(i, tm), 0, 0)),
                  pl.BlockSpec((1, D_MODEL), lambda i: (0, 0)),
                  pl.BlockSpec((D_MODEL, N_IN_PAD), lambda i: (0, 0))],
        out_specs=pl.BlockSpec((tm, N_IN_PAD), lambda i: (i, 0)),
        out_shape=jax.ShapeDtypeStruct((N_TOK, N_IN_PAD), F32),
        compiler_params=_cparams(("arbitrary",)),
        name="mixer_in_proj",
    )(x, mod_l, nw, w_in_p)


def _merge_kernel(x_ref, mod_ref, nw_ref, *rest):
    ctx_refs, lat_refs = rest[0:4], rest[4:8]
    wg_ref, wb_ref, wo_ref, o_ref = rest[8:12]
    x = x_ref[...]
    sh = mod_ref[0, 3:4, :]
    sc = mod_ref[0, 4:5, :]
    ga = mod_ref[0, 5:6, :]
    h = (_rms(x, nw_ref[...]) * (1.0 + sc) + sh).astype(BF)
    is_lat = pl.program_id(0) >= N_CTX_TOK // TM_MERGE
    mixed = None
    for n in range(4):
        gate = _sigmoid(jnp.dot(h, wg_ref[:, n * D_MODEL:(n + 1) * D_MODEL], preferred_element_type=F32))
        br = jnp.where(is_lat, lat_refs[n][...], ctx_refs[n][...])
        up = jnp.dot(br.astype(BF), wb_ref[n], preferred_element_type=F32)
        mixed = gate * up if mixed is None else mixed + gate * up
    out = jnp.dot(mixed.astype(BF), wo_ref[...], preferred_element_type=F32)
    o_ref[...] = x + ga * out


def _merge_call(x, mod_l, nw, ctx_branches, lat_branches, w_mgate, w_branch, w_out):
    tm = TM_MERGE
    n_ctx = N_CTX_TOK // tm
    cspec = pl.BlockSpec((tm, BRANCH_W), lambda i: (jnp.minimum(i, n_ctx - 1), 0))
    lspec = pl.BlockSpec((tm, BRANCH_W), lambda i: (jnp.maximum(i - n_ctx, 0), 0))
    return pl.pallas_call(
        _merge_kernel,
        grid=(N_TOK // tm,),
        in_specs=[pl.BlockSpec((tm, D_MODEL), lambda i: (i, 0)),
                  pl.BlockSpec((1, N_MOD, D_MODEL), lambda i: (_mod_row(i, tm), 0, 0)),
                  pl.BlockSpec((1, D_MODEL), lambda i: (0, 0)),
                  cspec, cspec, cspec, cspec, lspec, lspec, lspec, lspec,
                  pl.BlockSpec((D_MODEL, 4 * D_MODEL), lambda i: (0, 0)),
                  pl.BlockSpec((4, BRANCH_W, D_MODEL), lambda i: (0, 0, 0)),
                  pl.BlockSpec((D_MODEL, D_MODEL), lambda i: (0, 0))],
        out_specs=pl.BlockSpec((tm, D_MODEL), lambda i: (i, 0)),
        out_shape=jax.ShapeDtypeStruct((N_TOK, D_MODEL), F32),
        compiler_params=_cparams(("arbitrary",)),
        name="gated_merge",
    )(x, mod_l, nw, *ctx_branches, *lat_branches, w_mgate, w_branch, w_out)


def _gated_chunk(qc, kc, vc, gc, st, ms_ref, pm_ref, hm_ref, hv_ref, bdt_ref, reverse):
    c = CHUNK
    e = jnp.exp(_sel_l(ms_ref[...], gc, 2))
    kst = _stack_heads(kc, hm_ref)
    a = pm_ref[N_LEVELS] * _mm_nt(qc, kst)
    for lv in range(N_LEVELS):
        el = e[lv * c:(lv + 1) * c]
        a = a + pm_ref[lv] * _mm_nt(qc * el, kst * jnp.concatenate([el] * N_HEADS, axis=0))
    e_b = e[N_LEVELS * c:(N_LEVELS + 1) * c]
    e_t = e[(N_LEVELS + 1) * c:(N_LEVELS + 2) * c]
    o = _mm(a, _stack_heads(vc, hv_ref)) + _mm_nt(qc * e_b, st)
    dec = e_b[0:1, :] if reverse else e_b[c - 1:c, :]
    st_new = st * dec + bdt_ref[...] * _mm_tn(vc, kc * e_t)
    return o, st_new


def _gated_scan(q_ref, kf_ref, kb_ref, v_ref, gf_ref, gb_ref, of_ref, ob_ref, s0f, s0b, cf, cb, n_chunks):
    def body(t, carry):
        sf, sb = carry
        rf = pl.ds(pl.multiple_of(t * CHUNK, CHUNK), CHUNK)
        rb = pl.ds(pl.multiple_of((n_chunks - 1 - t) * CHUNK, CHUNK), CHUNK)
        o_f, sf = _gated_chunk(q_ref[rf, :], kf_ref[rf, :], v_ref[rf, :], gf_ref[rf, :], sf, *cf, False)
        o_b, sb = _gated_chunk(q_ref[rb, :], kb_ref[rb, :], v_ref[rb, :], gb_ref[rb, :], sb, *cb, True)
        of_ref[rf, :] = o_f
        ob_ref[rb, :] = o_b
        return sf, sb

    return lax.fori_loop(0, n_chunks, body, (s0f, s0b))


def _write_states(st_ref, st_f, st_b, dk, eye_ref):
    for d, st in enumerate((st_f, st_b)):
        for h in range(N_HEADS):
            tr = _sel_tn(st[h * HEAD_V:(h + 1) * HEAD_V, :], eye_ref[...])
            st_ref[0, d, h] = tr[h * dk:(h + 1) * dk, :]


def _gla_kernel(*refs, t_len, has_state):
    (blk_ref, small_ref, w2_ref, gb_ref, nw_ref,
     msf_ref, pmf_ref, msb_ref, pmb_ref, hm_ref, hv_ref, bdt_ref, ones_ref, eye_ref) = refs[:14]
    pos = 14
    if has_state:
        s0_ref = refs[pos]
        pos += 1
    out_ref = refs[pos]
    pos += 1
    if not has_state:
        st_ref = refs[pos]
        pos += 1
    q_s, gf_s, gb_s, of_s, ob_s = refs[pos:pos + 5]

    q_s[...] = blk_ref[:, 0:128] * (GLA_DK ** -0.5)
    z = _mm(small_ref[...], w2_ref[...]) + gb_ref[...]
    g = _log_sigmoid(z) * (1.0 / GLA_TAU)
    gf_s[...] = g[:, 0:128]
    gb_s[...] = g[:, 128:256]
    k_ref = blk_ref.at[:, 128:256]
    v_ref = blk_ref.at[:, 256:512]
    n_chunks = t_len // CHUNK
    if has_state:
        s0f, s0b = s0_ref[0, 0], s0_ref[0, 1]
    else:
        s0f = s0b = jnp.zeros((BRANCH_W, GLA_KW), F32)
    cf = (msf_ref, pmf_ref, hm_ref, hv_ref, bdt_ref)
    cb = (msb_ref, pmb_ref, hm_ref, hv_ref, bdt_ref)
    s_f, s_b = _gated_scan(q_s, k_ref, k_ref, v_ref, gf_s, gb_s, of_s, ob_s, s0f, s0b, cf, cb, n_chunks)
    out_ref[...] = _head_norm_gate(of_s[...] + ob_s[...], blk_ref[:, 512:768], nw_ref[...], ones_ref[...])
    if not has_state:
        _write_states(st_ref, s_f, s_b, GLA_DK, eye_ref)


def _const_spec(a):
    nd = a.ndim
    return pl.BlockSpec(a.shape, lambda b, _n=nd: (0,) * _n)


def _scan_const_arrays(w):
    msf, pmf = _scan_consts(w, False)
    msb, pmb = _scan_consts(w, True)
    hm, hv, bd, ones_bd = _head_consts(w)
    return (jnp.asarray(msf, BF), jnp.asarray(pmf, F32), jnp.asarray(msb, BF), jnp.asarray(pmb, F32),
            jnp.asarray(hm, F32), jnp.asarray(hv, F32), jnp.asarray(bd.T, F32), jnp.asarray(ones_bd, BF),
            jnp.eye(HEAD_V, dtype=BF))


def _gla_call(proj, w2bd, gbias, nw, s0, t_len, n_seq, row_blk0):
    msf, pmf, msb, pmb, hm, hv, bdt, ones_bd, eye = _scan_const_arrays(GLA_KW)
    has_state = s0 is not None
    consts = (w2bd, gbias, nw, msf, pmf, msb, pmb, hm, hv, bdt, ones_bd, eye)
    in_specs = [pl.BlockSpec((t_len, 768), lambda b: (row_blk0 + b, 0)),
                pl.BlockSpec((t_len, 128), lambda b: (row_blk0 + b, COL_SMALL_BLOCK))]
    in_specs += [_const_spec(a) for a in consts]
    args = [proj, proj, *consts]
    if has_state:
        in_specs.append(pl.BlockSpec((1, 2, BRANCH_W, GLA_KW), lambda b: (b, 0, 0, 0)))
        args.append(s0)
    out_shape = [jax.ShapeDtypeStruct((n_seq * t_len, BRANCH_W), F32)]
    out_specs = [pl.BlockSpec((t_len, BRANCH_W), lambda b: (b, 0))]
    if not has_state:
        out_shape.append(jax.ShapeDtypeStruct((n_seq, 2, N_HEADS, GLA_DK, HEAD_V), F32))
        out_specs.append(pl.BlockSpec((1, 2, N_HEADS, GLA_DK, HEAD_V), lambda b: (b, 0, 0, 0, 0)))
    return pl.pallas_call(
        functools.partial(_gla_kernel, t_len=t_len, has_state=has_state),
        grid=(n_seq,),
        in_specs=in_specs,
        out_specs=out_specs,
        out_shape=out_shape,
        scratch_shapes=[pltpu.VMEM((t_len, GLA_KW), F32)] * 3 + [pltpu.VMEM((t_len, BRANCH_W), F32)] * 2,
        compiler_params=_cparams(("arbitrary",)),
        name="gla_mixer",
    )(*args)


def _hgrn_kernel(*refs, t_len, has_state, layer):
    (q_ref, f_ref, v_ref, gate_ref, lbl_ref, nw_ref,
     msf_ref, pmf_ref, msb_ref, pmb_ref, hm_ref, hv_ref, bdt_ref, ones_ref, eye_ref) = refs[:15]
    pos = 15
    if has_state:
        s0_ref = refs[pos]
        pos += 1
    out_ref = refs[pos]
    pos += 1
    if not has_state:
        st_ref = refs[pos]
        pos += 1
    q_s, kf_s, kb_s, gf_s, gb_s, of_s, ob_s = refs[pos:pos + 7]

    lg = lbl_ref[...]
    mx = jnp.max(lg, axis=0, keepdims=True)
    ex = jnp.exp(lg - mx)
    p = ex / jnp.sum(ex, axis=0, keepdims=True)
    lb = jnp.sum(p[0:layer + 1], axis=0, keepdims=True) - p[0:1]

    q_s[...] = _silu(q_ref[...]) * (HG_DK ** -0.5)
    f = lb + (1.0 - lb) * _sigmoid(f_ref[...])
    kf_s[...] = 1.0 - f[:, 0:HG_W]
    kb_s[...] = 1.0 - f[:, HG_W:2 * HG_W]
    lf = jnp.log(f)
    gf_s[...] = lf[:, 0:HG_W]
    gb_s[...] = lf[:, HG_W:2 * HG_W]
    n_chunks = t_len // CHUNK
    if has_state:
        s0f, s0b = s0_ref[0, 0], s0_ref[0, 1]
    else:
        s0f = s0b = jnp.zeros((BRANCH_W, HG_W), F32)
    cf = (msf_ref, pmf_ref, hm_ref, hv_ref, bdt_ref)
    cb = (msb_ref, pmb_ref, hm_ref, hv_ref, bdt_ref)
    s_f, s_b = _gated_scan(q_s, kf_s, kb_s, v_ref, gf_s, gb_s, of_s, ob_s, s0f, s0b, cf, cb, n_chunks)
    out_ref[...] = _head_norm_gate(of_s[...] + ob_s[...], gate_ref[...], nw_ref[...], ones_ref[...])
    if not has_state:
        _write_states(st_ref, s_f, s_b, HG_DK, eye_ref)


def _hgrn_call(proj, lb_logits, nw, s0, t_len, n_seq, row_blk0, layer):
    msf, pmf, msb, pmb, hm, hv, bdt, ones_bd, eye = _scan_const_arrays(HG_W)
    has_state = s0 is not None
    consts = (lb_logits, nw, msf, pmf, msb, pmb, hm, hv, bdt, ones_bd, eye)
    in_specs = [pl.BlockSpec((t_len, 256), lambda b: (row_blk0 + b, 7)),
                pl.BlockSpec((t_len, 512), lambda b: (row_blk0 + b, 4)),
                pl.BlockSpec((t_len, 256), lambda b: (row_blk0 + b, 10)),
                pl.BlockSpec((t_len, 256), lambda b: (row_blk0 + b, 11))]
    in_specs += [_const_spec(a) for a in consts]
    args = [proj, proj, proj, proj, *consts]
    if has_state:
        in_specs.append(pl.BlockSpec((1, 2, BRANCH_W, HG_W), lambda b: (b, 0, 0, 0)))
        args.append(s0)
    out_shape = [jax.ShapeDtypeStruct((n_seq * t_len, BRANCH_W), F32)]
    out_specs = [pl.BlockSpec((t_len, BRANCH_W), lambda b: (b, 0))]
    if not has_state:
        out_shape.append(jax.ShapeDtypeStruct((n_seq, 2, N_HEADS, HG_DK, HEAD_V), F32))
        out_specs.append(pl.BlockSpec((1, 2, N_HEADS, HG_DK, HEAD_V), lambda b: (b, 0, 0, 0, 0)))
    return pl.pallas_call(
        functools.partial(_hgrn_kernel, t_len=t_len, has_state=has_state, layer=layer),
        grid=(n_seq,),
        in_specs=in_specs,
        out_specs=out_specs,
        out_shape=out_shape,
        scratch_shapes=[pltpu.VMEM((t_len, HG_W), F32)] * 5 + [pltpu.VMEM((t_len, BRANCH_W), F32)] * 2,
        compiler_params=_cparams(("arbitrary",)),
        name="hgrn2_mixer",
    )(*args)


N_PAIR = 2
N_COMBO = N_PAIR * N_HEADS


def _dn_solve(l_ref, r_ref, x_ref, reverse):
    c = CHUNK
    order = range(c - 1, -1, -1) if reverse else range(c)
    done = []
    for i in order:
        li = l_ref[pl.ds(i, N_COMBO, stride=c), :]
        acc = r_ref[pl.ds(i, N_COMBO, stride=c), :]
        for j in done:
            acc = acc - li[:, j:j + 1] * x_ref[j * N_COMBO:(j + 1) * N_COMBO, :]
        x_ref[i * N_COMBO:(i + 1) * N_COMBO, :] = acc
        done.append(i)


def _dn_chunk_pass(q_s, k_s, v_s, be_s, ge_s, o_s, l_s, r_s, x_s, s_pair, ci, d, t_len, consts):
    tri_ref, strict_ref, eye_ref, blk_ref, hm_ref, hv_ref, bd_ref = consts
    c = CHUNK
    lane = lax.broadcasted_iota(jnp.int32, (c, 128), 1)
    lo = lane < HEAD_V
    keep = []
    for sq in range(N_PAIR):
        r0 = pl.multiple_of(sq * t_len + ci * c, c)
        rows = pl.ds(r0, c)
        qc, kc, vc = q_s[rows, :], k_s[rows, :], v_s[rows, :]
        bexp = be_s[rows, d * BRANCH_W:(d + 1) * BRANCH_W]
        gexp = ge_s[rows, d * BRANCH_W:(d + 1) * BRANCH_W]
        dexp = _sel_l(tri_ref[d], gexp)
        dcol = jnp.concatenate([_head_lanes(dexp, h) for h in range(N_HEADS)], axis=0)
        drow = _sel_l(blk_ref[...], dcol * eye_ref[...])
        dec_s = jnp.exp(jnp.where(strict_ref[d] > 0.5, dcol - drow, -1e30))
        dec_i = dec_s + eye_ref[...]
        kst = _stack_heads(kc, hm_ref)
        kk = _mm_nt(kst, kc)
        qk = _mm_nt(_stack_heads(qc, hm_ref), kc)
        bcol = jnp.concatenate([_head_lanes(bexp, h) for h in range(N_HEADS)], axis=0)
        l_s[sq * N_HEADS * c:(sq + 1) * N_HEADS * c, :] = bcol * kk * dec_s
        ed = jnp.exp(dexp)
        vb = vc * bexp
        kb = kc * bexp * ed
        for pr in range(2):
            a = vb[:, pr * 128:(pr + 1) * 128]
            b2 = kb[:, pr * 128:(pr + 1) * 128]
            base = (sq * N_HEADS + 2 * pr) * c
            r_s[base:base + c, :] = jnp.where(lo, a, pltpu.roll(b2, 64, 1))
            r_s[base + c:base + 2 * c, :] = jnp.where(lo, pltpu.roll(a, 64, 1), b2)
        keep.append((rows, qc, kc, dexp, ed, qk * dec_i))
    _dn_solve(l_s, r_s, x_s, d == 1)
    new_states = []
    for sq in range(N_PAIR):
        rows, qc, kc, dexp, ed, qkd = keep[sq]
        s = s_pair[sq]
        us, ws = [], []
        for pr in range(2):
            cidx = sq * N_HEADS + 2 * pr
            s0 = x_s[pl.ds(cidx, c, stride=N_COMBO), :]
            s1 = x_s[pl.ds(cidx + 1, c, stride=N_COMBO), :]
            us.append(jnp.where(lo, s0, pltpu.roll(s1, 64, 1)))
            ws.append(jnp.where(lo, pltpu.roll(s0, 64, 1), s1))
        u = jnp.concatenate(us, axis=1)
        w = jnp.concatenate(ws, axis=1)
        v_new = u - _mm(w, s)
        o = _mm(qc * ed, s) + _head_diag(_mm(qkd, v_new), hv_ref, c)
        dl = dexp[0:1, :] if d == 1 else dexp[c - 1:c, :]
        s = s * jnp.exp(dl) + bd_ref[...] * _mm_tn(kc * jnp.exp(dl - dexp), v_new)
        if d == 1:
            o_s[rows, :] += o
        else:
            o_s[rows, :] = o
        new_states.append(s)
    return tuple(new_states)


def _dn_kernel(*refs, t_len, has_state):
    (x_ref, gate_ref, small_ref, cw_ref, alog_ref, dtb_ref, nw_ref,
     tri_ref, strict_ref, eye_ref, blk_ref, exb_ref, exa_ref,
     hm_ref, hv_ref, bd_ref, ones_ref) = refs[:17]
    pos = 17
    if has_state:
        s0_ref = refs[pos]
        pos += 1
    out_ref = refs[pos]
    pos += 1
    if not has_state:
        st_ref = refs[pos]
        pos += 1
    q_s, k_s, v_s, be_s, ge_s, o_s, l_s, r_s, x_s = refs[pos:pos + 9]
    n_rows = N_PAIR * t_len

    x = x_ref[...]
    row = lax.broadcasted_iota(jnp.int32, (n_rows, 1), 0) % t_len
    x_prev = jnp.where(row == 0, 0.0, pltpu.roll(x, 1, 0))
    x_next = jnp.where(row == t_len - 1, 0.0, pltpu.roll(x, n_rows - 1, 0))
    y = _silu(x_prev * cw_ref[0:1, :] + x * cw_ref[1:2, :] + x_next * cw_ref[2:3, :])
    cq, ck = y[:, 0:256], y[:, 256:512]
    v_s[...] = y[:, 512:768]
    q_s[...] = cq * lax.rsqrt(_sel_r(cq * cq, ones_ref[...]) + EPS) * (DN_DK ** -0.5)
    k_s[...] = ck * lax.rsqrt(_sel_r(ck * ck, ones_ref[...]) + EPS)
    sm = small_ref[...]
    be_s[...] = _sel_r(_sigmoid(sm), exb_ref[...])
    ge_s[...] = _sel_r(-jnp.exp(alog_ref[...]) * _softplus(sm + dtb_ref[...]), exa_ref[...])

    n_chunks = t_len // CHUNK
    consts = (tri_ref, strict_ref, eye_ref, blk_ref, hm_ref, hv_ref, bd_ref)
    finals = []
    for d in range(2):
        if has_state:
            init = tuple(s0_ref[sq, d] for sq in range(N_PAIR))
        else:
            init = tuple(jnp.zeros((BRANCH_W, BRANCH_W), F32) for _ in range(N_PAIR))

        def body(t, s_pair, d=d):
            ci = (n_chunks - 1 - t) if d == 1 else t
            return _dn_chunk_pass(q_s, k_s, v_s, be_s, ge_s, o_s, l_s, r_s, x_s, s_pair, ci, d, t_len, consts)

        finals.append(lax.fori_loop(0, n_chunks, body, init))
    out_ref[...] = _head_norm_gate(o_s[...], gate_ref[...], nw_ref[...], ones_ref[...])
    if not has_state:
        for sq in range(N_PAIR):
            for d in range(2):
                for h in range(N_HEADS):
                    st_ref[sq, d, h] = _head_lanes(finals[d][sq][h * DN_DK:(h + 1) * DN_DK, :], h)


def _dn_call(proj, conv_w, alog_row, dtb_row, nw, s0, t_len, n_seq, row_blk0):
    tri, strict, eye, blk, exb, exa = _dn_consts()
    tri, blk, exb, exa = (jnp.asarray(a, BF) for a in (tri, blk, exb, exa))
    strict, eye = jnp.asarray(strict, F32), jnp.asarray(eye, F32)
    hm, hv, bd, ones_bd = _head_consts(BRANCH_W)
    hm, hv, bd, ones_bd = jnp.asarray(hm, F32), jnp.asarray(hv, F32), jnp.asarray(bd, F32), jnp.asarray(ones_bd, BF)
    has_state = s0 is not None
    n_rows = N_PAIR * t_len
    consts = (conv_w, alog_row, dtb_row, nw, tri, strict, eye, blk, exb, exa, hm, hv, bd, ones_bd)
    in_specs = [pl.BlockSpec((n_rows, 768), lambda b: (row_blk0 + b, 1)),
                pl.BlockSpec((n_rows, 256), lambda b: (row_blk0 + b, 6)),
                pl.BlockSpec((n_rows, 128), lambda b: (row_blk0 + b, COL_SMALL_BLOCK))]
    in_specs += [_const_spec(a) for a in consts]
    args = [proj, proj, proj, *consts]
    if has_state:
        in_specs.append(pl.BlockSpec((N_PAIR, 2, BRANCH_W, BRANCH_W), lambda b: (b, 0, 0, 0)))
        args.append(s0)
    out_shape = [jax.ShapeDtypeStruct((n_seq * t_len, BRANCH_W), F32)]
    out_specs = [pl.BlockSpec((n_rows, BRANCH_W), lambda b: (b, 0))]
    if not has_state:
        out_shape.append(jax.ShapeDtypeStruct((n_seq, 2, N_HEADS, DN_DK, HEAD_V), F32))
        out_specs.append(pl.BlockSpec((N_PAIR, 2, N_HEADS, DN_DK, HEAD_V), lambda b: (b, 0, 0, 0, 0)))
    return pl.pallas_call(
        functools.partial(_dn_kernel, t_len=t_len, has_state=has_state),
        grid=(n_seq // N_PAIR,),
        in_specs=in_specs,
        out_specs=out_specs,
        out_shape=out_shape,
        scratch_shapes=[pltpu.VMEM((n_rows, BRANCH_W), F32)] * 3
                       + [pltpu.VMEM((n_rows, 2 * BRANCH_W), F32)] * 2
                       + [pltpu.VMEM((n_rows, BRANCH_W), F32),
                          pltpu.VMEM((N_COMBO * CHUNK, CHUNK), F32),
                          pltpu.VMEM((N_COMBO * CHUNK, 128), F32),
                          pltpu.VMEM((N_COMBO * CHUNK, 128), F32)],
        compiler_params=_cparams(("arbitrary",)),
        name="deltanet_mixer",
    )(*args)


def _rope(x, cos, sin):
    lane = lax.broadcasted_iota(jnp.int32, x.shape, 1) % 16
    n = x.shape[1]
    xrot = jnp.where(lane < 8, -pltpu.roll(x, n - 8, 1), pltpu.roll(x, 8, 1))
    return x * cos + xrot * sin


def _att_kernel(*refs, t_len, lat, lam_init):
    blk_ref, lam_ref, nw_ref, qm_ref, hv_ref, ones_ref = refs[:6]
    pos = 6
    if lat:
        cos_ref, sin_ref, ck_ref, cv_ref = refs[pos:pos + 4]
        pos += 4
    out_ref = refs[pos]
    pos += 1
    if not lat:
        nk_ref, nv_ref = refs[pos:pos + 2]
        pos += 2
    if lat:
        q_s, k_s, v_s = refs[pos:pos + 3]

    lv = lam_ref[...]
    lam = (jnp.exp(jnp.sum(lv[0:1] * lv[1:2], axis=1, keepdims=True))
           - jnp.exp(jnp.sum(lv[2:3] * lv[3:4], axis=1, keepdims=True)) + lam_init)
    q = blk_ref[:, 0:256]
    k = blk_ref[:, 256:512]
    v = blk_ref[:, 512:768]
    if lat:
        cos, sin = cos_ref[...], sin_ref[...]
        q_s[...] = _rope(q, cos, sin)
        k_s[0:PAST_LEN, :] = ck_ref[0]
        k_s[PAST_LEN:PAST_LEN + t_len, :] = _rope(k, cos, sin)
        v_s[0:PAST_LEN, :] = cv_ref[0]
        v_s[PAST_LEN:PAST_LEN + t_len, :] = v
        keys = k_s[...].astype(BF)
        vals = v_s[...].astype(BF)
    else:
        keys = k.astype(BF)
        vals = v.astype(BF)
        for h in range(N_HEADS):
            nk_ref[0, h] = _head_lanes(k, h)
            nv_ref[0, h] = _head_lanes(v, h)
    tq = TQ_ATT
    scale = DF_DH ** -0.5
    for qi in range(t_len // tq):
        qt = q_s[qi * tq:(qi + 1) * tq, :] if lat else q[qi * tq:(qi + 1) * tq, :]
        qs = jnp.concatenate([qt * qm_ref[r:r + 1, :] for r in range(2 * N_HEADS)], axis=0)
        s = _mm_nt(qs, keys) * scale
        s = jnp.exp(s - jnp.max(s, axis=1, keepdims=True))
        p = s / jnp.sum(s, axis=1, keepdims=True)
        a = jnp.concatenate(
            [p[(2 * h) * tq:(2 * h + 1) * tq] - lam * p[(2 * h + 1) * tq:(2 * h + 2) * tq]
             for h in range(N_HEADS)], axis=0)
        o = _head_diag(_mm(a, vals), hv_ref, tq)
        ms = _sel_r(o * o, ones_ref[...], 2) * (1.0 / HEAD_V)
        out_ref[qi * tq:(qi + 1) * tq, :] = o * lax.rsqrt(ms + EPS) * nw_ref[...] * (1.0 - lam_init)


def _att_call(proj, lam_p, nw, cache_k, cache_v, rope, t_len, n_seq, row_blk0, lam_init):
    qm = jnp.asarray(_att_consts(), F32)
    _, hv, _, ones_bd = _head_consts(BRANCH_W)
    hv, ones_bd = jnp.asarray(hv, F32), jnp.asarray(ones_bd, BF)
    lat = cache_k is not None
    consts = (lam_p, nw, qm, hv, ones_bd)
    in_specs = [pl.BlockSpec((t_len, 768), lambda b: (row_blk0 + b, 4))]
    in_specs += [_const_spec(a) for a in consts]
    args = [proj, *consts]
    out_shape = [jax.ShapeDtypeStruct((n_seq * t_len, BRANCH_W), F32)]
    out_specs = [pl.BlockSpec((t_len, BRANCH_W), lambda b: (b, 0))]
    scratch = []
    if lat:
        cos, sin = rope
        in_specs += [_const_spec(cos), _const_spec(sin),
                     pl.BlockSpec((1, PAST_LEN, BRANCH_W), lambda b: (b, 0, 0)),
                     pl.BlockSpec((1, PAST_LEN, BRANCH_W), lambda b: (b, 0, 0))]
        args += [cos, sin, cache_k, cache_v]
        scratch = [pltpu.VMEM((t_len, BRANCH_W), F32),
                   pltpu.VMEM((PAST_LEN + t_len, BRANCH_W), F32),
                   pltpu.VMEM((PAST_LEN + t_len, BRANCH_W), F32)]
    else:
        for _ in range(2):
            out_shape.append(jax.ShapeDtypeStruct((n_seq, N_HEADS, t_len, HEAD_V), F32))
            out_specs.append(pl.BlockSpec((1, N_HEADS, t_len, HEAD_V), lambda b: (b, 0, 0, 0)))
    return pl.pallas_call(
        functools.partial(_att_kernel, t_len=t_len, lat=lat, lam_init=lam_init),
        grid=(n_seq,),
        in_specs=in_specs,
        out_specs=out_specs,
        out_shape=out_shape,
        scratch_shapes=scratch,
        compiler_params=_cparams(("arbitrary",)),
        name="diff_attention",
    )(*args)


def _block_diag_states(st, dk, transposed):
    eye = jnp.eye(N_HEADS, dtype=st.dtype)
    b = st.shape[0]
    if transposed:
        return jnp.einsum('bnhde,hg->bnhegd', st, eye).reshape(b, 2, N_HEADS * HEAD_V, N_HEADS * dk)
    return jnp.einsum('bnhde,hg->bnhdge', st, eye).reshape(b, 2, N_HEADS * dk, N_HEADS * HEAD_V)


def _in_perm():
    offs = np.concatenate([[0], np.cumsum(IN_ORIG)])
    seg = lambda a, b: np.arange(offs[a], offs[b])
    return np.concatenate([seg(0, 4), seg(5, 6), seg(8, 9), seg(9, 16), seg(4, 5), seg(6, 8)])


def kernel(x_prompt, x_sample, cache_diff_k, cache_diff_v, state_gla, state_dn, state_hgrn, c, c_ctx,
           norm_w, w_mod, b_mod, ffn1_in, ffn1_down, ffn2_in, ffn2_down, w_in, gla_w2, gla_b, gla_norm,
           dn_conv, dn_a_log, dn_dt_bias, dn_norm, hg_lb_logits, hg_norm, diff_lambda, diff_norm,
           w_branch, w_mgate, w_out, final_norm):
    x = jnp.concatenate([x_prompt.reshape(N_CTX_TOK, D_MODEL), x_sample.reshape(-1, D_MODEL)], axis=0)
    c_rows = jnp.concatenate([c_ctx[None, :], c, jnp.zeros((8 - 1 - N_LAT_SEQ, D_MODEL), F32)], axis=0)
    mod = _mod_call(c_rows, w_mod, b_mod).reshape(DEPTH, 8, N_MOD, D_MODEL)
    rope = _rope_tables()
    perm = _in_perm()
    lb_logits = hg_lb_logits.reshape(DEPTH, 2 * HG_W)
    lat_blk = N_CTX_TOK // T_LAT
    tile4 = lambda a: jnp.tile(a, N_HEADS)[None, :]
    fin = final_norm[None, :]
    new_k, new_v, new_gla, new_dn, new_hg = [], [], [], [], []
    for l in range(DEPTH):
        mod_l = mod[l]
        lam_init = 0.8 - 0.6 * math.exp(-0.3 * l)
        w_in_p = jnp.pad(w_in[l][:, perm], ((0, 0), (0, N_IN_PAD - N_IN))).astype(BF)
        w2bd = jnp.zeros((128, 2 * GLA_KW), F32)
        w2bd = w2bd.at[0:GLA_LOWRANK, 0:GLA_KW].set(gla_w2[l, 0])
        w2bd = w2bd.at[GLA_LOWRANK:2 * GLA_LOWRANK, GLA_KW:].set(gla_w2[l, 1]).astype(BF)
        gbias = gla_b[l].reshape(1, 2 * GLA_KW)
        alog_row = jnp.zeros((1, 128), F32).at[0, SMALL_DNA:SMALL_DNA + 8].set(dn_a_log[l].reshape(-1))
        dtb_row = jnp.zeros((1, 128), F32).at[0, SMALL_DNA:SMALL_DNA + 8].set(dn_dt_bias[l].reshape(-1))

        x = _ffn_call(x, mod_l, norm_w[l, 0][None, :], ffn1_in[l].astype(BF), ffn1_down[l].astype(BF),
                      fin, 0, False)
        proj = _proj_call(x, mod_l, norm_w[l, 1][None, :], w_in_p)

        a_c, st_a = _gla_call(proj, w2bd, gbias, tile4(gla_norm[l]), None, T_CTX, N_CTX_SEQ, 0)
        b_c, st_b = _dn_call(proj, dn_conv[l], alog_row, dtb_row, tile4(dn_norm[l]), None,
                             T_CTX, N_CTX_SEQ, 0)
        c_c, st_c = _hgrn_call(proj, lb_logits, tile4(hg_norm[l]), None, T_CTX, N_CTX_SEQ, 0, l)
        d_c, k_l, v_l = _att_call(proj, diff_lambda[l], tile4(diff_norm[l]), None, None, None,
                                  T_CTX, N_CTX_SEQ, 0, lam_init)
        ck = cache_diff_k[:, l].transpose(0, 2, 1, 3).reshape(N_LAT_SEQ, PAST_LEN, BRANCH_W)
        cv = cache_diff_v[:, l].transpose(0, 2, 1, 3).reshape(N_LAT_SEQ, PAST_LEN, BRANCH_W)
        (a_l,) = _gla_call(proj, w2bd, gbias, tile4(gla_norm[l]),
                           _block_diag_states(state_gla[:, l], GLA_DK, True), T_LAT, N_LAT_SEQ, lat_blk)
        (b_l,) = _dn_call(proj, dn_conv[l], alog_row, dtb_row, tile4(dn_norm[l]),
                          _block_diag_states(state_dn[:, l], DN_DK, False), T_LAT, N_LAT_SEQ,
                          N_CTX_TOK // (N_PAIR * T_LAT))
        (c_l,) = _hgrn_call(proj, lb_logits, tile4(hg_norm[l]),
                            _block_diag_states(state_hgrn[:, l], HG_DK, True), T_LAT, N_LAT_SEQ, lat_blk, l)
        (d_l,) = _att_call(proj, diff_lambda[l], tile4(diff_norm[l]), ck, cv, rope,
                           T_LAT, N_LAT_SEQ, lat_blk, lam_init)

        x = _merge_call(x, mod_l, norm_w[l, 1][None, :], (a_c, b_c, c_c, d_c), (a_l, b_l, c_l, d_l),
                        w_mgate[l].astype(BF), w_branch[l].astype(BF), w_out[l].astype(BF))
        x = _ffn_call(x, mod_l, norm_w[l, 2][None, :], ffn2_in[l].astype(BF), ffn2_down[l].astype(BF),
                      fin, 2, l == DEPTH - 1)
        new_k.append(k_l)
        new_v.append(v_l)
        new_gla.append(st_a)
        new_dn.append(st_b)
        new_hg.append(st_c)
    y_prompt = x[:N_CTX_TOK].reshape(N_CTX_SEQ, T_CTX, D_MODEL)
    y_sample = x[N_CTX_TOK:].reshape(N_LAT_SEQ, T_LAT, D_MODEL)
    return (y_prompt, y_sample, jnp.stack(new_k, axis=1), jnp.stack(new_v, axis=1),
            jnp.stack(new_gla, axis=1), jnp.stack(new_dn, axis=1), jnp.stack(new_hg, axis=1))
```

```python
import functools
import math

import numpy as np
import jax
import jax.numpy as jnp
from jax import lax
from jax.experimental import pallas as pl
from jax.experimental.pallas import tpu as pltpu

F32 = jnp.float32
BF = jnp.bfloat16

D_MODEL = 1024
N_CTX_SEQ = 32
T_CTX = 256
DEPTH = 2
N_LAT_SEQ = 2
T_LAT = 1024
PAST_LEN = 512
GRID_W = 64
N_HEADS = 4
BRANCH_W = 256
HEAD_V = 64
GLA_DK = 32
GLA_KW = 128
GLA_LOWRANK = 16
GLA_TAU = 16.0
DN_DK = 64
HG_DK = 64
HG_W = 256
DF_DH = 32
ROPE_BASE = 10000.0
D_FF = 2816
N_MOD = 9
CHUNK = 64
EPS = 1e-6
N_CTX_TOK = N_CTX_SEQ * T_CTX
N_TOK = N_CTX_TOK + N_LAT_SEQ * T_LAT
N_LEVELS = 6

IN_ORIG = (128, 128, 256, 256, 32, 768, 8, 8, 256, 256, 512, 256, 256, 256, 256, 256)
N_IN = sum(IN_ORIG)
N_IN_PAD = 3968
COL_SMALL_BLOCK = 30
SMALL_LR = 0
SMALL_DNB = 32
SMALL_DNA = 40

VMEM_LIMIT = 56 * 1024 * 1024

TM_FFN = 512
TF_FFN = 1408
TM_PROJ = 512
TM_MERGE = 512
TN_MOD = 2304
TQ_ATT = 256


def _silu(x):
    return x * (1.0 / (1.0 + jnp.exp(-x)))


def _sigmoid(x):
    return 1.0 / (1.0 + jnp.exp(-x))


def _softplus(x):
    return jnp.maximum(x, 0.0) + jnp.log(1.0 + jnp.exp(-jnp.abs(x)))


def _log_sigmoid(x):
    return -_softplus(-x)


def _mm(a, b):
    return jnp.dot(a.astype(BF), b.astype(BF), preferred_element_type=F32)


def _mm_nt(a, b):
    return lax.dot_general(a.astype(BF), b.astype(BF), (((1,), (1,)), ((), ())),
                           preferred_element_type=F32)


def _mm_tn(a, b):
    return lax.dot_general(a.astype(BF), b.astype(BF), (((0,), (0,)), ((), ())),
                           preferred_element_type=F32)


def _split(x, n):
    parts = []
    r = x
    for i in range(n):
        p = r.astype(BF)
        parts.append(p)
        if i + 1 < n:
            r = r - p.astype(F32)
    return parts


def _sel_l(m01, x, n=3):
    out = None
    for p in _split(x, n):
        t = jnp.dot(m01, p, preferred_element_type=F32)
        out = t if out is None else out + t
    return out


def _sel_r(x, m01, n=3):
    out = None
    for p in _split(x, n):
        t = jnp.dot(p, m01, preferred_element_type=F32)
        out = t if out is None else out + t
    return out


def _sel_tn(x, m01, n=3):
    out = None
    for p in _split(x, n):
        t = lax.dot_general(p, m01, (((0,), (0,)), ((), ())), preferred_element_type=F32)
        out = t if out is None else out + t
    return out


def _rms(x, w):
    return x * lax.rsqrt(jnp.mean(x * x, axis=-1, keepdims=True) + EPS) * w


def _head_lanes(x, h):
    blk = x[:, (h // 2) * 128:(h // 2 + 1) * 128]
    if h % 2:
        blk = pltpu.roll(blk, 64, 1)
    return blk[:, :HEAD_V]


def _stack_heads(x, hm_ref):
    return jnp.concatenate([x * hm_ref[h:h + 1, :] for h in range(N_HEADS)], axis=0)


def _head_diag(o_full, hv_ref, c):
    out = None
    for h in range(N_HEADS):
        t = o_full[h * c:(h + 1) * c, :] * hv_ref[h:h + 1, :]
        out = t if out is None else out + t
    return out


def _head_norm_gate(o, gate, nw, ones_bd):
    ms = _sel_r(o * o, ones_bd, 2) * (1.0 / HEAD_V)
    return o * lax.rsqrt(ms + EPS) * nw * _silu(gate)


def _mod_row(i, tm):
    return jnp.maximum(i * tm - (N_CTX_TOK - T_LAT), 0) // T_LAT


def _cparams(sem):
    return pltpu.CompilerParams(dimension_semantics=sem, vmem_limit_bytes=VMEM_LIMIT)


@functools.lru_cache(maxsize=None)
def _scan_consts(w, reverse):
    c = CHUNK
    idx = np.arange(c)
    i = idx[:, None]
    m = idx[None, :]
    ms, pm = [], []
    s = c // 2
    while s >= 1:
        par = idx // (2 * s)
        right = (idx % (2 * s)) >= s
        same = par[:, None] == par[None, :]
        if not reverse:
            e = (par * 2 * s + s - 1)[:, None]
            mat = np.where(right[:, None], (m > e) & (m <= i), (m > i) & (m <= e))
            p = same & right[:, None] & (~right)[None, :]
        else:
            e = (par * 2 * s + s)[:, None]
            mat = np.where(right[:, None], (m >= e) & (m < i), (m >= i) & (m < e))
            p = same & (~right)[:, None] & right[None, :]
        ms.append(mat)
        pm.append(p)
        s //= 2
    if not reverse:
        ms.append(m <= i)
        ms.append(m > i)
    else:
        ms.append(m >= i)
        ms.append(m < i)
    pm.append(i == m)
    mstack = np.concatenate(ms, axis=0).astype(np.float32)
    pmask = np.stack([np.tile(p, (1, N_HEADS)) for p in pm]).astype(np.float32)
    return mstack, pmask


@functools.lru_cache(maxsize=None)
def _head_consts(w):
    dk = w // N_HEADS
    hm = np.zeros((N_HEADS, w), np.float32)
    hv = np.zeros((N_HEADS, BRANCH_W), np.float32)
    for h in range(N_HEADS):
        hm[h, h * dk:(h + 1) * dk] = 1.0
        hv[h, h * HEAD_V:(h + 1) * HEAD_V] = 1.0
    bd = hm.T @ hv
    ones_bd = hv.T @ hv
    return hm, hv, bd, ones_bd


@functools.lru_cache(maxsize=None)
def _dn_consts():
    c = CHUNK
    idx = np.arange(c)
    i = idx[:, None]
    j = idx[None, :]
    tri = np.stack([(j <= i), (j >= i)]).astype(np.float32)
    strict = np.stack([np.tile(j < i, (1, N_HEADS)), np.tile(j > i, (1, N_HEADS))]).astype(np.float32)
    eye = np.tile(np.eye(c), (1, N_HEADS)).astype(np.float32)
    blk = np.ones((c, c), np.float32)
    exb = np.zeros((128, 2 * BRANCH_W), np.float32)
    exa = np.zeros((128, 2 * BRANCH_W), np.float32)
    for n in range(2):
        for h in range(N_HEADS):
            lo = n * BRANCH_W + h * HEAD_V
            exb[SMALL_DNB + n * N_HEADS + h, lo:lo + HEAD_V] = 1.0
            exa[SMALL_DNA + n * N_HEADS + h, lo:lo + HEAD_V] = 1.0
    return tri, strict, eye, blk, exb, exa


@functools.lru_cache(maxsize=None)
def _att_consts():
    qm = np.zeros((2 * N_HEADS, BRANCH_W), np.float32)
    for h in range(N_HEADS):
        for mp in range(2):
            lo = h * HEAD_V + mp * DF_DH
            qm[2 * h + mp, lo:lo + DF_DH] = 1.0
    return qm


def _rope_tables():
    rows = T_LAT // GRID_W
    row = jnp.repeat(jnp.arange(rows), GRID_W).astype(F32)
    col = jnp.tile(jnp.arange(GRID_W), rows).astype(F32)
    half = DF_DH // 2
    inv = ROPE_BASE ** (-jnp.arange(0, half, 2, dtype=F32) / half)

    def angles(pos):
        a = pos[:, None] * inv[None, :]
        return jnp.concatenate([a, a], axis=-1)

    ang = jnp.concatenate([angles(row), angles(col)], axis=-1)
    reps = BRANCH_W // DF_DH
    return jnp.tile(jnp.cos(ang), (1, reps)), jnp.tile(jnp.sin(ang), (1, reps))


def _mod_kernel(c_ref, w_ref, b_ref, o_ref):
    a = _silu(c_ref[...])
    w = w_ref[0]
    out = None
    for ap in _split(a, 2):
        for wp in _split(w, 2):
            t = jnp.dot(ap, wp, preferred_element_type=F32)
            out = t if out is None else out + t
    o_ref[0] = out + b_ref[0]


def _mod_call(c_rows, w_mod, b_mod):
    n_t = (N_MOD * D_MODEL) // TN_MOD
    return pl.pallas_call(
        _mod_kernel,
        grid=(DEPTH, n_t),
        in_specs=[pl.BlockSpec((8, D_MODEL), lambda l, j: (0, 0)),
                  pl.BlockSpec((1, D_MODEL, TN_MOD), lambda l, j: (l, 0, j)),
                  pl.BlockSpec((1, 1, TN_MOD), lambda l, j: (l, 0, j))],
        out_specs=pl.BlockSpec((1, 8, TN_MOD), lambda l, j: (l, 0, j)),
        out_shape=jax.ShapeDtypeStruct((DEPTH, 8, N_MOD * D_MODEL), F32),
        compiler_params=_cparams(("arbitrary", "arbitrary")),
        name="mod_vectors",
    )(c_rows, w_mod, b_mod.reshape(DEPTH, 1, N_MOD * D_MODEL))


def _ffn_kernel(x_ref, mod_ref, nw_ref, wg_ref, wu_ref, wd_ref, fn_ref, o_ref, h_ref, acc_ref,
                *, sub, final):
    f = pl.program_id(1)

    @pl.when(f == 0)
    def _():
        x = x_ref[...]
        sh = mod_ref[0, 3 * sub:3 * sub + 1, :]
        sc = mod_ref[0, 3 * sub + 1:3 * sub + 2, :]
        h_ref[...] = (_rms(x, nw_ref[...]) * (1.0 + sc) + sh).astype(BF)
        acc_ref[...] = jnp.zeros_like(acc_ref)

    h = h_ref[...]
    g = jnp.dot(h, wg_ref[...], preferred_element_type=F32)
    u = jnp.dot(h, wu_ref[...], preferred_element_type=F32)
    acc_ref[...] += jnp.dot((_silu(g) * u).astype(BF), wd_ref[...], preferred_element_type=F32)

    @pl.when(f == pl.num_programs(1) - 1)
    def _():
        ga = mod_ref[0, 3 * sub + 2:3 * sub + 3, :]
        y = x_ref[...] + 0.5 * ga * acc_ref[...]
        if final:
            y = _rms(y, fn_ref[...])
        o_ref[...] = y


def _ffn_call(x, mod_l, nw, w_in, w_down, final_w, sub, final):
    n_f = D_FF // TF_FFN
    tm = TM_FFN
    return pl.pallas_call(
        functools.partial(_ffn_kernel, sub=sub, final=final),
        grid=(N_TOK // tm, n_f),
        in_specs=[pl.BlockSpec((tm, D_MODEL), lambda i, f: (i, 0)),
                  pl.BlockSpec((1, N_MOD, D_MODEL), lambda i, f: (_mod_row(i, tm), 0, 0)),
                  pl.BlockSpec((1, D_MODEL), lambda i, f: (0, 0)),
                  pl.BlockSpec((D_MODEL, TF_FFN), lambda i, f: (0, f)),
                  pl.BlockSpec((D_MODEL, TF_FFN), lambda i, f: (0, n_f + f)),
                  pl.BlockSpec((TF_FFN, D_MODEL), lambda i, f: (f, 0)),
                  pl.BlockSpec((1, D_MODEL), lambda i, f: (0, 0))],
        out_specs=pl.BlockSpec((tm, D_MODEL), lambda i, f: (i, 0)),
        out_shape=jax.ShapeDtypeStruct((N_TOK, D_MODEL), F32),
        scratch_shapes=[pltpu.VMEM((tm, D_MODEL), BF), pltpu.VMEM((tm, D_MODEL), F32)],
        compiler_params=_cparams(("arbitrary", "arbitrary")),
        name="swiglu_half_step",
    )(x, mod_l, nw, w_in, w_in, w_down, final_w)


def _proj_kernel(x_ref, mod_ref, nw_ref, w_ref, o_ref):
    sh = mod_ref[0, 3:4, :]
    sc = mod_ref[0, 4:5, :]
    h = (_rms(x_ref[...], nw_ref[...]) * (1.0 + sc) + sh).astype(BF)
    o_ref[...] = jnp.dot(h, w_ref[...], preferred_element_type=F32)


def _proj_call(x, mod_l, nw, w_in_p):
    tm = TM_PROJ
    return pl.pallas_call(
        _proj_kernel,
        grid=(N_TOK // tm,),
        in_specs=[pl.BlockSpec((tm, D_MODEL), lambda i: (i, 0)),
                  pl.BlockSpec((1, N_MOD, D_MODEL), lambda i: (_mod_row(i, tm), 0, 0)),
                  pl.BlockSpec((1, D_MODEL), lambda i: (0, 0)),
                  pl.BlockSpec((D_MODEL, N_IN_PAD), lambda i: (0, 0))],
        out_specs=pl.BlockSpec((tm, N_IN_PAD), lambda i: (i, 0)),
        out_shape=jax.ShapeDtypeStruct((N_TOK, N_IN_PAD), F32),
        compiler_params=_cparams(("arbitrary",)),
        name="mixer_in_proj",
    )(x, mod_l, nw, w_in_p)


def _merge_kernel(x_ref, mod_ref, nw_ref, *rest):
    ctx_refs, lat_refs = rest[0:4], rest[4:8]
    wg_ref, wb_ref, wo_ref, o_ref = rest[8:12]
    x = x_ref[...]
    sh = mod_ref[0, 3:4, :]
    sc = mod_ref[0, 4:5, :]
    ga = mod_ref[0, 5:6, :]
    h = (_rms(x, nw_ref[...]) * (1.0 + sc) + sh).astype(BF)
    is_lat = pl.program_id(0) >= N_CTX_TOK // TM_MERGE
    mixed = None
    for n in range(4):
        gate = _sigmoid(jnp.dot(h, wg_ref[:, n * D_MODEL:(n + 1) * D_MODEL], preferred_element_type=F32))
        br = jnp.where(is_lat, lat_refs[n][...], ctx_refs[n][...])
        up = jnp.dot(br.astype(BF), wb_ref[n], preferred_element_type=F32)
        mixed = gate * up if mixed is None else mixed + gate * up
    out = jnp.dot(mixed.astype(BF), wo_ref[...], preferred_element_type=F32)
    o_ref[...] = x + ga * out


def _merge_call(x, mod_l, nw, ctx_branches, lat_branches, w_mgate, w_branch, w_out):
    tm = TM_MERGE
    n_ctx = N_CTX_TOK // tm
    cspec = pl.BlockSpec((tm, BRANCH_W), lambda i: (jnp.minimum(i, n_ctx - 1), 0))
    lspec = pl.BlockSpec((tm, BRANCH_W), lambda i: (jnp.maximum(i - n_ctx, 0), 0))
    return pl.pallas_call(
        _merge_kernel,
        grid=(N_TOK // tm,),
        in_specs=[pl.BlockSpec((tm, D_MODEL), lambda i: (i, 0)),
                  pl.BlockSpec((1, N_MOD, D_MODEL), lambda i: (_mod_row(i, tm), 0, 0)),
                  pl.BlockSpec((1, D_MODEL), lambda i: (0, 0)),
                  cspec, cspec, cspec, cspec, lspec, lspec, lspec, lspec,
                  pl.BlockSpec((D_MODEL, 4 * D_MODEL), lambda i: (0, 0)),
                  pl.BlockSpec((4, BRANCH_W, D_MODEL), lambda i: (0, 0, 0)),
                  pl.BlockSpec((D_MODEL, D_MODEL), lambda i: (0, 0))],
        out_specs=pl.BlockSpec((tm, D_MODEL), lambda i: (i, 0)),
        out_shape=jax.ShapeDtypeStruct((N_TOK, D_MODEL), F32),
        compiler_params=_cparams(("arbitrary",)),
        name="gated_merge",
    )(x, mod_l, nw, *ctx_branches, *lat_branches, w_mgate, w_branch, w_out)


def _gated_chunk(qc, kc, vc, gc, st, ms_ref, pm_ref, hm_ref, hv_ref, bdt_ref, reverse):
    c = CHUNK
    e = jnp.exp(_sel_l(ms_ref[...], gc, 2))
    kst = _stack_heads(kc, hm_ref)
    a = pm_ref[N_LEVELS] * _mm_nt(qc, kst)
    for lv in range(N_LEVELS):
        el = e[lv * c:(lv + 1) * c]
        a = a + pm_ref[lv] * _mm_nt(qc * el, kst * jnp.concatenate([el] * N_HEADS, axis=0))
    e_b = e[N_LEVELS * c:(N_LEVELS + 1) * c]
    e_t = e[(N_LEVELS + 1) * c:(N_LEVELS + 2) * c]
    o = _mm(a, _stack_heads(vc, hv_ref)) + _mm_nt(qc * e_b, st)
    dec = e_b[0:1, :] if reverse else e_b[c - 1:c, :]
    st_new = st * dec + bdt_ref[...] * _mm_tn(vc, kc * e_t)
    return o, st_new


def _gated_scan(q_ref, kf_ref, kb_ref, v_ref, gf_ref, gb_ref, of_ref, ob_ref, s0f, s0b, cf, cb, n_chunks):
    def body(t, carry):
        sf, sb = carry
        rf = pl.ds(pl.multiple_of(t * CHUNK, CHUNK), CHUNK)
        rb = pl.ds(pl.multiple_of((n_chunks - 1 - t) * CHUNK, CHUNK), CHUNK)
        o_f, sf = _gated_chunk(q_ref[rf, :], kf_ref[rf, :], v_ref[rf, :], gf_ref[rf, :], sf, *cf, False)
        o_b, sb = _gated_chunk(q_ref[rb, :], kb_ref[rb, :], v_ref[rb, :], gb_ref[rb, :], sb, *cb, True)
        of_ref[rf, :] = o_f
        ob_ref[rb, :] = o_b
        return sf, sb

    return lax.fori_loop(0, n_chunks, body, (s0f, s0b))


def _write_states(st_ref, st_f, st_b, dk, eye_ref):
    for d, st in enumerate((st_f, st_b)):
        for h in range(N_HEADS):
            tr = _sel_tn(st[h * HEAD_V:(h + 1) * HEAD_V, :], eye_ref[...])
            st_ref[0, d, h] = tr[h * dk:(h + 1) * dk, :]


def _gla_kernel(*refs, t_len, has_state):
    (blk_ref, small_ref, w2_ref, gb_ref, nw_ref,
     msf_ref, pmf_ref, msb_ref, pmb_ref, hm_ref, hv_ref, bdt_ref, ones_ref, eye_ref) = refs[:14]
    pos = 14
    if has_state:
        s0_ref = refs[pos]
        pos += 1
    out_ref = refs[pos]
    pos += 1
    if not has_state:
        st_ref = refs[pos]
        pos += 1
    q_s, gf_s, gb_s, of_s, ob_s = refs[pos:pos + 5]

    q_s[...] = blk_ref[:, 0:128] * (GLA_DK ** -0.5)
    z = _mm(small_ref[...], w2_ref[...]) + gb_ref[...]
    g = _log_sigmoid(z) * (1.0 / GLA_TAU)
    gf_s[...] = g[:, 0:128]
    gb_s[...] = g[:, 128:256]
    k_ref = blk_ref.at[:, 128:256]
    v_ref = blk_ref.at[:, 256:512]
    n_chunks = t_len // CHUNK
    if has_state:
        s0f, s0b = s0_ref[0, 0], s0_ref[0, 1]
    else:
        s0f = s0b = jnp.zeros((BRANCH_W, GLA_KW), F32)
    cf = (msf_ref, pmf_ref, hm_ref, hv_ref, bdt_ref)
    cb = (msb_ref, pmb_ref, hm_ref, hv_ref, bdt_ref)
    s_f, s_b = _gated_scan(q_s, k_ref, k_ref, v_ref, gf_s, gb_s, of_s, ob_s, s0f, s0b, cf, cb, n_chunks)
    out_ref[...] = _head_norm_gate(of_s[...] + ob_s[...], blk_ref[:, 512:768], nw_ref[...], ones_ref[...])
    if not has_state:
        _write_states(st_ref, s_f, s_b, GLA_DK, eye_ref)


def _const_spec(a):
    nd = a.ndim
    return pl.BlockSpec(a.shape, lambda b, _n=nd: (0,) * _n)


def _scan_const_arrays(w):
    msf, pmf = _scan_consts(w, False)
    msb, pmb = _scan_consts(w, True)
    hm, hv, bd, ones_bd = _head_consts(w)
    return (jnp.asarray(msf, BF), jnp.asarray(pmf, F32), jnp.asarray(msb, BF), jnp.asarray(pmb, F32),
            jnp.asarray(hm, F32), jnp.asarray(hv, F32), jnp.asarray(bd.T, F32), jnp.asarray(ones_bd, BF),
            jnp.eye(HEAD_V, dtype=BF))


def _gla_call(proj, w2bd, gbias, nw, s0, t_len, n_seq, row_blk0):
    msf, pmf, msb, pmb, hm, hv, bdt, ones_bd, eye = _scan_const_arrays(GLA_KW)
    has_state = s0 is not None
    consts = (w2bd, gbias, nw, msf, pmf, msb, pmb, hm, hv, bdt, ones_bd, eye)
    in_specs = [pl.BlockSpec((t_len, 768), lambda b: (row_blk0 + b, 0)),
                pl.BlockSpec((t_len, 128), lambda b: (row_blk0 + b, COL_SMALL_BLOCK))]
    in_specs += [_const_spec(a) for a in consts]
    args = [proj, proj, *consts]
    if has_state:
        in_specs.append(pl.BlockSpec((1, 2, BRANCH_W, GLA_KW), lambda b: (b, 0, 0, 0)))
        args.append(s0)
    out_shape = [jax.ShapeDtypeStruct((n_seq * t_len, BRANCH_W), F32)]
    out_specs = [pl.BlockSpec((t_len, BRANCH_W), lambda b: (b, 0))]
    if not has_state:
        out_shape.append(jax.ShapeDtypeStruct((n_seq, 2, N_HEADS, GLA_DK, HEAD_V), F32))
        out_specs.append(pl.BlockSpec((1, 2, N_HEADS, GLA_DK, HEAD_V), lambda b: (b, 0, 0, 0, 0)))
    return pl.pallas_call(
        functools.partial(_gla_kernel, t_len=t_len, has_state=has_state),
        grid=(n_seq,),
        in_specs=in_specs,
        out_specs=out_specs,
        out_shape=out_shape,
        scratch_shapes=[pltpu.VMEM((t_len, GLA_KW), F32)] * 3 + [pltpu.VMEM((t_len, BRANCH_W), F32)] * 2,
        compiler_params=_cparams(("arbitrary",)),
        name="gla_mixer",
    )(*args)


def _hgrn_kernel(*refs, t_len, has_state, layer):
    (q_ref, f_ref, v_ref, gate_ref, lbl_ref, nw_ref,
     msf_ref, pmf_ref, msb_ref, pmb_ref, hm_ref, hv_ref, bdt_ref, ones_ref, eye_ref) = refs[:15]
    pos = 15
    if has_state:
        s0_ref = refs[pos]
        pos += 1
    out_ref = refs[pos]
    pos += 1
    if not has_state:
        st_ref = refs[pos]
        pos += 1
    q_s, kf_s, kb_s, gf_s, gb_s, of_s, ob_s = refs[pos:pos + 7]

    lg = lbl_ref[...]
    mx = jnp.max(lg, axis=0, keepdims=True)
    ex = jnp.exp(lg - mx)
    p = ex / jnp.sum(ex, axis=0, keepdims=True)
    lb = jnp.sum(p[0:layer + 1], axis=0, keepdims=True) - p[0:1]

    q_s[...] = _silu(q_ref[...]) * (HG_DK ** -0.5)
    f = lb + (1.0 - lb) * _sigmoid(f_ref[...])
    kf_s[...] = 1.0 - f[:, 0:HG_W]
    kb_s[...] = 1.0 - f[:, HG_W:2 * HG_W]
    lf = jnp.log(f)
    gf_s[...] = lf[:, 0:HG_W]
    gb_s[...] = lf[:, HG_W:2 * HG_W]
    n_chunks = t_len // CHUNK
    if has_state:
        s0f, s0b = s0_ref[0, 0], s0_ref[0, 1]
    else:
        s0f = s0b = jnp.zeros((BRANCH_W, HG_W), F32)
    cf = (msf_ref, pmf_ref, hm_ref, hv_ref, bdt_ref)
    cb = (msb_ref, pmb_ref, hm_ref, hv_ref, bdt_ref)
    s_f, s_b = _gated_scan(q_s, kf_s, kb_s, v_ref, gf_s, gb_s, of_s, ob_s, s0f, s0b, cf, cb, n_chunks)
    out_ref[...] = _head_norm_gate(of_s[...] + ob_s[...], gate_ref[...], nw_ref[...], ones_ref[...])
    if not has_state:
        _write_states(st_ref, s_f, s_b, HG_DK, eye_ref)


def _hgrn_call(proj, lb_logits, nw, s0, t_len, n_seq, row_blk0, layer):
    msf, pmf, msb, pmb, hm, hv, bdt, ones_bd, eye = _scan_const_arrays(HG_W)
    has_state = s0 is not None
    consts = (lb_logits, nw, msf, pmf, msb, pmb, hm, hv, bdt, ones_bd, eye)
    in_specs = [pl.BlockSpec((t_len, 256), lambda b: (row_blk0 + b, 7)),
                pl.BlockSpec((t_len, 512), lambda b: (row_blk0 + b, 4)),
                pl.BlockSpec((t_len, 256), lambda b: (row_blk0 + b, 10)),
                pl.BlockSpec((t_len, 256), lambda b: (row_blk0 + b, 11))]
    in_specs += [_const_spec(a) for a in consts]
    args = [proj, proj, proj, proj, *consts]
    if has_state:
        in_specs.append(pl.BlockSpec((1, 2, BRANCH_W, HG_W), lambda b: (b, 0, 0, 0)))
        args.append(s0)
    out_shape = [jax.ShapeDtypeStruct((n_seq * t_len, BRANCH_W), F32)]
    out_specs = [pl.BlockSpec((t_len, BRANCH_W), lambda b: (b, 0))]
    if not has_state:
        out_shape.append(jax.ShapeDtypeStruct((n_seq, 2, N_HEADS, HG_DK, HEAD_V), F32))
        out_specs.append(pl.BlockSpec((1, 2, N_HEADS, HG_DK, HEAD_V), lambda b: (b, 0, 0, 0, 0)))
    return pl.pallas_call(
        functools.partial(_hgrn_kernel, t_len=t_len, has_state=has_state, layer=layer),
        grid=(n_seq,),
        in_specs=in_specs,
        out_specs=out_specs,
        out_shape=out_shape,
        scratch_shapes=[pltpu.VMEM((t_len, HG_W), F32)] * 5 + [pltpu.VMEM((t_len, BRANCH_W), F32)] * 2,
        compiler_params=_cparams(("arbitrary",)),
        name="hgrn2_mixer",
    )(*args)


N_PAIR = 2
N_SYS = 128
SOLVE_JB = 8


def _dn_solve_kernel(lt_ref, rt_ref, x_ref):
    c = CHUNK
    x_ref[...] = jnp.zeros_like(x_ref)

    def outer(i, carry):
        def inner(jb, acc):
            j0 = pl.multiple_of(jb * SOLVE_JB, SOLVE_JB)
            for r in range(SOLVE_JB):
                acc = acc - lt_ref[0, i, pl.ds(j0 + r, 1), :] * x_ref[0, j0 + r]
            return acc

        n_blk = lax.shift_right_logical(i, 3) + 1
        x_ref[0, i] = lax.fori_loop(0, n_blk, inner, rt_ref[0, i])
        return carry

    lax.fori_loop(0, c, outer, 0)


def _dn_solve_call(lt, rt):
    n_grp = lt.shape[0]
    return pl.pallas_call(
        _dn_solve_kernel,
        grid=(n_grp,),
        in_specs=[pl.BlockSpec((1, CHUNK, CHUNK, N_SYS), lambda g: (g, 0, 0, 0)),
                  pl.BlockSpec((1, CHUNK, 2 * HEAD_V, N_SYS), lambda g: (g, 0, 0, 0))],
        out_specs=pl.BlockSpec((1, CHUNK, 2 * HEAD_V, N_SYS), lambda g: (g, 0, 0, 0)),
        out_shape=jax.ShapeDtypeStruct((n_grp, CHUNK, 2 * HEAD_V, N_SYS), F32),
        compiler_params=_cparams(("arbitrary",)),
        name="deltanet_solve",
    )(lt, rt)


def _dn_to_systems(lw, vb, kb):
    n_chunk = lw.shape[1] // CHUNK
    shp = (2, n_chunk, CHUNK, N_HEADS, HEAD_V)
    l = lw.reshape(shp).transpose(0, 1, 3, 2, 4)
    r = jnp.stack([vb.reshape(shp), kb.reshape(shp)], axis=4)
    r = r.transpose(0, 1, 3, 2, 4, 5).reshape(2, n_chunk, N_HEADS, CHUNK, 2 * HEAD_V)
    l = jnp.stack([l[0], l[1][..., ::-1, ::-1]])
    r = jnp.stack([r[0], r[1][..., ::-1, :]])
    n_grp = 2 * n_chunk * N_HEADS // N_SYS
    lt = l.reshape(n_grp, N_SYS, CHUNK, CHUNK).transpose(0, 2, 3, 1)
    rt = r.reshape(n_grp, N_SYS, CHUNK, 2 * HEAD_V).transpose(0, 2, 3, 1)
    return lt, rt


def _dn_from_systems(x):
    n_grp = x.shape[0]
    n_chunk = n_grp * N_SYS // (2 * N_HEADS)
    s = x.transpose(0, 3, 1, 2).reshape(2, n_chunk, N_HEADS, CHUNK, 2, HEAD_V)
    s = jnp.stack([s[0], s[1][:, :, ::-1]])
    s = s.transpose(4, 0, 1, 3, 2, 5).reshape(2, 2, n_chunk * CHUNK, BRANCH_W)
    return s[0], s[1]


def _dn_build_kernel(x_ref, small_ref, cw_ref, alog_ref, dtb_ref,
                     tri_ref, strict_ref, eye_ref, blk_ref, exb_ref, exa_ref, hm_ref, ones_ref,
                     q_ref, k_ref, d_ref, l_ref, vb_ref, kb_ref, qkd_ref, v_ref, be_s, ge_s, *, t_len):
    n_rows = N_PAIR * t_len

    x = x_ref[...]
    row = lax.broadcasted_iota(jnp.int32, (n_rows, 1), 0) % t_len
    x_prev = jnp.where(row == 0, 0.0, pltpu.roll(x, 1, 0))
    x_next = jnp.where(row == t_len - 1, 0.0, pltpu.roll(x, n_rows - 1, 0))
    y = _silu(x_prev * cw_ref[0:1, :] + x * cw_ref[1:2, :] + x_next * cw_ref[2:3, :])
    cq, ck = y[:, 0:256], y[:, 256:512]
    v_ref[...] = y[:, 512:768]
    q_ref[...] = cq * lax.rsqrt(_sel_r(cq * cq, ones_ref[...]) + EPS) * (DN_DK ** -0.5)
    k_ref[...] = ck * lax.rsqrt(_sel_r(ck * ck, ones_ref[...]) + EPS)
    sm = small_ref[...]
    be_s[...] = _sel_r(_sigmoid(sm), exb_ref[...])
    ge_s[...] = _sel_r(-jnp.exp(alog_ref[...]) * _softplus(sm + dtb_ref[...]), exa_ref[...])

    def body(ci, carry):
        rows = pl.ds(pl.multiple_of(ci * CHUNK, CHUNK), CHUNK)
        qc, kc, vc = q_ref[rows, :], k_ref[rows, :], v_ref[rows, :]
        kst = _stack_heads(kc, hm_ref)
        kk = _mm_nt(kc, kst)
        qk = _mm_nt(qc, kst)
        for d in range(2):
            bexp = be_s[rows, d * BRANCH_W:(d + 1) * BRANCH_W]
            dexp = _sel_l(tri_ref[d], ge_s[rows, d * BRANCH_W:(d + 1) * BRANCH_W])
            drow = _sel_l(blk_ref[...], dexp * eye_ref[...])
            dec_s = jnp.exp(jnp.where(strict_ref[d] > 0.5, dexp - drow, -1e30))
            d_ref[d, rows, :] = dexp
            l_ref[d, rows, :] = bexp * kk * dec_s
            vb_ref[d, rows, :] = vc * bexp
            kb_ref[d, rows, :] = kc * bexp * jnp.exp(dexp)
            qkd_ref[d, rows, :] = (qk * (dec_s + eye_ref[...])).astype(BF)
        return carry

    lax.fori_loop(0, n_rows // CHUNK, body, 0)


def _dn_build_call(proj, conv_w, alog_row, dtb_row, t_len, n_seq, row_blk0):
    tri, strict, eye, blk, exb, exa = _dn_consts()
    tri, blk, exb, exa = (jnp.asarray(a, BF) for a in (tri, blk, exb, exa))
    strict, eye = jnp.asarray(strict, F32), jnp.asarray(eye, F32)
    hm, _, _, ones_bd = _head_consts(BRANCH_W)
    hm, ones_bd = jnp.asarray(hm, F32), jnp.asarray(ones_bd, BF)
    n_rows = N_PAIR * t_len
    n_tok = n_seq * t_len
    consts = (conv_w, alog_row, dtb_row, tri, strict, eye, blk, exb, exa, hm, ones_bd)
    in_specs = [pl.BlockSpec((n_rows, 768), lambda b: (row_blk0 + b, 1)),
                pl.BlockSpec((n_rows, 128), lambda b: (row_blk0 + b, COL_SMALL_BLOCK))]
    in_specs += [_const_spec(a) for a in consts]
    tok_spec = pl.BlockSpec((n_rows, BRANCH_W), lambda b: (b, 0))
    dir_spec = pl.BlockSpec((2, n_rows, BRANCH_W), lambda b: (0, b, 0))
    tok = jax.ShapeDtypeStruct((n_tok, BRANCH_W), F32)
    per_dir = jax.ShapeDtypeStruct((2, n_tok, BRANCH_W), F32)
    return pl.pallas_call(
        functools.partial(_dn_build_kernel, t_len=t_len),
        grid=(n_seq // N_PAIR,),
        in_specs=in_specs,
        out_specs=[tok_spec] * 2 + [dir_spec] * 5,
        out_shape=[tok] * 2 + [per_dir] * 4 + [jax.ShapeDtypeStruct((2, n_tok, BRANCH_W), BF)],
        scratch_shapes=[pltpu.VMEM((n_rows, BRANCH_W), F32)] + [pltpu.VMEM((n_rows, 2 * BRANCH_W), F32)] * 2,
        compiler_params=_cparams(("arbitrary",)),
        name="deltanet_build",
    )(proj, proj, *consts)


def _dn_scan_kernel(*refs, t_len, has_state):
    (q_ref, k_ref, d_ref, u_ref, w_ref, qkd_ref, gate_ref, nw_ref, hv_ref, bd_ref, ones_ref) = refs[:11]
    pos = 11
    if has_state:
        s0_ref = refs[pos]
        pos += 1
    out_ref = refs[pos]
    pos += 1
    if not has_state:
        st_ref = refs[pos]
        pos += 1
    of_s, ob_s = refs[pos:pos + 2]
    c = CHUNK
    n_chunks = t_len // c

    def step(s, rows, d, o_s):
        qc, kc = q_ref[rows, :], k_ref[rows, :]
        dexp = d_ref[d, rows, :]
        v_new = u_ref[d, rows, :] - _mm(w_ref[d, rows, :], s)
        o_s[rows, :] = _mm(qc * jnp.exp(dexp), s) + _mm(qkd_ref[d, rows, :], _stack_heads(v_new, hv_ref))
        dl = dexp[0:1, :] if d == 1 else dexp[c - 1:c, :]
        return s * jnp.exp(dl) + bd_ref[...] * _mm_tn(kc * jnp.exp(dl - dexp), v_new)

    def body(t, carry):
        new = []
        for sq in range(N_PAIR):
            rf = pl.ds(pl.multiple_of(sq * t_len + t * c, c), c)
            rb = pl.ds(pl.multiple_of(sq * t_len + (n_chunks - 1 - t) * c, c), c)
            new.append(step(carry[2 * sq], rf, 0, of_s))
            new.append(step(carry[2 * sq + 1], rb, 1, ob_s))
        return tuple(new)

    if has_state:
        init = tuple(s0_ref[sq, d] for sq in range(N_PAIR) for d in range(2))
    else:
        init = tuple(jnp.zeros((BRANCH_W, BRANCH_W), F32) for _ in range(2 * N_PAIR))
    finals = lax.fori_loop(0, n_chunks, body, init)
    out_ref[...] = _head_norm_gate(of_s[...] + ob_s[...], gate_ref[...], nw_ref[...], ones_ref[...])
    if not has_state:
        for sq in range(N_PAIR):
            for d in range(2):
                for h in range(N_HEADS):
                    st_ref[sq, d, h] = _head_lanes(finals[2 * sq + d][h * DN_DK:(h + 1) * DN_DK, :], h)


def _dn_scan_call(proj, q, k, dd, u, w, qkd, nw, s0, t_len, n_seq, row_blk0):
    _, hv, bd, ones_bd = _head_consts(BRANCH_W)
    hv, bd, ones_bd = jnp.asarray(hv, F32), jnp.asarray(bd, F32), jnp.asarray(ones_bd, BF)
    has_state = s0 is not None
    n_rows = N_PAIR * t_len
    consts = (nw, hv, bd, ones_bd)
    tok_spec = pl.BlockSpec((n_rows, BRANCH_W), lambda b: (b, 0))
    dir_spec = pl.BlockSpec((2, n_rows, BRANCH_W), lambda b: (0, b, 0))
    all_spec = pl.BlockSpec((2, n_rows, BRANCH_W), lambda b: (0, row_blk0 + b, 0))
    in_specs = [tok_spec, tok_spec, dir_spec, all_spec, all_spec, dir_spec,
                pl.BlockSpec((n_rows, 256), lambda b: (row_blk0 + b, 6))]
    in_specs += [_const_spec(a) for a in consts]
    args = [q, k, dd, u, w, qkd, proj, *consts]
    if has_state:
        in_specs.append(pl.BlockSpec((N_PAIR, 2, BRANCH_W, BRANCH_W), lambda b: (b, 0, 0, 0)))
        args.append(s0)
    out_shape = [jax.ShapeDtypeStruct((n_seq * t_len, BRANCH_W), F32)]
    out_specs = [pl.BlockSpec((n_rows, BRANCH_W), lambda b: (b, 0))]
    if not has_state:
        out_shape.append(jax.ShapeDtypeStruct((n_seq, 2, N_HEADS, DN_DK, HEAD_V), F32))
        out_specs.append(pl.BlockSpec((N_PAIR, 2, N_HEADS, DN_DK, HEAD_V), lambda b: (b, 0, 0, 0, 0)))
    return pl.pallas_call(
        functools.partial(_dn_scan_kernel, t_len=t_len, has_state=has_state),
        grid=(n_seq // N_PAIR,),
        in_specs=in_specs,
        out_specs=out_specs,
        out_shape=out_shape,
        scratch_shapes=[pltpu.VMEM((n_rows, BRANCH_W), F32)] * 2,
        compiler_params=_cparams(("arbitrary",)),
        name="deltanet_scan",
    )(*args)


def _rope(x, cos, sin):
    lane = lax.broadcasted_iota(jnp.int32, x.shape, 1) % 16
    n = x.shape[1]
    xrot = jnp.where(lane < 8, -pltpu.roll(x, n - 8, 1), pltpu.roll(x, 8, 1))
    return x * cos + xrot * sin


def _att_kernel(*refs, t_len, lat, lam_init):
    blk_ref, lam_ref, nw_ref, qm_ref, hv_ref, ones_ref = refs[:6]
    pos = 6
    if lat:
        cos_ref, sin_ref, ck_ref, cv_ref = refs[pos:pos + 4]
        pos += 4
    out_ref = refs[pos]
    pos += 1
    if not lat:
        nk_ref, nv_ref = refs[pos:pos + 2]
        pos += 2
    if lat:
        q_s, k_s, v_s = refs[pos:pos + 3]

    lv = lam_ref[...]
    lam = (jnp.exp(jnp.sum(lv[0:1] * lv[1:2], axis=1, keepdims=True))
           - jnp.exp(jnp.sum(lv[2:3] * lv[3:4], axis=1, keepdims=True)) + lam_init)
    q = blk_ref[:, 0:256]
    k = blk_ref[:, 256:512]
    v = blk_ref[:, 512:768]
    if lat:
        cos, sin = cos_ref[...], sin_ref[...]
        q_s[...] = _rope(q, cos, sin)
        k_s[0:PAST_LEN, :] = ck_ref[0]
        k_s[PAST_LEN:PAST_LEN + t_len, :] = _rope(k, cos, sin)
        v_s[0:PAST_LEN, :] = cv_ref[0]
        v_s[PAST_LEN:PAST_LEN + t_len, :] = v
        keys = k_s[...].astype(BF)
        vals = v_s[...].astype(BF)
    else:
        keys = k.astype(BF)
        vals = v.astype(BF)
        for h in range(N_HEADS):
            nk_ref[0, h] = _head_lanes(k, h)
            nv_ref[0, h] = _head_lanes(v, h)
    tq = TQ_ATT
    scale = DF_DH ** -0.5
    for qi in range(t_len // tq):
        qt = q_s[qi * tq:(qi + 1) * tq, :] if lat else q[qi * tq:(qi + 1) * tq, :]
        qs = jnp.concatenate([qt * qm_ref[r:r + 1, :] for r in range(2 * N_HEADS)], axis=0)
        s = _mm_nt(qs, keys) * scale
        s = jnp.exp(s - jnp.max(s, axis=1, keepdims=True))
        p = s / jnp.sum(s, axis=1, keepdims=True)
        a = jnp.concatenate(
            [p[(2 * h) * tq:(2 * h + 1) * tq] - lam * p[(2 * h + 1) * tq:(2 * h + 2) * tq]
             for h in range(N_HEADS)], axis=0)
        o = _head_diag(_mm(a, vals), hv_ref, tq)
        ms = _sel_r(o * o, ones_ref[...], 2) * (1.0 / HEAD_V)
        out_ref[qi * tq:(qi + 1) * tq, :] = o * lax.rsqrt(ms + EPS) * nw_ref[...] * (1.0 - lam_init)


def _att_call(proj, lam_p, nw, cache_k, cache_v, rope, t_len, n_seq, row_blk0, lam_init):
    qm = jnp.asarray(_att_consts(), F32)
    _, hv, _, ones_bd = _head_consts(BRANCH_W)
    hv, ones_bd = jnp.asarray(hv, F32), jnp.asarray(ones_bd, BF)
    lat = cache_k is not None
    consts = (lam_p, nw, qm, hv, ones_bd)
    in_specs = [pl.BlockSpec((t_len, 768), lambda b: (row_blk0 + b, 4))]
    in_specs += [_const_spec(a) for a in consts]
    args = [proj, *consts]
    out_shape = [jax.ShapeDtypeStruct((n_seq * t_len, BRANCH_W), F32)]
    out_specs = [pl.BlockSpec((t_len, BRANCH_W), lambda b: (b, 0))]
    scratch = []
    if lat:
        cos, sin = rope
        in_specs += [_const_spec(cos), _const_spec(sin),
                     pl.BlockSpec((1, PAST_LEN, BRANCH_W), lambda b: (b, 0, 0)),
                     pl.BlockSpec((1, PAST_LEN, BRANCH_W), lambda b: (b, 0, 0))]
        args += [cos, sin, cache_k, cache_v]
        scratch = [pltpu.VMEM((t_len, BRANCH_W), F32),
                   pltpu.VMEM((PAST_LEN + t_len, BRANCH_W), F32),
                   pltpu.VMEM((PAST_LEN + t_len, BRANCH_W), F32)]
    else:
        for _ in range(2):
            out_shape.append(jax.ShapeDtypeStruct((n_seq, N_HEADS, t_len, HEAD_V), F32))
            out_specs.append(pl.BlockSpec((1, N_HEADS, t_len, HEAD_V), lambda b: (b, 0, 0, 0)))
    return pl.pallas_call(
        functools.partial(_att_kernel, t_len=t_len, lat=lat, lam_init=lam_init),
        grid=(n_seq,),
        in_specs=in_specs,
        out_specs=out_specs,
        out_shape=out_shape,
        scratch_shapes=scratch,
        compiler_params=_cparams(("arbitrary",)),
        name="diff_attention",
    )(*args)


def _block_diag_states(st, dk, transposed):
    eye = jnp.eye(N_HEADS, dtype=st.dtype)
    b = st.shape[0]
    if transposed:
        return jnp.einsum('bnhde,hg->bnhegd', st, eye).reshape(b, 2, N_HEADS * HEAD_V, N_HEADS * dk)
    return jnp.einsum('bnhde,hg->bnhdge', st, eye).reshape(b, 2, N_HEADS * dk, N_HEADS * HEAD_V)


def _in_perm():
    offs = np.concatenate([[0], np.cumsum(IN_ORIG)])
    seg = lambda a, b: np.arange(offs[a], offs[b])
    return np.concatenate([seg(0, 4), seg(5, 6), seg(8, 9), seg(9, 16), seg(4, 5), seg(6, 8)])


def kernel(x_prompt, x_sample, cache_diff_k, cache_diff_v, state_gla, state_dn, state_hgrn, c, c_ctx,
           norm_w, w_mod, b_mod, ffn1_in, ffn1_down, ffn2_in, ffn2_down, w_in, gla_w2, gla_b, gla_norm,
           dn_conv, dn_a_log, dn_dt_bias, dn_norm, hg_lb_logits, hg_norm, diff_lambda, diff_norm,
           w_branch, w_mgate, w_out, final_norm):
    x = jnp.concatenate([x_prompt.reshape(N_CTX_TOK, D_MODEL), x_sample.reshape(-1, D_MODEL)], axis=0)
    c_rows = jnp.concatenate([c_ctx[None, :], c, jnp.zeros((8 - 1 - N_LAT_SEQ, D_MODEL), F32)], axis=0)
    mod = _mod_call(c_rows, w_mod, b_mod).reshape(DEPTH, 8, N_MOD, D_MODEL)
    rope = _rope_tables()
    perm = _in_perm()
    lb_logits = hg_lb_logits.reshape(DEPTH, 2 * HG_W)
    lat_blk = N_CTX_TOK // T_LAT
    tile4 = lambda a: jnp.tile(a, N_HEADS)[None, :]
    fin = final_norm[None, :]
    new_k, new_v, new_gla, new_dn, new_hg = [], [], [], [], []
    for l in range(DEPTH):
        mod_l = mod[l]
        lam_init = 0.8 - 0.6 * math.exp(-0.3 * l)
        w_in_p = jnp.pad(w_in[l][:, perm], ((0, 0), (0, N_IN_PAD - N_IN))).astype(BF)
        w2bd = jnp.zeros((128, 2 * GLA_KW), F32)
        w2bd = w2bd.at[0:GLA_LOWRANK, 0:GLA_KW].set(gla_w2[l, 0])
        w2bd = w2bd.at[GLA_LOWRANK:2 * GLA_LOWRANK, GLA_KW:].set(gla_w2[l, 1]).astype(BF)
        gbias = gla_b[l].reshape(1, 2 * GLA_KW)
        alog_row = jnp.zeros((1, 128), F32).at[0, SMALL_DNA:SMALL_DNA + 8].set(dn_a_log[l].reshape(-1))
        dtb_row = jnp.zeros((1, 128), F32).at[0, SMALL_DNA:SMALL_DNA + 8].set(dn_dt_bias[l].reshape(-1))

        x = _ffn_call(x, mod_l, norm_w[l, 0][None, :], ffn1_in[l].astype(BF), ffn1_down[l].astype(BF),
                      fin, 0, False)
        proj = _proj_call(x, mod_l, norm_w[l, 1][None, :], w_in_p)

        a_c, st_a = _gla_call(proj, w2bd, gbias, tile4(gla_norm[l]), None, T_CTX, N_CTX_SEQ, 0)
        dn_lat_blk = N_CTX_TOK // (N_PAIR * T_LAT)
        q_c, k_c, dd_c, lw_c, vb_c, kb_c, qkd_c = _dn_build_call(proj, dn_conv[l], alog_row, dtb_row,
                                                                T_CTX, N_CTX_SEQ, 0)
        q_l, k_l, dd_l, lw_l, vb_l, kb_l, qkd_l = _dn_build_call(proj, dn_conv[l], alog_row, dtb_row,
                                                                T_LAT, N_LAT_SEQ, dn_lat_blk)
        cat = lambda a, b: jnp.concatenate([a, b], axis=1)
        dn_u, dn_w = _dn_from_systems(_dn_solve_call(*_dn_to_systems(cat(lw_c, lw_l), cat(vb_c, vb_l),
                                                                     cat(kb_c, kb_l))))
        b_c, st_b = _dn_scan_call(proj, q_c, k_c, dd_c, dn_u, dn_w, qkd_c, tile4(dn_norm[l]), None,
                                  T_CTX, N_CTX_SEQ, 0)
        c_c, st_c = _hgrn_call(proj, lb_logits, tile4(hg_norm[l]), None, T_CTX, N_CTX_SEQ, 0, l)
        d_c, new_k_l, new_v_l = _att_call(proj, diff_lambda[l], tile4(diff_norm[l]), None, None, None,
                                  T_CTX, N_CTX_SEQ, 0, lam_init)
        ck = cache_diff_k[:, l].transpose(0, 2, 1, 3).reshape(N_LAT_SEQ, PAST_LEN, BRANCH_W)
        cv = cache_diff_v[:, l].transpose(0, 2, 1, 3).reshape(N_LAT_SEQ, PAST_LEN, BRANCH_W)
        (a_l,) = _gla_call(proj, w2bd, gbias, tile4(gla_norm[l]),
                           _block_diag_states(state_gla[:, l], GLA_DK, True), T_LAT, N_LAT_SEQ, lat_blk)
        (b_l,) = _dn_scan_call(proj, q_l, k_l, dd_l, dn_u, dn_w, qkd_l, tile4(dn_norm[l]),
                               _block_diag_states(state_dn[:, l], DN_DK, False), T_LAT, N_LAT_SEQ, dn_lat_blk)
        (c_l,) = _hgrn_call(proj, lb_logits, tile4(hg_norm[l]),
                            _block_diag_states(state_hgrn[:, l], HG_DK, True), T_LAT, N_LAT_SEQ, lat_blk, l)
        (d_l,) = _att_call(proj, diff_lambda[l], tile4(diff_norm[l]), ck, cv, rope,
                           T_LAT, N_LAT_SEQ, lat_blk, lam_init)

        x = _merge_call(x, mod_l, norm_w[l, 1][None, :], (a_c, b_c, c_c, d_c), (a_l, b_l, c_l, d_l),
                        w_mgate[l].astype(BF), w_branch[l].astype(BF), w_out[l].astype(BF))
        x = _ffn_call(x, mod_l, norm_w[l, 2][None, :], ffn2_in[l].astype(BF), ffn2_down[l].astype(BF),
                      fin, 2, l == DEPTH - 1)
        new_k.append(new_k_l)
        new_v.append(new_v_l)
        new_gla.append(st_a)
        new_dn.append(st_b)
        new_hg.append(st_c)
    y_prompt = x[:N_CTX_TOK].reshape(N_CTX_SEQ, T_CTX, D_MODEL)
    y_sample = x[N_CTX_TOK:].reshape(N_LAT_SEQ, T_LAT, D_MODEL)
    return (y_prompt, y_sample, jnp.stack(new_k, axis=1), jnp.stack(new_v, axis=1),
            jnp.stack(new_gla, axis=1), jnp.stack(new_dn, axis=1), jnp.stack(new_hg, axis=1))
```

```python
import functools
import math

import numpy as np
import jax
import jax.numpy as jnp
from jax import lax
from jax.experimental import pallas as pl
from jax.experimental.pallas import tpu as pltpu

F32 = jnp.float32
BF = jnp.bfloat16

D_MODEL = 1024
N_CTX_SEQ = 32
T_CTX = 256
DEPTH = 2
N_LAT_SEQ = 2
T_LAT = 1024
PAST_LEN = 512
GRID_W = 64
N_HEADS = 4
BRANCH_W = 256
HEAD_V = 64
GLA_DK = 32
GLA_KW = 128
GLA_LOWRANK = 16
GLA_TAU = 16.0
DN_DK = 64
HG_DK = 64
HG_W = 256
DF_DH = 32
ROPE_BASE = 10000.0
D_FF = 2816
N_MOD = 9
CHUNK = 64
EPS = 1e-6
N_CTX_TOK = N_CTX_SEQ * T_CTX
N_TOK = N_CTX_TOK + N_LAT_SEQ * T_LAT
N_LEVELS = 6

IN_ORIG = (128, 128, 256, 256, 32, 768, 8, 8, 256, 256, 512, 256, 256, 256, 256, 256)
N_IN = sum(IN_ORIG)
N_IN_PAD = 3968
COL_SMALL_BLOCK = 30
SMALL_LR = 0
SMALL_DNB = 32
SMALL_DNA = 40

VMEM_LIMIT = 56 * 1024 * 1024

TM_FFN = 512
TF_FFN = 1408
TM_PROJ = 512
TM_MERGE = 512
TN_MOD = 2304
TQ_ATT = 256


def _silu(x):
    return x * (1.0 / (1.0 + jnp.exp(-x)))


def _sigmoid(x):
    return 1.0 / (1.0 + jnp.exp(-x))


def _softplus(x):
    return jnp.maximum(x, 0.0) + jnp.log(1.0 + jnp.exp(-jnp.abs(x)))


def _log_sigmoid(x):
    return -_softplus(-x)


def _mm(a, b):
    return jnp.dot(a.astype(BF), b.astype(BF), preferred_element_type=F32)


def _mm_nt(a, b):
    return lax.dot_general(a.astype(BF), b.astype(BF), (((1,), (1,)), ((), ())),
                           preferred_element_type=F32)


def _mm_tn(a, b):
    return lax.dot_general(a.astype(BF), b.astype(BF), (((0,), (0,)), ((), ())),
                           preferred_element_type=F32)


def _split(x, n):
    parts = []
    r = x
    for i in range(n):
        p = r.astype(BF)
        parts.append(p)
        if i + 1 < n:
            r = r - p.astype(F32)
    return parts


def _sel_l(m01, x, n=3):
    out = None
    for p in _split(x, n):
        t = jnp.dot(m01, p, preferred_element_type=F32)
        out = t if out is None else out + t
    return out


def _sel_r(x, m01, n=3):
    out = None
    for p in _split(x, n):
        t = jnp.dot(p, m01, preferred_element_type=F32)
        out = t if out is None else out + t
    return out


def _sel_tn(x, m01, n=3):
    out = None
    for p in _split(x, n):
        t = lax.dot_general(p, m01, (((0,), (0,)), ((), ())), preferred_element_type=F32)
        out = t if out is None else out + t
    return out


def _rms(x, w):
    return x * lax.rsqrt(jnp.mean(x * x, axis=-1, keepdims=True) + EPS) * w


def _head_lanes(x, h):
    blk = x[:, (h // 2) * 128:(h // 2 + 1) * 128]
    if h % 2:
        blk = pltpu.roll(blk, 64, 1)
    return blk[:, :HEAD_V]


def _stack_heads(x, hm_ref):
    return jnp.concatenate([x * hm_ref[h:h + 1, :] for h in range(N_HEADS)], axis=0)


def _head_diag(o_full, hv_ref, c):
    out = None
    for h in range(N_HEADS):
        t = o_full[h * c:(h + 1) * c, :] * hv_ref[h:h + 1, :]
        out = t if out is None else out + t
    return out


def _head_norm_gate(o, gate, nw, ones_bd):
    ms = _sel_r(o * o, ones_bd, 2) * (1.0 / HEAD_V)
    return o * lax.rsqrt(ms + EPS) * nw * _silu(gate)


def _mod_row(i, tm):
    return jnp.maximum(i * tm - (N_CTX_TOK - T_LAT), 0) // T_LAT


def _cparams(sem):
    return pltpu.CompilerParams(dimension_semantics=sem, vmem_limit_bytes=VMEM_LIMIT)


@functools.lru_cache(maxsize=None)
def _scan_consts(w, reverse):
    c = CHUNK
    idx = np.arange(c)
    i = idx[:, None]
    m = idx[None, :]
    ms, pm = [], []
    s = c // 2
    while s >= 1:
        par = idx // (2 * s)
        right = (idx % (2 * s)) >= s
        same = par[:, None] == par[None, :]
        if not reverse:
            e = (par * 2 * s + s - 1)[:, None]
            mat = np.where(right[:, None], (m > e) & (m <= i), (m > i) & (m <= e))
            p = same & right[:, None] & (~right)[None, :]
        else:
            e = (par * 2 * s + s)[:, None]
            mat = np.where(right[:, None], (m >= e) & (m < i), (m >= i) & (m < e))
            p = same & (~right)[:, None] & right[None, :]
        ms.append(mat)
        pm.append(p)
        s //= 2
    if not reverse:
        ms.append(m <= i)
        ms.append(m > i)
    else:
        ms.append(m >= i)
        ms.append(m < i)
    pm.append(i == m)
    mstack = np.concatenate(ms, axis=0).astype(np.float32)
    pmask = np.stack([np.tile(p, (1, N_HEADS)) for p in pm]).astype(np.float32)
    return mstack, pmask


@functools.lru_cache(maxsize=None)
def _head_consts(w):
    dk = w // N_HEADS
    hm = np.zeros((N_HEADS, w), np.float32)
    hv = np.zeros((N_HEADS, BRANCH_W), np.float32)
    for h in range(N_HEADS):
        hm[h, h * dk:(h + 1) * dk] = 1.0
        hv[h, h * HEAD_V:(h + 1) * HEAD_V] = 1.0
    bd = hm.T @ hv
    ones_bd = hv.T @ hv
    return hm, hv, bd, ones_bd


@functools.lru_cache(maxsize=None)
def _dn_consts():
    c = CHUNK
    idx = np.arange(c)
    i = idx[:, None]
    j = idx[None, :]
    tri = np.stack([(j <= i), (j >= i)]).astype(np.float32)
    strict = np.stack([np.tile(j < i, (1, N_HEADS)), np.tile(j > i, (1, N_HEADS))]).astype(np.float32)
    eye = np.tile(np.eye(c), (1, N_HEADS)).astype(np.float32)
    blk = np.ones((c, c), np.float32)
    exb = np.zeros((128, 2 * BRANCH_W), np.float32)
    exa = np.zeros((128, 2 * BRANCH_W), np.float32)
    for n in range(2):
        for h in range(N_HEADS):
            lo = n * BRANCH_W + h * HEAD_V
            exb[SMALL_DNB + n * N_HEADS + h, lo:lo + HEAD_V] = 1.0
            exa[SMALL_DNA + n * N_HEADS + h, lo:lo + HEAD_V] = 1.0
    return tri, strict, eye, blk, exb, exa


@functools.lru_cache(maxsize=None)
def _att_consts():
    qm = np.zeros((2 * N_HEADS, BRANCH_W), np.float32)
    for h in range(N_HEADS):
        for mp in range(2):
            lo = h * HEAD_V + mp * DF_DH
            qm[2 * h + mp, lo:lo + DF_DH] = 1.0
    return qm


def _rope_tables():
    rows = T_LAT // GRID_W
    row = jnp.repeat(jnp.arange(rows), GRID_W).astype(F32)
    col = jnp.tile(jnp.arange(GRID_W), rows).astype(F32)
    half = DF_DH // 2
    inv = ROPE_BASE ** (-jnp.arange(0, half, 2, dtype=F32) / half)

    def angles(pos):
        a = pos[:, None] * inv[None, :]
        return jnp.concatenate([a, a], axis=-1)

    ang = jnp.concatenate([angles(row), angles(col)], axis=-1)
    reps = BRANCH_W // DF_DH
    return jnp.tile(jnp.cos(ang), (1, reps)), jnp.tile(jnp.sin(ang), (1, reps))


def _mod_kernel(c_ref, w_ref, b_ref, o_ref):
    a = _silu(c_ref[...])
    w = w_ref[0]
    out = None
    for ap in _split(a, 2):
        for wp in _split(w, 2):
            t = jnp.dot(ap, wp, preferred_element_type=F32)
            out = t if out is None else out + t
    o_ref[0] = out + b_ref[0]


def _mod_call(c_rows, w_mod, b_mod):
    n_t = (N_MOD * D_MODEL) // TN_MOD
    return pl.pallas_call(
        _mod_kernel,
        grid=(DEPTH, n_t),
        in_specs=[pl.BlockSpec((8, D_MODEL), lambda l, j: (0, 0)),
                  pl.BlockSpec((1, D_MODEL, TN_MOD), lambda l, j: (l, 0, j)),
                  pl.BlockSpec((1, 1, TN_MOD), lambda l, j: (l, 0, j))],
        out_specs=pl.BlockSpec((1, 8, TN_MOD), lambda l, j: (l, 0, j)),
        out_shape=jax.ShapeDtypeStruct((DEPTH, 8, N_MOD * D_MODEL), F32),
        compiler_params=_cparams(("arbitrary", "arbitrary")),
        name="mod_vectors",
    )(c_rows, w_mod, b_mod.reshape(DEPTH, 1, N_MOD * D_MODEL))


def _ffn_kernel(x_ref, mod_ref, nw_ref, wg_ref, wu_ref, wd_ref, fn_ref, o_ref, h_ref, acc_ref,
                *, sub, final):
    f = pl.program_id(1)

    @pl.when(f == 0)
    def _():
        x = x_ref[...]
        sh = mod_ref[0, 3 * sub:3 * sub + 1, :]
        sc = mod_ref[0, 3 * sub + 1:3 * sub + 2, :]
        h_ref[...] = (_rms(x, nw_ref[...]) * (1.0 + sc) + sh).astype(BF)
        acc_ref[...] = jnp.zeros_like(acc_ref)

    h = h_ref[...]
    g = jnp.dot(h, wg_ref[...], preferred_element_type=F32)
    u = jnp.dot(h, wu_ref[...], preferred_element_type=F32)
    acc_ref[...] += jnp.dot((_silu(g) * u).astype(BF), wd_ref[...], preferred_element_type=F32)

    @pl.when(f == pl.num_programs(1) - 1)
    def _():
        ga = mod_ref[0, 3 * sub + 2:3 * sub + 3, :]
        y = x_ref[...] + 0.5 * ga * acc_ref[...]
        if final:
            y = _rms(y, fn_ref[...])
        o_ref[...] = y


def _ffn_call(x, mod_l, nw, w_in, w_down, final_w, sub, final):
    n_f = D_FF // TF_FFN
    tm = TM_FFN
    return pl.pallas_call(
        functools.partial(_ffn_kernel, sub=sub, final=final),
        grid=(N_TOK // tm, n_f),
        in_specs=[pl.BlockSpec((tm, D_MODEL), lambda i, f: (i, 0)),
                  pl.BlockSpec((1, N_MOD, D_MODEL), lambda i, f: (_mod_row(i, tm), 0, 0)),
                  pl.BlockSpec((1, D_MODEL), lambda i, f: (0, 0)),
                  pl.BlockSpec((D_MODEL, TF_FFN), lambda i, f: (0, f)),
                  pl.BlockSpec((D_MODEL, TF_FFN), lambda i, f: (0, n_f + f)),
                  pl.BlockSpec((TF_FFN, D_MODEL), lambda i, f: (f, 0)),
                  pl.BlockSpec((1, D_MODEL), lambda i, f: (0, 0))],
        out_specs=pl.BlockSpec((tm, D_MODEL), lambda i, f: (i, 0)),
        out_shape=jax.ShapeDtypeStruct((N_TOK, D_MODEL), F32),
        scratch_shapes=[pltpu.VMEM((tm, D_MODEL), BF), pltpu.VMEM((tm, D_MODEL), F32)],
        compiler_params=_cparams(("arbitrary", "arbitrary")),
        name="swiglu_half_step",
    )(x, mod_l, nw, w_in, w_in, w_down, final_w)


def _proj_kernel(x_ref, mod_ref, nw_ref, w_ref, o_ref):
    sh = mod_ref[0, 3:4, :]
    sc = mod_ref[0, 4:5, :]
    h = (_rms(x_ref[...], nw_ref[...]) * (1.0 + sc) + sh).astype(BF)
    o_ref[...] = jnp.dot(h, w_ref[...], preferred_element_type=F32)


def _proj_call(x, mod_l, nw, w_in_p):
    tm = TM_PROJ
    return pl.pallas_call(
        _proj_kernel,
        grid=(N_TOK // tm,),
        in_specs=[pl.BlockSpec((tm, D_MODEL), lambda i: (i, 0)),
                  pl.BlockSpec((1, N_MOD, D_MODEL), lambda i: (_mod_row(i, tm), 0, 0)),
                  pl.BlockSpec((1, D_MODEL), lambda i: (0, 0)),
                  pl.BlockSpec((D_MODEL, N_IN_PAD), lambda i: (0, 0))],
        out_specs=pl.BlockSpec((tm, N_IN_PAD), lambda i: (i, 0)),
        out_shape=jax.ShapeDtypeStruct((N_TOK, N_IN_PAD), F32),
        compiler_params=_cparams(("arbitrary",)),
        name="mixer_in_proj",
    )(x, mod_l, nw, w_in_p)


def _merge_kernel(x_ref, mod_ref, nw_ref, *rest):
    ctx_refs, lat_refs = rest[0:4], rest[4:8]
    wg_ref, wb_ref, wo_ref, o_ref = rest[8:12]
    x = x_ref[...]
    sh = mod_ref[0, 3:4, :]
    sc = mod_ref[0, 4:5, :]
    ga = mod_ref[0, 5:6, :]
    h = (_rms(x, nw_ref[...]) * (1.0 + sc) + sh).astype(BF)
    is_lat = pl.program_id(0) >= N_CTX_TOK // TM_MERGE
    mixed = None
    for n in range(4):
        gate = _sigmoid(jnp.dot(h, wg_ref[:, n * D_MODEL:(n + 1) * D_MODEL], preferred_element_type=F32))
        br = jnp.where(is_lat, lat_refs[n][...], ctx_refs[n][...])
        up = jnp.dot(br.astype(BF), wb_ref[n], preferred_element_type=F32)
        mixed = gate * up if mixed is None else mixed + gate * up
    out = jnp.dot(mixed.astype(BF), wo_ref[...], preferred_element_type=F32)
    o_ref[...] = x + ga * out


def _merge_call(x, mod_l, nw, ctx_branches, lat_branches, w_mgate, w_branch, w_out):
    tm = TM_MERGE
    n_ctx = N_CTX_TOK // tm
    cspec = pl.BlockSpec((tm, BRANCH_W), lambda i: (jnp.minimum(i, n_ctx - 1), 0))
    lspec = pl.BlockSpec((tm, BRANCH_W), lambda i: (jnp.maximum(i - n_ctx, 0), 0))
    return pl.pallas_call(
        _merge_kernel,
        grid=(N_TOK // tm,),
        in_specs=[pl.BlockSpec((tm, D_MODEL), lambda i: (i, 0)),
                  pl.BlockSpec((1, N_MOD, D_MODEL), lambda i: (_mod_row(i, tm), 0, 0)),
                  pl.BlockSpec((1, D_MODEL), lambda i: (0, 0)),
                  cspec, cspec, cspec, cspec, lspec, lspec, lspec, lspec,
                  pl.BlockSpec((D_MODEL, 4 * D_MODEL), lambda i: (0, 0)),
                  pl.BlockSpec((4, BRANCH_W, D_MODEL), lambda i: (0, 0, 0)),
                  pl.BlockSpec((D_MODEL, D_MODEL), lambda i: (0, 0))],
        out_specs=pl.BlockSpec((tm, D_MODEL), lambda i: (i, 0)),
        out_shape=jax.ShapeDtypeStruct((N_TOK, D_MODEL), F32),
        compiler_params=_cparams(("arbitrary",)),
        name="gated_merge",
    )(x, mod_l, nw, *ctx_branches, *lat_branches, w_mgate, w_branch, w_out)


def _gated_chunk(qc, kc, vc, gc, st, ms_ref, pm_ref, hm_ref, hv_ref, bdt_ref, reverse):
    c = CHUNK
    e = jnp.exp(_sel_l(ms_ref[...], gc, 2))
    kst = _stack_heads(kc, hm_ref)
    a = pm_ref[N_LEVELS] * _mm_nt(qc, kst)
    for lv in range(N_LEVELS):
        el = e[lv * c:(lv + 1) * c]
        a = a + pm_ref[lv] * _mm_nt(qc * el, kst * jnp.concatenate([el] * N_HEADS, axis=0))
    e_b = e[N_LEVELS * c:(N_LEVELS + 1) * c]
    e_t = e[(N_LEVELS + 1) * c:(N_LEVELS + 2) * c]
    o = _mm(a, _stack_heads(vc, hv_ref)) + _mm_nt(qc * e_b, st)
    dec = e_b[0:1, :] if reverse else e_b[c - 1:c, :]
    st_new = st * dec + bdt_ref[...] * _mm_tn(vc, kc * e_t)
    return o, st_new


def _gated_scan(q_ref, kf_ref, kb_ref, v_ref, gf_ref, gb_ref, of_ref, ob_ref, s0f, s0b, cf, cb, n_chunks):
    def body(t, carry):
        sf, sb = carry
        rf = pl.ds(pl.multiple_of(t * CHUNK, CHUNK), CHUNK)
        rb = pl.ds(pl.multiple_of((n_chunks - 1 - t) * CHUNK, CHUNK), CHUNK)
        o_f, sf = _gated_chunk(q_ref[rf, :], kf_ref[rf, :], v_ref[rf, :], gf_ref[rf, :], sf, *cf, False)
        o_b, sb = _gated_chunk(q_ref[rb, :], kb_ref[rb, :], v_ref[rb, :], gb_ref[rb, :], sb, *cb, True)
        of_ref[rf, :] = o_f
        ob_ref[rb, :] = o_b
        return sf, sb

    return lax.fori_loop(0, n_chunks, body, (s0f, s0b))


def _write_states(st_ref, st_f, st_b, dk, eye_ref):
    for d, st in enumerate((st_f, st_b)):
        for h in range(N_HEADS):
            tr = _sel_tn(st[h * HEAD_V:(h + 1) * HEAD_V, :], eye_ref[...])
            st_ref[0, d, h] = tr[h * dk:(h + 1) * dk, :]


def _gla_kernel(*refs, t_len, has_state):
    (blk_ref, small_ref, w2_ref, gb_ref, nw_ref,
     msf_ref, pmf_ref, msb_ref, pmb_ref, hm_ref, hv_ref, bdt_ref, ones_ref, eye_ref) = refs[:14]
    pos = 14
    if has_state:
        s0_ref = refs[pos]
        pos += 1
    out_ref = refs[pos]
    pos += 1
    if not has_state:
        st_ref = refs[pos]
        pos += 1
    q_s, gf_s, gb_s, of_s, ob_s = refs[pos:pos + 5]

    q_s[...] = blk_ref[:, 0:128] * (GLA_DK ** -0.5)
    z = _mm(small_ref[...], w2_ref[...]) + gb_ref[...]
    g = _log_sigmoid(z) * (1.0 / GLA_TAU)
    gf_s[...] = g[:, 0:128]
    gb_s[...] = g[:, 128:256]
    k_ref = blk_ref.at[:, 128:256]
    v_ref = blk_ref.at[:, 256:512]
    n_chunks = t_len // CHUNK
    if has_state:
        s0f, s0b = s0_ref[0, 0], s0_ref[0, 1]
    else:
        s0f = s0b = jnp.zeros((BRANCH_W, GLA_KW), F32)
    cf = (msf_ref, pmf_ref, hm_ref, hv_ref, bdt_ref)
    cb = (msb_ref, pmb_ref, hm_ref, hv_ref, bdt_ref)
    s_f, s_b = _gated_scan(q_s, k_ref, k_ref, v_ref, gf_s, gb_s, of_s, ob_s, s0f, s0b, cf, cb, n_chunks)
    out_ref[...] = _head_norm_gate(of_s[...] + ob_s[...], blk_ref[:, 512:768], nw_ref[...], ones_ref[...])
    if not has_state:
        _write_states(st_ref, s_f, s_b, GLA_DK, eye_ref)


def _const_spec(a):
    nd = a.ndim
    return pl.BlockSpec(a.shape, lambda b, _n=nd: (0,) * _n)


def _scan_const_arrays(w):
    msf, pmf = _scan_consts(w, False)
    msb, pmb = _scan_consts(w, True)
    hm, hv, bd, ones_bd = _head_consts(w)
    return (jnp.asarray(msf, BF), jnp.asarray(pmf, F32), jnp.asarray(msb, BF), jnp.asarray(pmb, F32),
            jnp.asarray(hm, F32), jnp.asarray(hv, F32), jnp.asarray(bd.T, F32), jnp.asarray(ones_bd, BF),
            jnp.eye(HEAD_V, dtype=BF))


def _gla_call(proj, w2bd, gbias, nw, s0, t_len, n_seq, row_blk0):
    msf, pmf, msb, pmb, hm, hv, bdt, ones_bd, eye = _scan_const_arrays(GLA_KW)
    has_state = s0 is not None
    consts = (w2bd, gbias, nw, msf, pmf, msb, pmb, hm, hv, bdt, ones_bd, eye)
    in_specs = [pl.BlockSpec((t_len, 768), lambda b: (row_blk0 + b, 0)),
                pl.BlockSpec((t_len, 128), lambda b: (row_blk0 + b, COL_SMALL_BLOCK))]
    in_specs += [_const_spec(a) for a in consts]
    args = [proj, proj, *consts]
    if has_state:
        in_specs.append(pl.BlockSpec((1, 2, BRANCH_W, GLA_KW), lambda b: (b, 0, 0, 0)))
        args.append(s0)
    out_shape = [jax.ShapeDtypeStruct((n_seq * t_len, BRANCH_W), F32)]
    out_specs = [pl.BlockSpec((t_len, BRANCH_W), lambda b: (b, 0))]
    if not has_state:
        out_shape.append(jax.ShapeDtypeStruct((n_seq, 2, N_HEADS, GLA_DK, HEAD_V), F32))
        out_specs.append(pl.BlockSpec((1, 2, N_HEADS, GLA_DK, HEAD_V), lambda b: (b, 0, 0, 0, 0)))
    return pl.pallas_call(
        functools.partial(_gla_kernel, t_len=t_len, has_state=has_state),
        grid=(n_seq,),
        in_specs=in_specs,
        out_specs=out_specs,
        out_shape=out_shape,
        scratch_shapes=[pltpu.VMEM((t_len, GLA_KW), F32)] * 3 + [pltpu.VMEM((t_len, BRANCH_W), F32)] * 2,
        compiler_params=_cparams(("arbitrary",)),
        name="gla_mixer",
    )(*args)


def _hgrn_kernel(*refs, t_len, has_state, layer):
    (q_ref, f_ref, v_ref, gate_ref, lbl_ref, nw_ref,
     msf_ref, pmf_ref, msb_ref, pmb_ref, hm_ref, hv_ref, bdt_ref, ones_ref, eye_ref) = refs[:15]
    pos = 15
    if has_state:
        s0_ref = refs[pos]
        pos += 1
    out_ref = refs[pos]
    pos += 1
    if not has_state:
        st_ref = refs[pos]
        pos += 1
    q_s, kf_s, kb_s, gf_s, gb_s, of_s, ob_s = refs[pos:pos + 7]

    lg = lbl_ref[...]
    mx = jnp.max(lg, axis=0, keepdims=True)
    ex = jnp.exp(lg - mx)
    p = ex / jnp.sum(ex, axis=0, keepdims=True)
    lb = jnp.sum(p[0:layer + 1], axis=0, keepdims=True) - p[0:1]

    q_s[...] = _silu(q_ref[...]) * (HG_DK ** -0.5)
    f = lb + (1.0 - lb) * _sigmoid(f_ref[...])
    kf_s[...] = 1.0 - f[:, 0:HG_W]
    kb_s[...] = 1.0 - f[:, HG_W:2 * HG_W]
    lf = jnp.log(f)
    gf_s[...] = lf[:, 0:HG_W]
    gb_s[...] = lf[:, HG_W:2 * HG_W]
    n_chunks = t_len // CHUNK
    if has_state:
        s0f, s0b = s0_ref[0, 0], s0_ref[0, 1]
    else:
        s0f = s0b = jnp.zeros((BRANCH_W, HG_W), F32)
    cf = (msf_ref, pmf_ref, hm_ref, hv_ref, bdt_ref)
    cb = (msb_ref, pmb_ref, hm_ref, hv_ref, bdt_ref)
    s_f, s_b = _gated_scan(q_s, kf_s, kb_s, v_ref, gf_s, gb_s, of_s, ob_s, s0f, s0b, cf, cb, n_chunks)
    out_ref[...] = _head_norm_gate(of_s[...] + ob_s[...], gate_ref[...], nw_ref[...], ones_ref[...])
    if not has_state:
        _write_states(st_ref, s_f, s_b, HG_DK, eye_ref)


def _hgrn_call(proj, lb_logits, nw, s0, t_len, n_seq, row_blk0, layer):
    msf, pmf, msb, pmb, hm, hv, bdt, ones_bd, eye = _scan_const_arrays(HG_W)
    has_state = s0 is not None
    consts = (lb_logits, nw, msf, pmf, msb, pmb, hm, hv, bdt, ones_bd, eye)
    in_specs = [pl.BlockSpec((t_len, 256), lambda b: (row_blk0 + b, 7)),
                pl.BlockSpec((t_len, 512), lambda b: (row_blk0 + b, 4)),
                pl.BlockSpec((t_len, 256), lambda b: (row_blk0 + b, 10)),
                pl.BlockSpec((t_len, 256), lambda b: (row_blk0 + b, 11))]
    in_specs += [_const_spec(a) for a in consts]
    args = [proj, proj, proj, proj, *consts]
    if has_state:
        in_specs.append(pl.BlockSpec((1, 2, BRANCH_W, HG_W), lambda b: (b, 0, 0, 0)))
        args.append(s0)
    out_shape = [jax.ShapeDtypeStruct((n_seq * t_len, BRANCH_W), F32)]
    out_specs = [pl.BlockSpec((t_len, BRANCH_W), lambda b: (b, 0))]
    if not has_state:
        out_shape.append(jax.ShapeDtypeStruct((n_seq, 2, N_HEADS, HG_DK, HEAD_V), F32))
        out_specs.append(pl.BlockSpec((1, 2, N_HEADS, HG_DK, HEAD_V), lambda b: (b, 0, 0, 0, 0)))
    return pl.pallas_call(
        functools.partial(_hgrn_kernel, t_len=t_len, has_state=has_state, layer=layer),
        grid=(n_seq,),
        in_specs=in_specs,
        out_specs=out_specs,
        out_shape=out_shape,
        scratch_shapes=[pltpu.VMEM((t_len, HG_W), F32)] * 5 + [pltpu.VMEM((t_len, BRANCH_W), F32)] * 2,
        compiler_params=_cparams(("arbitrary",)),
        name="hgrn2_mixer",
    )(*args)


N_PAIR = 2
SOLVE_JB = 8


def _dn_solve_kernel(lt_ref, vb_ref, kb_ref, u_ref, w_ref):
    c = CHUNK
    rev = pl.program_id(1)
    u_ref[...] = jnp.zeros_like(u_ref)
    w_ref[...] = jnp.zeros_like(w_ref)

    def outer(t, carry):
        i = t + rev * (c - 1 - 2 * t)

        def inner(jb, acc):
            au, aw = acc
            j0 = pl.multiple_of(jb * SOLVE_JB, SOLVE_JB)
            for r in range(SOLVE_JB):
                coef = lt_ref[0, i, pl.ds(j0 + r, 1), :]
                au = au - coef * u_ref[0, j0 + r]
                aw = aw - coef * w_ref[0, j0 + r]
            return au, aw

        blk = lax.shift_right_logical(i, 3)
        lo = rev * blk
        hi = blk + 1 + rev * (c // SOLVE_JB - blk - 1)
        au, aw = lax.fori_loop(lo, hi, inner, (vb_ref[0, i], kb_ref[0, i]))
        u_ref[0, i] = au
        w_ref[0, i] = aw
        return carry

    lax.fori_loop(0, c, outer, 0)


def _dn_solve_call(lt, vbt, kbt):
    n_chunk = lt.shape[-1]
    spec = pl.BlockSpec((1, CHUNK, HEAD_V, n_chunk), lambda h, d: (d, 0, h, 0))
    shape = jax.ShapeDtypeStruct((2, CHUNK, BRANCH_W, n_chunk), F32)
    return pl.pallas_call(
        _dn_solve_kernel,
        grid=(N_HEADS, 2),
        in_specs=[spec, spec, spec],
        out_specs=[spec, spec],
        out_shape=[shape, shape],
        compiler_params=_cparams(("arbitrary", "arbitrary")),
        name="deltanet_solve",
    )(lt, vbt, kbt)


def _chunks_to_lanes(a):
    return a.reshape(2, a.shape[1] // CHUNK, CHUNK, BRANCH_W).transpose(0, 2, 3, 1)


def _chunks_from_lanes(a):
    return a.transpose(0, 3, 1, 2).reshape(2, -1, BRANCH_W)


def _dn_build_kernel(x_ref, small_ref, cw_ref, alog_ref, dtb_ref,
                     tri_ref, strict_ref, eye_ref, blk_ref, exb_ref, exa_ref, hm_ref, ones_ref,
                     q_ref, k_ref, d_ref, l_ref, vb_ref, kb_ref, qkd_ref, v_ref, be_s, ge_s, *, t_len):
    n_rows = N_PAIR * t_len

    x = x_ref[...]
    row = lax.broadcasted_iota(jnp.int32, (n_rows, 1), 0) % t_len
    x_prev = jnp.where(row == 0, 0.0, pltpu.roll(x, 1, 0))
    x_next = jnp.where(row == t_len - 1, 0.0, pltpu.roll(x, n_rows - 1, 0))
    y = _silu(x_prev * cw_ref[0:1, :] + x * cw_ref[1:2, :] + x_next * cw_ref[2:3, :])
    cq, ck = y[:, 0:256], y[:, 256:512]
    v_ref[...] = y[:, 512:768]
    q_ref[...] = cq * lax.rsqrt(_sel_r(cq * cq, ones_ref[...]) + EPS) * (DN_DK ** -0.5)
    k_ref[...] = ck * lax.rsqrt(_sel_r(ck * ck, ones_ref[...]) + EPS)
    sm = small_ref[...]
    be_s[...] = _sel_r(_sigmoid(sm), exb_ref[...])
    ge_s[...] = _sel_r(-jnp.exp(alog_ref[...]) * _softplus(sm + dtb_ref[...]), exa_ref[...])

    def body(ci, carry):
        rows = pl.ds(pl.multiple_of(ci * CHUNK, CHUNK), CHUNK)
        qc, kc, vc = q_ref[rows, :], k_ref[rows, :], v_ref[rows, :]
        kst = _stack_heads(kc, hm_ref)
        kk = _mm_nt(kc, kst)
        qk = _mm_nt(qc, kst)
        for d in range(2):
            bexp = be_s[rows, d * BRANCH_W:(d + 1) * BRANCH_W]
            dexp = _sel_l(tri_ref[d], ge_s[rows, d * BRANCH_W:(d + 1) * BRANCH_W])
            drow = _sel_l(blk_ref[...], dexp * eye_ref[...])
            dec_s = jnp.exp(jnp.where(strict_ref[d] > 0.5, dexp - drow, -1e30))
            d_ref[d, rows, :] = dexp
            l_ref[d, rows, :] = bexp * kk * dec_s
            vb_ref[d, rows, :] = vc * bexp
            kb_ref[d, rows, :] = kc * bexp * jnp.exp(dexp)
            qkd_ref[d, rows, :] = (qk * (dec_s + eye_ref[...])).astype(BF)
        return carry

    lax.fori_loop(0, n_rows // CHUNK, body, 0)


def _dn_build_call(proj, conv_w, alog_row, dtb_row, t_len, n_seq, row_blk0):
    tri, strict, eye, blk, exb, exa = _dn_consts()
    tri, blk, exb, exa = (jnp.asarray(a, BF) for a in (tri, blk, exb, exa))
    strict, eye = jnp.asarray(strict, F32), jnp.asarray(eye, F32)
    hm, _, _, ones_bd = _head_consts(BRANCH_W)
    hm, ones_bd = jnp.asarray(hm, F32), jnp.asarray(ones_bd, BF)
    n_rows = N_PAIR * t_len
    n_tok = n_seq * t_len
    consts = (conv_w, alog_row, dtb_row, tri, strict, eye, blk, exb, exa, hm, ones_bd)
    in_specs = [pl.BlockSpec((n_rows, 768), lambda b: (row_blk0 + b, 1)),
                pl.BlockSpec((n_rows, 128), lambda b: (row_blk0 + b, COL_SMALL_BLOCK))]
    in_specs += [_const_spec(a) for a in consts]
    tok_spec = pl.BlockSpec((n_rows, BRANCH_W), lambda b: (b, 0))
    dir_spec = pl.BlockSpec((2, n_rows, BRANCH_W), lambda b: (0, b, 0))
    tok = jax.ShapeDtypeStruct((n_tok, BRANCH_W), F32)
    per_dir = jax.ShapeDtypeStruct((2, n_tok, BRANCH_W), F32)
    return pl.pallas_call(
        functools.partial(_dn_build_kernel, t_len=t_len),
        grid=(n_seq // N_PAIR,),
        in_specs=in_specs,
        out_specs=[tok_spec] * 2 + [dir_spec] * 5,
        out_shape=[tok] * 2 + [per_dir] * 4 + [jax.ShapeDtypeStruct((2, n_tok, BRANCH_W), BF)],
        scratch_shapes=[pltpu.VMEM((n_rows, BRANCH_W), F32)] + [pltpu.VMEM((n_rows, 2 * BRANCH_W), F32)] * 2,
        compiler_params=_cparams(("arbitrary",)),
        name="deltanet_build",
    )(proj, proj, *consts)


def _dn_scan_kernel(*refs, t_len, has_state):
    (q_ref, k_ref, d_ref, u_ref, w_ref, qkd_ref, gate_ref, nw_ref, hv_ref, bd_ref, ones_ref) = refs[:11]
    pos = 11
    if has_state:
        s0_ref = refs[pos]
        pos += 1
    out_ref = refs[pos]
    pos += 1
    if not has_state:
        st_ref = refs[pos]
        pos += 1
    of_s, ob_s = refs[pos:pos + 2]
    c = CHUNK
    n_chunks = t_len // c

    def step(s, rows, d, o_s):
        qc, kc = q_ref[rows, :], k_ref[rows, :]
        dexp = d_ref[d, rows, :]
        v_new = u_ref[d, rows, :] - _mm(w_ref[d, rows, :], s)
        o_s[rows, :] = _mm(qc * jnp.exp(dexp), s) + _mm(qkd_ref[d, rows, :], _stack_heads(v_new, hv_ref))
        dl = dexp[0:1, :] if d == 1 else dexp[c - 1:c, :]
        return s * jnp.exp(dl) + bd_ref[...] * _mm_tn(kc * jnp.exp(dl - dexp), v_new)

    def body(t, carry):
        new = []
        for sq in range(N_PAIR):
            rf = pl.ds(pl.multiple_of(sq * t_len + t * c, c), c)
            rb = pl.ds(pl.multiple_of(sq * t_len + (n_chunks - 1 - t) * c, c), c)
            new.append(step(carry[2 * sq], rf, 0, of_s))
            new.append(step(carry[2 * sq + 1], rb, 1, ob_s))
        return tuple(new)

    if has_state:
        init = tuple(s0_ref[sq, d] for sq in range(N_PAIR) for d in range(2))
    else:
        init = tuple(jnp.zeros((BRANCH_W, BRANCH_W), F32) for _ in range(2 * N_PAIR))
    finals = lax.fori_loop(0, n_chunks, body, init)
    out_ref[...] = _head_norm_gate(of_s[...] + ob_s[...], gate_ref[...], nw_ref[...], ones_ref[...])
    if not has_state:
        for sq in range(N_PAIR):
            for d in range(2):
                for h in range(N_HEADS):
                    st_ref[sq, d, h] = _head_lanes(finals[2 * sq + d][h * DN_DK:(h + 1) * DN_DK, :], h)


def _dn_scan_call(proj, q, k, dd, u, w, qkd, nw, s0, t_len, n_seq, row_blk0):
    _, hv, bd, ones_bd = _head_consts(BRANCH_W)
    hv, bd, ones_bd = jnp.asarray(hv, F32), jnp.asarray(bd, F32), jnp.asarray(ones_bd, BF)
    has_state = s0 is not None
    n_rows = N_PAIR * t_len
    consts = (nw, hv, bd, ones_bd)
    tok_spec = pl.BlockSpec((n_rows, BRANCH_W), lambda b: (b, 0))
    dir_spec = pl.BlockSpec((2, n_rows, BRANCH_W), lambda b: (0, b, 0))
    all_spec = pl.BlockSpec((2, n_rows, BRANCH_W), lambda b: (0, row_blk0 + b, 0))
    in_specs = [tok_spec, tok_spec, dir_spec, all_spec, all_spec, dir_spec,
                pl.BlockSpec((n_rows, 256), lambda b: (row_blk0 + b, 6))]
    in_specs += [_const_spec(a) for a in consts]
    args = [q, k, dd, u, w, qkd, proj, *consts]
    if has_state:
        in_specs.append(pl.BlockSpec((N_PAIR, 2, BRANCH_W, BRANCH_W), lambda b: (b, 0, 0, 0)))
        args.append(s0)
    out_shape = [jax.ShapeDtypeStruct((n_seq * t_len, BRANCH_W), F32)]
    out_specs = [pl.BlockSpec((n_rows, BRANCH_W), lambda b: (b, 0))]
    if not has_state:
        out_shape.append(jax.ShapeDtypeStruct((n_seq, 2, N_HEADS, DN_DK, HEAD_V), F32))
        out_specs.append(pl.BlockSpec((N_PAIR, 2, N_HEADS, DN_DK, HEAD_V), lambda b: (b, 0, 0, 0, 0)))
    return pl.pallas_call(
        functools.partial(_dn_scan_kernel, t_len=t_len, has_state=has_state),
        grid=(n_seq // N_PAIR,),
        in_specs=in_specs,
        out_specs=out_specs,
        out_shape=out_shape,
        scratch_shapes=[pltpu.VMEM((n_rows, BRANCH_W), F32)] * 2,
        compiler_params=_cparams(("arbitrary",)),
        name="deltanet_scan",
    )(*args)


def _rope(x, cos, sin):
    lane = lax.broadcasted_iota(jnp.int32, x.shape, 1) % 16
    n = x.shape[1]
    xrot = jnp.where(lane < 8, -pltpu.roll(x, n - 8, 1), pltpu.roll(x, 8, 1))
    return x * cos + xrot * sin


def _att_kernel(*refs, t_len, lat, lam_init):
    blk_ref, lam_ref, nw_ref, qm_ref, hv_ref, ones_ref = refs[:6]
    pos = 6
    if lat:
        cos_ref, sin_ref, ck_ref, cv_ref = refs[pos:pos + 4]
        pos += 4
    out_ref = refs[pos]
    pos += 1
    if not lat:
        nk_ref, nv_ref = refs[pos:pos + 2]
        pos += 2
    if lat:
        q_s, k_s, v_s = refs[pos:pos + 3]

    lv = lam_ref[...]
    lam = (jnp.exp(jnp.sum(lv[0:1] * lv[1:2], axis=1, keepdims=True))
           - jnp.exp(jnp.sum(lv[2:3] * lv[3:4], axis=1, keepdims=True)) + lam_init)
    q = blk_ref[:, 0:256]
    k = blk_ref[:, 256:512]
    v = blk_ref[:, 512:768]
    if lat:
        cos, sin = cos_ref[...], sin_ref[...]
        q_s[...] = _rope(q, cos, sin)
        k_s[0:PAST_LEN, :] = ck_ref[0]
        k_s[PAST_LEN:PAST_LEN + t_len, :] = _rope(k, cos, sin)
        v_s[0:PAST_LEN, :] = cv_ref[0]
        v_s[PAST_LEN:PAST_LEN + t_len, :] = v
        keys = k_s[...].astype(BF)
        vals = v_s[...].astype(BF)
    else:
        keys = k.astype(BF)
        vals = v.astype(BF)
        for h in range(N_HEADS):
            nk_ref[0, h] = _head_lanes(k, h)
            nv_ref[0, h] = _head_lanes(v, h)
    tq = TQ_ATT
    scale = DF_DH ** -0.5
    for qi in range(t_len // tq):
        qt = q_s[qi * tq:(qi + 1) * tq, :] if lat else q[qi * tq:(qi + 1) * tq, :]
        qs = jnp.concatenate([qt * qm_ref[r:r + 1, :] for r in range(2 * N_HEADS)], axis=0)
        s = _mm_nt(qs, keys) * scale
        s = jnp.exp(s - jnp.max(s, axis=1, keepdims=True))
        p = s / jnp.sum(s, axis=1, keepdims=True)
        a = jnp.concatenate(
            [p[(2 * h) * tq:(2 * h + 1) * tq] - lam * p[(2 * h + 1) * tq:(2 * h + 2) * tq]
             for h in range(N_HEADS)], axis=0)
        o = _head_diag(_mm(a, vals), hv_ref, tq)
        ms = _sel_r(o * o, ones_ref[...], 2) * (1.0 / HEAD_V)
        out_ref[qi * tq:(qi + 1) * tq, :] = o * lax.rsqrt(ms + EPS) * nw_ref[...] * (1.0 - lam_init)


def _att_call(proj, lam_p, nw, cache_k, cache_v, rope, t_len, n_seq, row_blk0, lam_init):
    qm = jnp.asarray(_att_consts(), F32)
    _, hv, _, ones_bd = _head_consts(BRANCH_W)
    hv, ones_bd = jnp.asarray(hv, F32), jnp.asarray(ones_bd, BF)
    lat = cache_k is not None
    consts = (lam_p, nw, qm, hv, ones_bd)
    in_specs = [pl.BlockSpec((t_len, 768), lambda b: (row_blk0 + b, 4))]
    in_specs += [_const_spec(a) for a in consts]
    args = [proj, *consts]
    out_shape = [jax.ShapeDtypeStruct((n_seq * t_len, BRANCH_W), F32)]
    out_specs = [pl.BlockSpec((t_len, BRANCH_W), lambda b: (b, 0))]
    scratch = []
    if lat:
        cos, sin = rope
        in_specs += [_const_spec(cos), _const_spec(sin),
                     pl.BlockSpec((1, PAST_LEN, BRANCH_W), lambda b: (b, 0, 0)),
                     pl.BlockSpec((1, PAST_LEN, BRANCH_W), lambda b: (b, 0, 0))]
        args += [cos, sin, cache_k, cache_v]
        scratch = [pltpu.VMEM((t_len, BRANCH_W), F32),
                   pltpu.VMEM((PAST_LEN + t_len, BRANCH_W), F32),
                   pltpu.VMEM((PAST_LEN + t_len, BRANCH_W), F32)]
    else:
        for _ in range(2):
            out_shape.append(jax.ShapeDtypeStruct((n_seq, N_HEADS, t_len, HEAD_V), F32))
            out_specs.append(pl.BlockSpec((1, N_HEADS, t_len, HEAD_V), lambda b: (b, 0, 0, 0)))
    return pl.pallas_call(
        functools.partial(_att_kernel, t_len=t_len, lat=lat, lam_init=lam_init),
        grid=(n_seq,),
        in_specs=in_specs,
        out_specs=out_specs,
        out_shape=out_shape,
        scratch_shapes=scratch,
        compiler_params=_cparams(("arbitrary",)),
        name="diff_attention",
    )(*args)


def _block_diag_states(st, dk, transposed):
    eye = jnp.eye(N_HEADS, dtype=st.dtype)
    b = st.shape[0]
    if transposed:
        return jnp.einsum('bnhde,hg->bnhegd', st, eye).reshape(b, 2, N_HEADS * HEAD_V, N_HEADS * dk)
    return jnp.einsum('bnhde,hg->bnhdge', st, eye).reshape(b, 2, N_HEADS * dk, N_HEADS * HEAD_V)


def _in_perm():
    offs = np.concatenate([[0], np.cumsum(IN_ORIG)])
    seg = lambda a, b: np.arange(offs[a], offs[b])
    return np.concatenate([seg(0, 4), seg(5, 6), seg(8, 9), seg(9, 16), seg(4, 5), seg(6, 8)])


def kernel(x_prompt, x_sample, cache_diff_k, cache_diff_v, state_gla, state_dn, state_hgrn, c, c_ctx,
           norm_w, w_mod, b_mod, ffn1_in, ffn1_down, ffn2_in, ffn2_down, w_in, gla_w2, gla_b, gla_norm,
           dn_conv, dn_a_log, dn_dt_bias, dn_norm, hg_lb_logits, hg_norm, diff_lambda, diff_norm,
           w_branch, w_mgate, w_out, final_norm):
    x = jnp.concatenate([x_prompt.reshape(N_CTX_TOK, D_MODEL), x_sample.reshape(-1, D_MODEL)], axis=0)
    c_rows = jnp.concatenate([c_ctx[None, :], c, jnp.zeros((8 - 1 - N_LAT_SEQ, D_MODEL), F32)], axis=0)
    mod = _mod_call(c_rows, w_mod, b_mod).reshape(DEPTH, 8, N_MOD, D_MODEL)
    rope = _rope_tables()
    perm = _in_perm()
    lb_logits = hg_lb_logits.reshape(DEPTH, 2 * HG_W)
    lat_blk = N_CTX_TOK // T_LAT
    tile4 = lambda a: jnp.tile(a, N_HEADS)[None, :]
    fin = final_norm[None, :]
    new_k, new_v, new_gla, new_dn, new_hg = [], [], [], [], []
    for l in range(DEPTH):
        mod_l = mod[l]
        lam_init = 0.8 - 0.6 * math.exp(-0.3 * l)
        w_in_p = jnp.pad(w_in[l][:, perm], ((0, 0), (0, N_IN_PAD - N_IN))).astype(BF)
        w2bd = jnp.zeros((128, 2 * GLA_KW), F32)
        w2bd = w2bd.at[0:GLA_LOWRANK, 0:GLA_KW].set(gla_w2[l, 0])
        w2bd = w2bd.at[GLA_LOWRANK:2 * GLA_LOWRANK, GLA_KW:].set(gla_w2[l, 1]).astype(BF)
        gbias = gla_b[l].reshape(1, 2 * GLA_KW)
        alog_row = jnp.zeros((1, 128), F32).at[0, SMALL_DNA:SMALL_DNA + 8].set(dn_a_log[l].reshape(-1))
        dtb_row = jnp.zeros((1, 128), F32).at[0, SMALL_DNA:SMALL_DNA + 8].set(dn_dt_bias[l].reshape(-1))

        x = _ffn_call(x, mod_l, norm_w[l, 0][None, :], ffn1_in[l].astype(BF), ffn1_down[l].astype(BF),
                      fin, 0, False)
        proj = _proj_call(x, mod_l, norm_w[l, 1][None, :], w_in_p)

        a_c, st_a = _gla_call(proj, w2bd, gbias, tile4(gla_norm[l]), None, T_CTX, N_CTX_SEQ, 0)
        dn_lat_blk = N_CTX_TOK // (N_PAIR * T_LAT)
        q_c, k_c, dd_c, lw_c, vb_c, kb_c, qkd_c = _dn_build_call(proj, dn_conv[l], alog_row, dtb_row,
                                                                T_CTX, N_CTX_SEQ, 0)
        q_l, k_l, dd_l, lw_l, vb_l, kb_l, qkd_l = _dn_build_call(proj, dn_conv[l], alog_row, dtb_row,
                                                                T_LAT, N_LAT_SEQ, dn_lat_blk)
        cat = lambda a, b: jnp.concatenate([a, b], axis=1)
        dn_ut, dn_wt = _dn_solve_call(_chunks_to_lanes(cat(lw_c, lw_l)), _chunks_to_lanes(cat(vb_c, vb_l)),
                                      _chunks_to_lanes(cat(kb_c, kb_l)))
        dn_u, dn_w = _chunks_from_lanes(dn_ut), _chunks_from_lanes(dn_wt)
        b_c, st_b = _dn_scan_call(proj, q_c, k_c, dd_c, dn_u, dn_w, qkd_c, tile4(dn_norm[l]), None,
                                  T_CTX, N_CTX_SEQ, 0)
        c_c, st_c = _hgrn_call(proj, lb_logits, tile4(hg_norm[l]), None, T_CTX, N_CTX_SEQ, 0, l)
        d_c, new_k_l, new_v_l = _att_call(proj, diff_lambda[l], tile4(diff_norm[l]), None, None, None,
                                  T_CTX, N_CTX_SEQ, 0, lam_init)
        ck = cache_diff_k[:, l].transpose(0, 2, 1, 3).reshape(N_LAT_SEQ, PAST_LEN, BRANCH_W)
        cv = cache_diff_v[:, l].transpose(0, 2, 1, 3).reshape(N_LAT_SEQ, PAST_LEN, BRANCH_W)
        (a_l,) = _gla_call(proj, w2bd, gbias, tile4(gla_norm[l]),
                           _block_diag_states(state_gla[:, l], GLA_DK, True), T_LAT, N_LAT_SEQ, lat_blk)
        (b_l,) = _dn_scan_call(proj, q_l, k_l, dd_l, dn_u, dn_w, qkd_l, tile4(dn_norm[l]),
                               _block_diag_states(state_dn[:, l], DN_DK, False), T_LAT, N_LAT_SEQ, dn_lat_blk)
        (c_l,) = _hgrn_call(proj, lb_logits, tile4(hg_norm[l]),
                            _block_diag_states(state_hgrn[:, l], HG_DK, True), T_LAT, N_LAT_SEQ, lat_blk, l)
        (d_l,) = _att_call(proj, diff_lambda[l], tile4(diff_norm[l]), ck, cv, rope,
                           T_LAT, N_LAT_SEQ, lat_blk, lam_init)

        x = _merge_call(x, mod_l, norm_w[l, 1][None, :], (a_c, b_c, c_c, d_c), (a_l, b_l, c_l, d_l),
                        w_mgate[l].astype(BF), w_branch[l].astype(BF), w_out[l].astype(BF))
        x = _ffn_call(x, mod_l, norm_w[l, 2][None, :], ffn2_in[l].astype(BF), ffn2_down[l].astype(BF),
                      fin, 2, l == DEPTH - 1)
        new_k.append(new_k_l)
        new_v.append(new_v_l)
        new_gla.append(st_a)
        new_dn.append(st_b)
        new_hg.append(st_c)
    y_prompt = x[:N_CTX_TOK].reshape(N_CTX_SEQ, T_CTX, D_MODEL)
    y_sample = x[N_CTX_TOK:].reshape(N_LAT_SEQ, T_LAT, D_MODEL)
    return (y_prompt, y_sample, jnp.stack(new_k, axis=1), jnp.stack(new_v, axis=1),
            jnp.stack(new_gla, axis=1), jnp.stack(new_dn, axis=1), jnp.stack(new_hg, axis=1))
```

```python
import functools
import math

import numpy as np
import jax
import jax.numpy as jnp
from jax import lax
from jax.experimental import pallas as pl
from jax.experimental.pallas import tpu as pltpu

F32 = jnp.float32
BF = jnp.bfloat16

D_MODEL = 1024
N_CTX_SEQ = 32
T_CTX = 256
DEPTH = 2
N_LAT_SEQ = 2
T_LAT = 1024
PAST_LEN = 512
GRID_W = 64
N_HEADS = 4
BRANCH_W = 256
HEAD_V = 64
GLA_DK = 32
GLA_KW = 128
GLA_LOWRANK = 16
GLA_TAU = 16.0
DN_DK = 64
HG_DK = 64
HG_W = 256
DF_DH = 32
ROPE_BASE = 10000.0
D_FF = 2816
N_MOD = 9
CHUNK = 64
EPS = 1e-6
N_CTX_TOK = N_CTX_SEQ * T_CTX
N_TOK = N_CTX_TOK + N_LAT_SEQ * T_LAT
N_LEVELS = 6
SMALL_LEVEL = 4
N_SCAN_CONSTS = 13

IN_ORIG = (128, 128, 256, 256, 32, 768, 8, 8, 256, 256, 512, 256, 256, 256, 256, 256)
N_IN = sum(IN_ORIG)
N_IN_PAD = 3968
COL_SMALL_BLOCK = 30
SMALL_LR = 0
SMALL_DNB = 32
SMALL_DNA = 40

VMEM_LIMIT = 56 * 1024 * 1024

TM_FFN = 512
TF_FFN = 1408
TM_PROJ = 512
TM_MERGE = 512
TN_MOD = 2304
TQ_ATT = 256


def _silu(x):
    return x * (1.0 / (1.0 + jnp.exp(-x)))


def _sigmoid(x):
    return 1.0 / (1.0 + jnp.exp(-x))


def _softplus(x):
    return jnp.maximum(x, 0.0) + jnp.log(1.0 + jnp.exp(-jnp.abs(x)))


def _log_sigmoid(x):
    return -_softplus(-x)


def _mm(a, b):
    return jnp.dot(a.astype(BF), b.astype(BF), preferred_element_type=F32)


def _mm_nt(a, b):
    return lax.dot_general(a.astype(BF), b.astype(BF), (((1,), (1,)), ((), ())),
                           preferred_element_type=F32)


def _mm_tn(a, b):
    return lax.dot_general(a.astype(BF), b.astype(BF), (((0,), (0,)), ((), ())),
                           preferred_element_type=F32)


def _split(x, n):
    parts = []
    r = x
    for i in range(n):
        p = r.astype(BF)
        parts.append(p)
        if i + 1 < n:
            r = r - p.astype(F32)
    return parts


def _sel_l(m01, x, n=3):
    out = None
    for p in _split(x, n):
        t = jnp.dot(m01, p, preferred_element_type=F32)
        out = t if out is None else out + t
    return out


def _sel_r(x, m01, n=3):
    out = None
    for p in _split(x, n):
        t = jnp.dot(p, m01, preferred_element_type=F32)
        out = t if out is None else out + t
    return out


def _sel_tn(x, m01, n=3):
    out = None
    for p in _split(x, n):
        t = lax.dot_general(p, m01, (((0,), (0,)), ((), ())), preferred_element_type=F32)
        out = t if out is None else out + t
    return out


def _rms(x, w):
    return x * lax.rsqrt(jnp.mean(x * x, axis=-1, keepdims=True) + EPS) * w


def _head_lanes(x, h):
    blk = x[:, (h // 2) * 128:(h // 2 + 1) * 128]
    if h % 2:
        blk = pltpu.roll(blk, 64, 1)
    return blk[:, :HEAD_V]


def _stack_heads(x, hm_ref):
    return jnp.concatenate([x * hm_ref[h:h + 1, :] for h in range(N_HEADS)], axis=0)


def _head_diag(o_full, hv_ref, c):
    out = None
    for h in range(N_HEADS):
        t = o_full[h * c:(h + 1) * c, :] * hv_ref[h:h + 1, :]
        out = t if out is None else out + t
    return out


def _head_norm_gate(o, gate, nw, ones_bd):
    ms = _sel_r(o * o, ones_bd, 2) * (1.0 / HEAD_V)
    return o * lax.rsqrt(ms + EPS) * nw * _silu(gate)


def _mod_row(i, tm):
    return jnp.maximum(i * tm - (N_CTX_TOK - T_LAT), 0) // T_LAT


def _cparams(sem):
    return pltpu.CompilerParams(dimension_semantics=sem, vmem_limit_bytes=VMEM_LIMIT)


@functools.lru_cache(maxsize=None)
def _scan_consts(w, reverse):
    c = CHUNK
    idx = np.arange(c)
    i = idx[:, None]
    m = idx[None, :]
    pm, sg = [], []
    s = c // 2
    while s >= 1:
        par = idx // (2 * s)
        right = (idx % (2 * s)) >= s
        same = par[:, None] == par[None, :]
        query = ~right if reverse else right
        pm.append(same & query[:, None] & (~query)[None, :])
        if s >= SMALL_LEVEL:
            sg.append(np.where(query, 1.0, -1.0))
        s //= 2
    pm.append(i == m)
    tri = (m >= i) if reverse else (m <= i)
    r4, r2 = idx % 4, idx % 2
    if not reverse:
        cf = [r2 == 1, r4 >= 2, r4 == 3, r4 == 0]
    else:
        cf = [r2 == 0, r4 <= 1, r4 == 3, r4 == 0]
    wide = lambda rows: np.repeat(np.stack(rows).astype(np.float32)[:, :, None], w, axis=2)
    pmask = np.stack([np.tile(p, (1, N_HEADS)) for p in pm]).astype(np.float32)
    return tri.astype(np.float32), wide(sg), wide(cf), pmask


@functools.lru_cache(maxsize=None)
def _head_consts(w):
    dk = w // N_HEADS
    hm = np.zeros((N_HEADS, w), np.float32)
    hv = np.zeros((N_HEADS, BRANCH_W), np.float32)
    for h in range(N_HEADS):
        hm[h, h * dk:(h + 1) * dk] = 1.0
        hv[h, h * HEAD_V:(h + 1) * HEAD_V] = 1.0
    bd = hm.T @ hv
    ones_bd = hv.T @ hv
    return hm, hv, bd, ones_bd


@functools.lru_cache(maxsize=None)
def _dn_consts():
    c = CHUNK
    idx = np.arange(c)
    i = idx[:, None]
    j = idx[None, :]
    tri = np.stack([(j <= i), (j >= i)]).astype(np.float32)
    strict = np.stack([np.tile(j < i, (1, N_HEADS)), np.tile(j > i, (1, N_HEADS))]).astype(np.float32)
    eye = np.tile(np.eye(c), (1, N_HEADS)).astype(np.float32)
    blk = np.ones((c, c), np.float32)
    exb = np.zeros((128, 2 * BRANCH_W), np.float32)
    exa = np.zeros((128, 2 * BRANCH_W), np.float32)
    for n in range(2):
        for h in range(N_HEADS):
            lo = n * BRANCH_W + h * HEAD_V
            exb[SMALL_DNB + n * N_HEADS + h, lo:lo + HEAD_V] = 1.0
            exa[SMALL_DNA + n * N_HEADS + h, lo:lo + HEAD_V] = 1.0
    return tri, strict, eye, blk, exb, exa


@functools.lru_cache(maxsize=None)
def _att_consts():
    qm = np.zeros((2 * N_HEADS, BRANCH_W), np.float32)
    for h in range(N_HEADS):
        for mp in range(2):
            lo = h * HEAD_V + mp * DF_DH
            qm[2 * h + mp, lo:lo + DF_DH] = 1.0
    return qm


def _rope_tables():
    rows = T_LAT // GRID_W
    row = jnp.repeat(jnp.arange(rows), GRID_W).astype(F32)
    col = jnp.tile(jnp.arange(GRID_W), rows).astype(F32)
    half = DF_DH // 2
    inv = ROPE_BASE ** (-jnp.arange(0, half, 2, dtype=F32) / half)

    def angles(pos):
        a = pos[:, None] * inv[None, :]
        return jnp.concatenate([a, a], axis=-1)

    ang = jnp.concatenate([angles(row), angles(col)], axis=-1)
    reps = BRANCH_W // DF_DH
    return jnp.tile(jnp.cos(ang), (1, reps)), jnp.tile(jnp.sin(ang), (1, reps))


def _mod_kernel(c_ref, w_ref, b_ref, o_ref):
    a = _silu(c_ref[...])
    w = w_ref[0]
    out = None
    for ap in _split(a, 2):
        for wp in _split(w, 2):
            t = jnp.dot(ap, wp, preferred_element_type=F32)
            out = t if out is None else out + t
    o_ref[0] = out + b_ref[0]


def _mod_call(c_rows, w_mod, b_mod):
    n_t = (N_MOD * D_MODEL) // TN_MOD
    return pl.pallas_call(
        _mod_kernel,
        grid=(DEPTH, n_t),
        in_specs=[pl.BlockSpec((8, D_MODEL), lambda l, j: (0, 0)),
                  pl.BlockSpec((1, D_MODEL, TN_MOD), lambda l, j: (l, 0, j)),
                  pl.BlockSpec((1, 1, TN_MOD), lambda l, j: (l, 0, j))],
        out_specs=pl.BlockSpec((1, 8, TN_MOD), lambda l, j: (l, 0, j)),
        out_shape=jax.ShapeDtypeStruct((DEPTH, 8, N_MOD * D_MODEL), F32),
        compiler_params=_cparams(("arbitrary", "arbitrary")),
        name="mod_vectors",
    )(c_rows, w_mod, b_mod.reshape(DEPTH, 1, N_MOD * D_MODEL))


def _ffn_kernel(x_ref, mod_ref, nw_ref, wg_ref, wu_ref, wd_ref, fn_ref, o_ref, h_ref, acc_ref,
                *, sub, final):
    f = pl.program_id(1)

    @pl.when(f == 0)
    def _():
        x = x_ref[...]
        sh = mod_ref[0, 3 * sub:3 * sub + 1, :]
        sc = mod_ref[0, 3 * sub + 1:3 * sub + 2, :]
        h_ref[...] = (_rms(x, nw_ref[...]) * (1.0 + sc) + sh).astype(BF)
        acc_ref[...] = jnp.zeros_like(acc_ref)

    h = h_ref[...]
    g = jnp.dot(h, wg_ref[...], preferred_element_type=F32)
    u = jnp.dot(h, wu_ref[...], preferred_element_type=F32)
    acc_ref[...] += jnp.dot((_silu(g) * u).astype(BF), wd_ref[...], preferred_element_type=F32)

    @pl.when(f == pl.num_programs(1) - 1)
    def _():
        ga = mod_ref[0, 3 * sub + 2:3 * sub + 3, :]
        y = x_ref[...] + 0.5 * ga * acc_ref[...]
        if final:
            y = _rms(y, fn_ref[...])
        o_ref[...] = y


def _ffn_call(x, mod_l, nw, w_in, w_down, final_w, sub, final):
    n_f = D_FF // TF_FFN
    tm = TM_FFN
    return pl.pallas_call(
        functools.partial(_ffn_kernel, sub=sub, final=final),
        grid=(N_TOK // tm, n_f),
        in_specs=[pl.BlockSpec((tm, D_MODEL), lambda i, f: (i, 0)),
                  pl.BlockSpec((1, N_MOD, D_MODEL), lambda i, f: (_mod_row(i, tm), 0, 0)),
                  pl.BlockSpec((1, D_MODEL), lambda i, f: (0, 0)),
                  pl.BlockSpec((D_MODEL, TF_FFN), lambda i, f: (0, f)),
                  pl.BlockSpec((D_MODEL, TF_FFN), lambda i, f: (0, n_f + f)),
                  pl.BlockSpec((TF_FFN, D_MODEL), lambda i, f: (f, 0)),
                  pl.BlockSpec((1, D_MODEL), lambda i, f: (0, 0))],
        out_specs=pl.BlockSpec((tm, D_MODEL), lambda i, f: (i, 0)),
        out_shape=jax.ShapeDtypeStruct((N_TOK, D_MODEL), F32),
        scratch_shapes=[pltpu.VMEM((tm, D_MODEL), BF), pltpu.VMEM((tm, D_MODEL), F32)],
        compiler_params=_cparams(("arbitrary", "arbitrary")),
        name="swiglu_half_step",
    )(x, mod_l, nw, w_in, w_in, w_down, final_w)


def _proj_kernel(x_ref, mod_ref, nw_ref, w_ref, o_ref):
    sh = mod_ref[0, 3:4, :]
    sc = mod_ref[0, 4:5, :]
    h = (_rms(x_ref[...], nw_ref[...]) * (1.0 + sc) + sh).astype(BF)
    o_ref[...] = jnp.dot(h, w_ref[...], preferred_element_type=F32)


def _proj_call(x, mod_l, nw, w_in_p):
    tm = TM_PROJ
    return pl.pallas_call(
        _proj_kernel,
        grid=(N_TOK // tm,),
        in_specs=[pl.BlockSpec((tm, D_MODEL), lambda i: (i, 0)),
                  pl.BlockSpec((1, N_MOD, D_MODEL), lambda i: (_mod_row(i, tm), 0, 0)),
                  pl.BlockSpec((1, D_MODEL), lambda i: (0, 0)),
                  pl.BlockSpec((D_MODEL, N_IN_PAD), lambda i: (0, 0))],
        out_specs=pl.BlockSpec((tm, N_IN_PAD), lambda i: (i, 0)),
        out_shape=jax.ShapeDtypeStruct((N_TOK, N_IN_PAD), F32),
        compiler_params=_cparams(("arbitrary",)),
        name="mixer_in_proj",
    )(x, mod_l, nw, w_in_p)


def _merge_kernel(x_ref, mod_ref, nw_ref, *rest):
    ctx_refs, lat_refs = rest[0:4], rest[4:8]
    wg_ref, wb_ref, wo_ref, o_ref = rest[8:12]
    x = x_ref[...]
    sh = mod_ref[0, 3:4, :]
    sc = mod_ref[0, 4:5, :]
    ga = mod_ref[0, 5:6, :]
    h = (_rms(x, nw_ref[...]) * (1.0 + sc) + sh).astype(BF)
    is_lat = pl.program_id(0) >= N_CTX_TOK // TM_MERGE
    mixed = None
    for n in range(4):
        gate = _sigmoid(jnp.dot(h, wg_ref[:, n * D_MODEL:(n + 1) * D_MODEL], preferred_element_type=F32))
        br = jnp.where(is_lat, lat_refs[n][...], ctx_refs[n][...])
        up = jnp.dot(br.astype(BF), wb_ref[n], preferred_element_type=F32)
        mixed = gate * up if mixed is None else mixed + gate * up
    out = jnp.dot(mixed.astype(BF), wo_ref[...], preferred_element_type=F32)
    o_ref[...] = x + ga * out


def _merge_call(x, mod_l, nw, ctx_branches, lat_branches, w_mgate, w_branch, w_out):
    tm = TM_MERGE
    n_ctx = N_CTX_TOK // tm
    cspec = pl.BlockSpec((tm, BRANCH_W), lambda i: (jnp.minimum(i, n_ctx - 1), 0))
    lspec = pl.BlockSpec((tm, BRANCH_W), lambda i: (jnp.maximum(i - n_ctx, 0), 0))
    return pl.pallas_call(
        _merge_kernel,
        grid=(N_TOK // tm,),
        in_specs=[pl.BlockSpec((tm, D_MODEL), lambda i: (i, 0)),
                  pl.BlockSpec((1, N_MOD, D_MODEL), lambda i: (_mod_row(i, tm), 0, 0)),
                  pl.BlockSpec((1, D_MODEL), lambda i: (0, 0)),
                  cspec, cspec, cspec, cspec, lspec, lspec, lspec, lspec,
                  pl.BlockSpec((D_MODEL, 4 * D_MODEL), lambda i: (0, 0)),
                  pl.BlockSpec((4, BRANCH_W, D_MODEL), lambda i: (0, 0, 0)),
                  pl.BlockSpec((D_MODEL, D_MODEL), lambda i: (0, 0))],
        out_specs=pl.BlockSpec((tm, D_MODEL), lambda i: (i, 0)),
        out_shape=jax.ShapeDtypeStruct((N_TOK, D_MODEL), F32),
        compiler_params=_cparams(("arbitrary",)),
        name="gated_merge",
    )(x, mod_l, nw, *ctx_branches, *lat_branches, w_mgate, w_branch, w_out)


def _gated_chunk(qc, kc, vc, gc, st, tri_ref, sg_ref, cf_ref, pm_ref, hm_ref, hv_ref, bdt_ref, reverse):
    c = CHUNK
    w = gc.shape[1]
    cum = _sel_l(tri_ref[...], gc, 2)
    kst = _stack_heads(kc, hm_ref).astype(BF)
    qb = qc.astype(BF)
    a = pm_ref[N_LEVELS] * _mm_nt(qb, kst)
    lv = 0
    s = c // 2
    while s >= 1:
        if s >= SMALL_LEVEL:
            first = s if reverse else s - 1
            bnd = [jnp.broadcast_to(cum[p * 2 * s + first:p * 2 * s + first + 1, :], (2 * s, w))
                   for p in range(c // (2 * s))]
            bnd = jnp.concatenate(bnd, axis=0) if len(bnd) > 1 else bnd[0]
            dl = (cum - bnd) * sg_ref[lv]
        elif s == 2:
            dl = (gc * cf_ref[1] + pltpu.roll(gc, 1, 0) * cf_ref[2] + pltpu.roll(gc, c - 1, 0) * cf_ref[3])
        else:
            dl = gc * cf_ref[0]
        el = jnp.exp(dl).astype(BF)
        a = a + pm_ref[lv] * _mm_nt(qb * el, kst * jnp.concatenate([el] * N_HEADS, axis=0))
        lv += 1
        s //= 2
    tot = cum[0:1, :] if reverse else cum[c - 1:c, :]
    o = _mm(a, _stack_heads(vc, hv_ref)) + _mm_nt(qc * jnp.exp(cum), st)
    st_new = st * jnp.exp(tot) + bdt_ref[...] * _mm_tn(vc, kc * jnp.exp(tot - cum))
    return o, st_new


def _gated_scan(q_ref, kf_ref, kb_ref, v_ref, gf_ref, gb_ref, of_ref, ob_ref, s0f, s0b, cf, cb, n_chunks):
    def body(t, carry):
        sf, sb = carry
        rf = pl.ds(pl.multiple_of(t * CHUNK, CHUNK), CHUNK)
        rb = pl.ds(pl.multiple_of((n_chunks - 1 - t) * CHUNK, CHUNK), CHUNK)
        o_f, sf = _gated_chunk(q_ref[rf, :], kf_ref[rf, :], v_ref[rf, :], gf_ref[rf, :], sf, *cf, False)
        o_b, sb = _gated_chunk(q_ref[rb, :], kb_ref[rb, :], v_ref[rb, :], gb_ref[rb, :], sb, *cb, True)
        of_ref[rf, :] = o_f
        ob_ref[rb, :] = o_b
        return sf, sb

    return lax.fori_loop(0, n_chunks, body, (s0f, s0b))


def _write_states(st_ref, st_f, st_b, dk, eye_ref):
    for d, st in enumerate((st_f, st_b)):
        for h in range(N_HEADS):
            tr = _sel_tn(st[h * HEAD_V:(h + 1) * HEAD_V, :], eye_ref[...])
            st_ref[0, d, h] = tr[h * dk:(h + 1) * dk, :]


def _gla_kernel(*refs, t_len, has_state):
    blk_ref, small_ref, w2_ref, gb_ref, nw_ref = refs[:5]
    pos = 5 + N_SCAN_CONSTS
    cf, cb, ones_ref, eye_ref = _split_scan_consts(refs[5:pos])
    if has_state:
        s0_ref = refs[pos]
        pos += 1
    out_ref = refs[pos]
    pos += 1
    if not has_state:
        st_ref = refs[pos]
        pos += 1
    q_s, gf_s, gb_s, of_s, ob_s = refs[pos:pos + 5]

    q_s[...] = blk_ref[:, 0:128] * (GLA_DK ** -0.5)
    z = _mm(small_ref[...], w2_ref[...]) + gb_ref[...]
    g = _log_sigmoid(z) * (1.0 / GLA_TAU)
    gf_s[...] = g[:, 0:128]
    gb_s[...] = g[:, 128:256]
    k_ref = blk_ref.at[:, 128:256]
    v_ref = blk_ref.at[:, 256:512]
    n_chunks = t_len // CHUNK
    if has_state:
        s0f, s0b = s0_ref[0, 0], s0_ref[0, 1]
    else:
        s0f = s0b = jnp.zeros((BRANCH_W, GLA_KW), F32)
    s_f, s_b = _gated_scan(q_s, k_ref, k_ref, v_ref, gf_s, gb_s, of_s, ob_s, s0f, s0b, cf, cb, n_chunks)
    out_ref[...] = _head_norm_gate(of_s[...] + ob_s[...], blk_ref[:, 512:768], nw_ref[...], ones_ref[...])
    if not has_state:
        _write_states(st_ref, s_f, s_b, GLA_DK, eye_ref)


def _const_spec(a):
    nd = a.ndim
    return pl.BlockSpec(a.shape, lambda b, _n=nd: (0,) * _n)


def _scan_const_arrays(w):
    out = []
    for reverse in (False, True):
        tri, sg, cf, pm = _scan_consts(w, reverse)
        out += [jnp.asarray(tri, BF), jnp.asarray(sg, F32), jnp.asarray(cf, F32), jnp.asarray(pm, F32)]
    hm, hv, bd, ones_bd = _head_consts(w)
    out += [jnp.asarray(hm, F32), jnp.asarray(hv, F32), jnp.asarray(bd.T, F32), jnp.asarray(ones_bd, BF),
            jnp.eye(HEAD_V, dtype=BF)]
    return tuple(out)


def _split_scan_consts(refs):
    shared = tuple(refs[8:11])
    return tuple(refs[0:4]) + shared, tuple(refs[4:8]) + shared, refs[11], refs[12]


def _gla_call(proj, w2bd, gbias, nw, s0, t_len, n_seq, row_blk0):
    has_state = s0 is not None
    consts = (w2bd, gbias, nw) + _scan_const_arrays(GLA_KW)
    in_specs = [pl.BlockSpec((t_len, 768), lambda b: (row_blk0 + b, 0)),
                pl.BlockSpec((t_len, 128), lambda b: (row_blk0 + b, COL_SMALL_BLOCK))]
    in_specs += [_const_spec(a) for a in consts]
    args = [proj, proj, *consts]
    if has_state:
        in_specs.append(pl.BlockSpec((1, 2, BRANCH_W, GLA_KW), lambda b: (b, 0, 0, 0)))
        args.append(s0)
    out_shape = [jax.ShapeDtypeStruct((n_seq * t_len, BRANCH_W), F32)]
    out_specs = [pl.BlockSpec((t_len, BRANCH_W), lambda b: (b, 0))]
    if not has_state:
        out_shape.append(jax.ShapeDtypeStruct((n_seq, 2, N_HEADS, GLA_DK, HEAD_V), F32))
        out_specs.append(pl.BlockSpec((1, 2, N_HEADS, GLA_DK, HEAD_V), lambda b: (b, 0, 0, 0, 0)))
    return pl.pallas_call(
        functools.partial(_gla_kernel, t_len=t_len, has_state=has_state),
        grid=(n_seq,),
        in_specs=in_specs,
        out_specs=out_specs,
        out_shape=out_shape,
        scratch_shapes=[pltpu.VMEM((t_len, GLA_KW), F32)] * 3 + [pltpu.VMEM((t_len, BRANCH_W), F32)] * 2,
        compiler_params=_cparams(("arbitrary",)),
        name="gla_mixer",
    )(*args)


def _hgrn_kernel(*refs, t_len, has_state, layer):
    q_ref, f_ref, v_ref, gate_ref, lbl_ref, nw_ref = refs[:6]
    pos = 6 + N_SCAN_CONSTS
    cf, cb, ones_ref, eye_ref = _split_scan_consts(refs[6:pos])
    if has_state:
        s0_ref = refs[pos]
        pos += 1
    out_ref = refs[pos]
    pos += 1
    if not has_state:
        st_ref = refs[pos]
        pos += 1
    q_s, kf_s, kb_s, gf_s, gb_s, of_s, ob_s = refs[pos:pos + 7]

    lg = lbl_ref[...]
    mx = jnp.max(lg, axis=0, keepdims=True)
    ex = jnp.exp(lg - mx)
    p = ex / jnp.sum(ex, axis=0, keepdims=True)
    lb = jnp.sum(p[0:layer + 1], axis=0, keepdims=True) - p[0:1]

    q_s[...] = _silu(q_ref[...]) * (HG_DK ** -0.5)
    f = lb + (1.0 - lb) * _sigmoid(f_ref[...])
    kf_s[...] = 1.0 - f[:, 0:HG_W]
    kb_s[...] = 1.0 - f[:, HG_W:2 * HG_W]
    lf = jnp.log(f)
    gf_s[...] = lf[:, 0:HG_W]
    gb_s[...] = lf[:, HG_W:2 * HG_W]
    n_chunks = t_len // CHUNK
    if has_state:
        s0f, s0b = s0_ref[0, 0], s0_ref[0, 1]
    else:
        s0f = s0b = jnp.zeros((BRANCH_W, HG_W), F32)
    s_f, s_b = _gated_scan(q_s, kf_s, kb_s, v_ref, gf_s, gb_s, of_s, ob_s, s0f, s0b, cf, cb, n_chunks)
    out_ref[...] = _head_norm_gate(of_s[...] + ob_s[...], gate_ref[...], nw_ref[...], ones_ref[...])
    if not has_state:
        _write_states(st_ref, s_f, s_b, HG_DK, eye_ref)


def _hgrn_call(proj, lb_logits, nw, s0, t_len, n_seq, row_blk0, layer):
    has_state = s0 is not None
    consts = (lb_logits, nw) + _scan_const_arrays(HG_W)
    in_specs = [pl.BlockSpec((t_len, 256), lambda b: (row_blk0 + b, 7)),
                pl.BlockSpec((t_len, 512), lambda b: (row_blk0 + b, 4)),
                pl.BlockSpec((t_len, 256), lambda b: (row_blk0 + b, 10)),
                pl.BlockSpec((t_len, 256), lambda b: (row_blk0 + b, 11))]
    in_specs += [_const_spec(a) for a in consts]
    args = [proj, proj, proj, proj, *consts]
    if has_state:
        in_specs.append(pl.BlockSpec((1, 2, BRANCH_W, HG_W), lambda b: (b, 0, 0, 0)))
        args.append(s0)
    out_shape = [jax.ShapeDtypeStruct((n_seq * t_len, BRANCH_W), F32)]
    out_specs = [pl.BlockSpec((t_len, BRANCH_W), lambda b: (b, 0))]
    if not has_state:
        out_shape.append(jax.ShapeDtypeStruct((n_seq, 2, N_HEADS, HG_DK, HEAD_V), F32))
        out_specs.append(pl.BlockSpec((1, 2, N_HEADS, HG_DK, HEAD_V), lambda b: (b, 0, 0, 0, 0)))
    return pl.pallas_call(
        functools.partial(_hgrn_kernel, t_len=t_len, has_state=has_state, layer=layer),
        grid=(n_seq,),
        in_specs=in_specs,
        out_specs=out_specs,
        out_shape=out_shape,
        scratch_shapes=[pltpu.VMEM((t_len, HG_W), F32)] * 5 + [pltpu.VMEM((t_len, BRANCH_W), F32)] * 2,
        compiler_params=_cparams(("arbitrary",)),
        name="hgrn2_mixer",
    )(*args)


N_PAIR = 2
SOLVE_JB = 8


def _dn_solve_kernel(lt_ref, vb_ref, kb_ref, u_ref, w_ref):
    c = CHUNK
    rev = pl.program_id(1)
    u_ref[...] = jnp.zeros_like(u_ref)
    w_ref[...] = jnp.zeros_like(w_ref)

    def outer(t, carry):
        i = t + rev * (c - 1 - 2 * t)

        def inner(jb, acc):
            au, aw = acc
            j0 = pl.multiple_of(jb * SOLVE_JB, SOLVE_JB)
            for r in range(SOLVE_JB):
                coef = lt_ref[0, i, pl.ds(j0 + r, 1), :]
                au = au - coef * u_ref[0, j0 + r]
                aw = aw - coef * w_ref[0, j0 + r]
            return au, aw

        blk = lax.shift_right_logical(i, 3)
        lo = rev * blk
        hi = blk + 1 + rev * (c // SOLVE_JB - blk - 1)
        au, aw = lax.fori_loop(lo, hi, inner, (vb_ref[0, i], kb_ref[0, i]))
        u_ref[0, i] = au
        w_ref[0, i] = aw
        return carry

    lax.fori_loop(0, c, outer, 0)


def _dn_solve_call(lt, vbt, kbt):
    n_chunk = lt.shape[-1]
    spec = pl.BlockSpec((1, CHUNK, HEAD_V, n_chunk), lambda h, d: (d, 0, h, 0))
    shape = jax.ShapeDtypeStruct((2, CHUNK, BRANCH_W, n_chunk), F32)
    return pl.pallas_call(
        _dn_solve_kernel,
        grid=(N_HEADS, 2),
        in_specs=[spec, spec, spec],
        out_specs=[spec, spec],
        out_shape=[shape, shape],
        compiler_params=_cparams(("arbitrary", "arbitrary")),
        name="deltanet_solve",
    )(lt, vbt, kbt)


def _chunks_to_lanes(a):
    return a.reshape(2, a.shape[1] // CHUNK, CHUNK, BRANCH_W).transpose(0, 2, 3, 1)


def _chunks_from_lanes(a):
    return a.transpose(0, 3, 1, 2).reshape(2, -1, BRANCH_W)


def _dn_build_kernel(x_ref, small_ref, cw_ref, alog_ref, dtb_ref,
                     tri_ref, strict_ref, eye_ref, blk_ref, exb_ref, exa_ref, hm_ref, ones_ref,
                     q_ref, k_ref, d_ref, l_ref, vb_ref, kb_ref, qkd_ref, v_ref, be_s, ge_s, *, t_len):
    n_rows = N_PAIR * t_len

    x = x_ref[...]
    row = lax.broadcasted_iota(jnp.int32, (n_rows, 1), 0) % t_len
    x_prev = jnp.where(row == 0, 0.0, pltpu.roll(x, 1, 0))
    x_next = jnp.where(row == t_len - 1, 0.0, pltpu.roll(x, n_rows - 1, 0))
    y = _silu(x_prev * cw_ref[0:1, :] + x * cw_ref[1:2, :] + x_next * cw_ref[2:3, :])
    cq, ck = y[:, 0:256], y[:, 256:512]
    v_ref[...] = y[:, 512:768]
    q_ref[...] = cq * lax.rsqrt(_sel_r(cq * cq, ones_ref[...]) + EPS) * (DN_DK ** -0.5)
    k_ref[...] = ck * lax.rsqrt(_sel_r(ck * ck, ones_ref[...]) + EPS)
    sm = small_ref[...]
    be_s[...] = _sel_r(_sigmoid(sm), exb_ref[...])
    ge_s[...] = _sel_r(-jnp.exp(alog_ref[...]) * _softplus(sm + dtb_ref[...]), exa_ref[...])

    def body(ci, carry):
        rows = pl.ds(pl.multiple_of(ci * CHUNK, CHUNK), CHUNK)
        qc, kc, vc = q_ref[rows, :], k_ref[rows, :], v_ref[rows, :]
        kst = _stack_heads(kc, hm_ref)
        kk = _mm_nt(kc, kst)
        qk = _mm_nt(qc, kst)
        for d in range(2):
            bexp = be_s[rows, d * BRANCH_W:(d + 1) * BRANCH_W]
            dexp = _sel_l(tri_ref[d], ge_s[rows, d * BRANCH_W:(d + 1) * BRANCH_W])
            drow = _sel_l(blk_ref[...], dexp * eye_ref[...])
            dec_s = jnp.exp(jnp.where(strict_ref[d] > 0.5, dexp - drow, -1e30))
            d_ref[d, rows, :] = dexp
            l_ref[d, rows, :] = bexp * kk * dec_s
            vb_ref[d, rows, :] = vc * bexp
            kb_ref[d, rows, :] = kc * bexp * jnp.exp(dexp)
            qkd_ref[d, rows, :] = (qk * (dec_s + eye_ref[...])).astype(BF)
        return carry

    lax.fori_loop(0, n_rows // CHUNK, body, 0)


def _dn_build_call(proj, conv_w, alog_row, dtb_row, t_len, n_seq, row_blk0):
    tri, strict, eye, blk, exb, exa = _dn_consts()
    tri, blk, exb, exa = (jnp.asarray(a, BF) for a in (tri, blk, exb, exa))
    strict, eye = jnp.asarray(strict, F32), jnp.asarray(eye, F32)
    hm, _, _, ones_bd = _head_consts(BRANCH_W)
    hm, ones_bd = jnp.asarray(hm, F32), jnp.asarray(ones_bd, BF)
    n_rows = N_PAIR * t_len
    n_tok = n_seq * t_len
    consts = (conv_w, alog_row, dtb_row, tri, strict, eye, blk, exb, exa, hm, ones_bd)
    in_specs = [pl.BlockSpec((n_rows, 768), lambda b: (row_blk0 + b, 1)),
                pl.BlockSpec((n_rows, 128), lambda b: (row_blk0 + b, COL_SMALL_BLOCK))]
    in_specs += [_const_spec(a) for a in consts]
    tok_spec = pl.BlockSpec((n_rows, BRANCH_W), lambda b: (b, 0))
    dir_spec = pl.BlockSpec((2, n_rows, BRANCH_W), lambda b: (0, b, 0))
    tok = jax.ShapeDtypeStruct((n_tok, BRANCH_W), F32)
    per_dir = jax.ShapeDtypeStruct((2, n_tok, BRANCH_W), F32)
    return pl.pallas_call(
        functools.partial(_dn_build_kernel, t_len=t_len),
        grid=(n_seq // N_PAIR,),
        in_specs=in_specs,
        out_specs=[tok_spec] * 2 + [dir_spec] * 5,
        out_shape=[tok] * 2 + [per_dir] * 4 + [jax.ShapeDtypeStruct((2, n_tok, BRANCH_W), BF)],
        scratch_shapes=[pltpu.VMEM((n_rows, BRANCH_W), F32)] + [pltpu.VMEM((n_rows, 2 * BRANCH_W), F32)] * 2,
        compiler_params=_cparams(("arbitrary",)),
        name="deltanet_build",
    )(proj, proj, *consts)


def _dn_scan_kernel(*refs, t_len, has_state):
    (q_ref, k_ref, d_ref, u_ref, w_ref, qkd_ref, gate_ref, nw_ref, hv_ref, bd_ref, ones_ref) = refs[:11]
    pos = 11
    if has_state:
        s0_ref = refs[pos]
        pos += 1
    out_ref = refs[pos]
    pos += 1
    if not has_state:
        st_ref = refs[pos]
        pos += 1
    of_s, ob_s = refs[pos:pos + 2]
    c = CHUNK
    n_chunks = t_len // c

    def step(s, rows, d, o_s):
        qc, kc = q_ref[rows, :], k_ref[rows, :]
        dexp = d_ref[d, rows, :]
        v_new = u_ref[d, rows, :] - _mm(w_ref[d, rows, :], s)
        o_s[rows, :] = _mm(qc * jnp.exp(dexp), s) + _mm(qkd_ref[d, rows, :], _stack_heads(v_new, hv_ref))
        dl = dexp[0:1, :] if d == 1 else dexp[c - 1:c, :]
        return s * jnp.exp(dl) + bd_ref[...] * _mm_tn(kc * jnp.exp(dl - dexp), v_new)

    def body(t, carry):
        new = []
        for sq in range(N_PAIR):
            rf = pl.ds(pl.multiple_of(sq * t_len + t * c, c), c)
            rb = pl.ds(pl.multiple_of(sq * t_len + (n_chunks - 1 - t) * c, c), c)
            new.append(step(carry[2 * sq], rf, 0, of_s))
            new.append(step(carry[2 * sq + 1], rb, 1, ob_s))
        return tuple(new)

    if has_state:
        init = tuple(s0_ref[sq, d] for sq in range(N_PAIR) for d in range(2))
    else:
        init = tuple(jnp.zeros((BRANCH_W, BRANCH_W), F32) for _ in range(2 * N_PAIR))
    finals = lax.fori_loop(0, n_chunks, body, init)
    out_ref[...] = _head_norm_gate(of_s[...] + ob_s[...], gate_ref[...], nw_ref[...], ones_ref[...])
    if not has_state:
        for sq in range(N_PAIR):
            for d in range(2):
                for h in range(N_HEADS):
                    st_ref[sq, d, h] = _head_lanes(finals[2 * sq + d][h * DN_DK:(h + 1) * DN_DK, :], h)


def _dn_scan_call(proj, q, k, dd, u, w, qkd, nw, s0, t_len, n_seq, row_blk0):
    _, hv, bd, ones_bd = _head_consts(BRANCH_W)
    hv, bd, ones_bd = jnp.asarray(hv, F32), jnp.asarray(bd, F32), jnp.asarray(ones_bd, BF)
    has_state = s0 is not None
    n_rows = N_PAIR * t_len
    consts = (nw, hv, bd, ones_bd)
    tok_spec = pl.BlockSpec((n_rows, BRANCH_W), lambda b: (b, 0))
    dir_spec = pl.BlockSpec((2, n_rows, BRANCH_W), lambda b: (0, b, 0))
    all_spec = pl.BlockSpec((2, n_rows, BRANCH_W), lambda b: (0, row_blk0 + b, 0))
    in_specs = [tok_spec, tok_spec, dir_spec, all_spec, all_spec, dir_spec,
                pl.BlockSpec((n_rows, 256), lambda b: (row_blk0 + b, 6))]
    in_specs += [_const_spec(a) for a in consts]
    args = [q, k, dd, u, w, qkd, proj, *consts]
    if has_state:
        in_specs.append(pl.BlockSpec((N_PAIR, 2, BRANCH_W, BRANCH_W), lambda b: (b, 0, 0, 0)))
        args.append(s0)
    out_shape = [jax.ShapeDtypeStruct((n_seq * t_len, BRANCH_W), F32)]
    out_specs = [pl.BlockSpec((n_rows, BRANCH_W), lambda b: (b, 0))]
    if not has_state:
        out_shape.append(jax.ShapeDtypeStruct((n_seq, 2, N_HEADS, DN_DK, HEAD_V), F32))
        out_specs.append(pl.BlockSpec((N_PAIR, 2, N_HEADS, DN_DK, HEAD_V), lambda b: (b, 0, 0, 0, 0)))
    return pl.pallas_call(
        functools.partial(_dn_scan_kernel, t_len=t_len, has_state=has_state),
        grid=(n_seq // N_PAIR,),
        in_specs=in_specs,
        out_specs=out_specs,
        out_shape=out_shape,
        scratch_shapes=[pltpu.VMEM((n_rows, BRANCH_W), F32)] * 2,
        compiler_params=_cparams(("arbitrary",)),
        name="deltanet_scan",
    )(*args)


def _rope(x, cos, sin):
    lane = lax.broadcasted_iota(jnp.int32, x.shape, 1) % 16
    n = x.shape[1]
    xrot = jnp.where(lane < 8, -pltpu.roll(x, n - 8, 1), pltpu.roll(x, 8, 1))
    return x * cos + xrot * sin


def _att_kernel(*refs, t_len, lat, lam_init):
    blk_ref, lam_ref, nw_ref, qm_ref, hv_ref, ones_ref = refs[:6]
    pos = 6
    if lat:
        cos_ref, sin_ref, ck_ref, cv_ref = refs[pos:pos + 4]
        pos += 4
    out_ref = refs[pos]
    pos += 1
    if not lat:
        nk_ref, nv_ref = refs[pos:pos + 2]
        pos += 2
    if lat:
        q_s, k_s, v_s = refs[pos:pos + 3]

    lv = lam_ref[...]
    lam = (jnp.exp(jnp.sum(lv[0:1] * lv[1:2], axis=1, keepdims=True))
           - jnp.exp(jnp.sum(lv[2:3] * lv[3:4], axis=1, keepdims=True)) + lam_init)
    q = blk_ref[:, 0:256]
    k = blk_ref[:, 256:512]
    v = blk_ref[:, 512:768]
    if lat:
        cos, sin = cos_ref[...], sin_ref[...]
        q_s[...] = _rope(q, cos, sin)
        k_s[0:PAST_LEN, :] = ck_ref[0]
        k_s[PAST_LEN:PAST_LEN + t_len, :] = _rope(k, cos, sin)
        v_s[0:PAST_LEN, :] = cv_ref[0]
        v_s[PAST_LEN:PAST_LEN + t_len, :] = v
        keys = k_s[...].astype(BF)
        vals = v_s[...].astype(BF)
    else:
        keys = k.astype(BF)
        vals = v.astype(BF)
        for h in range(N_HEADS):
            nk_ref[0, h] = _head_lanes(k, h)
            nv_ref[0, h] = _head_lanes(v, h)
    tq = TQ_ATT
    scale = DF_DH ** -0.5
    for qi in range(t_len // tq):
        qt = q_s[qi * tq:(qi + 1) * tq, :] if lat else q[qi * tq:(qi + 1) * tq, :]
        qs = jnp.concatenate([qt * qm_ref[r:r + 1, :] for r in range(2 * N_HEADS)], axis=0)
        s = _mm_nt(qs, keys) * scale
        s = jnp.exp(s - jnp.max(s, axis=1, keepdims=True))
        p = s / jnp.sum(s, axis=1, keepdims=True)
        a = jnp.concatenate(
            [p[(2 * h) * tq:(2 * h + 1) * tq] - lam * p[(2 * h + 1) * tq:(2 * h + 2) * tq]
             for h in range(N_HEADS)], axis=0)
        o = _head_diag(_mm(a, vals), hv_ref, tq)
        ms = _sel_r(o * o, ones_ref[...], 2) * (1.0 / HEAD_V)
        out_ref[qi * tq:(qi + 1) * tq, :] = o * lax.rsqrt(ms + EPS) * nw_ref[...] * (1.0 - lam_init)


def _att_call(proj, lam_p, nw, cache_k, cache_v, rope, t_len, n_seq, row_blk0, lam_init):
    qm = jnp.asarray(_att_consts(), F32)
    _, hv, _, ones_bd = _head_consts(BRANCH_W)
    hv, ones_bd = jnp.asarray(hv, F32), jnp.asarray(ones_bd, BF)
    lat = cache_k is not None
    consts = (lam_p, nw, qm, hv, ones_bd)
    in_specs = [pl.BlockSpec((t_len, 768), lambda b: (row_blk0 + b, 4))]
    in_specs += [_const_spec(a) for a in consts]
    args = [proj, *consts]
    out_shape = [jax.ShapeDtypeStruct((n_seq * t_len, BRANCH_W), F32)]
    out_specs = [pl.BlockSpec((t_len, BRANCH_W), lambda b: (b, 0))]
    scratch = []
    if lat:
        cos, sin = rope
        in_specs += [_const_spec(cos), _const_spec(sin),
                     pl.BlockSpec((1, PAST_LEN, BRANCH_W), lambda b: (b, 0, 0)),
                     pl.BlockSpec((1, PAST_LEN, BRANCH_W), lambda b: (b, 0, 0))]
        args += [cos, sin, cache_k, cache_v]
        scratch = [pltpu.VMEM((t_len, BRANCH_W), F32),
                   pltpu.VMEM((PAST_LEN + t_len, BRANCH_W), F32),
                   pltpu.VMEM((PAST_LEN + t_len, BRANCH_W), F32)]
    else:
        for _ in range(2):
            out_shape.append(jax.ShapeDtypeStruct((n_seq, N_HEADS, t_len, HEAD_V), F32))
            out_specs.append(pl.BlockSpec((1, N_HEADS, t_len, HEAD_V), lambda b: (b, 0, 0, 0)))
    return pl.pallas_call(
        functools.partial(_att_kernel, t_len=t_len, lat=lat, lam_init=lam_init),
        grid=(n_seq,),
        in_specs=in_specs,
        out_specs=out_specs,
        out_shape=out_shape,
        scratch_shapes=scratch,
        compiler_params=_cparams(("arbitrary",)),
        name="diff_attention",
    )(*args)


def _block_diag_states(st, dk, transposed):
    eye = jnp.eye(N_HEADS, dtype=st.dtype)
    b = st.shape[0]
    if transposed:
        return jnp.einsum('bnhde,hg->bnhegd', st, eye).reshape(b, 2, N_HEADS * HEAD_V, N_HEADS * dk)
    return jnp.einsum('bnhde,hg->bnhdge', st, eye).reshape(b, 2, N_HEADS * dk, N_HEADS * HEAD_V)


def _in_perm():
    offs = np.concatenate([[0], np.cumsum(IN_ORIG)])
    seg = lambda a, b: np.arange(offs[a], offs[b])
    return np.concatenate([seg(0, 4), seg(5, 6), seg(8, 9), seg(9, 16), seg(4, 5), seg(6, 8)])


def kernel(x_prompt, x_sample, cache_diff_k, cache_diff_v, state_gla, state_dn, state_hgrn, c, c_ctx,
           norm_w, w_mod, b_mod, ffn1_in, ffn1_down, ffn2_in, ffn2_down, w_in, gla_w2, gla_b, gla_norm,
           dn_conv, dn_a_log, dn_dt_bias, dn_norm, hg_lb_logits, hg_norm, diff_lambda, diff_norm,
           w_branch, w_mgate, w_out, final_norm):
    x = jnp.concatenate([x_prompt.reshape(N_CTX_TOK, D_MODEL), x_sample.reshape(-1, D_MODEL)], axis=0)
    c_rows = jnp.concatenate([c_ctx[None, :], c, jnp.zeros((8 - 1 - N_LAT_SEQ, D_MODEL), F32)], axis=0)
    mod = _mod_call(c_rows, w_mod, b_mod).reshape(DEPTH, 8, N_MOD, D_MODEL)
    rope = _rope_tables()
    perm = _in_perm()
    lb_logits = hg_lb_logits.reshape(DEPTH, 2 * HG_W)
    lat_blk = N_CTX_TOK // T_LAT
    tile4 = lambda a: jnp.tile(a, N_HEADS)[None, :]
    fin = final_norm[None, :]
    new_k, new_v, new_gla, new_dn, new_hg = [], [], [], [], []
    for l in range(DEPTH):
        mod_l = mod[l]
        lam_init = 0.8 - 0.6 * math.exp(-0.3 * l)
        w_in_p = jnp.pad(w_in[l][:, perm], ((0, 0), (0, N_IN_PAD - N_IN))).astype(BF)
        w2bd = jnp.zeros((128, 2 * GLA_KW), F32)
        w2bd = w2bd.at[0:GLA_LOWRANK, 0:GLA_KW].set(gla_w2[l, 0])
        w2bd = w2bd.at[GLA_LOWRANK:2 * GLA_LOWRANK, GLA_KW:].set(gla_w2[l, 1]).astype(BF)
        gbias = gla_b[l].reshape(1, 2 * GLA_KW)
        alog_row = jnp.zeros((1, 128), F32).at[0, SMALL_DNA:SMALL_DNA + 8].set(dn_a_log[l].reshape(-1))
        dtb_row = jnp.zeros((1, 128), F32).at[0, SMALL_DNA:SMALL_DNA + 8].set(dn_dt_bias[l].reshape(-1))

        x = _ffn_call(x, mod_l, norm_w[l, 0][None, :], ffn1_in[l].astype(BF), ffn1_down[l].astype(BF),
                      fin, 0, False)
        proj = _proj_call(x, mod_l, norm_w[l, 1][None, :], w_in_p)

        a_c, st_a = _gla_call(proj, w2bd, gbias, tile4(gla_norm[l]), None, T_CTX, N_CTX_SEQ, 0)
        dn_lat_blk = N_CTX_TOK // (N_PAIR * T_LAT)
        q_c, k_c, dd_c, lw_c, vb_c, kb_c, qkd_c = _dn_build_call(proj, dn_conv[l], alog_row, dtb_row,
                                                                T_CTX, N_CTX_SEQ, 0)
        q_l, k_l, dd_l, lw_l, vb_l, kb_l, qkd_l = _dn_build_call(proj, dn_conv[l], alog_row, dtb_row,
                                                                T_LAT, N_LAT_SEQ, dn_lat_blk)
        cat = lambda a, b: jnp.concatenate([a, b], axis=1)
        dn_ut, dn_wt = _dn_solve_call(_chunks_to_lanes(cat(lw_c, lw_l)), _chunks_to_lanes(cat(vb_c, vb_l)),
                                      _chunks_to_lanes(cat(kb_c, kb_l)))
        dn_u, dn_w = _chunks_from_lanes(dn_ut), _chunks_from_lanes(dn_wt)
        b_c, st_b = _dn_scan_call(proj, q_c, k_c, dd_c, dn_u, dn_w, qkd_c, tile4(dn_norm[l]), None,
                                  T_CTX, N_CTX_SEQ, 0)
        c_c, st_c = _hgrn_call(proj, lb_logits, tile4(hg_norm[l]), None, T_CTX, N_CTX_SEQ, 0, l)
        d_c, new_k_l, new_v_l = _att_call(proj, diff_lambda[l], tile4(diff_norm[l]), None, None, None,
                                  T_CTX, N_CTX_SEQ, 0, lam_init)
        ck = cache_diff_k[:, l].transpose(0, 2, 1, 3).reshape(N_LAT_SEQ, PAST_LEN, BRANCH_W)
        cv = cache_diff_v[:, l].transpose(0, 2, 1, 3).reshape(N_LAT_SEQ, PAST_LEN, BRANCH_W)
        (a_l,) = _gla_call(proj, w2bd, gbias, tile4(gla_norm[l]),
                           _block_diag_states(state_gla[:, l], GLA_DK, True), T_LAT, N_LAT_SEQ, lat_blk)
        (b_l,) = _dn_scan_call(proj, q_l, k_l, dd_l, dn_u, dn_w, qkd_l, tile4(dn_norm[l]),
                               _block_diag_states(state_dn[:, l], DN_DK, False), T_LAT, N_LAT_SEQ, dn_lat_blk)
        (c_l,) = _hgrn_call(proj, lb_logits, tile4(hg_norm[l]),
                            _block_diag_states(state_hgrn[:, l], HG_DK, True), T_LAT, N_LAT_SEQ, lat_blk, l)
        (d_l,) = _att_call(proj, diff_lambda[l], tile4(diff_norm[l]), ck, cv, rope,
                           T_LAT, N_LAT_SEQ, lat_blk, lam_init)

        x = _merge_call(x, mod_l, norm_w[l, 1][None, :], (a_c, b_c, c_c, d_c), (a_l, b_l, c_l, d_l),
                        w_mgate[l].astype(BF), w_branch[l].astype(BF), w_out[l].astype(BF))
        x = _ffn_call(x, mod_l, norm_w[l, 2][None, :], ffn2_in[l].astype(BF), ffn2_down[l].astype(BF),
                      fin, 2, l == DEPTH - 1)
        new_k.append(new_k_l)
        new_v.append(new_v_l)
        new_gla.append(st_a)
        new_dn.append(st_b)
        new_hg.append(st_c)
    y_prompt = x[:N_CTX_TOK].reshape(N_CTX_SEQ, T_CTX, D_MODEL)
    y_sample = x[N_CTX_TOK:].reshape(N_LAT_SEQ, T_LAT, D_MODEL)
    return (y_prompt, y_sample, jnp.stack(new_k, axis=1), jnp.stack(new_v, axis=1),
            jnp.stack(new_gla, axis=1), jnp.stack(new_dn, axis=1), jnp.stack(new_hg, axis=1))
```

```python
import functools
import math

import numpy as np
import jax
import jax.numpy as jnp
from jax import lax
from jax.experimental import pallas as pl
from jax.experimental.pallas import tpu as pltpu

F32 = jnp.float32
BF = jnp.bfloat16

D_MODEL = 1024
N_CTX_SEQ = 32
T_CTX = 256
DEPTH = 2
N_LAT_SEQ = 2
T_LAT = 1024
PAST_LEN = 512
GRID_W = 64
N_HEADS = 4
BRANCH_W = 256
HEAD_V = 64
GLA_DK = 32
GLA_KW = 128
GLA_LOWRANK = 16
GLA_TAU = 16.0
DN_DK = 64
HG_DK = 64
HG_W = 256
DF_DH = 32
ROPE_BASE = 10000.0
D_FF = 2816
N_MOD = 9
CHUNK = 64
EPS = 1e-6
N_CTX_TOK = N_CTX_SEQ * T_CTX
N_TOK = N_CTX_TOK + N_LAT_SEQ * T_LAT
N_LEVELS = 6
SMALL_LEVEL = 4
N_SCAN_CONSTS = 13

IN_ORIG = (128, 128, 256, 256, 32, 768, 8, 8, 256, 256, 512, 256, 256, 256, 256, 256)
N_IN = sum(IN_ORIG)
N_IN_PAD = 3968
COL_SMALL_BLOCK = 30
SMALL_LR = 0
SMALL_DNB = 32
SMALL_DNA = 40

VMEM_LIMIT = 56 * 1024 * 1024

TM_FFN = 512
TF_FFN = 1408
TM_PROJ = 512
TM_MERGE = 512
TN_MOD = 2304
TQ_ATT = 256


def _silu(x):
    return x * (1.0 / (1.0 + jnp.exp(-x)))


def _sigmoid(x):
    return 1.0 / (1.0 + jnp.exp(-x))


def _softplus(x):
    return jnp.maximum(x, 0.0) + jnp.log(1.0 + jnp.exp(-jnp.abs(x)))


def _log_sigmoid(x):
    return -_softplus(-x)


def _mm(a, b):
    return jnp.dot(a.astype(BF), b.astype(BF), preferred_element_type=F32)


def _mm_nt(a, b):
    return lax.dot_general(a.astype(BF), b.astype(BF), (((1,), (1,)), ((), ())),
                           preferred_element_type=F32)


def _mm_tn(a, b):
    return lax.dot_general(a.astype(BF), b.astype(BF), (((0,), (0,)), ((), ())),
                           preferred_element_type=F32)


def _split(x, n):
    parts = []
    r = x
    for i in range(n):
        p = r.astype(BF)
        parts.append(p)
        if i + 1 < n:
            r = r - p.astype(F32)
    return parts


def _sel_l(m01, x, n=3):
    out = None
    for p in _split(x, n):
        t = jnp.dot(m01, p, preferred_element_type=F32)
        out = t if out is None else out + t
    return out


def _sel_r(x, m01, n=3):
    out = None
    for p in _split(x, n):
        t = jnp.dot(p, m01, preferred_element_type=F32)
        out = t if out is None else out + t
    return out


def _sel_tn(x, m01, n=3):
    out = None
    for p in _split(x, n):
        t = lax.dot_general(p, m01, (((0,), (0,)), ((), ())), preferred_element_type=F32)
        out = t if out is None else out + t
    return out


def _rms(x, w):
    return x * lax.rsqrt(jnp.mean(x * x, axis=-1, keepdims=True) + EPS) * w


def _head_lanes(x, h):
    blk = x[:, (h // 2) * 128:(h // 2 + 1) * 128]
    if h % 2:
        blk = pltpu.roll(blk, 64, 1)
    return blk[:, :HEAD_V]


def _stack_heads(x, hm_ref):
    return jnp.concatenate([x * hm_ref[h:h + 1, :] for h in range(N_HEADS)], axis=0)


def _stack_heads_bf(x, hm_ref):
    xb = x.astype(BF)
    return jnp.concatenate([xb * hm_ref[h:h + 1, :].astype(BF) for h in range(N_HEADS)], axis=0)


def _head_diag(o_full, hv_ref, c):
    out = None
    for h in range(N_HEADS):
        t = o_full[h * c:(h + 1) * c, :] * hv_ref[h:h + 1, :]
        out = t if out is None else out + t
    return out


def _head_norm_gate(o, gate, nw, ones_bd):
    ms = _sel_r(o * o, ones_bd, 2) * (1.0 / HEAD_V)
    return o * lax.rsqrt(ms + EPS) * nw * _silu(gate)


def _mod_row(i, tm):
    return jnp.maximum(i * tm - (N_CTX_TOK - T_LAT), 0) // T_LAT


def _cparams(sem):
    return pltpu.CompilerParams(dimension_semantics=sem, vmem_limit_bytes=VMEM_LIMIT)


@functools.lru_cache(maxsize=None)
def _scan_consts(w, reverse):
    c = CHUNK
    idx = np.arange(c)
    i = idx[:, None]
    m = idx[None, :]
    pm, sg = [], []
    s = c // 2
    while s >= 1:
        par = idx // (2 * s)
        right = (idx % (2 * s)) >= s
        same = par[:, None] == par[None, :]
        query = ~right if reverse else right
        pm.append(same & query[:, None] & (~query)[None, :])
        if s >= SMALL_LEVEL:
            sg.append(np.where(query, 1.0, -1.0))
        s //= 2
    pm.append(i == m)
    tri = (m >= i) if reverse else (m <= i)
    r4, r2 = idx % 4, idx % 2
    if not reverse:
        cf = [r2 == 1, r4 >= 2, r4 == 3, r4 == 0]
    else:
        cf = [r2 == 0, r4 <= 1, r4 == 3, r4 == 0]
    wide = lambda rows: np.repeat(np.stack(rows).astype(np.float32)[:, :, None], w, axis=2)
    pmask = np.stack([np.tile(p, (1, N_HEADS)) for p in pm]).astype(np.float32)
    return tri.astype(np.float32), wide(sg), wide(cf), pmask


@functools.lru_cache(maxsize=None)
def _head_consts(w):
    dk = w // N_HEADS
    hm = np.zeros((N_HEADS, w), np.float32)
    hv = np.zeros((N_HEADS, BRANCH_W), np.float32)
    for h in range(N_HEADS):
        hm[h, h * dk:(h + 1) * dk] = 1.0
        hv[h, h * HEAD_V:(h + 1) * HEAD_V] = 1.0
    bd = hm.T @ hv
    ones_bd = hv.T @ hv
    return hm, hv, bd, ones_bd


@functools.lru_cache(maxsize=None)
def _dn_consts():
    c = CHUNK
    idx = np.arange(c)
    i = idx[:, None]
    j = idx[None, :]
    tri = np.stack([(j <= i), (j >= i)]).astype(np.float32)
    strict = np.stack([np.tile(j < i, (1, N_HEADS)), np.tile(j > i, (1, N_HEADS))]).astype(np.float32)
    eye = np.tile(np.eye(c), (1, N_HEADS)).astype(np.float32)
    blk = np.ones((c, c), np.float32)
    exb = np.zeros((128, 2 * BRANCH_W), np.float32)
    exa = np.zeros((128, 2 * BRANCH_W), np.float32)
    for n in range(2):
        for h in range(N_HEADS):
            lo = n * BRANCH_W + h * HEAD_V
            exb[SMALL_DNB + n * N_HEADS + h, lo:lo + HEAD_V] = 1.0
            exa[SMALL_DNA + n * N_HEADS + h, lo:lo + HEAD_V] = 1.0
    return tri, strict, eye, blk, exb, exa


@functools.lru_cache(maxsize=None)
def _att_consts():
    qm = np.zeros((2 * N_HEADS, BRANCH_W), np.float32)
    for h in range(N_HEADS):
        for mp in range(2):
            lo = h * HEAD_V + mp * DF_DH
            qm[2 * h + mp, lo:lo + DF_DH] = 1.0
    return qm


def _rope_tables():
    rows = T_LAT // GRID_W
    row = jnp.repeat(jnp.arange(rows), GRID_W).astype(F32)
    col = jnp.tile(jnp.arange(GRID_W), rows).astype(F32)
    half = DF_DH // 2
    inv = ROPE_BASE ** (-jnp.arange(0, half, 2, dtype=F32) / half)

    def angles(pos):
        a = pos[:, None] * inv[None, :]
        return jnp.concatenate([a, a], axis=-1)

    ang = jnp.concatenate([angles(row), angles(col)], axis=-1)
    reps = BRANCH_W // DF_DH
    return jnp.tile(jnp.cos(ang), (1, reps)), jnp.tile(jnp.sin(ang), (1, reps))


def _mod_kernel(c_ref, w_ref, b_ref, o_ref):
    a = _silu(c_ref[...])
    w = w_ref[0]
    out = None
    for ap in _split(a, 2):
        for wp in _split(w, 2):
            t = jnp.dot(ap, wp, preferred_element_type=F32)
            out = t if out is None else out + t
    o_ref[0] = out + b_ref[0]


def _mod_call(c_rows, w_mod, b_mod):
    n_t = (N_MOD * D_MODEL) // TN_MOD
    return pl.pallas_call(
        _mod_kernel,
        grid=(DEPTH, n_t),
        in_specs=[pl.BlockSpec((8, D_MODEL), lambda l, j: (0, 0)),
                  pl.BlockSpec((1, D_MODEL, TN_MOD), lambda l, j: (l, 0, j)),
                  pl.BlockSpec((1, 1, TN_MOD), lambda l, j: (l, 0, j))],
        out_specs=pl.BlockSpec((1, 8, TN_MOD), lambda l, j: (l, 0, j)),
        out_shape=jax.ShapeDtypeStruct((DEPTH, 8, N_MOD * D_MODEL), F32),
        compiler_params=_cparams(("arbitrary", "arbitrary")),
        name="mod_vectors",
    )(c_rows, w_mod, b_mod.reshape(DEPTH, 1, N_MOD * D_MODEL))


def _ffn_kernel(x_ref, mod_ref, nw_ref, wg_ref, wu_ref, wd_ref, fn_ref, o_ref, h_ref, acc_ref,
                *, sub, final):
    f = pl.program_id(1)

    @pl.when(f == 0)
    def _():
        x = x_ref[...]
        sh = mod_ref[0, 3 * sub:3 * sub + 1, :]
        sc = mod_ref[0, 3 * sub + 1:3 * sub + 2, :]
        h_ref[...] = (_rms(x, nw_ref[...]) * (1.0 + sc) + sh).astype(BF)
        acc_ref[...] = jnp.zeros_like(acc_ref)

    h = h_ref[...]
    g = jnp.dot(h, wg_ref[...], preferred_element_type=F32)
    u = jnp.dot(h, wu_ref[...], preferred_element_type=F32)
    acc_ref[...] += jnp.dot((_silu(g) * u).astype(BF), wd_ref[...], preferred_element_type=F32)

    @pl.when(f == pl.num_programs(1) - 1)
    def _():
        ga = mod_ref[0, 3 * sub + 2:3 * sub + 3, :]
        y = x_ref[...] + 0.5 * ga * acc_ref[...]
        if final:
            y = _rms(y, fn_ref[...])
        o_ref[...] = y


def _ffn_call(x, mod_l, nw, w_in, w_down, final_w, sub, final):
    n_f = D_FF // TF_FFN
    tm = TM_FFN
    return pl.pallas_call(
        functools.partial(_ffn_kernel, sub=sub, final=final),
        grid=(N_TOK // tm, n_f),
        in_specs=[pl.BlockSpec((tm, D_MODEL), lambda i, f: (i, 0)),
                  pl.BlockSpec((1, N_MOD, D_MODEL), lambda i, f: (_mod_row(i, tm), 0, 0)),
                  pl.BlockSpec((1, D_MODEL), lambda i, f: (0, 0)),
                  pl.BlockSpec((D_MODEL, TF_FFN), lambda i, f: (0, f)),
                  pl.BlockSpec((D_MODEL, TF_FFN), lambda i, f: (0, n_f + f)),
                  pl.BlockSpec((TF_FFN, D_MODEL), lambda i, f: (f, 0)),
                  pl.BlockSpec((1, D_MODEL), lambda i, f: (0, 0))],
        out_specs=pl.BlockSpec((tm, D_MODEL), lambda i, f: (i, 0)),
        out_shape=jax.ShapeDtypeStruct((N_TOK, D_MODEL), F32),
        scratch_shapes=[pltpu.VMEM((tm, D_MODEL), BF), pltpu.VMEM((tm, D_MODEL), F32)],
        compiler_params=_cparams(("arbitrary", "arbitrary")),
        name="swiglu_half_step",
    )(x, mod_l, nw, w_in, w_in, w_down, final_w)


def _proj_kernel(x_ref, mod_ref, nw_ref, w_ref, o_ref):
    sh = mod_ref[0, 3:4, :]
    sc = mod_ref[0, 4:5, :]
    h = (_rms(x_ref[...], nw_ref[...]) * (1.0 + sc) + sh).astype(BF)
    o_ref[...] = jnp.dot(h, w_ref[...], preferred_element_type=F32)


def _proj_call(x, mod_l, nw, w_in_p):
    tm = TM_PROJ
    return pl.pallas_call(
        _proj_kernel,
        grid=(N_TOK // tm,),
        in_specs=[pl.BlockSpec((tm, D_MODEL), lambda i: (i, 0)),
                  pl.BlockSpec((1, N_MOD, D_MODEL), lambda i: (_mod_row(i, tm), 0, 0)),
                  pl.BlockSpec((1, D_MODEL), lambda i: (0, 0)),
                  pl.BlockSpec((D_MODEL, N_IN_PAD), lambda i: (0, 0))],
        out_specs=pl.BlockSpec((tm, N_IN_PAD), lambda i: (i, 0)),
        out_shape=jax.ShapeDtypeStruct((N_TOK, N_IN_PAD), F32),
        compiler_params=_cparams(("arbitrary",)),
        name="mixer_in_proj",
    )(x, mod_l, nw, w_in_p)


def _merge_kernel(x_ref, mod_ref, nw_ref, *rest):
    ctx_refs, lat_refs = rest[0:4], rest[4:8]
    wg_ref, wb_ref, wo_ref, o_ref = rest[8:12]
    x = x_ref[...]
    sh = mod_ref[0, 3:4, :]
    sc = mod_ref[0, 4:5, :]
    ga = mod_ref[0, 5:6, :]
    h = (_rms(x, nw_ref[...]) * (1.0 + sc) + sh).astype(BF)
    is_lat = pl.program_id(0) >= N_CTX_TOK // TM_MERGE
    mixed = None
    for n in range(4):
        gate = _sigmoid(jnp.dot(h, wg_ref[:, n * D_MODEL:(n + 1) * D_MODEL], preferred_element_type=F32))
        br = jnp.where(is_lat, lat_refs[n][...], ctx_refs[n][...])
        up = jnp.dot(br.astype(BF), wb_ref[n], preferred_element_type=F32)
        mixed = gate * up if mixed is None else mixed + gate * up
    out = jnp.dot(mixed.astype(BF), wo_ref[...], preferred_element_type=F32)
    o_ref[...] = x + ga * out


def _merge_call(x, mod_l, nw, ctx_branches, lat_branches, w_mgate, w_branch, w_out):
    tm = TM_MERGE
    n_ctx = N_CTX_TOK // tm
    cspec = pl.BlockSpec((tm, BRANCH_W), lambda i: (jnp.minimum(i, n_ctx - 1), 0))
    lspec = pl.BlockSpec((tm, BRANCH_W), lambda i: (jnp.maximum(i - n_ctx, 0), 0))
    return pl.pallas_call(
        _merge_kernel,
        grid=(N_TOK // tm,),
        in_specs=[pl.BlockSpec((tm, D_MODEL), lambda i: (i, 0)),
                  pl.BlockSpec((1, N_MOD, D_MODEL), lambda i: (_mod_row(i, tm), 0, 0)),
                  pl.BlockSpec((1, D_MODEL), lambda i: (0, 0)),
                  cspec, cspec, cspec, cspec, lspec, lspec, lspec, lspec,
                  pl.BlockSpec((D_MODEL, 4 * D_MODEL), lambda i: (0, 0)),
                  pl.BlockSpec((4, BRANCH_W, D_MODEL), lambda i: (0, 0, 0)),
                  pl.BlockSpec((D_MODEL, D_MODEL), lambda i: (0, 0))],
        out_specs=pl.BlockSpec((tm, D_MODEL), lambda i: (i, 0)),
        out_shape=jax.ShapeDtypeStruct((N_TOK, D_MODEL), F32),
        compiler_params=_cparams(("arbitrary",)),
        name="gated_merge",
    )(x, mod_l, nw, *ctx_branches, *lat_branches, w_mgate, w_branch, w_out)


def _gated_chunk(qc, kc, vc, gc, st, tri_ref, sg_ref, cf_ref, pm_ref, hm_ref, hv_ref, bdt_ref, reverse):
    c = CHUNK
    w = gc.shape[1]
    cum = _sel_l(tri_ref[...], gc, 2)
    kst = _stack_heads_bf(kc, hm_ref)
    qb = qc.astype(BF)
    a = pm_ref[N_LEVELS] * _mm_nt(qb, kst)
    lv = 0
    s = c // 2
    while s >= 1:
        if s >= SMALL_LEVEL:
            first = s if reverse else s - 1
            bnd = [jnp.broadcast_to(cum[p * 2 * s + first:p * 2 * s + first + 1, :], (2 * s, w))
                   for p in range(c // (2 * s))]
            bnd = jnp.concatenate(bnd, axis=0) if len(bnd) > 1 else bnd[0]
            dl = (cum - bnd) * sg_ref[lv]
        elif s == 2:
            dl = (gc * cf_ref[1] + pltpu.roll(gc, 1, 0) * cf_ref[2] + pltpu.roll(gc, c - 1, 0) * cf_ref[3])
        else:
            dl = gc * cf_ref[0]
        el = jnp.exp(dl).astype(BF)
        a = a + pm_ref[lv] * _mm_nt(qb * el, kst * jnp.concatenate([el] * N_HEADS, axis=0))
        lv += 1
        s //= 2
    tot = cum[0:1, :] if reverse else cum[c - 1:c, :]
    o = _mm(a, _stack_heads_bf(vc, hv_ref)) + _mm_nt(qc * jnp.exp(cum), st)
    st_new = st * jnp.exp(tot) + bdt_ref[...] * _mm_tn(vc, kc * jnp.exp(tot - cum))
    return o, st_new


def _gated_scan(q_ref, kf_ref, kb_ref, v_ref, gf_ref, gb_ref, of_ref, ob_ref, s0f, s0b, cf, cb, n_chunks):
    def body(t, carry):
        sf, sb = carry
        rf = pl.ds(pl.multiple_of(t * CHUNK, CHUNK), CHUNK)
        rb = pl.ds(pl.multiple_of((n_chunks - 1 - t) * CHUNK, CHUNK), CHUNK)
        o_f, sf = _gated_chunk(q_ref[rf, :], kf_ref[rf, :], v_ref[rf, :], gf_ref[rf, :], sf, *cf, False)
        o_b, sb = _gated_chunk(q_ref[rb, :], kb_ref[rb, :], v_ref[rb, :], gb_ref[rb, :], sb, *cb, True)
        of_ref[rf, :] = o_f
        ob_ref[rb, :] = o_b
        return sf, sb

    return lax.fori_loop(0, n_chunks, body, (s0f, s0b))


def _write_states(st_ref, st_f, st_b, dk, eye_ref):
    for d, st in enumerate((st_f, st_b)):
        for h in range(N_HEADS):
            tr = _sel_tn(st[h * HEAD_V:(h + 1) * HEAD_V, :], eye_ref[...])
            st_ref[0, d, h] = tr[h * dk:(h + 1) * dk, :]


def _gla_kernel(*refs, t_len, has_state):
    blk_ref, small_ref, w2_ref, gb_ref, nw_ref = refs[:5]
    pos = 5 + N_SCAN_CONSTS
    cf, cb, ones_ref, eye_ref = _split_scan_consts(refs[5:pos])
    if has_state:
        s0_ref = refs[pos]
        pos += 1
    out_ref = refs[pos]
    pos += 1
    if not has_state:
        st_ref = refs[pos]
        pos += 1
    q_s, gf_s, gb_s, of_s, ob_s = refs[pos:pos + 5]

    q_s[...] = blk_ref[:, 0:128] * (GLA_DK ** -0.5)
    z = _mm(small_ref[...], w2_ref[...]) + gb_ref[...]
    g = _log_sigmoid(z) * (1.0 / GLA_TAU)
    gf_s[...] = g[:, 0:128]
    gb_s[...] = g[:, 128:256]
    k_ref = blk_ref.at[:, 128:256]
    v_ref = blk_ref.at[:, 256:512]
    n_chunks = t_len // CHUNK
    if has_state:
        s0f, s0b = s0_ref[0, 0], s0_ref[0, 1]
    else:
        s0f = s0b = jnp.zeros((BRANCH_W, GLA_KW), F32)
    s_f, s_b = _gated_scan(q_s, k_ref, k_ref, v_ref, gf_s, gb_s, of_s, ob_s, s0f, s0b, cf, cb, n_chunks)
    out_ref[...] = _head_norm_gate(of_s[...] + ob_s[...], blk_ref[:, 512:768], nw_ref[...], ones_ref[...])
    if not has_state:
        _write_states(st_ref, s_f, s_b, GLA_DK, eye_ref)


def _const_spec(a):
    nd = a.ndim
    return pl.BlockSpec(a.shape, lambda b, _n=nd: (0,) * _n)


def _scan_const_arrays(w):
    out = []
    for reverse in (False, True):
        tri, sg, cf, pm = _scan_consts(w, reverse)
        out += [jnp.asarray(tri, BF), jnp.asarray(sg, F32), jnp.asarray(cf, F32), jnp.asarray(pm, F32)]
    hm, hv, bd, ones_bd = _head_consts(w)
    out += [jnp.asarray(hm, F32), jnp.asarray(hv, F32), jnp.asarray(bd.T, F32), jnp.asarray(ones_bd, BF),
            jnp.eye(HEAD_V, dtype=BF)]
    return tuple(out)


def _split_scan_consts(refs):
    shared = tuple(refs[8:11])
    return tuple(refs[0:4]) + shared, tuple(refs[4:8]) + shared, refs[11], refs[12]


def _gla_call(proj, w2bd, gbias, nw, s0, t_len, n_seq, row_blk0):
    has_state = s0 is not None
    consts = (w2bd, gbias, nw) + _scan_const_arrays(GLA_KW)
    in_specs = [pl.BlockSpec((t_len, 768), lambda b: (row_blk0 + b, 0)),
                pl.BlockSpec((t_len, 128), lambda b: (row_blk0 + b, COL_SMALL_BLOCK))]
    in_specs += [_const_spec(a) for a in consts]
    args = [proj, proj, *consts]
    if has_state:
        in_specs.append(pl.BlockSpec((1, 2, BRANCH_W, GLA_KW), lambda b: (b, 0, 0, 0)))
        args.append(s0)
    out_shape = [jax.ShapeDtypeStruct((n_seq * t_len, BRANCH_W), F32)]
    out_specs = [pl.BlockSpec((t_len, BRANCH_W), lambda b: (b, 0))]
    if not has_state:
        out_shape.append(jax.ShapeDtypeStruct((n_seq, 2, N_HEADS, GLA_DK, HEAD_V), F32))
        out_specs.append(pl.BlockSpec((1, 2, N_HEADS, GLA_DK, HEAD_V), lambda b: (b, 0, 0, 0, 0)))
    return pl.pallas_call(
        functools.partial(_gla_kernel, t_len=t_len, has_state=has_state),
        grid=(n_seq,),
        in_specs=in_specs,
        out_specs=out_specs,
        out_shape=out_shape,
        scratch_shapes=[pltpu.VMEM((t_len, GLA_KW), F32)] * 3 + [pltpu.VMEM((t_len, BRANCH_W), F32)] * 2,
        compiler_params=_cparams(("arbitrary",)),
        name="gla_mixer",
    )(*args)


def _hgrn_kernel(*refs, t_len, has_state, layer):
    q_ref, f_ref, v_ref, gate_ref, lbl_ref, nw_ref = refs[:6]
    pos = 6 + N_SCAN_CONSTS
    cf, cb, ones_ref, eye_ref = _split_scan_consts(refs[6:pos])
    if has_state:
        s0_ref = refs[pos]
        pos += 1
    out_ref = refs[pos]
    pos += 1
    if not has_state:
        st_ref = refs[pos]
        pos += 1
    q_s, kf_s, kb_s, gf_s, gb_s, of_s, ob_s = refs[pos:pos + 7]

    lg = lbl_ref[...]
    mx = jnp.max(lg, axis=0, keepdims=True)
    ex = jnp.exp(lg - mx)
    p = ex / jnp.sum(ex, axis=0, keepdims=True)
    lb = jnp.sum(p[0:layer + 1], axis=0, keepdims=True) - p[0:1]

    q_s[...] = _silu(q_ref[...]) * (HG_DK ** -0.5)
    f = lb + (1.0 - lb) * _sigmoid(f_ref[...])
    kf_s[...] = 1.0 - f[:, 0:HG_W]
    kb_s[...] = 1.0 - f[:, HG_W:2 * HG_W]
    lf = jnp.log(f)
    gf_s[...] = lf[:, 0:HG_W]
    gb_s[...] = lf[:, HG_W:2 * HG_W]
    n_chunks = t_len // CHUNK
    if has_state:
        s0f, s0b = s0_ref[0, 0], s0_ref[0, 1]
    else:
        s0f = s0b = jnp.zeros((BRANCH_W, HG_W), F32)
    s_f, s_b = _gated_scan(q_s, kf_s, kb_s, v_ref, gf_s, gb_s, of_s, ob_s, s0f, s0b, cf, cb, n_chunks)
    out_ref[...] = _head_norm_gate(of_s[...] + ob_s[...], gate_ref[...], nw_ref[...], ones_ref[...])
    if not has_state:
        _write_states(st_ref, s_f, s_b, HG_DK, eye_ref)


def _hgrn_call(proj, lb_logits, nw, s0, t_len, n_seq, row_blk0, layer):
    has_state = s0 is not None
    consts = (lb_logits, nw) + _scan_const_arrays(HG_W)
    in_specs = [pl.BlockSpec((t_len, 256), lambda b: (row_blk0 + b, 7)),
                pl.BlockSpec((t_len, 512), lambda b: (row_blk0 + b, 4)),
                pl.BlockSpec((t_len, 256), lambda b: (row_blk0 + b, 10)),
                pl.BlockSpec((t_len, 256), lambda b: (row_blk0 + b, 11))]
    in_specs += [_const_spec(a) for a in consts]
    args = [proj, proj, proj, proj, *consts]
    if has_state:
        in_specs.append(pl.BlockSpec((1, 2, BRANCH_W, HG_W), lambda b: (b, 0, 0, 0)))
        args.append(s0)
    out_shape = [jax.ShapeDtypeStruct((n_seq * t_len, BRANCH_W), F32)]
    out_specs = [pl.BlockSpec((t_len, BRANCH_W), lambda b: (b, 0))]
    if not has_state:
        out_shape.append(jax.ShapeDtypeStruct((n_seq, 2, N_HEADS, HG_DK, HEAD_V), F32))
        out_specs.append(pl.BlockSpec((1, 2, N_HEADS, HG_DK, HEAD_V), lambda b: (b, 0, 0, 0, 0)))
    return pl.pallas_call(
        functools.partial(_hgrn_kernel, t_len=t_len, has_state=has_state, layer=layer),
        grid=(n_seq,),
        in_specs=in_specs,
        out_specs=out_specs,
        out_shape=out_shape,
        scratch_shapes=[pltpu.VMEM((t_len, HG_W), F32)] * 5 + [pltpu.VMEM((t_len, BRANCH_W), F32)] * 2,
        compiler_params=_cparams(("arbitrary",)),
        name="hgrn2_mixer",
    )(*args)


N_PAIR = 2
SOLVE_JB = 8
SOLVE_IB = 8


def _dn_solve_kernel(lt_ref, vb_ref, kb_ref, u_ref, w_ref):
    _dn_substitute(pl.program_id(1), lt_ref.at[0], vb_ref.at[0], kb_ref.at[0], u_ref.at[0], w_ref.at[0])


def _dn_substitute(rev, l_ref, vb_ref, kb_ref, u_ref, w_ref):
    c = CHUNK
    u_ref[...] = jnp.zeros_like(u_ref)
    w_ref[...] = jnp.zeros_like(w_ref)

    def outer(t, carry):
        i = t + rev * (c - 1 - 2 * t)

        def inner(jb, acc):
            au, aw = acc
            j0 = pl.multiple_of(jb * SOLVE_JB, SOLVE_JB)
            for r in range(SOLVE_JB):
                coef = l_ref[i, pl.ds(j0 + r, 1), :]
                au = au - coef * u_ref[j0 + r]
                aw = aw - coef * w_ref[j0 + r]
            return au, aw

        blk = lax.shift_right_logical(i, 3)
        lo = rev * blk
        hi = blk + 1 + rev * (c // SOLVE_JB - blk - 1)
        au, aw = lax.fori_loop(lo, hi, inner, (vb_ref[i], kb_ref[i]))
        u_ref[i] = au
        w_ref[i] = aw
        return carry

    lax.fori_loop(0, c, outer, 0)


def _dn_solve_packed_kernel(lt_ref, vb_ref, kb_ref, u_ref, w_ref, lp_s, vp_s, kp_s, up_s, wp_s):
    n_chunk = lt_ref.shape[-1]
    n_ib = CHUNK // SOLVE_IB
    s = pl.program_id(1)

    @pl.when(s < n_ib)
    def _():
        for ii in range(SOLVE_IB):
            for src, dst in ((lt_ref, lp_s), (vb_ref, vp_s), (kb_ref, kp_s)):
                dst[s * SOLVE_IB + ii] = jnp.concatenate(
                    [src[0, ii, h * HEAD_V:(h + 1) * HEAD_V, :] for h in range(N_HEADS)], axis=1)

    @pl.when(s == n_ib - 1)
    def _():
        _dn_substitute(pl.program_id(0), lp_s, vp_s, kp_s, up_s, wp_s)

    @pl.when(s >= n_ib)
    def _():
        for ii in range(SOLVE_IB):
            for src, dst in ((up_s, u_ref), (wp_s, w_ref)):
                row = src[(s - n_ib) * SOLVE_IB + ii]
                for h in range(N_HEADS):
                    dst[0, ii, h * HEAD_V:(h + 1) * HEAD_V, :] = row[:, h * n_chunk:(h + 1) * n_chunk]


def _dn_solve_call(lt, vbt, kbt):
    n_chunk = lt.shape[-1]
    shape = jax.ShapeDtypeStruct((2, CHUNK, BRANCH_W, n_chunk), F32)
    if n_chunk * N_HEADS == 128:
        n_ib = CHUNK // SOLVE_IB
        blk = (1, SOLVE_IB, BRANCH_W, n_chunk)
        spec = pl.BlockSpec(blk, lambda d, s: (d, jnp.minimum(s, n_ib - 1), 0, 0))
        ospec = pl.BlockSpec(blk, lambda d, s: (d, jnp.maximum(s - n_ib, 0), 0, 0))
        return pl.pallas_call(
            _dn_solve_packed_kernel,
            grid=(2, 2 * n_ib),
            in_specs=[spec, spec, spec],
            out_specs=[ospec, ospec],
            out_shape=[shape, shape],
            scratch_shapes=[pltpu.VMEM((CHUNK, HEAD_V, 128), F32)] * 5,
            compiler_params=_cparams(("arbitrary", "arbitrary")),
            name="deltanet_solve_packed",
        )(lt, vbt, kbt)
    spec = pl.BlockSpec((1, CHUNK, HEAD_V, n_chunk), lambda h, d: (d, 0, h, 0))
    return pl.pallas_call(
        _dn_solve_kernel,
        grid=(N_HEADS, 2),
        in_specs=[spec, spec, spec],
        out_specs=[spec, spec],
        out_shape=[shape, shape],
        compiler_params=_cparams(("arbitrary", "arbitrary")),
        name="deltanet_solve",
    )(lt, vbt, kbt)


def _chunks_to_lanes(a):
    return a.reshape(2, a.shape[1] // CHUNK, CHUNK, BRANCH_W).transpose(0, 2, 3, 1)


def _chunks_from_lanes(a):
    return a.transpose(0, 3, 1, 2).reshape(2, -1, BRANCH_W)


def _dn_build_kernel(x_ref, small_ref, cw_ref, alog_ref, dtb_ref,
                     tri_ref, strict_ref, eye_ref, blk_ref, exb_ref, exa_ref, hm_ref, ones_ref,
                     q_ref, k_ref, d_ref, l_ref, vb_ref, kb_ref, qkd_ref, v_ref, be_s, ge_s, *, t_len):
    n_rows = N_PAIR * t_len

    x = x_ref[...]
    row = lax.broadcasted_iota(jnp.int32, (n_rows, 1), 0) % t_len
    x_prev = jnp.where(row == 0, 0.0, pltpu.roll(x, 1, 0))
    x_next = jnp.where(row == t_len - 1, 0.0, pltpu.roll(x, n_rows - 1, 0))
    y = _silu(x_prev * cw_ref[0:1, :] + x * cw_ref[1:2, :] + x_next * cw_ref[2:3, :])
    cq, ck = y[:, 0:256], y[:, 256:512]
    v_ref[...] = y[:, 512:768]
    q_ref[...] = cq * lax.rsqrt(_sel_r(cq * cq, ones_ref[...], 2) + EPS) * (DN_DK ** -0.5)
    k_ref[...] = ck * lax.rsqrt(_sel_r(ck * ck, ones_ref[...], 2) + EPS)
    sm = small_ref[...]
    be_s[...] = _sel_r(_sigmoid(sm), exb_ref[...])
    ge_s[...] = _sel_r(-jnp.exp(alog_ref[...]) * _softplus(sm + dtb_ref[...]), exa_ref[...])

    def body(ci, carry):
        rows = pl.ds(pl.multiple_of(ci * CHUNK, CHUNK), CHUNK)
        qc, kc, vc = q_ref[rows, :], k_ref[rows, :], v_ref[rows, :]
        kst = _stack_heads(kc, hm_ref)
        kk = _mm_nt(kc, kst)
        qk = _mm_nt(qc, kst)
        for d in range(2):
            bexp = be_s[rows, d * BRANCH_W:(d + 1) * BRANCH_W]
            dexp = _sel_l(tri_ref[d], ge_s[rows, d * BRANCH_W:(d + 1) * BRANCH_W])
            drow = _sel_l(blk_ref[...], dexp * eye_ref[...])
            dec_s = jnp.exp(jnp.where(strict_ref[d] > 0.5, dexp - drow, -1e30))
            d_ref[d, rows, :] = dexp
            l_ref[d, rows, :] = bexp * kk * dec_s
            vb_ref[d, rows, :] = vc * bexp
            kb_ref[d, rows, :] = kc * bexp * jnp.exp(dexp)
            qkd_ref[d, rows, :] = (qk * (dec_s + eye_ref[...])).astype(BF)
        return carry

    lax.fori_loop(0, n_rows // CHUNK, body, 0)


def _dn_build_call(proj, conv_w, alog_row, dtb_row, t_len, n_seq, row_blk0):
    tri, strict, eye, blk, exb, exa = _dn_consts()
    tri, blk, exb, exa = (jnp.asarray(a, BF) for a in (tri, blk, exb, exa))
    strict, eye = jnp.asarray(strict, F32), jnp.asarray(eye, F32)
    hm, _, _, ones_bd = _head_consts(BRANCH_W)
    hm, ones_bd = jnp.asarray(hm, F32), jnp.asarray(ones_bd, BF)
    n_rows = N_PAIR * t_len
    n_tok = n_seq * t_len
    consts = (conv_w, alog_row, dtb_row, tri, strict, eye, blk, exb, exa, hm, ones_bd)
    in_specs = [pl.BlockSpec((n_rows, 768), lambda b: (row_blk0 + b, 1)),
                pl.BlockSpec((n_rows, 128), lambda b: (row_blk0 + b, COL_SMALL_BLOCK))]
    in_specs += [_const_spec(a) for a in consts]
    tok_spec = pl.BlockSpec((n_rows, BRANCH_W), lambda b: (b, 0))
    dir_spec = pl.BlockSpec((2, n_rows, BRANCH_W), lambda b: (0, b, 0))
    tok = jax.ShapeDtypeStruct((n_tok, BRANCH_W), F32)
    per_dir = jax.ShapeDtypeStruct((2, n_tok, BRANCH_W), F32)
    return pl.pallas_call(
        functools.partial(_dn_build_kernel, t_len=t_len),
        grid=(n_seq // N_PAIR,),
        in_specs=in_specs,
        out_specs=[tok_spec] * 2 + [dir_spec] * 5,
        out_shape=[tok] * 2 + [per_dir] * 4 + [jax.ShapeDtypeStruct((2, n_tok, BRANCH_W), BF)],
        scratch_shapes=[pltpu.VMEM((n_rows, BRANCH_W), F32)] + [pltpu.VMEM((n_rows, 2 * BRANCH_W), F32)] * 2,
        compiler_params=_cparams(("arbitrary",)),
        name="deltanet_build",
    )(proj, proj, *consts)


def _dn_scan_kernel(*refs, t_len, has_state):
    (q_ref, k_ref, d_ref, u_ref, w_ref, qkd_ref, gate_ref, nw_ref, hv_ref, bd_ref, ones_ref) = refs[:11]
    pos = 11
    if has_state:
        s0_ref = refs[pos]
        pos += 1
    out_ref = refs[pos]
    pos += 1
    if not has_state:
        st_ref = refs[pos]
        pos += 1
    of_s, ob_s = refs[pos:pos + 2]
    c = CHUNK
    n_chunks = t_len // c

    def step(s, rows, d, o_s):
        qc, kc = q_ref[rows, :], k_ref[rows, :]
        dexp = d_ref[d, rows, :]
        v_new = u_ref[d, rows, :] - _mm(w_ref[d, rows, :], s)
        o_s[rows, :] = _mm(qc * jnp.exp(dexp), s) + _mm(qkd_ref[d, rows, :], _stack_heads(v_new, hv_ref))
        dl = dexp[0:1, :] if d == 1 else dexp[c - 1:c, :]
        return s * jnp.exp(dl) + bd_ref[...] * _mm_tn(kc * jnp.exp(dl - dexp), v_new)

    def body(t, carry):
        new = []
        for sq in range(N_PAIR):
            rf = pl.ds(pl.multiple_of(sq * t_len + t * c, c), c)
            rb = pl.ds(pl.multiple_of(sq * t_len + (n_chunks - 1 - t) * c, c), c)
            new.append(step(carry[2 * sq], rf, 0, of_s))
            new.append(step(carry[2 * sq + 1], rb, 1, ob_s))
        return tuple(new)

    if has_state:
        init = tuple(s0_ref[sq, d] for sq in range(N_PAIR) for d in range(2))
    else:
        init = tuple(jnp.zeros((BRANCH_W, BRANCH_W), F32) for _ in range(2 * N_PAIR))
    finals = lax.fori_loop(0, n_chunks, body, init)
    out_ref[...] = _head_norm_gate(of_s[...] + ob_s[...], gate_ref[...], nw_ref[...], ones_ref[...])
    if not has_state:
        for sq in range(N_PAIR):
            for d in range(2):
                for h in range(N_HEADS):
                    st_ref[sq, d, h] = _head_lanes(finals[2 * sq + d][h * DN_DK:(h + 1) * DN_DK, :], h)


def _dn_scan_call(proj, q, k, dd, u, w, qkd, nw, s0, t_len, n_seq, row_blk0):
    _, hv, bd, ones_bd = _head_consts(BRANCH_W)
    hv, bd, ones_bd = jnp.asarray(hv, F32), jnp.asarray(bd, F32), jnp.asarray(ones_bd, BF)
    has_state = s0 is not None
    n_rows = N_PAIR * t_len
    consts = (nw, hv, bd, ones_bd)
    tok_spec = pl.BlockSpec((n_rows, BRANCH_W), lambda b: (b, 0))
    dir_spec = pl.BlockSpec((2, n_rows, BRANCH_W), lambda b: (0, b, 0))
    in_specs = [tok_spec, tok_spec, dir_spec, dir_spec, dir_spec, dir_spec,
                pl.BlockSpec((n_rows, 256), lambda b: (row_blk0 + b, 6))]
    in_specs += [_const_spec(a) for a in consts]
    args = [q, k, dd, u, w, qkd, proj, *consts]
    if has_state:
        in_specs.append(pl.BlockSpec((N_PAIR, 2, BRANCH_W, BRANCH_W), lambda b: (b, 0, 0, 0)))
        args.append(s0)
    out_shape = [jax.ShapeDtypeStruct((n_seq * t_len, BRANCH_W), F32)]
    out_specs = [pl.BlockSpec((n_rows, BRANCH_W), lambda b: (b, 0))]
    if not has_state:
        out_shape.append(jax.ShapeDtypeStruct((n_seq, 2, N_HEADS, DN_DK, HEAD_V), F32))
        out_specs.append(pl.BlockSpec((N_PAIR, 2, N_HEADS, DN_DK, HEAD_V), lambda b: (b, 0, 0, 0, 0)))
    return pl.pallas_call(
        functools.partial(_dn_scan_kernel, t_len=t_len, has_state=has_state),
        grid=(n_seq // N_PAIR,),
        in_specs=in_specs,
        out_specs=out_specs,
        out_shape=out_shape,
        scratch_shapes=[pltpu.VMEM((n_rows, BRANCH_W), F32)] * 2,
        compiler_params=_cparams(("arbitrary",)),
        name="deltanet_scan",
    )(*args)


def _rope(x, cos, sin):
    lane = lax.broadcasted_iota(jnp.int32, x.shape, 1) % 16
    n = x.shape[1]
    xrot = jnp.where(lane < 8, -pltpu.roll(x, n - 8, 1), pltpu.roll(x, 8, 1))
    return x * cos + xrot * sin


def _att_kernel(*refs, t_len, lat, lam_init):
    blk_ref, lam_ref, nw_ref, qm_ref, hv_ref, ones_ref = refs[:6]
    pos = 6
    if lat:
        cos_ref, sin_ref, ck_ref, cv_ref = refs[pos:pos + 4]
        pos += 4
    out_ref = refs[pos]
    pos += 1
    if not lat:
        nk_ref, nv_ref = refs[pos:pos + 2]
        pos += 2
    if lat:
        q_s, k_s, v_s = refs[pos:pos + 3]

    lv = lam_ref[...]
    lam = (jnp.exp(jnp.sum(lv[0:1] * lv[1:2], axis=1, keepdims=True))
           - jnp.exp(jnp.sum(lv[2:3] * lv[3:4], axis=1, keepdims=True)) + lam_init)
    q = blk_ref[:, 0:256]
    k = blk_ref[:, 256:512]
    v = blk_ref[:, 512:768]
    if lat:
        cos, sin = cos_ref[...], sin_ref[...]
        q_s[...] = _rope(q, cos, sin)
        k_s[0:PAST_LEN, :] = ck_ref[0]
        k_s[PAST_LEN:PAST_LEN + t_len, :] = _rope(k, cos, sin)
        v_s[0:PAST_LEN, :] = cv_ref[0]
        v_s[PAST_LEN:PAST_LEN + t_len, :] = v
        keys = k_s[...].astype(BF)
        vals = v_s[...].astype(BF)
    else:
        keys = k.astype(BF)
        vals = v.astype(BF)
        for h in range(N_HEADS):
            nk_ref[0, h] = _head_lanes(k, h)
            nv_ref[0, h] = _head_lanes(v, h)
    tq = TQ_ATT
    scale = DF_DH ** -0.5
    for qi in range(t_len // tq):
        qt = q_s[qi * tq:(qi + 1) * tq, :] if lat else q[qi * tq:(qi + 1) * tq, :]
        qs = jnp.concatenate([qt * qm_ref[r:r + 1, :] for r in range(2 * N_HEADS)], axis=0)
        s = _mm_nt(qs, keys) * scale
        s = jnp.exp(s - jnp.max(s, axis=1, keepdims=True))
        p = s / jnp.sum(s, axis=1, keepdims=True)
        a = jnp.concatenate(
            [p[(2 * h) * tq:(2 * h + 1) * tq] - lam * p[(2 * h + 1) * tq:(2 * h + 2) * tq]
             for h in range(N_HEADS)], axis=0)
        o = _head_diag(_mm(a, vals), hv_ref, tq)
        ms = _sel_r(o * o, ones_ref[...], 2) * (1.0 / HEAD_V)
        out_ref[qi * tq:(qi + 1) * tq, :] = o * lax.rsqrt(ms + EPS) * nw_ref[...] * (1.0 - lam_init)


def _att_call(proj, lam_p, nw, cache_k, cache_v, rope, t_len, n_seq, row_blk0, lam_init):
    qm = jnp.asarray(_att_consts(), F32)
    _, hv, _, ones_bd = _head_consts(BRANCH_W)
    hv, ones_bd = jnp.asarray(hv, F32), jnp.asarray(ones_bd, BF)
    lat = cache_k is not None
    consts = (lam_p, nw, qm, hv, ones_bd)
    in_specs = [pl.BlockSpec((t_len, 768), lambda b: (row_blk0 + b, 4))]
    in_specs += [_const_spec(a) for a in consts]
    args = [proj, *consts]
    out_shape = [jax.ShapeDtypeStruct((n_seq * t_len, BRANCH_W), F32)]
    out_specs = [pl.BlockSpec((t_len, BRANCH_W), lambda b: (b, 0))]
    scratch = []
    if lat:
        cos, sin = rope
        in_specs += [_const_spec(cos), _const_spec(sin),
                     pl.BlockSpec((1, PAST_LEN, BRANCH_W), lambda b: (b, 0, 0)),
                     pl.BlockSpec((1, PAST_LEN, BRANCH_W), lambda b: (b, 0, 0))]
        args += [cos, sin, cache_k, cache_v]
        scratch = [pltpu.VMEM((t_len, BRANCH_W), F32),
                   pltpu.VMEM((PAST_LEN + t_len, BRANCH_W), F32),
                   pltpu.VMEM((PAST_LEN + t_len, BRANCH_W), F32)]
    else:
        for _ in range(2):
            out_shape.append(jax.ShapeDtypeStruct((n_seq, N_HEADS, t_len, HEAD_V), F32))
            out_specs.append(pl.BlockSpec((1, N_HEADS, t_len, HEAD_V), lambda b: (b, 0, 0, 0)))
    return pl.pallas_call(
        functools.partial(_att_kernel, t_len=t_len, lat=lat, lam_init=lam_init),
        grid=(n_seq,),
        in_specs=in_specs,
        out_specs=out_specs,
        out_shape=out_shape,
        scratch_shapes=scratch,
        compiler_params=_cparams(("arbitrary",)),
        name="diff_attention",
    )(*args)


def _block_diag_states(st, dk, transposed):
    eye = jnp.eye(N_HEADS, dtype=st.dtype)
    b = st.shape[0]
    if transposed:
        return jnp.einsum('bnhde,hg->bnhegd', st, eye).reshape(b, 2, N_HEADS * HEAD_V, N_HEADS * dk)
    return jnp.einsum('bnhde,hg->bnhdge', st, eye).reshape(b, 2, N_HEADS * dk, N_HEADS * HEAD_V)


def _in_perm():
    offs = np.concatenate([[0], np.cumsum(IN_ORIG)])
    seg = lambda a, b: np.arange(offs[a], offs[b])
    return np.concatenate([seg(0, 4), seg(5, 6), seg(8, 9), seg(9, 16), seg(4, 5), seg(6, 8)])


def kernel(x_prompt, x_sample, cache_diff_k, cache_diff_v, state_gla, state_dn, state_hgrn, c, c_ctx,
           norm_w, w_mod, b_mod, ffn1_in, ffn1_down, ffn2_in, ffn2_down, w_in, gla_w2, gla_b, gla_norm,
           dn_conv, dn_a_log, dn_dt_bias, dn_norm, hg_lb_logits, hg_norm, diff_lambda, diff_norm,
           w_branch, w_mgate, w_out, final_norm):
    x = jnp.concatenate([x_prompt.reshape(N_CTX_TOK, D_MODEL), x_sample.reshape(-1, D_MODEL)], axis=0)
    c_rows = jnp.concatenate([c_ctx[None, :], c, jnp.zeros((8 - 1 - N_LAT_SEQ, D_MODEL), F32)], axis=0)
    mod = _mod_call(c_rows, w_mod, b_mod).reshape(DEPTH, 8, N_MOD, D_MODEL)
    rope = _rope_tables()
    perm = _in_perm()
    lb_logits = hg_lb_logits.reshape(DEPTH, 2 * HG_W)
    lat_blk = N_CTX_TOK // T_LAT
    tile4 = lambda a: jnp.tile(a, N_HEADS)[None, :]
    fin = final_norm[None, :]
    new_k, new_v, new_gla, new_dn, new_hg = [], [], [], [], []
    for l in range(DEPTH):
        mod_l = mod[l]
        lam_init = 0.8 - 0.6 * math.exp(-0.3 * l)
        w_in_p = jnp.pad(w_in[l][:, perm], ((0, 0), (0, N_IN_PAD - N_IN))).astype(BF)
        w2bd = jnp.zeros((128, 2 * GLA_KW), F32)
        w2bd = w2bd.at[0:GLA_LOWRANK, 0:GLA_KW].set(gla_w2[l, 0])
        w2bd = w2bd.at[GLA_LOWRANK:2 * GLA_LOWRANK, GLA_KW:].set(gla_w2[l, 1]).astype(BF)
        gbias = gla_b[l].reshape(1, 2 * GLA_KW)
        alog_row = jnp.zeros((1, 128), F32).at[0, SMALL_DNA:SMALL_DNA + 8].set(dn_a_log[l].reshape(-1))
        dtb_row = jnp.zeros((1, 128), F32).at[0, SMALL_DNA:SMALL_DNA + 8].set(dn_dt_bias[l].reshape(-1))

        x = _ffn_call(x, mod_l, norm_w[l, 0][None, :], ffn1_in[l].astype(BF), ffn1_down[l].astype(BF),
                      fin, 0, False)
        proj = _proj_call(x, mod_l, norm_w[l, 1][None, :], w_in_p)

        a_c, st_a = _gla_call(proj, w2bd, gbias, tile4(gla_norm[l]), None, T_CTX, N_CTX_SEQ, 0)
        dn_lat_blk = N_CTX_TOK // (N_PAIR * T_LAT)
        q_c, k_c, dd_c, lw_c, vb_c, kb_c, qkd_c = _dn_build_call(proj, dn_conv[l], alog_row, dtb_row,
                                                                T_CTX, N_CTX_SEQ, 0)
        q_l, k_l, dd_l, lw_l, vb_l, kb_l, qkd_l = _dn_build_call(proj, dn_conv[l], alog_row, dtb_row,
                                                                T_LAT, N_LAT_SEQ, dn_lat_blk)
        u_c, w_c = (_chunks_from_lanes(a) for a in _dn_solve_call(
            _chunks_to_lanes(lw_c), _chunks_to_lanes(vb_c), _chunks_to_lanes(kb_c)))
        u_l, w_l = (_chunks_from_lanes(a) for a in _dn_solve_call(
            _chunks_to_lanes(lw_l), _chunks_to_lanes(vb_l), _chunks_to_lanes(kb_l)))
        b_c, st_b = _dn_scan_call(proj, q_c, k_c, dd_c, u_c, w_c, qkd_c, tile4(dn_norm[l]), None,
                                  T_CTX, N_CTX_SEQ, 0)
        c_c, st_c = _hgrn_call(proj, lb_logits, tile4(hg_norm[l]), None, T_CTX, N_CTX_SEQ, 0, l)
        d_c, new_k_l, new_v_l = _att_call(proj, diff_lambda[l], tile4(diff_norm[l]), None, None, None,
                                  T_CTX, N_CTX_SEQ, 0, lam_init)
        ck = cache_diff_k[:, l].transpose(0, 2, 1, 3).reshape(N_LAT_SEQ, PAST_LEN, BRANCH_W)
        cv = cache_diff_v[:, l].transpose(0, 2, 1, 3).reshape(N_LAT_SEQ, PAST_LEN, BRANCH_W)
        (a_l,) = _gla_call(proj, w2bd, gbias, tile4(gla_norm[l]),
                           _block_diag_states(state_gla[:, l], GLA_DK, True), T_LAT, N_LAT_SEQ, lat_blk)
        (b_l,) = _dn_scan_call(proj, q_l, k_l, dd_l, u_l, w_l, qkd_l, tile4(dn_norm[l]),
                               _block_diag_states(state_dn[:, l], DN_DK, False), T_LAT, N_LAT_SEQ, dn_lat_blk)
        (c_l,) = _hgrn_call(proj, lb_logits, tile4(hg_norm[l]),
                            _block_diag_states(state_hgrn[:, l], HG_DK, True), T_LAT, N_LAT_SEQ, lat_blk, l)
        (d_l,) = _att_call(proj, diff_lambda[l], tile4(diff_norm[l]), ck, cv, rope,
                           T_LAT, N_LAT_SEQ, lat_blk, lam_init)

        x = _merge_call(x, mod_l, norm_w[l, 1][None, :], (a_c, b_c, c_c, d_c), (a_l, b_l, c_l, d_l),
                        w_mgate[l].astype(BF), w_branch[l].astype(BF), w_out[l].astype(BF))
        x = _ffn_call(x, mod_l, norm_w[l, 2][None, :], ffn2_in[l].astype(BF), ffn2_down[l].astype(BF),
                      fin, 2, l == DEPTH - 1)
        new_k.append(new_k_l)
        new_v.append(new_v_l)
        new_gla.append(st_a)
        new_dn.append(st_b)
        new_hg.append(st_c)
    y_prompt = x[:N_CTX_TOK].reshape(N_CTX_SEQ, T_CTX, D_MODEL)
    y_sample = x[N_CTX_TOK:].reshape(N_LAT_SEQ, T_LAT, D_MODEL)
    return (y_prompt, y_sample, jnp.stack(new_k, axis=1), jnp.stack(new_v, axis=1),
            jnp.stack(new_gla, axis=1), jnp.stack(new_dn, axis=1), jnp.stack(new_hg, axis=1))
```

```python
import functools
import math

import numpy as np
import jax
import jax.numpy as jnp
from jax import lax
from jax.experimental import pallas as pl
from jax.experimental.pallas import tpu as pltpu

F32 = jnp.float32
BF = jnp.bfloat16

D_MODEL = 1024
N_CTX_SEQ = 32
T_CTX = 256
DEPTH = 2
N_LAT_SEQ = 2
T_LAT = 1024
PAST_LEN = 512
GRID_W = 64
N_HEADS = 4
BRANCH_W = 256
HEAD_V = 64
GLA_DK = 32
GLA_KW = 128
GLA_LOWRANK = 16
GLA_TAU = 16.0
DN_DK = 64
HG_DK = 64
HG_W = 256
DF_DH = 32
ROPE_BASE = 10000.0
D_FF = 2816
N_MOD = 9
CHUNK = 64
EPS = 1e-6
N_CTX_TOK = N_CTX_SEQ * T_CTX
N_TOK = N_CTX_TOK + N_LAT_SEQ * T_LAT
N_LEVELS = 6
SMALL_LEVEL = 4
N_SCAN_CONSTS = 13

IN_ORIG = (128, 128, 256, 256, 32, 768, 8, 8, 256, 256, 512, 256, 256, 256, 256, 256)
N_IN = sum(IN_ORIG)
N_IN_PAD = 3968
COL_SMALL_BLOCK = 30
SMALL_LR = 0
SMALL_DNB = 32
SMALL_DNA = 40

VMEM_LIMIT = 56 * 1024 * 1024

TM_FFN = 512
TF_FFN = 1408
TM_PROJ = 512
TM_MERGE = 512
TN_MOD = 2304
TQ_ATT = 256


def _silu(x):
    return x * (1.0 / (1.0 + jnp.exp(-x)))


def _sigmoid(x):
    return 1.0 / (1.0 + jnp.exp(-x))


def _softplus(x):
    return jnp.maximum(x, 0.0) + jnp.log(1.0 + jnp.exp(-jnp.abs(x)))


def _log_sigmoid(x):
    return -_softplus(-x)


def _mm(a, b):
    return jnp.dot(a.astype(BF), b.astype(BF), preferred_element_type=F32)


def _mm_nt(a, b):
    return lax.dot_general(a.astype(BF), b.astype(BF), (((1,), (1,)), ((), ())),
                           preferred_element_type=F32)


def _mm_tn(a, b):
    return lax.dot_general(a.astype(BF), b.astype(BF), (((0,), (0,)), ((), ())),
                           preferred_element_type=F32)


def _split(x, n):
    parts = []
    r = x
    for i in range(n):
        p = r.astype(BF)
        parts.append(p)
        if i + 1 < n:
            r = r - p.astype(F32)
    return parts


def _sel_l(m01, x, n=3):
    out = None
    for p in _split(x, n):
        t = jnp.dot(m01, p, preferred_element_type=F32)
        out = t if out is None else out + t
    return out


def _sel_r(x, m01, n=3):
    out = None
    for p in _split(x, n):
        t = jnp.dot(p, m01, preferred_element_type=F32)
        out = t if out is None else out + t
    return out


def _sel_tn(x, m01, n=3):
    out = None
    for p in _split(x, n):
        t = lax.dot_general(p, m01, (((0,), (0,)), ((), ())), preferred_element_type=F32)
        out = t if out is None else out + t
    return out


def _rms(x, w):
    return x * lax.rsqrt(jnp.mean(x * x, axis=-1, keepdims=True) + EPS) * w


def _head_lanes(x, h):
    blk = x[:, (h // 2) * 128:(h // 2 + 1) * 128]
    if h % 2:
        blk = pltpu.roll(blk, 64, 1)
    return blk[:, :HEAD_V]


def _stack_heads(x, hm_ref):
    return jnp.concatenate([x * hm_ref[h:h + 1, :] for h in range(N_HEADS)], axis=0)


def _stack_heads_bf(x, hm_ref):
    xb = x.astype(BF)
    return jnp.concatenate([xb * hm_ref[h:h + 1, :].astype(BF) for h in range(N_HEADS)], axis=0)


def _head_diag(o_full, hv_ref, c):
    out = None
    for h in range(N_HEADS):
        t = o_full[h * c:(h + 1) * c, :] * hv_ref[h:h + 1, :]
        out = t if out is None else out + t
    return out


def _head_norm_gate(o, gate, nw, ones_bd):
    ms = _sel_r(o * o, ones_bd, 2) * (1.0 / HEAD_V)
    return o * lax.rsqrt(ms + EPS) * nw * _silu(gate)


def _mod_row(i, tm):
    return jnp.maximum(i * tm - (N_CTX_TOK - T_LAT), 0) // T_LAT


def _cparams(sem):
    return pltpu.CompilerParams(dimension_semantics=sem, vmem_limit_bytes=VMEM_LIMIT)


@functools.lru_cache(maxsize=None)
def _scan_consts(w, reverse):
    c = CHUNK
    idx = np.arange(c)
    i = idx[:, None]
    m = idx[None, :]
    pm, sg = [], []
    s = c // 2
    while s >= 1:
        par = idx // (2 * s)
        right = (idx % (2 * s)) >= s
        same = par[:, None] == par[None, :]
        query = ~right if reverse else right
        pm.append(same & query[:, None] & (~query)[None, :])
        if s >= SMALL_LEVEL:
            sg.append(np.where(query, 1.0, -1.0))
        s //= 2
    pm.append(i == m)
    tri = (m >= i) if reverse else (m <= i)
    r4, r2 = idx % 4, idx % 2
    if not reverse:
        cf = [r2 == 1, r4 >= 2, r4 == 3, r4 == 0]
    else:
        cf = [r2 == 0, r4 <= 1, r4 == 3, r4 == 0]
    wide = lambda rows: np.repeat(np.stack(rows).astype(np.float32)[:, :, None], w, axis=2)
    pmask = np.stack([np.tile(p, (1, N_HEADS)) for p in pm]).astype(np.float32)
    return tri.astype(np.float32), wide(sg), wide(cf), pmask


@functools.lru_cache(maxsize=None)
def _head_consts(w):
    dk = w // N_HEADS
    hm = np.zeros((N_HEADS, w), np.float32)
    hv = np.zeros((N_HEADS, BRANCH_W), np.float32)
    for h in range(N_HEADS):
        hm[h, h * dk:(h + 1) * dk] = 1.0
        hv[h, h * HEAD_V:(h + 1) * HEAD_V] = 1.0
    bd = hm.T @ hv
    ones_bd = hv.T @ hv
    return hm, hv, bd, ones_bd


@functools.lru_cache(maxsize=None)
def _dn_consts():
    c = CHUNK
    idx = np.arange(c)
    i = idx[:, None]
    j = idx[None, :]
    tri = np.stack([(j <= i), (j >= i)]).astype(np.float32)
    strict = np.stack([np.tile(j < i, (1, N_HEADS)), np.tile(j > i, (1, N_HEADS))]).astype(np.float32)
    eye = np.tile(np.eye(c), (1, N_HEADS)).astype(np.float32)
    blk = np.ones((c, c), np.float32)
    exb = np.zeros((128, 2 * BRANCH_W), np.float32)
    exa = np.zeros((128, 2 * BRANCH_W), np.float32)
    for n in range(2):
        for h in range(N_HEADS):
            lo = n * BRANCH_W + h * HEAD_V
            exb[SMALL_DNB + n * N_HEADS + h, lo:lo + HEAD_V] = 1.0
            exa[SMALL_DNA + n * N_HEADS + h, lo:lo + HEAD_V] = 1.0
    return tri, strict, eye, blk, exb, exa


@functools.lru_cache(maxsize=None)
def _att_consts():
    qm = np.zeros((2 * N_HEADS, BRANCH_W), np.float32)
    for h in range(N_HEADS):
        for mp in range(2):
            lo = h * HEAD_V + mp * DF_DH
            qm[2 * h + mp, lo:lo + DF_DH] = 1.0
    return qm


def _rope_tables():
    rows = T_LAT // GRID_W
    row = jnp.repeat(jnp.arange(rows), GRID_W).astype(F32)
    col = jnp.tile(jnp.arange(GRID_W), rows).astype(F32)
    half = DF_DH // 2
    inv = ROPE_BASE ** (-jnp.arange(0, half, 2, dtype=F32) / half)

    def angles(pos):
        a = pos[:, None] * inv[None, :]
        return jnp.concatenate([a, a], axis=-1)

    ang = jnp.concatenate([angles(row), angles(col)], axis=-1)
    reps = BRANCH_W // DF_DH
    return jnp.tile(jnp.cos(ang), (1, reps)), jnp.tile(jnp.sin(ang), (1, reps))


def _mod_kernel(c_ref, w_ref, b_ref, o_ref):
    a = _silu(c_ref[...])
    w = w_ref[0]
    out = None
    for ap in _split(a, 2):
        for wp in _split(w, 2):
            t = jnp.dot(ap, wp, preferred_element_type=F32)
            out = t if out is None else out + t
    o_ref[0] = out + b_ref[0]


def _mod_call(c_rows, w_mod, b_mod):
    n_t = (N_MOD * D_MODEL) // TN_MOD
    return pl.pallas_call(
        _mod_kernel,
        grid=(DEPTH, n_t),
        in_specs=[pl.BlockSpec((8, D_MODEL), lambda l, j: (0, 0)),
                  pl.BlockSpec((1, D_MODEL, TN_MOD), lambda l, j: (l, 0, j)),
                  pl.BlockSpec((1, 1, TN_MOD), lambda l, j: (l, 0, j))],
        out_specs=pl.BlockSpec((1, 8, TN_MOD), lambda l, j: (l, 0, j)),
        out_shape=jax.ShapeDtypeStruct((DEPTH, 8, N_MOD * D_MODEL), F32),
        compiler_params=_cparams(("arbitrary", "arbitrary")),
        name="mod_vectors",
    )(c_rows, w_mod, b_mod.reshape(DEPTH, 1, N_MOD * D_MODEL))


def _ffn_kernel(*refs, sub, final, split_in, split_out):
    n_x = 2 if split_in else 1
    mod_ref, nw_ref, wup_ref, wd_ref, fn_ref = refs[n_x:n_x + 5]
    outs = refs[n_x + 5:]
    is_lat = pl.program_id(0) >= N_CTX_TOK // TM_FFN
    x = jnp.where(is_lat, refs[1][...], refs[0][...]) if split_in else refs[0][...]
    sh = mod_ref[0, 3 * sub:3 * sub + 1, :]
    sc = mod_ref[0, 3 * sub + 1:3 * sub + 2, :]
    ga = mod_ref[0, 3 * sub + 2:3 * sub + 3, :]
    h = (_rms(x, nw_ref[...]) * (1.0 + sc) + sh).astype(BF)
    acc = None
    for f in range(D_FF // TF_FFN):
        lo = f * TF_FFN
        g = jnp.dot(h, wup_ref[:, lo:lo + TF_FFN], preferred_element_type=F32)
        u = jnp.dot(h, wup_ref[:, D_FF + lo:D_FF + lo + TF_FFN], preferred_element_type=F32)
        t = jnp.dot((_silu(g) * u).astype(BF), wd_ref[lo:lo + TF_FFN, :], preferred_element_type=F32)
        acc = t if acc is None else acc + t
    y = x + 0.5 * ga * acc
    if final:
        y = _rms(y, fn_ref[...])
    if split_out:
        @pl.when(jnp.logical_not(is_lat))
        def _():
            outs[0][...] = y

        @pl.when(is_lat)
        def _():
            outs[1][...] = y
    else:
        outs[0][...] = y


def _ffn_call(xs, mod_l, nw, w_in, w_down, final_w, sub, final, split_out):
    tm = TM_FFN
    n_ctx = N_CTX_TOK // tm
    ctx_map = lambda i: (jnp.minimum(i, n_ctx - 1), 0)
    lat_map = lambda i: (jnp.maximum(i - n_ctx, 0), 0)
    tile = lambda m: pl.BlockSpec((tm, D_MODEL), m)
    resident = lambda a: pl.BlockSpec(a.shape, lambda i: (0, 0), pipeline_mode=pl.Buffered(1))
    split_in = len(xs) == 2
    in_specs = [tile(ctx_map), tile(lat_map)] if split_in else [tile(lambda i: (i, 0))]
    in_specs += [pl.BlockSpec((1, N_MOD, D_MODEL), lambda i: (_mod_row(i, tm), 0, 0)),
                 pl.BlockSpec((1, D_MODEL), lambda i: (0, 0)),
                 resident(w_in), resident(w_down),
                 pl.BlockSpec((1, D_MODEL), lambda i: (0, 0))]
    if split_out:
        out_specs = [tile(ctx_map), tile(lat_map)]
        out_shape = [jax.ShapeDtypeStruct((N_CTX_TOK, D_MODEL), F32),
                     jax.ShapeDtypeStruct((N_TOK - N_CTX_TOK, D_MODEL), F32)]
    else:
        out_specs = [tile(lambda i: (i, 0))]
        out_shape = [jax.ShapeDtypeStruct((N_TOK, D_MODEL), F32)]
    return pl.pallas_call(
        functools.partial(_ffn_kernel, sub=sub, final=final, split_in=split_in, split_out=split_out),
        grid=(N_TOK // tm,),
        in_specs=in_specs,
        out_specs=out_specs,
        out_shape=out_shape,
        compiler_params=_cparams(("arbitrary",)),
        name="swiglu_half_step",
    )(*xs, mod_l, nw, w_in, w_down, final_w)


def _proj_kernel(x_ref, mod_ref, nw_ref, w_ref, o_ref):
    sh = mod_ref[0, 3:4, :]
    sc = mod_ref[0, 4:5, :]
    h = (_rms(x_ref[...], nw_ref[...]) * (1.0 + sc) + sh).astype(BF)
    o_ref[...] = jnp.dot(h, w_ref[...], preferred_element_type=F32)


def _proj_call(x, mod_l, nw, w_in_p):
    tm = TM_PROJ
    return pl.pallas_call(
        _proj_kernel,
        grid=(N_TOK // tm,),
        in_specs=[pl.BlockSpec((tm, D_MODEL), lambda i: (i, 0)),
                  pl.BlockSpec((1, N_MOD, D_MODEL), lambda i: (_mod_row(i, tm), 0, 0)),
                  pl.BlockSpec((1, D_MODEL), lambda i: (0, 0)),
                  pl.BlockSpec((D_MODEL, N_IN_PAD), lambda i: (0, 0))],
        out_specs=pl.BlockSpec((tm, N_IN_PAD), lambda i: (i, 0)),
        out_shape=jax.ShapeDtypeStruct((N_TOK, N_IN_PAD), F32),
        compiler_params=_cparams(("arbitrary",)),
        name="mixer_in_proj",
    )(x, mod_l, nw, w_in_p)


def _merge_kernel(x_ref, mod_ref, nw_ref, *rest):
    ctx_refs, lat_refs = rest[0:4], rest[4:8]
    wg_ref, wb_ref, wo_ref, o_ref = rest[8:12]
    x = x_ref[...]
    sh = mod_ref[0, 3:4, :]
    sc = mod_ref[0, 4:5, :]
    ga = mod_ref[0, 5:6, :]
    h = (_rms(x, nw_ref[...]) * (1.0 + sc) + sh).astype(BF)
    is_lat = pl.program_id(0) >= N_CTX_TOK // TM_MERGE
    mixed = None
    for n in range(4):
        gate = _sigmoid(jnp.dot(h, wg_ref[:, n * D_MODEL:(n + 1) * D_MODEL], preferred_element_type=F32))
        br = jnp.where(is_lat, lat_refs[n][...], ctx_refs[n][...])
        up = jnp.dot(br.astype(BF), wb_ref[n], preferred_element_type=F32)
        mixed = gate * up if mixed is None else mixed + gate * up
    out = jnp.dot(mixed.astype(BF), wo_ref[...], preferred_element_type=F32)
    o_ref[...] = x + ga * out


def _merge_call(x, mod_l, nw, ctx_branches, lat_branches, w_mgate, w_branch, w_out):
    tm = TM_MERGE
    n_ctx = N_CTX_TOK // tm
    cspec = pl.BlockSpec((tm, BRANCH_W), lambda i: (jnp.minimum(i, n_ctx - 1), 0))
    lspec = pl.BlockSpec((tm, BRANCH_W), lambda i: (jnp.maximum(i - n_ctx, 0), 0))
    return pl.pallas_call(
        _merge_kernel,
        grid=(N_TOK // tm,),
        in_specs=[pl.BlockSpec((tm, D_MODEL), lambda i: (i, 0)),
                  pl.BlockSpec((1, N_MOD, D_MODEL), lambda i: (_mod_row(i, tm), 0, 0)),
                  pl.BlockSpec((1, D_MODEL), lambda i: (0, 0)),
                  cspec, cspec, cspec, cspec, lspec, lspec, lspec, lspec,
                  pl.BlockSpec((D_MODEL, 4 * D_MODEL), lambda i: (0, 0)),
                  pl.BlockSpec((4, BRANCH_W, D_MODEL), lambda i: (0, 0, 0)),
                  pl.BlockSpec((D_MODEL, D_MODEL), lambda i: (0, 0))],
        out_specs=pl.BlockSpec((tm, D_MODEL), lambda i: (i, 0)),
        out_shape=jax.ShapeDtypeStruct((N_TOK, D_MODEL), F32),
        compiler_params=_cparams(("arbitrary",)),
        name="gated_merge",
    )(x, mod_l, nw, *ctx_branches, *lat_branches, w_mgate, w_branch, w_out)


def _gated_chunk(qc, kc, vc, gc, st, tri_ref, sg_ref, cf_ref, pm_ref, hm_ref, hv_ref, bdt_ref, reverse):
    c = CHUNK
    w = gc.shape[1]
    cum = _sel_l(tri_ref[...], gc, 2)
    kst = _stack_heads_bf(kc, hm_ref)
    qb = qc.astype(BF)
    a = pm_ref[N_LEVELS] * _mm_nt(qb, kst)
    lv = 0
    s = c // 2
    while s >= 1:
        if s >= SMALL_LEVEL:
            first = s if reverse else s - 1
            bnd = [jnp.broadcast_to(cum[p * 2 * s + first:p * 2 * s + first + 1, :], (2 * s, w))
                   for p in range(c // (2 * s))]
            bnd = jnp.concatenate(bnd, axis=0) if len(bnd) > 1 else bnd[0]
            dl = (cum - bnd) * sg_ref[lv]
        elif s == 2:
            dl = (gc * cf_ref[1] + pltpu.roll(gc, 1, 0) * cf_ref[2] + pltpu.roll(gc, c - 1, 0) * cf_ref[3])
        else:
            dl = gc * cf_ref[0]
        el = jnp.exp(dl).astype(BF)
        a = a + pm_ref[lv] * _mm_nt(qb * el, kst * jnp.concatenate([el] * N_HEADS, axis=0))
        lv += 1
        s //= 2
    tot = cum[0:1, :] if reverse else cum[c - 1:c, :]
    o = _mm(a, _stack_heads_bf(vc, hv_ref)) + _mm_nt(qc * jnp.exp(cum), st)
    st_new = st * jnp.exp(tot) + bdt_ref[...] * _mm_tn(vc, kc * jnp.exp(tot - cum))
    return o, st_new


def _gated_scan(q_ref, kf_ref, kb_ref, v_ref, gf_ref, gb_ref, of_ref, ob_ref, s0f, s0b, cf, cb, n_chunks):
    def body(t, carry):
        sf, sb = carry
        rf = pl.ds(pl.multiple_of(t * CHUNK, CHUNK), CHUNK)
        rb = pl.ds(pl.multiple_of((n_chunks - 1 - t) * CHUNK, CHUNK), CHUNK)
        o_f, sf = _gated_chunk(q_ref[rf, :], kf_ref[rf, :], v_ref[rf, :], gf_ref[rf, :], sf, *cf, False)
        o_b, sb = _gated_chunk(q_ref[rb, :], kb_ref[rb, :], v_ref[rb, :], gb_ref[rb, :], sb, *cb, True)
        of_ref[rf, :] = o_f
        ob_ref[rb, :] = o_b
        return sf, sb

    return lax.fori_loop(0, n_chunks, body, (s0f, s0b))


def _write_states(st_ref, st_f, st_b, dk, eye_ref):
    for d, st in enumerate((st_f, st_b)):
        for h in range(N_HEADS):
            tr = _sel_tn(st[h * HEAD_V:(h + 1) * HEAD_V, :], eye_ref[...])
            st_ref[0, d, h] = tr[h * dk:(h + 1) * dk, :]


def _gla_kernel(*refs, t_len, has_state):
    blk_ref, small_ref, w2_ref, gb_ref, nw_ref = refs[:5]
    pos = 5 + N_SCAN_CONSTS
    cf, cb, ones_ref, eye_ref = _split_scan_consts(refs[5:pos])
    if has_state:
        s0_ref = refs[pos]
        pos += 1
    out_ref = refs[pos]
    pos += 1
    if not has_state:
        st_ref = refs[pos]
        pos += 1
    q_s, gf_s, gb_s, of_s, ob_s = refs[pos:pos + 5]

    q_s[...] = blk_ref[:, 0:128] * (GLA_DK ** -0.5)
    z = _mm(small_ref[...], w2_ref[...]) + gb_ref[...]
    g = _log_sigmoid(z) * (1.0 / GLA_TAU)
    gf_s[...] = g[:, 0:128]
    gb_s[...] = g[:, 128:256]
    k_ref = blk_ref.at[:, 128:256]
    v_ref = blk_ref.at[:, 256:512]
    n_chunks = t_len // CHUNK
    if has_state:
        s0f, s0b = s0_ref[0, 0], s0_ref[0, 1]
    else:
        s0f = s0b = jnp.zeros((BRANCH_W, GLA_KW), F32)
    s_f, s_b = _gated_scan(q_s, k_ref, k_ref, v_ref, gf_s, gb_s, of_s, ob_s, s0f, s0b, cf, cb, n_chunks)
    out_ref[...] = _head_norm_gate(of_s[...] + ob_s[...], blk_ref[:, 512:768], nw_ref[...], ones_ref[...])
    if not has_state:
        _write_states(st_ref, s_f, s_b, GLA_DK, eye_ref)


def _const_spec(a):
    nd = a.ndim
    return pl.BlockSpec(a.shape, lambda b, _n=nd: (0,) * _n)


def _scan_const_arrays(w):
    out = []
    for reverse in (False, True):
        tri, sg, cf, pm = _scan_consts(w, reverse)
        out += [jnp.asarray(tri, BF), jnp.asarray(sg, F32), jnp.asarray(cf, F32), jnp.asarray(pm, F32)]
    hm, hv, bd, ones_bd = _head_consts(w)
    out += [jnp.asarray(hm, F32), jnp.asarray(hv, F32), jnp.asarray(bd.T, F32), jnp.asarray(ones_bd, BF),
            jnp.eye(HEAD_V, dtype=BF)]
    return tuple(out)


def _split_scan_consts(refs):
    shared = tuple(refs[8:11])
    return tuple(refs[0:4]) + shared, tuple(refs[4:8]) + shared, refs[11], refs[12]


def _gla_call(proj, w2bd, gbias, nw, s0, t_len, n_seq, row_blk0):
    has_state = s0 is not None
    consts = (w2bd, gbias, nw) + _scan_const_arrays(GLA_KW)
    in_specs = [pl.BlockSpec((t_len, 768), lambda b: (row_blk0 + b, 0)),
                pl.BlockSpec((t_len, 128), lambda b: (row_blk0 + b, COL_SMALL_BLOCK))]
    in_specs += [_const_spec(a) for a in consts]
    args = [proj, proj, *consts]
    if has_state:
        in_specs.append(pl.BlockSpec((1, 2, BRANCH_W, GLA_KW), lambda b: (b, 0, 0, 0)))
        args.append(s0)
    out_shape = [jax.ShapeDtypeStruct((n_seq * t_len, BRANCH_W), F32)]
    out_specs = [pl.BlockSpec((t_len, BRANCH_W), lambda b: (b, 0))]
    if not has_state:
        out_shape.append(jax.ShapeDtypeStruct((n_seq, 2, N_HEADS, GLA_DK, HEAD_V), F32))
        out_specs.append(pl.BlockSpec((1, 2, N_HEADS, GLA_DK, HEAD_V), lambda b: (b, 0, 0, 0, 0)))
    return pl.pallas_call(
        functools.partial(_gla_kernel, t_len=t_len, has_state=has_state),
        grid=(n_seq,),
        in_specs=in_specs,
        out_specs=out_specs,
        out_shape=out_shape,
        scratch_shapes=[pltpu.VMEM((t_len, GLA_KW), F32)] * 3 + [pltpu.VMEM((t_len, BRANCH_W), F32)] * 2,
        compiler_params=_cparams(("arbitrary",)),
        name="gla_mixer",
    )(*args)


def _hgrn_kernel(*refs, t_len, has_state, layer):
    q_ref, f_ref, v_ref, gate_ref, lbl_ref, nw_ref = refs[:6]
    pos = 6 + N_SCAN_CONSTS
    cf, cb, ones_ref, eye_ref = _split_scan_consts(refs[6:pos])
    if has_state:
        s0_ref = refs[pos]
        pos += 1
    out_ref = refs[pos]
    pos += 1
    if not has_state:
        st_ref = refs[pos]
        pos += 1
    q_s, kf_s, kb_s, gf_s, gb_s, of_s, ob_s = refs[pos:pos + 7]

    lg = lbl_ref[...]
    mx = jnp.max(lg, axis=0, keepdims=True)
    ex = jnp.exp(lg - mx)
    p = ex / jnp.sum(ex, axis=0, keepdims=True)
    lb = jnp.sum(p[0:layer + 1], axis=0, keepdims=True) - p[0:1]

    q_s[...] = _silu(q_ref[...]) * (HG_DK ** -0.5)
    f = lb + (1.0 - lb) * _sigmoid(f_ref[...])
    kf_s[...] = 1.0 - f[:, 0:HG_W]
    kb_s[...] = 1.0 - f[:, HG_W:2 * HG_W]
    lf = jnp.log(f)
    gf_s[...] = lf[:, 0:HG_W]
    gb_s[...] = lf[:, HG_W:2 * HG_W]
    n_chunks = t_len // CHUNK
    if has_state:
        s0f, s0b = s0_ref[0, 0], s0_ref[0, 1]
    else:
        s0f = s0b = jnp.zeros((BRANCH_W, HG_W), F32)
    s_f, s_b = _gated_scan(q_s, kf_s, kb_s, v_ref, gf_s, gb_s, of_s, ob_s, s0f, s0b, cf, cb, n_chunks)
    out_ref[...] = _head_norm_gate(of_s[...] + ob_s[...], gate_ref[...], nw_ref[...], ones_ref[...])
    if not has_state:
        _write_states(st_ref, s_f, s_b, HG_DK, eye_ref)


def _hgrn_call(proj, lb_logits, nw, s0, t_len, n_seq, row_blk0, layer):
    has_state = s0 is not None
    consts = (lb_logits, nw) + _scan_const_arrays(HG_W)
    in_specs = [pl.BlockSpec((t_len, 256), lambda b: (row_blk0 + b, 7)),
                pl.BlockSpec((t_len, 512), lambda b: (row_blk0 + b, 4)),
                pl.BlockSpec((t_len, 256), lambda b: (row_blk0 + b, 10)),
                pl.BlockSpec((t_len, 256), lambda b: (row_blk0 + b, 11))]
    in_specs += [_const_spec(a) for a in consts]
    args = [proj, proj, proj, proj, *consts]
    if has_state:
        in_specs.append(pl.BlockSpec((1, 2, BRANCH_W, HG_W), lambda b: (b, 0, 0, 0)))
        args.append(s0)
    out_shape = [jax.ShapeDtypeStruct((n_seq * t_len, BRANCH_W), F32)]
    out_specs = [pl.BlockSpec((t_len, BRANCH_W), lambda b: (b, 0))]
    if not has_state:
        out_shape.append(jax.ShapeDtypeStruct((n_seq, 2, N_HEADS, HG_DK, HEAD_V), F32))
        out_specs.append(pl.BlockSpec((1, 2, N_HEADS, HG_DK, HEAD_V), lambda b: (b, 0, 0, 0, 0)))
    return pl.pallas_call(
        functools.partial(_hgrn_kernel, t_len=t_len, has_state=has_state, layer=layer),
        grid=(n_seq,),
        in_specs=in_specs,
        out_specs=out_specs,
        out_shape=out_shape,
        scratch_shapes=[pltpu.VMEM((t_len, HG_W), F32)] * 5 + [pltpu.VMEM((t_len, BRANCH_W), F32)] * 2,
        compiler_params=_cparams(("arbitrary",)),
        name="hgrn2_mixer",
    )(*args)


N_PAIR = 2
SOLVE_JB = 8
SOLVE_IB = 8


def _dn_solve_kernel(lt_ref, vb_ref, kb_ref, u_ref, w_ref):
    _dn_substitute(pl.program_id(1), lt_ref.at[0], vb_ref.at[0], kb_ref.at[0], u_ref.at[0], w_ref.at[0])


def _dn_substitute(rev, l_ref, vb_ref, kb_ref, u_ref, w_ref):
    c = CHUNK
    u_ref[...] = jnp.zeros_like(u_ref)
    w_ref[...] = jnp.zeros_like(w_ref)

    def outer(t, carry):
        i = t + rev * (c - 1 - 2 * t)

        def inner(jb, acc):
            au, aw = acc
            j0 = pl.multiple_of(jb * SOLVE_JB, SOLVE_JB)
            for r in range(SOLVE_JB):
                coef = l_ref[i, pl.ds(j0 + r, 1), :]
                au = au - coef * u_ref[j0 + r]
                aw = aw - coef * w_ref[j0 + r]
            return au, aw

        blk = lax.shift_right_logical(i, 3)
        lo = rev * blk
        hi = blk + 1 + rev * (c // SOLVE_JB - blk - 1)
        au, aw = lax.fori_loop(lo, hi, inner, (vb_ref[i], kb_ref[i]))
        u_ref[i] = au
        w_ref[i] = aw
        return carry

    lax.fori_loop(0, c, outer, 0)


def _dn_solve_packed_kernel(lt_ref, vb_ref, kb_ref, u_ref, w_ref, lp_s, vp_s, kp_s, up_s, wp_s):
    n_chunk = lt_ref.shape[-1]
    n_ib = CHUNK // SOLVE_IB
    s = pl.program_id(1)

    @pl.when(s < n_ib)
    def _():
        for ii in range(SOLVE_IB):
            for src, dst in ((lt_ref, lp_s), (vb_ref, vp_s), (kb_ref, kp_s)):
                dst[s * SOLVE_IB + ii] = jnp.concatenate(
                    [src[0, ii, h * HEAD_V:(h + 1) * HEAD_V, :] for h in range(N_HEADS)], axis=1)

    @pl.when(s == n_ib - 1)
    def _():
        _dn_substitute(pl.program_id(0), lp_s, vp_s, kp_s, up_s, wp_s)

    @pl.when(s >= n_ib)
    def _():
        for ii in range(SOLVE_IB):
            for src, dst in ((up_s, u_ref), (wp_s, w_ref)):
                row = src[(s - n_ib) * SOLVE_IB + ii]
                for h in range(N_HEADS):
                    dst[0, ii, h * HEAD_V:(h + 1) * HEAD_V, :] = row[:, h * n_chunk:(h + 1) * n_chunk]


def _dn_solve_call(lt, vbt, kbt):
    n_chunk = lt.shape[-1]
    shape = jax.ShapeDtypeStruct((2, CHUNK, BRANCH_W, n_chunk), F32)
    if n_chunk * N_HEADS == 128:
        n_ib = CHUNK // SOLVE_IB
        blk = (1, SOLVE_IB, BRANCH_W, n_chunk)
        spec = pl.BlockSpec(blk, lambda d, s: (d, jnp.minimum(s, n_ib - 1), 0, 0))
        ospec = pl.BlockSpec(blk, lambda d, s: (d, jnp.maximum(s - n_ib, 0), 0, 0))
        return pl.pallas_call(
            _dn_solve_packed_kernel,
            grid=(2, 2 * n_ib),
            in_specs=[spec, spec, spec],
            out_specs=[ospec, ospec],
            out_shape=[shape, shape],
            scratch_shapes=[pltpu.VMEM((CHUNK, HEAD_V, 128), F32)] * 5,
            compiler_params=_cparams(("arbitrary", "arbitrary")),
            name="deltanet_solve_packed",
        )(lt, vbt, kbt)
    spec = pl.BlockSpec((1, CHUNK, HEAD_V, n_chunk), lambda h, d: (d, 0, h, 0))
    return pl.pallas_call(
        _dn_solve_kernel,
        grid=(N_HEADS, 2),
        in_specs=[spec, spec, spec],
        out_specs=[spec, spec],
        out_shape=[shape, shape],
        compiler_params=_cparams(("arbitrary", "arbitrary")),
        name="deltanet_solve",
    )(lt, vbt, kbt)


def _chunks_to_lanes(a):
    return a.reshape(2, a.shape[1] // CHUNK, CHUNK, BRANCH_W).transpose(0, 2, 3, 1)


def _chunks_from_lanes(a):
    return a.transpose(0, 3, 1, 2).reshape(2, -1, BRANCH_W)


def _dn_build_kernel(x_ref, small_ref, cw_ref, alog_ref, dtb_ref,
                     tri_ref, strict_ref, eye_ref, blk_ref, exb_ref, exa_ref, hm_ref, ones_ref,
                     q_ref, k_ref, d_ref, l_ref, vb_ref, kb_ref, qkd_ref, v_ref, be_s, ge_s, *, t_len):
    n_rows = N_PAIR * t_len

    x = x_ref[...]
    row = lax.broadcasted_iota(jnp.int32, (n_rows, 1), 0) % t_len
    x_prev = jnp.where(row == 0, 0.0, pltpu.roll(x, 1, 0))
    x_next = jnp.where(row == t_len - 1, 0.0, pltpu.roll(x, n_rows - 1, 0))
    y = _silu(x_prev * cw_ref[0:1, :] + x * cw_ref[1:2, :] + x_next * cw_ref[2:3, :])
    cq, ck = y[:, 0:256], y[:, 256:512]
    v_ref[...] = y[:, 512:768]
    q_ref[...] = cq * lax.rsqrt(_sel_r(cq * cq, ones_ref[...], 2) + EPS) * (DN_DK ** -0.5)
    k_ref[...] = ck * lax.rsqrt(_sel_r(ck * ck, ones_ref[...], 2) + EPS)
    sm = small_ref[...]
    be_s[...] = _sel_r(_sigmoid(sm), exb_ref[...], 2)
    ge_s[...] = _sel_r(-jnp.exp(alog_ref[...]) * _softplus(sm + dtb_ref[...]), exa_ref[...], 2)

    def body(ci, carry):
        rows = pl.ds(pl.multiple_of(ci * CHUNK, CHUNK), CHUNK)
        qc, kc, vc = q_ref[rows, :], k_ref[rows, :], v_ref[rows, :]
        kst = _stack_heads(kc, hm_ref)
        kk = _mm_nt(kc, kst)
        qk = _mm_nt(qc, kst)
        for d in range(2):
            bexp = be_s[rows, d * BRANCH_W:(d + 1) * BRANCH_W]
            dexp = _sel_l(tri_ref[d], ge_s[rows, d * BRANCH_W:(d + 1) * BRANCH_W], 2)
            drow = _sel_l(blk_ref[...], dexp * eye_ref[...])
            dec_s = jnp.exp(jnp.where(strict_ref[d] > 0.5, dexp - drow, -1e30))
            d_ref[d, rows, :] = dexp
            l_ref[d, rows, :] = bexp * kk * dec_s
            vb_ref[d, rows, :] = vc * bexp
            kb_ref[d, rows, :] = kc * bexp * jnp.exp(dexp)
            qkd_ref[d, rows, :] = (qk * (dec_s + eye_ref[...])).astype(BF)
        return carry

    lax.fori_loop(0, n_rows // CHUNK, body, 0)


def _dn_build_call(proj, conv_w, alog_row, dtb_row, t_len, n_seq, row_blk0):
    tri, strict, eye, blk, exb, exa = _dn_consts()
    tri, blk, exb, exa = (jnp.asarray(a, BF) for a in (tri, blk, exb, exa))
    strict, eye = jnp.asarray(strict, F32), jnp.asarray(eye, F32)
    hm, _, _, ones_bd = _head_consts(BRANCH_W)
    hm, ones_bd = jnp.asarray(hm, F32), jnp.asarray(ones_bd, BF)
    n_rows = N_PAIR * t_len
    n_tok = n_seq * t_len
    consts = (conv_w, alog_row, dtb_row, tri, strict, eye, blk, exb, exa, hm, ones_bd)
    in_specs = [pl.BlockSpec((n_rows, 768), lambda b: (row_blk0 + b, 1)),
                pl.BlockSpec((n_rows, 128), lambda b: (row_blk0 + b, COL_SMALL_BLOCK))]
    in_specs += [_const_spec(a) for a in consts]
    tok_spec = pl.BlockSpec((n_rows, BRANCH_W), lambda b: (b, 0))
    dir_spec = pl.BlockSpec((2, n_rows, BRANCH_W), lambda b: (0, b, 0))
    tok = jax.ShapeDtypeStruct((n_tok, BRANCH_W), F32)
    per_dir = jax.ShapeDtypeStruct((2, n_tok, BRANCH_W), F32)
    return pl.pallas_call(
        functools.partial(_dn_build_kernel, t_len=t_len),
        grid=(n_seq // N_PAIR,),
        in_specs=in_specs,
        out_specs=[tok_spec] * 2 + [dir_spec] * 5,
        out_shape=[tok] * 2 + [per_dir] * 4 + [jax.ShapeDtypeStruct((2, n_tok, BRANCH_W), BF)],
        scratch_shapes=[pltpu.VMEM((n_rows, BRANCH_W), F32)] + [pltpu.VMEM((n_rows, 2 * BRANCH_W), F32)] * 2,
        compiler_params=_cparams(("arbitrary",)),
        name="deltanet_build",
    )(proj, proj, *consts)


def _dn_scan_kernel(*refs, t_len, has_state):
    (q_ref, k_ref, d_ref, u_ref, w_ref, qkd_ref, gate_ref, nw_ref, hv_ref, bd_ref, ones_ref) = refs[:11]
    pos = 11
    if has_state:
        s0_ref = refs[pos]
        pos += 1
    out_ref = refs[pos]
    pos += 1
    if not has_state:
        st_ref = refs[pos]
        pos += 1
    of_s, ob_s = refs[pos:pos + 2]
    c = CHUNK
    n_chunks = t_len // c

    def step(s, rows, d, o_s):
        qc, kc = q_ref[rows, :], k_ref[rows, :]
        dexp = d_ref[d, rows, :]
        v_new = u_ref[d, rows, :] - _mm(w_ref[d, rows, :], s)
        o_s[rows, :] = _mm(qc * jnp.exp(dexp), s) + _mm(qkd_ref[d, rows, :], _stack_heads(v_new, hv_ref))
        dl = dexp[0:1, :] if d == 1 else dexp[c - 1:c, :]
        return s * jnp.exp(dl) + bd_ref[...] * _mm_tn(kc * jnp.exp(dl - dexp), v_new)

    def body(t, carry):
        new = []
        for sq in range(N_PAIR):
            rf = pl.ds(pl.multiple_of(sq * t_len + t * c, c), c)
            rb = pl.ds(pl.multiple_of(sq * t_len + (n_chunks - 1 - t) * c, c), c)
            new.append(step(carry[2 * sq], rf, 0, of_s))
            new.append(step(carry[2 * sq + 1], rb, 1, ob_s))
        return tuple(new)

    if has_state:
        init = tuple(s0_ref[sq, d] for sq in range(N_PAIR) for d in range(2))
    else:
        init = tuple(jnp.zeros((BRANCH_W, BRANCH_W), F32) for _ in range(2 * N_PAIR))
    finals = lax.fori_loop(0, n_chunks, body, init)
    out_ref[...] = _head_norm_gate(of_s[...] + ob_s[...], gate_ref[...], nw_ref[...], ones_ref[...])
    if not has_state:
        for sq in range(N_PAIR):
            for d in range(2):
                for h in range(N_HEADS):
                    st_ref[sq, d, h] = _head_lanes(finals[2 * sq + d][h * DN_DK:(h + 1) * DN_DK, :], h)


def _dn_scan_call(proj, q, k, dd, u, w, qkd, nw, s0, t_len, n_seq, row_blk0):
    _, hv, bd, ones_bd = _head_consts(BRANCH_W)
    hv, bd, ones_bd = jnp.asarray(hv, F32), jnp.asarray(bd, F32), jnp.asarray(ones_bd, BF)
    has_state = s0 is not None
    n_rows = N_PAIR * t_len
    consts = (nw, hv, bd, ones_bd)
    tok_spec = pl.BlockSpec((n_rows, BRANCH_W), lambda b: (b, 0))
    dir_spec = pl.BlockSpec((2, n_rows, BRANCH_W), lambda b: (0, b, 0))
    in_specs = [tok_spec, tok_spec, dir_spec, dir_spec, dir_spec, dir_spec,
                pl.BlockSpec((n_rows, 256), lambda b: (row_blk0 + b, 6))]
    in_specs += [_const_spec(a) for a in consts]
    args = [q, k, dd, u, w, qkd, proj, *consts]
    if has_state:
        in_specs.append(pl.BlockSpec((N_PAIR, 2, BRANCH_W, BRANCH_W), lambda b: (b, 0, 0, 0)))
        args.append(s0)
    out_shape = [jax.ShapeDtypeStruct((n_seq * t_len, BRANCH_W), F32)]
    out_specs = [pl.BlockSpec((n_rows, BRANCH_W), lambda b: (b, 0))]
    if not has_state:
        out_shape.append(jax.ShapeDtypeStruct((n_seq, 2, N_HEADS, DN_DK, HEAD_V), F32))
        out_specs.append(pl.BlockSpec((N_PAIR, 2, N_HEADS, DN_DK, HEAD_V), lambda b: (b, 0, 0, 0, 0)))
    return pl.pallas_call(
        functools.partial(_dn_scan_kernel, t_len=t_len, has_state=has_state),
        grid=(n_seq // N_PAIR,),
        in_specs=in_specs,
        out_specs=out_specs,
        out_shape=out_shape,
        scratch_shapes=[pltpu.VMEM((n_rows, BRANCH_W), F32)] * 2,
        compiler_params=_cparams(("arbitrary",)),
        name="deltanet_scan",
    )(*args)


def _rope(x, cos, sin):
    lane = lax.broadcasted_iota(jnp.int32, x.shape, 1) % 16
    n = x.shape[1]
    xrot = jnp.where(lane < 8, -pltpu.roll(x, n - 8, 1), pltpu.roll(x, 8, 1))
    return x * cos + xrot * sin


def _att_kernel(*refs, t_len, lat, lam_init):
    blk_ref, lam_ref, nw_ref, qm_ref, hv_ref, ones_ref = refs[:6]
    pos = 6
    if lat:
        cos_ref, sin_ref, ck_ref, cv_ref = refs[pos:pos + 4]
        pos += 4
    out_ref = refs[pos]
    pos += 1
    if not lat:
        nk_ref, nv_ref = refs[pos:pos + 2]
        pos += 2
    if lat:
        q_s, k_s, v_s = refs[pos:pos + 3]

    lv = lam_ref[...]
    lam = (jnp.exp(jnp.sum(lv[0:1] * lv[1:2], axis=1, keepdims=True))
           - jnp.exp(jnp.sum(lv[2:3] * lv[3:4], axis=1, keepdims=True)) + lam_init)
    q = blk_ref[:, 0:256]
    k = blk_ref[:, 256:512]
    v = blk_ref[:, 512:768]
    if lat:
        cos, sin = cos_ref[...], sin_ref[...]
        q_s[...] = _rope(q, cos, sin)
        k_s[0:PAST_LEN, :] = ck_ref[0]
        k_s[PAST_LEN:PAST_LEN + t_len, :] = _rope(k, cos, sin)
        v_s[0:PAST_LEN, :] = cv_ref[0]
        v_s[PAST_LEN:PAST_LEN + t_len, :] = v
        keys = k_s[...].astype(BF)
        vals = v_s[...].astype(BF)
    else:
        keys = k.astype(BF)
        vals = v.astype(BF)
        for h in range(N_HEADS):
            nk_ref[0, h] = _head_lanes(k, h)
            nv_ref[0, h] = _head_lanes(v, h)
    tq = TQ_ATT
    scale = DF_DH ** -0.5
    for qi in range(t_len // tq):
        qt = q_s[qi * tq:(qi + 1) * tq, :] if lat else q[qi * tq:(qi + 1) * tq, :]
        qs = jnp.concatenate([qt * qm_ref[r:r + 1, :] for r in range(2 * N_HEADS)], axis=0)
        s = _mm_nt(qs, keys) * scale
        s = jnp.exp(s - jnp.max(s, axis=1, keepdims=True))
        p = s / jnp.sum(s, axis=1, keepdims=True)
        a = jnp.concatenate(
            [p[(2 * h) * tq:(2 * h + 1) * tq] - lam * p[(2 * h + 1) * tq:(2 * h + 2) * tq]
             for h in range(N_HEADS)], axis=0)
        o = _head_diag(_mm(a, vals), hv_ref, tq)
        ms = _sel_r(o * o, ones_ref[...], 2) * (1.0 / HEAD_V)
        out_ref[qi * tq:(qi + 1) * tq, :] = o * lax.rsqrt(ms + EPS) * nw_ref[...] * (1.0 - lam_init)


def _att_call(proj, lam_p, nw, cache_k, cache_v, rope, t_len, n_seq, row_blk0, lam_init):
    qm = jnp.asarray(_att_consts(), F32)
    _, hv, _, ones_bd = _head_consts(BRANCH_W)
    hv, ones_bd = jnp.asarray(hv, F32), jnp.asarray(ones_bd, BF)
    lat = cache_k is not None
    consts = (lam_p, nw, qm, hv, ones_bd)
    in_specs = [pl.BlockSpec((t_len, 768), lambda b: (row_blk0 + b, 4))]
    in_specs += [_const_spec(a) for a in consts]
    args = [proj, *consts]
    out_shape = [jax.ShapeDtypeStruct((n_seq * t_len, BRANCH_W), F32)]
    out_specs = [pl.BlockSpec((t_len, BRANCH_W), lambda b: (b, 0))]
    scratch = []
    if lat:
        cos, sin = rope
        in_specs += [_const_spec(cos), _const_spec(sin),
                     pl.BlockSpec((1, PAST_LEN, BRANCH_W), lambda b: (b, 0, 0)),
                     pl.BlockSpec((1, PAST_LEN, BRANCH_W), lambda b: (b, 0, 0))]
        args += [cos, sin, cache_k, cache_v]
        scratch = [pltpu.VMEM((t_len, BRANCH_W), F32),
                   pltpu.VMEM((PAST_LEN + t_len, BRANCH_W), F32),
                   pltpu.VMEM((PAST_LEN + t_len, BRANCH_W), F32)]
    else:
        for _ in range(2):
            out_shape.append(jax.ShapeDtypeStruct((n_seq, N_HEADS, t_len, HEAD_V), F32))
            out_specs.append(pl.BlockSpec((1, N_HEADS, t_len, HEAD_V), lambda b: (b, 0, 0, 0)))
    return pl.pallas_call(
        functools.partial(_att_kernel, t_len=t_len, lat=lat, lam_init=lam_init),
        grid=(n_seq,),
        in_specs=in_specs,
        out_specs=out_specs,
        out_shape=out_shape,
        scratch_shapes=scratch,
        compiler_params=_cparams(("arbitrary",)),
        name="diff_attention",
    )(*args)


def _block_diag_states(st, dk, transposed):
    eye = jnp.eye(N_HEADS, dtype=st.dtype)
    b = st.shape[0]
    if transposed:
        return jnp.einsum('bnhde,hg->bnhegd', st, eye).reshape(b, 2, N_HEADS * HEAD_V, N_HEADS * dk)
    return jnp.einsum('bnhde,hg->bnhdge', st, eye).reshape(b, 2, N_HEADS * dk, N_HEADS * HEAD_V)


def _in_perm():
    offs = np.concatenate([[0], np.cumsum(IN_ORIG)])
    seg = lambda a, b: np.arange(offs[a], offs[b])
    return np.concatenate([seg(0, 4), seg(5, 6), seg(8, 9), seg(9, 16), seg(4, 5), seg(6, 8)])


def kernel(x_prompt, x_sample, cache_diff_k, cache_diff_v, state_gla, state_dn, state_hgrn, c, c_ctx,
           norm_w, w_mod, b_mod, ffn1_in, ffn1_down, ffn2_in, ffn2_down, w_in, gla_w2, gla_b, gla_norm,
           dn_conv, dn_a_log, dn_dt_bias, dn_norm, hg_lb_logits, hg_norm, diff_lambda, diff_norm,
           w_branch, w_mgate, w_out, final_norm):
    xs = (x_prompt.reshape(N_CTX_TOK, D_MODEL), x_sample.reshape(-1, D_MODEL))
    c_rows = jnp.concatenate([c_ctx[None, :], c, jnp.zeros((8 - 1 - N_LAT_SEQ, D_MODEL), F32)], axis=0)
    mod = _mod_call(c_rows, w_mod, b_mod).reshape(DEPTH, 8, N_MOD, D_MODEL)
    rope = _rope_tables()
    perm = _in_perm()
    lb_logits = hg_lb_logits.reshape(DEPTH, 2 * HG_W)
    lat_blk = N_CTX_TOK // T_LAT
    tile4 = lambda a: jnp.tile(a, N_HEADS)[None, :]
    fin = final_norm[None, :]
    new_k, new_v, new_gla, new_dn, new_hg = [], [], [], [], []
    for l in range(DEPTH):
        mod_l = mod[l]
        lam_init = 0.8 - 0.6 * math.exp(-0.3 * l)
        w_in_p = jnp.pad(w_in[l][:, perm], ((0, 0), (0, N_IN_PAD - N_IN))).astype(BF)
        w2bd = jnp.zeros((128, 2 * GLA_KW), F32)
        w2bd = w2bd.at[0:GLA_LOWRANK, 0:GLA_KW].set(gla_w2[l, 0])
        w2bd = w2bd.at[GLA_LOWRANK:2 * GLA_LOWRANK, GLA_KW:].set(gla_w2[l, 1]).astype(BF)
        gbias = gla_b[l].reshape(1, 2 * GLA_KW)
        alog_row = jnp.zeros((1, 128), F32).at[0, SMALL_DNA:SMALL_DNA + 8].set(dn_a_log[l].reshape(-1))
        dtb_row = jnp.zeros((1, 128), F32).at[0, SMALL_DNA:SMALL_DNA + 8].set(dn_dt_bias[l].reshape(-1))

        (x,) = _ffn_call(xs, mod_l, norm_w[l, 0][None, :], ffn1_in[l].astype(BF), ffn1_down[l].astype(BF),
                         fin, 0, False, False)
        proj = _proj_call(x, mod_l, norm_w[l, 1][None, :], w_in_p)

        a_c, st_a = _gla_call(proj, w2bd, gbias, tile4(gla_norm[l]), None, T_CTX, N_CTX_SEQ, 0)
        dn_lat_blk = N_CTX_TOK // (N_PAIR * T_LAT)
        q_c, k_c, dd_c, lw_c, vb_c, kb_c, qkd_c = _dn_build_call(proj, dn_conv[l], alog_row, dtb_row,
                                                                T_CTX, N_CTX_SEQ, 0)
        q_l, k_l, dd_l, lw_l, vb_l, kb_l, qkd_l = _dn_build_call(proj, dn_conv[l], alog_row, dtb_row,
                                                                T_LAT, N_LAT_SEQ, dn_lat_blk)
        u_c, w_c = (_chunks_from_lanes(a) for a in _dn_solve_call(
            _chunks_to_lanes(lw_c), _chunks_to_lanes(vb_c), _chunks_to_lanes(kb_c)))
        u_l, w_l = (_chunks_from_lanes(a) for a in _dn_solve_call(
            _chunks_to_lanes(lw_l), _chunks_to_lanes(vb_l), _chunks_to_lanes(kb_l)))
        b_c, st_b = _dn_scan_call(proj, q_c, k_c, dd_c, u_c, w_c, qkd_c, tile4(dn_norm[l]), None,
                                  T_CTX, N_CTX_SEQ, 0)
        c_c, st_c = _hgrn_call(proj, lb_logits, tile4(hg_norm[l]), None, T_CTX, N_CTX_SEQ, 0, l)
        d_c, new_k_l, new_v_l = _att_call(proj, diff_lambda[l], tile4(diff_norm[l]), None, None, None,
                                  T_CTX, N_CTX_SEQ, 0, lam_init)
        ck = cache_diff_k[:, l].transpose(0, 2, 1, 3).reshape(N_LAT_SEQ, PAST_LEN, BRANCH_W)
        cv = cache_diff_v[:, l].transpose(0, 2, 1, 3).reshape(N_LAT_SEQ, PAST_LEN, BRANCH_W)
        (a_l,) = _gla_call(proj, w2bd, gbias, tile4(gla_norm[l]),
                           _block_diag_states(state_gla[:, l], GLA_DK, True), T_LAT, N_LAT_SEQ, lat_blk)
        (b_l,) = _dn_scan_call(proj, q_l, k_l, dd_l, u_l, w_l, qkd_l, tile4(dn_norm[l]),
                               _block_diag_states(state_dn[:, l], DN_DK, False), T_LAT, N_LAT_SEQ, dn_lat_blk)
        (c_l,) = _hgrn_call(proj, lb_logits, tile4(hg_norm[l]),
                            _block_diag_states(state_hgrn[:, l], HG_DK, True), T_LAT, N_LAT_SEQ, lat_blk, l)
        (d_l,) = _att_call(proj, diff_lambda[l], tile4(diff_norm[l]), ck, cv, rope,
                           T_LAT, N_LAT_SEQ, lat_blk, lam_init)

        x = _merge_call(x, mod_l, norm_w[l, 1][None, :], (a_c, b_c, c_c, d_c), (a_l, b_l, c_l, d_l),
                        w_mgate[l].astype(BF), w_branch[l].astype(BF), w_out[l].astype(BF))
        xs = _ffn_call((x,), mod_l, norm_w[l, 2][None, :], ffn2_in[l].astype(BF), ffn2_down[l].astype(BF),
                       fin, 2, l == DEPTH - 1, l == DEPTH - 1)
        new_k.append(new_k_l)
        new_v.append(new_v_l)
        new_gla.append(st_a)
        new_dn.append(st_b)
        new_hg.append(st_c)
    y_prompt = xs[0].reshape(N_CTX_SEQ, T_CTX, D_MODEL)
    y_sample = xs[1].reshape(N_LAT_SEQ, T_LAT, D_MODEL)
    return (y_prompt, y_sample, jnp.stack(new_k, axis=1), jnp.stack(new_v, axis=1),
            jnp.stack(new_gla, axis=1), jnp.stack(new_dn, axis=1), jnp.stack(new_hg, axis=1))
```

```python
import functools
import math

import numpy as np
import jax
import jax.numpy as jnp
from jax import lax
from jax.experimental import pallas as pl
from jax.experimental.pallas import tpu as pltpu

F32 = jnp.float32
BF = jnp.bfloat16

D_MODEL = 1024
N_CTX_SEQ = 32
T_CTX = 256
DEPTH = 2
N_LAT_SEQ = 2
T_LAT = 1024
PAST_LEN = 512
GRID_W = 64
N_HEADS = 4
BRANCH_W = 256
HEAD_V = 64
GLA_DK = 32
GLA_KW = 128
GLA_LOWRANK = 16
GLA_TAU = 16.0
DN_DK = 64
HG_DK = 64
HG_W = 256
DF_DH = 32
ROPE_BASE = 10000.0
D_FF = 2816
N_MOD = 9
CHUNK = 64
EPS = 1e-6
N_CTX_TOK = N_CTX_SEQ * T_CTX
N_TOK = N_CTX_TOK + N_LAT_SEQ * T_LAT
N_LEVELS = 6
SMALL_LEVEL = 4
N_SCAN_CONSTS = 13

IN_ORIG = (128, 128, 256, 256, 32, 768, 8, 8, 256, 256, 512, 256, 256, 256, 256, 256)
N_IN = sum(IN_ORIG)
N_IN_PAD = 3968
COL_SMALL_BLOCK = 30
SMALL_LR = 0
SMALL_DNB = 32
SMALL_DNA = 40

VMEM_LIMIT = 56 * 1024 * 1024

TM_FFN = 512
TF_FFN = 1408
TM_PROJ = 512
TM_MERGE = 512
TN_MOD = 2304
TQ_ATT = 256


def _silu(x):
    return x * (1.0 / (1.0 + jnp.exp(-x)))


def _sigmoid(x):
    return 1.0 / (1.0 + jnp.exp(-x))


def _softplus(x):
    return jnp.maximum(x, 0.0) + jnp.log(1.0 + jnp.exp(-jnp.abs(x)))


def _log_sigmoid(x):
    return -_softplus(-x)


def _mm(a, b):
    return jnp.dot(a.astype(BF), b.astype(BF), preferred_element_type=F32)


def _mm_nt(a, b):
    return lax.dot_general(a.astype(BF), b.astype(BF), (((1,), (1,)), ((), ())),
                           preferred_element_type=F32)


def _mm_tn(a, b):
    return lax.dot_general(a.astype(BF), b.astype(BF), (((0,), (0,)), ((), ())),
                           preferred_element_type=F32)


def _split(x, n):
    parts = []
    r = x
    for i in range(n):
        p = r.astype(BF)
        parts.append(p)
        if i + 1 < n:
            r = r - p.astype(F32)
    return parts


def _sel_l(m01, x, n=3):
    out = None
    for p in _split(x, n):
        t = jnp.dot(m01, p, preferred_element_type=F32)
        out = t if out is None else out + t
    return out


def _sel_r(x, m01, n=3):
    out = None
    for p in _split(x, n):
        t = jnp.dot(p, m01, preferred_element_type=F32)
        out = t if out is None else out + t
    return out


def _sel_tn(x, m01, n=3):
    out = None
    for p in _split(x, n):
        t = lax.dot_general(p, m01, (((0,), (0,)), ((), ())), preferred_element_type=F32)
        out = t if out is None else out + t
    return out


def _rms(x, w):
    return x * lax.rsqrt(jnp.mean(x * x, axis=-1, keepdims=True) + EPS) * w


def _head_lanes(x, h):
    blk = x[:, (h // 2) * 128:(h // 2 + 1) * 128]
    if h % 2:
        blk = pltpu.roll(blk, 64, 1)
    return blk[:, :HEAD_V]


def _stack_heads(x, hm_ref):
    return jnp.concatenate([x * hm_ref[h:h + 1, :] for h in range(N_HEADS)], axis=0)


def _stack_heads_bf(x, hm_ref):
    xb = x.astype(BF)
    return jnp.concatenate([xb * hm_ref[h:h + 1, :].astype(BF) for h in range(N_HEADS)], axis=0)


def _head_diag(o_full, hv_ref, c):
    out = None
    for h in range(N_HEADS):
        t = o_full[h * c:(h + 1) * c, :] * hv_ref[h:h + 1, :]
        out = t if out is None else out + t
    return out


def _head_norm_gate(o, gate, nw, ones_bd):
    ms = _sel_r(o * o, ones_bd, 2) * (1.0 / HEAD_V)
    return o * lax.rsqrt(ms + EPS) * nw * _silu(gate)


def _mod_row(i, tm):
    return jnp.maximum(i * tm - (N_CTX_TOK - T_LAT), 0) // T_LAT


def _cparams(sem):
    return pltpu.CompilerParams(dimension_semantics=sem, vmem_limit_bytes=VMEM_LIMIT)


@functools.lru_cache(maxsize=None)
def _scan_consts(w, reverse):
    c = CHUNK
    idx = np.arange(c)
    i = idx[:, None]
    m = idx[None, :]
    pm, sg = [], []
    s = c // 2
    while s >= 1:
        par = idx // (2 * s)
        right = (idx % (2 * s)) >= s
        same = par[:, None] == par[None, :]
        query = ~right if reverse else right
        pm.append(same & query[:, None] & (~query)[None, :])
        if s >= SMALL_LEVEL:
            sg.append(np.where(query, 1.0, -1.0))
        s //= 2
    pm.append(i == m)
    tri = (m >= i) if reverse else (m <= i)
    r4, r2 = idx % 4, idx % 2
    if not reverse:
        cf = [r2 == 1, r4 >= 2, r4 == 3, r4 == 0]
    else:
        cf = [r2 == 0, r4 <= 1, r4 == 3, r4 == 0]
    wide = lambda rows: np.repeat(np.stack(rows).astype(np.float32)[:, :, None], w, axis=2)
    pmask = np.stack([np.tile(p, (1, N_HEADS)) for p in pm]).astype(np.float32)
    return tri.astype(np.float32), wide(sg), wide(cf), pmask


@functools.lru_cache(maxsize=None)
def _head_consts(w):
    dk = w // N_HEADS
    hm = np.zeros((N_HEADS, w), np.float32)
    hv = np.zeros((N_HEADS, BRANCH_W), np.float32)
    for h in range(N_HEADS):
        hm[h, h * dk:(h + 1) * dk] = 1.0
        hv[h, h * HEAD_V:(h + 1) * HEAD_V] = 1.0
    bd = hm.T @ hv
    ones_bd = hv.T @ hv
    return hm, hv, bd, ones_bd


@functools.lru_cache(maxsize=None)
def _dn_consts():
    c = CHUNK
    idx = np.arange(c)
    i = idx[:, None]
    j = idx[None, :]
    tri = np.stack([(j <= i), (j >= i)]).astype(np.float32)
    strict = np.stack([np.tile(j < i, (1, N_HEADS)), np.tile(j > i, (1, N_HEADS))]).astype(np.float32)
    eye = np.tile(np.eye(c), (1, N_HEADS)).astype(np.float32)
    blk = np.ones((c, c), np.float32)
    exb = np.zeros((128, 2 * BRANCH_W), np.float32)
    exa = np.zeros((128, 2 * BRANCH_W), np.float32)
    for n in range(2):
        for h in range(N_HEADS):
            lo = n * BRANCH_W + h * HEAD_V
            exb[SMALL_DNB + n * N_HEADS + h, lo:lo + HEAD_V] = 1.0
            exa[SMALL_DNA + n * N_HEADS + h, lo:lo + HEAD_V] = 1.0
    return tri, strict, eye, blk, exb, exa


@functools.lru_cache(maxsize=None)
def _att_consts():
    qm = np.zeros((2 * N_HEADS, BRANCH_W), np.float32)
    for h in range(N_HEADS):
        for mp in range(2):
            lo = h * HEAD_V + mp * DF_DH
            qm[2 * h + mp, lo:lo + DF_DH] = 1.0
    return qm


def _rope_tables():
    rows = T_LAT // GRID_W
    row = jnp.repeat(jnp.arange(rows), GRID_W).astype(F32)
    col = jnp.tile(jnp.arange(GRID_W), rows).astype(F32)
    half = DF_DH // 2
    inv = ROPE_BASE ** (-jnp.arange(0, half, 2, dtype=F32) / half)

    def angles(pos):
        a = pos[:, None] * inv[None, :]
        return jnp.concatenate([a, a], axis=-1)

    ang = jnp.concatenate([angles(row), angles(col)], axis=-1)
    reps = BRANCH_W // DF_DH
    return jnp.tile(jnp.cos(ang), (1, reps)), jnp.tile(jnp.sin(ang), (1, reps))


def _mod_kernel(c_ref, w_ref, b_ref, o_ref):
    a = _silu(c_ref[...])
    w = w_ref[0]
    out = None
    for ap in _split(a, 2):
        for wp in _split(w, 2):
            t = jnp.dot(ap, wp, preferred_element_type=F32)
            out = t if out is None else out + t
    o_ref[0] = out + b_ref[0]


def _mod_call(c_rows, w_mod, b_mod):
    n_t = (N_MOD * D_MODEL) // TN_MOD
    return pl.pallas_call(
        _mod_kernel,
        grid=(DEPTH, n_t),
        in_specs=[pl.BlockSpec((8, D_MODEL), lambda l, j: (0, 0)),
                  pl.BlockSpec((1, D_MODEL, TN_MOD), lambda l, j: (l, 0, j)),
                  pl.BlockSpec((1, 1, TN_MOD), lambda l, j: (l, 0, j))],
        out_specs=pl.BlockSpec((1, 8, TN_MOD), lambda l, j: (l, 0, j)),
        out_shape=jax.ShapeDtypeStruct((DEPTH, 8, N_MOD * D_MODEL), F32),
        compiler_params=_cparams(("arbitrary", "arbitrary")),
        name="mod_vectors",
    )(c_rows, w_mod, b_mod.reshape(DEPTH, 1, N_MOD * D_MODEL))


def _ffn_kernel(*refs, sub, final, split_in, split_out):
    n_x = 2 if split_in else 1
    mod_ref, nw_ref, wup_ref, wd_ref, fn_ref = refs[n_x:n_x + 5]
    outs = refs[n_x + 5:]
    is_lat = pl.program_id(0) >= N_CTX_TOK // TM_FFN
    x = jnp.where(is_lat, refs[1][...], refs[0][...]) if split_in else refs[0][...]
    sh = mod_ref[0, 3 * sub:3 * sub + 1, :]
    sc = mod_ref[0, 3 * sub + 1:3 * sub + 2, :]
    ga = mod_ref[0, 3 * sub + 2:3 * sub + 3, :]
    h = (_rms(x, nw_ref[...]) * (1.0 + sc) + sh).astype(BF)
    acc = None
    for f in range(D_FF // TF_FFN):
        lo = f * TF_FFN
        g = jnp.dot(h, wup_ref[:, lo:lo + TF_FFN], preferred_element_type=F32)
        u = jnp.dot(h, wup_ref[:, D_FF + lo:D_FF + lo + TF_FFN], preferred_element_type=F32)
        t = jnp.dot((_silu(g) * u).astype(BF), wd_ref[lo:lo + TF_FFN, :], preferred_element_type=F32)
        acc = t if acc is None else acc + t
    y = x + 0.5 * ga * acc
    if final:
        y = _rms(y, fn_ref[...])
    if split_out:
        @pl.when(jnp.logical_not(is_lat))
        def _():
            outs[0][...] = y

        @pl.when(is_lat)
        def _():
            outs[1][...] = y
    else:
        outs[0][...] = y


def _ffn_call(xs, mod_l, nw, w_in, w_down, final_w, sub, final, split_out):
    tm = TM_FFN
    n_ctx = N_CTX_TOK // tm
    ctx_map = lambda i: (jnp.minimum(i, n_ctx - 1), 0)
    lat_map = lambda i: (jnp.maximum(i - n_ctx, 0), 0)
    tile = lambda m: pl.BlockSpec((tm, D_MODEL), m)
    resident = lambda a: pl.BlockSpec(a.shape, lambda i: (0, 0), pipeline_mode=pl.Buffered(1))
    split_in = len(xs) == 2
    in_specs = [tile(ctx_map), tile(lat_map)] if split_in else [tile(lambda i: (i, 0))]
    in_specs += [pl.BlockSpec((1, N_MOD, D_MODEL), lambda i: (_mod_row(i, tm), 0, 0)),
                 pl.BlockSpec((1, D_MODEL), lambda i: (0, 0)),
                 resident(w_in), resident(w_down),
                 pl.BlockSpec((1, D_MODEL), lambda i: (0, 0))]
    if split_out:
        out_specs = [tile(ctx_map), tile(lat_map)]
        out_shape = [jax.ShapeDtypeStruct((N_CTX_TOK, D_MODEL), F32),
                     jax.ShapeDtypeStruct((N_TOK - N_CTX_TOK, D_MODEL), F32)]
    else:
        out_specs = [tile(lambda i: (i, 0))]
        out_shape = [jax.ShapeDtypeStruct((N_TOK, D_MODEL), F32)]
    return pl.pallas_call(
        functools.partial(_ffn_kernel, sub=sub, final=final, split_in=split_in, split_out=split_out),
        grid=(N_TOK // tm,),
        in_specs=in_specs,
        out_specs=out_specs,
        out_shape=out_shape,
        compiler_params=_cparams(("arbitrary",)),
        name="swiglu_half_step",
    )(*xs, mod_l, nw, w_in, w_down, final_w)


def _proj_kernel(x_ref, mod_ref, nw_ref, w_ref, o_ref):
    sh = mod_ref[0, 3:4, :]
    sc = mod_ref[0, 4:5, :]
    h = (_rms(x_ref[...], nw_ref[...]) * (1.0 + sc) + sh).astype(BF)
    o_ref[...] = jnp.dot(h, w_ref[...], preferred_element_type=F32)


def _proj_call(x, mod_l, nw, w_in_p):
    tm = TM_PROJ
    return pl.pallas_call(
        _proj_kernel,
        grid=(N_TOK // tm,),
        in_specs=[pl.BlockSpec((tm, D_MODEL), lambda i: (i, 0)),
                  pl.BlockSpec((1, N_MOD, D_MODEL), lambda i: (_mod_row(i, tm), 0, 0)),
                  pl.BlockSpec((1, D_MODEL), lambda i: (0, 0)),
                  pl.BlockSpec((D_MODEL, N_IN_PAD), lambda i: (0, 0))],
        out_specs=pl.BlockSpec((tm, N_IN_PAD), lambda i: (i, 0)),
        out_shape=jax.ShapeDtypeStruct((N_TOK, N_IN_PAD), F32),
        compiler_params=_cparams(("arbitrary",)),
        name="mixer_in_proj",
    )(x, mod_l, nw, w_in_p)


def _merge_kernel(x_ref, mod_ref, nw_ref, *rest):
    ctx_refs, lat_refs = rest[0:4], rest[4:8]
    wg_ref, wb_ref, wo_ref, o_ref = rest[8:12]
    x = x_ref[...]
    sh = mod_ref[0, 3:4, :]
    sc = mod_ref[0, 4:5, :]
    ga = mod_ref[0, 5:6, :]
    h = (_rms(x, nw_ref[...]) * (1.0 + sc) + sh).astype(BF)
    is_lat = pl.program_id(0) >= N_CTX_TOK // TM_MERGE
    mixed = None
    for n in range(4):
        gate = _sigmoid(jnp.dot(h, wg_ref[:, n * D_MODEL:(n + 1) * D_MODEL], preferred_element_type=F32))
        br = jnp.where(is_lat, lat_refs[n][...], ctx_refs[n][...])
        up = jnp.dot(br.astype(BF), wb_ref[n], preferred_element_type=F32)
        mixed = gate * up if mixed is None else mixed + gate * up
    out = jnp.dot(mixed.astype(BF), wo_ref[...], preferred_element_type=F32)
    o_ref[...] = x + ga * out


def _merge_call(x, mod_l, nw, ctx_branches, lat_branches, w_mgate, w_branch, w_out):
    tm = TM_MERGE
    n_ctx = N_CTX_TOK // tm
    cspec = pl.BlockSpec((tm, BRANCH_W), lambda i: (jnp.minimum(i, n_ctx - 1), 0))
    lspec = pl.BlockSpec((tm, BRANCH_W), lambda i: (jnp.maximum(i - n_ctx, 0), 0))
    return pl.pallas_call(
        _merge_kernel,
        grid=(N_TOK // tm,),
        in_specs=[pl.BlockSpec((tm, D_MODEL), lambda i: (i, 0)),
                  pl.BlockSpec((1, N_MOD, D_MODEL), lambda i: (_mod_row(i, tm), 0, 0)),
                  pl.BlockSpec((1, D_MODEL), lambda i: (0, 0)),
                  cspec, cspec, cspec, cspec, lspec, lspec, lspec, lspec,
                  pl.BlockSpec((D_MODEL, 4 * D_MODEL), lambda i: (0, 0)),
                  pl.BlockSpec((4, BRANCH_W, D_MODEL), lambda i: (0, 0, 0)),
                  pl.BlockSpec((D_MODEL, D_MODEL), lambda i: (0, 0))],
        out_specs=pl.BlockSpec((tm, D_MODEL), lambda i: (i, 0)),
        out_shape=jax.ShapeDtypeStruct((N_TOK, D_MODEL), F32),
        compiler_params=_cparams(("arbitrary",)),
        name="gated_merge",
    )(x, mod_l, nw, *ctx_branches, *lat_branches, w_mgate, w_branch, w_out)


def _gated_chunk(qc, kc, vc, gc, st, tri_ref, sg_ref, cf_ref, pm_ref, hm_ref, hv_ref, bdt_ref, reverse):
    c = CHUNK
    w = gc.shape[1]
    cum = _sel_l(tri_ref[...], gc, 2)
    kst = _stack_heads_bf(kc, hm_ref)
    qb = qc.astype(BF)
    a = pm_ref[N_LEVELS] * _mm_nt(qb, kst)
    lv = 0
    s = c // 2
    while s >= 1:
        if s >= SMALL_LEVEL:
            first = s if reverse else s - 1
            bnd = [jnp.broadcast_to(cum[p * 2 * s + first:p * 2 * s + first + 1, :], (2 * s, w))
                   for p in range(c // (2 * s))]
            bnd = jnp.concatenate(bnd, axis=0) if len(bnd) > 1 else bnd[0]
            dl = (cum - bnd) * sg_ref[lv]
        elif s == 2:
            dl = (gc * cf_ref[1] + pltpu.roll(gc, 1, 0) * cf_ref[2] + pltpu.roll(gc, c - 1, 0) * cf_ref[3])
        else:
            dl = gc * cf_ref[0]
        el = jnp.exp(dl).astype(BF)
        a = a + pm_ref[lv] * _mm_nt(qb * el, kst * jnp.concatenate([el] * N_HEADS, axis=0))
        lv += 1
        s //= 2
    tot = cum[0:1, :] if reverse else cum[c - 1:c, :]
    o = _mm(a, _stack_heads_bf(vc, hv_ref)) + _mm_nt(qc * jnp.exp(cum), st)
    st_new = st * jnp.exp(tot) + bdt_ref[...] * _mm_tn(vc, kc * jnp.exp(tot - cum))
    return o, st_new


def _gated_scan(q_ref, kf_ref, kb_ref, v_ref, gf_ref, gb_ref, of_ref, ob_ref, s0_ref, st_ref, cf, cb,
                t_len, w, dk, eye_ref):
    n_chunks = t_len // CHUNK

    def body(t, carry):
        new = []
        for sq in range(N_PAIR):
            rf = pl.ds(pl.multiple_of(sq * t_len + t * CHUNK, CHUNK), CHUNK)
            rb = pl.ds(pl.multiple_of(sq * t_len + (n_chunks - 1 - t) * CHUNK, CHUNK), CHUNK)
            o_f, sf = _gated_chunk(q_ref[rf, :], kf_ref[rf, :], v_ref[rf, :], gf_ref[rf, :],
                                   carry[2 * sq], *cf, False)
            o_b, sb = _gated_chunk(q_ref[rb, :], kb_ref[rb, :], v_ref[rb, :], gb_ref[rb, :],
                                   carry[2 * sq + 1], *cb, True)
            of_ref[rf, :] = o_f
            ob_ref[rb, :] = o_b
            new += [sf, sb]
        return tuple(new)

    if s0_ref is not None:
        init = tuple(s0_ref[sq, d] for sq in range(N_PAIR) for d in range(2))
    else:
        init = tuple(jnp.zeros((BRANCH_W, w), F32) for _ in range(2 * N_PAIR))
    finals = lax.fori_loop(0, n_chunks, body, init)
    if st_ref is not None:
        for sq in range(N_PAIR):
            for d in range(2):
                st = finals[2 * sq + d]
                for h in range(N_HEADS):
                    tr = _sel_tn(st[h * HEAD_V:(h + 1) * HEAD_V, :], eye_ref[...])
                    st_ref[sq, d, h] = tr[h * dk:(h + 1) * dk, :]


def _gla_kernel(*refs, t_len, has_state):
    blk_ref, small_ref, w2_ref, gb_ref, nw_ref = refs[:5]
    pos = 5 + N_SCAN_CONSTS
    cf, cb, ones_ref, eye_ref = _split_scan_consts(refs[5:pos])
    s0_ref = st_ref = None
    if has_state:
        s0_ref = refs[pos]
        pos += 1
    out_ref = refs[pos]
    pos += 1
    if not has_state:
        st_ref = refs[pos]
        pos += 1
    q_s, gf_s, gb_s, of_s, ob_s = refs[pos:pos + 5]

    q_s[...] = blk_ref[:, 0:128] * (GLA_DK ** -0.5)
    z = _mm(small_ref[...], w2_ref[...]) + gb_ref[...]
    g = _log_sigmoid(z) * (1.0 / GLA_TAU)
    gf_s[...] = g[:, 0:128]
    gb_s[...] = g[:, 128:256]
    k_ref = blk_ref.at[:, 128:256]
    v_ref = blk_ref.at[:, 256:512]
    _gated_scan(q_s, k_ref, k_ref, v_ref, gf_s, gb_s, of_s, ob_s, s0_ref, st_ref, cf, cb,
                t_len, GLA_KW, GLA_DK, eye_ref)
    out_ref[...] = _head_norm_gate(of_s[...] + ob_s[...], blk_ref[:, 512:768], nw_ref[...], ones_ref[...])


def _const_spec(a):
    nd = a.ndim
    return pl.BlockSpec(a.shape, lambda b, _n=nd: (0,) * _n)


def _scan_const_arrays(w):
    out = []
    for reverse in (False, True):
        tri, sg, cf, pm = _scan_consts(w, reverse)
        out += [jnp.asarray(tri, BF), jnp.asarray(sg, F32), jnp.asarray(cf, F32), jnp.asarray(pm, F32)]
    hm, hv, bd, ones_bd = _head_consts(w)
    out += [jnp.asarray(hm, F32), jnp.asarray(hv, F32), jnp.asarray(bd.T, F32), jnp.asarray(ones_bd, BF),
            jnp.eye(HEAD_V, dtype=BF)]
    return tuple(out)


def _split_scan_consts(refs):
    shared = tuple(refs[8:11])
    return tuple(refs[0:4]) + shared, tuple(refs[4:8]) + shared, refs[11], refs[12]


def _gla_call(proj, w2bd, gbias, nw, s0, t_len, n_seq, row_blk0):
    has_state = s0 is not None
    consts = (w2bd, gbias, nw) + _scan_const_arrays(GLA_KW)
    n_rows = N_PAIR * t_len
    in_specs = [pl.BlockSpec((n_rows, 768), lambda b: (row_blk0 + b, 0)),
                pl.BlockSpec((n_rows, 128), lambda b: (row_blk0 + b, COL_SMALL_BLOCK))]
    in_specs += [_const_spec(a) for a in consts]
    args = [proj, proj, *consts]
    if has_state:
        in_specs.append(pl.BlockSpec((N_PAIR, 2, BRANCH_W, GLA_KW), lambda b: (b, 0, 0, 0)))
        args.append(s0)
    out_shape = [jax.ShapeDtypeStruct((n_seq * t_len, BRANCH_W), F32)]
    out_specs = [pl.BlockSpec((n_rows, BRANCH_W), lambda b: (b, 0))]
    if not has_state:
        out_shape.append(jax.ShapeDtypeStruct((n_seq, 2, N_HEADS, GLA_DK, HEAD_V), F32))
        out_specs.append(pl.BlockSpec((N_PAIR, 2, N_HEADS, GLA_DK, HEAD_V), lambda b: (b, 0, 0, 0, 0)))
    return pl.pallas_call(
        functools.partial(_gla_kernel, t_len=t_len, has_state=has_state),
        grid=(n_seq // N_PAIR,),
        in_specs=in_specs,
        out_specs=out_specs,
        out_shape=out_shape,
        scratch_shapes=[pltpu.VMEM((n_rows, GLA_KW), F32)] * 3 + [pltpu.VMEM((n_rows, BRANCH_W), F32)] * 2,
        compiler_params=_cparams(("arbitrary",)),
        name="gla_mixer",
    )(*args)


def _hgrn_kernel(*refs, t_len, has_state, layer):
    q_ref, f_ref, v_ref, gate_ref, lbl_ref, nw_ref = refs[:6]
    pos = 6 + N_SCAN_CONSTS
    cf, cb, ones_ref, eye_ref = _split_scan_consts(refs[6:pos])
    s0_ref = st_ref = None
    if has_state:
        s0_ref = refs[pos]
        pos += 1
    out_ref = refs[pos]
    pos += 1
    if not has_state:
        st_ref = refs[pos]
        pos += 1
    q_s, kf_s, kb_s, gf_s, gb_s, of_s, ob_s = refs[pos:pos + 7]

    lg = lbl_ref[...]
    mx = jnp.max(lg, axis=0, keepdims=True)
    ex = jnp.exp(lg - mx)
    p = ex / jnp.sum(ex, axis=0, keepdims=True)
    lb = jnp.sum(p[0:layer + 1], axis=0, keepdims=True) - p[0:1]

    q_s[...] = _silu(q_ref[...]) * (HG_DK ** -0.5)
    f = lb + (1.0 - lb) * _sigmoid(f_ref[...])
    kf_s[...] = 1.0 - f[:, 0:HG_W]
    kb_s[...] = 1.0 - f[:, HG_W:2 * HG_W]
    lf = jnp.log(f)
    gf_s[...] = lf[:, 0:HG_W]
    gb_s[...] = lf[:, HG_W:2 * HG_W]
    _gated_scan(q_s, kf_s, kb_s, v_ref, gf_s, gb_s, of_s, ob_s, s0_ref, st_ref, cf, cb,
                t_len, HG_W, HG_DK, eye_ref)
    out_ref[...] = _head_norm_gate(of_s[...] + ob_s[...], gate_ref[...], nw_ref[...], ones_ref[...])


def _hgrn_call(proj, lb_logits, nw, s0, t_len, n_seq, row_blk0, layer):
    has_state = s0 is not None
    consts = (lb_logits, nw) + _scan_const_arrays(HG_W)
    n_rows = N_PAIR * t_len
    in_specs = [pl.BlockSpec((n_rows, 256), lambda b: (row_blk0 + b, 7)),
                pl.BlockSpec((n_rows, 512), lambda b: (row_blk0 + b, 4)),
                pl.BlockSpec((n_rows, 256), lambda b: (row_blk0 + b, 10)),
                pl.BlockSpec((n_rows, 256), lambda b: (row_blk0 + b, 11))]
    in_specs += [_const_spec(a) for a in consts]
    args = [proj, proj, proj, proj, *consts]
    if has_state:
        in_specs.append(pl.BlockSpec((N_PAIR, 2, BRANCH_W, HG_W), lambda b: (b, 0, 0, 0)))
        args.append(s0)
    out_shape = [jax.ShapeDtypeStruct((n_seq * t_len, BRANCH_W), F32)]
    out_specs = [pl.BlockSpec((n_rows, BRANCH_W), lambda b: (b, 0))]
    if not has_state:
        out_shape.append(jax.ShapeDtypeStruct((n_seq, 2, N_HEADS, HG_DK, HEAD_V), F32))
        out_specs.append(pl.BlockSpec((N_PAIR, 2, N_HEADS, HG_DK, HEAD_V), lambda b: (b, 0, 0, 0, 0)))
    return pl.pallas_call(
        functools.partial(_hgrn_kernel, t_len=t_len, has_state=has_state, layer=layer),
        grid=(n_seq // N_PAIR,),
        in_specs=in_specs,
        out_specs=out_specs,
        out_shape=out_shape,
        scratch_shapes=[pltpu.VMEM((n_rows, HG_W), F32)] * 5 + [pltpu.VMEM((n_rows, BRANCH_W), F32)] * 2,
        compiler_params=_cparams(("arbitrary",)),
        name="hgrn2_mixer",
    )(*args)


N_PAIR = 2
SOLVE_JB = 8
SOLVE_IB = 8


def _dn_solve_kernel(lt_ref, vb_ref, kb_ref, u_ref, w_ref):
    _dn_substitute(pl.program_id(1), lt_ref.at[0], vb_ref.at[0], kb_ref.at[0], u_ref.at[0], w_ref.at[0])


def _dn_substitute(rev, l_ref, vb_ref, kb_ref, u_ref, w_ref):
    c = CHUNK
    u_ref[...] = jnp.zeros_like(u_ref)
    w_ref[...] = jnp.zeros_like(w_ref)

    def outer(t, carry):
        i = t + rev * (c - 1 - 2 * t)

        def inner(jb, acc):
            au, aw = acc
            j0 = pl.multiple_of(jb * SOLVE_JB, SOLVE_JB)
            for r in range(SOLVE_JB):
                coef = l_ref[i, pl.ds(j0 + r, 1), :]
                au = au - coef * u_ref[j0 + r]
                aw = aw - coef * w_ref[j0 + r]
            return au, aw

        blk = lax.shift_right_logical(i, 3)
        lo = rev * blk
        hi = blk + 1 + rev * (c // SOLVE_JB - blk - 1)
        au, aw = lax.fori_loop(lo, hi, inner, (vb_ref[i], kb_ref[i]))
        u_ref[i] = au
        w_ref[i] = aw
        return carry

    lax.fori_loop(0, c, outer, 0)


def _dn_solve_packed_kernel(lt_ref, vb_ref, kb_ref, u_ref, w_ref, lp_s, vp_s, kp_s, up_s, wp_s):
    n_chunk = lt_ref.shape[-1]
    n_ib = CHUNK // SOLVE_IB
    s = pl.program_id(1)

    @pl.when(s < n_ib)
    def _():
        for ii in range(SOLVE_IB):
            for src, dst in ((lt_ref, lp_s), (vb_ref, vp_s), (kb_ref, kp_s)):
                dst[s * SOLVE_IB + ii] = jnp.concatenate(
                    [src[0, ii, h * HEAD_V:(h + 1) * HEAD_V, :] for h in range(N_HEADS)], axis=1)

    @pl.when(s == n_ib - 1)
    def _():
        _dn_substitute(pl.program_id(0), lp_s, vp_s, kp_s, up_s, wp_s)

    @pl.when(s >= n_ib)
    def _():
        for ii in range(SOLVE_IB):
            for src, dst in ((up_s, u_ref), (wp_s, w_ref)):
                row = src[(s - n_ib) * SOLVE_IB + ii]
                for h in range(N_HEADS):
                    dst[0, ii, h * HEAD_V:(h + 1) * HEAD_V, :] = row[:, h * n_chunk:(h + 1) * n_chunk]


def _dn_solve_call(lt, vbt, kbt):
    n_chunk = lt.shape[-1]
    shape = jax.ShapeDtypeStruct((2, CHUNK, BRANCH_W, n_chunk), F32)
    if n_chunk * N_HEADS == 128:
        n_ib = CHUNK // SOLVE_IB
        blk = (1, SOLVE_IB, BRANCH_W, n_chunk)
        spec = pl.BlockSpec(blk, lambda d, s: (d, jnp.minimum(s, n_ib - 1), 0, 0))
        ospec = pl.BlockSpec(blk, lambda d, s: (d, jnp.maximum(s - n_ib, 0), 0, 0))
        return pl.pallas_call(
            _dn_solve_packed_kernel,
            grid=(2, 2 * n_ib),
            in_specs=[spec, spec, spec],
            out_specs=[ospec, ospec],
            out_shape=[shape, shape],
            scratch_shapes=[pltpu.VMEM((CHUNK, HEAD_V, 128), F32)] * 5,
            compiler_params=_cparams(("arbitrary", "arbitrary")),
            name="deltanet_solve_packed",
        )(lt, vbt, kbt)
    spec = pl.BlockSpec((1, CHUNK, HEAD_V, n_chunk), lambda h, d: (d, 0, h, 0))
    return pl.pallas_call(
        _dn_solve_kernel,
        grid=(N_HEADS, 2),
        in_specs=[spec, spec, spec],
        out_specs=[spec, spec],
        out_shape=[shape, shape],
        compiler_params=_cparams(("arbitrary", "arbitrary")),
        name="deltanet_solve",
    )(lt, vbt, kbt)


def _chunks_to_lanes(a):
    return a.reshape(2, a.shape[1] // CHUNK, CHUNK, BRANCH_W).transpose(0, 2, 3, 1)


def _chunks_from_lanes(a):
    return a.transpose(0, 3, 1, 2).reshape(2, -1, BRANCH_W)


def _dn_build_kernel(x_ref, small_ref, cw_ref, alog_ref, dtb_ref,
                     tri_ref, strict_ref, eye_ref, blk_ref, exb_ref, exa_ref, hm_ref, ones_ref,
                     q_ref, k_ref, d_ref, l_ref, vb_ref, kb_ref, qkd_ref, v_ref, be_s, ge_s, *, t_len):
    n_rows = N_PAIR * t_len

    x = x_ref[...]
    row = lax.broadcasted_iota(jnp.int32, (n_rows, 1), 0) % t_len
    x_prev = jnp.where(row == 0, 0.0, pltpu.roll(x, 1, 0))
    x_next = jnp.where(row == t_len - 1, 0.0, pltpu.roll(x, n_rows - 1, 0))
    y = _silu(x_prev * cw_ref[0:1, :] + x * cw_ref[1:2, :] + x_next * cw_ref[2:3, :])
    cq, ck = y[:, 0:256], y[:, 256:512]
    v_ref[...] = y[:, 512:768]
    q_ref[...] = cq * lax.rsqrt(_sel_r(cq * cq, ones_ref[...], 2) + EPS) * (DN_DK ** -0.5)
    k_ref[...] = ck * lax.rsqrt(_sel_r(ck * ck, ones_ref[...], 2) + EPS)
    sm = small_ref[...]
    be_s[...] = _sel_r(_sigmoid(sm), exb_ref[...], 2)
    ge_s[...] = _sel_r(-jnp.exp(alog_ref[...]) * _softplus(sm + dtb_ref[...]), exa_ref[...], 2)

    def body(ci, carry):
        rows = pl.ds(pl.multiple_of(ci * CHUNK, CHUNK), CHUNK)
        qc, kc, vc = q_ref[rows, :], k_ref[rows, :], v_ref[rows, :]
        kst = _stack_heads(kc, hm_ref)
        kk = _mm_nt(kc, kst)
        qk = _mm_nt(qc, kst)
        for d in range(2):
            bexp = be_s[rows, d * BRANCH_W:(d + 1) * BRANCH_W]
            dexp = _sel_l(tri_ref[d], ge_s[rows, d * BRANCH_W:(d + 1) * BRANCH_W], 2)
            drow = _sel_l(blk_ref[...], dexp * eye_ref[...])
            dec_s = jnp.exp(jnp.where(strict_ref[d] > 0.5, dexp - drow, -1e30))
            d_ref[d, rows, :] = dexp
            l_ref[d, rows, :] = bexp * kk * dec_s
            vb_ref[d, rows, :] = vc * bexp
            kb_ref[d, rows, :] = kc * bexp * jnp.exp(dexp)
            qkd_ref[d, rows, :] = (qk * (dec_s + eye_ref[...])).astype(BF)
        return carry

    lax.fori_loop(0, n_rows // CHUNK, body, 0)


def _dn_build_call(proj, conv_w, alog_row, dtb_row, t_len, n_seq, row_blk0):
    tri, strict, eye, blk, exb, exa = _dn_consts()
    tri, blk, exb, exa = (jnp.asarray(a, BF) for a in (tri, blk, exb, exa))
    strict, eye = jnp.asarray(strict, F32), jnp.asarray(eye, F32)
    hm, _, _, ones_bd = _head_consts(BRANCH_W)
    hm, ones_bd = jnp.asarray(hm, F32), jnp.asarray(ones_bd, BF)
    n_rows = N_PAIR * t_len
    n_tok = n_seq * t_len
    consts = (conv_w, alog_row, dtb_row, tri, strict, eye, blk, exb, exa, hm, ones_bd)
    in_specs = [pl.BlockSpec((n_rows, 768), lambda b: (row_blk0 + b, 1)),
                pl.BlockSpec((n_rows, 128), lambda b: (row_blk0 + b, COL_SMALL_BLOCK))]
    in_specs += [_const_spec(a) for a in consts]
    tok_spec = pl.BlockSpec((n_rows, BRANCH_W), lambda b: (b, 0))
    dir_spec = pl.BlockSpec((2, n_rows, BRANCH_W), lambda b: (0, b, 0))
    tok = jax.ShapeDtypeStruct((n_tok, BRANCH_W), F32)
    per_dir = jax.ShapeDtypeStruct((2, n_tok, BRANCH_W), F32)
    return pl.pallas_call(
        functools.partial(_dn_build_kernel, t_len=t_len),
        grid=(n_seq // N_PAIR,),
        in_specs=in_specs,
        out_specs=[tok_spec] * 2 + [dir_spec] * 5,
        out_shape=[tok] * 2 + [per_dir] * 4 + [jax.ShapeDtypeStruct((2, n_tok, BRANCH_W), BF)],
        scratch_shapes=[pltpu.VMEM((n_rows, BRANCH_W), F32)] + [pltpu.VMEM((n_rows, 2 * BRANCH_W), F32)] * 2,
        compiler_params=_cparams(("arbitrary",)),
        name="deltanet_build",
    )(proj, proj, *consts)


def _dn_scan_kernel(*refs, t_len, has_state):
    (q_ref, k_ref, d_ref, u_ref, w_ref, qkd_ref, gate_ref, nw_ref, hv_ref, bd_ref, ones_ref) = refs[:11]
    pos = 11
    if has_state:
        s0_ref = refs[pos]
        pos += 1
    out_ref = refs[pos]
    pos += 1
    if not has_state:
        st_ref = refs[pos]
        pos += 1
    of_s, ob_s = refs[pos:pos + 2]
    c = CHUNK
    n_chunks = t_len // c

    def step(s, rows, d, o_s):
        qc, kc = q_ref[rows, :], k_ref[rows, :]
        dexp = d_ref[d, rows, :]
        v_new = u_ref[d, rows, :] - _mm(w_ref[d, rows, :], s)
        o_s[rows, :] = _mm(qc * jnp.exp(dexp), s) + _mm(qkd_ref[d, rows, :], _stack_heads(v_new, hv_ref))
        dl = dexp[0:1, :] if d == 1 else dexp[c - 1:c, :]
        return s * jnp.exp(dl) + bd_ref[...] * _mm_tn(kc * jnp.exp(dl - dexp), v_new)

    def body(t, carry):
        new = []
        for sq in range(N_PAIR):
            rf = pl.ds(pl.multiple_of(sq * t_len + t * c, c), c)
            rb = pl.ds(pl.multiple_of(sq * t_len + (n_chunks - 1 - t) * c, c), c)
            new.append(step(carry[2 * sq], rf, 0, of_s))
            new.append(step(carry[2 * sq + 1], rb, 1, ob_s))
        return tuple(new)

    if has_state:
        init = tuple(s0_ref[sq, d] for sq in range(N_PAIR) for d in range(2))
    else:
        init = tuple(jnp.zeros((BRANCH_W, BRANCH_W), F32) for _ in range(2 * N_PAIR))
    finals = lax.fori_loop(0, n_chunks, body, init)
    out_ref[...] = _head_norm_gate(of_s[...] + ob_s[...], gate_ref[...], nw_ref[...], ones_ref[...])
    if not has_state:
        for sq in range(N_PAIR):
            for d in range(2):
                for h in range(N_HEADS):
                    st_ref[sq, d, h] = _head_lanes(finals[2 * sq + d][h * DN_DK:(h + 1) * DN_DK, :], h)


def _dn_scan_call(proj, q, k, dd, u, w, qkd, nw, s0, t_len, n_seq, row_blk0):
    _, hv, bd, ones_bd = _head_consts(BRANCH_W)
    hv, bd, ones_bd = jnp.asarray(hv, F32), jnp.asarray(bd, F32), jnp.asarray(ones_bd, BF)
    has_state = s0 is not None
    n_rows = N_PAIR * t_len
    consts = (nw, hv, bd, ones_bd)
    tok_spec = pl.BlockSpec((n_rows, BRANCH_W), lambda b: (b, 0))
    dir_spec = pl.BlockSpec((2, n_rows, BRANCH_W), lambda b: (0, b, 0))
    in_specs = [tok_spec, tok_spec, dir_spec, dir_spec, dir_spec, dir_spec,
                pl.BlockSpec((n_rows, 256), lambda b: (row_blk0 + b, 6))]
    in_specs += [_const_spec(a) for a in consts]
    args = [q, k, dd, u, w, qkd, proj, *consts]
    if has_state:
        in_specs.append(pl.BlockSpec((N_PAIR, 2, BRANCH_W, BRANCH_W), lambda b: (b, 0, 0, 0)))
        args.append(s0)
    out_shape = [jax.ShapeDtypeStruct((n_seq * t_len, BRANCH_W), F32)]
    out_specs = [pl.BlockSpec((n_rows, BRANCH_W), lambda b: (b, 0))]
    if not has_state:
        out_shape.append(jax.ShapeDtypeStruct((n_seq, 2, N_HEADS, DN_DK, HEAD_V), F32))
        out_specs.append(pl.BlockSpec((N_PAIR, 2, N_HEADS, DN_DK, HEAD_V), lambda b: (b, 0, 0, 0, 0)))
    return pl.pallas_call(
        functools.partial(_dn_scan_kernel, t_len=t_len, has_state=has_state),
        grid=(n_seq // N_PAIR,),
        in_specs=in_specs,
        out_specs=out_specs,
        out_shape=out_shape,
        scratch_shapes=[pltpu.VMEM((n_rows, BRANCH_W), F32)] * 2,
        compiler_params=_cparams(("arbitrary",)),
        name="deltanet_scan",
    )(*args)


def _rope(x, cos, sin):
    lane = lax.broadcasted_iota(jnp.int32, x.shape, 1) % 16
    n = x.shape[1]
    xrot = jnp.where(lane < 8, -pltpu.roll(x, n - 8, 1), pltpu.roll(x, 8, 1))
    return x * cos + xrot * sin


def _att_kernel(*refs, t_len, lat, lam_init):
    blk_ref, lam_ref, nw_ref, qm_ref, hv_ref, ones_ref = refs[:6]
    pos = 6
    if lat:
        cos_ref, sin_ref, ck_ref, cv_ref = refs[pos:pos + 4]
        pos += 4
    out_ref = refs[pos]
    pos += 1
    if not lat:
        nk_ref, nv_ref = refs[pos:pos + 2]
        pos += 2
    if lat:
        q_s, k_s, v_s = refs[pos:pos + 3]

    lv = lam_ref[...]
    lam = (jnp.exp(jnp.sum(lv[0:1] * lv[1:2], axis=1, keepdims=True))
           - jnp.exp(jnp.sum(lv[2:3] * lv[3:4], axis=1, keepdims=True)) + lam_init)
    q = blk_ref[:, 0:256]
    k = blk_ref[:, 256:512]
    v = blk_ref[:, 512:768]
    if lat:
        cos, sin = cos_ref[...], sin_ref[...]
        q_s[...] = _rope(q, cos, sin)
        k_s[0:PAST_LEN, :] = ck_ref[0]
        k_s[PAST_LEN:PAST_LEN + t_len, :] = _rope(k, cos, sin)
        v_s[0:PAST_LEN, :] = cv_ref[0]
        v_s[PAST_LEN:PAST_LEN + t_len, :] = v
        keys = k_s[...].astype(BF)
        vals = v_s[...].astype(BF)
    else:
        keys = k.astype(BF)
        vals = v.astype(BF)
        for h in range(N_HEADS):
            nk_ref[0, h] = _head_lanes(k, h)
            nv_ref[0, h] = _head_lanes(v, h)
    tq = TQ_ATT
    scale = DF_DH ** -0.5
    for qi in range(t_len // tq):
        qt = q_s[qi * tq:(qi + 1) * tq, :] if lat else q[qi * tq:(qi + 1) * tq, :]
        qs = jnp.concatenate([qt * qm_ref[r:r + 1, :] for r in range(2 * N_HEADS)], axis=0)
        s = _mm_nt(qs, keys) * scale
        s = jnp.exp(s - jnp.max(s, axis=1, keepdims=True))
        p = s / jnp.sum(s, axis=1, keepdims=True)
        a = jnp.concatenate(
            [p[(2 * h) * tq:(2 * h + 1) * tq] - lam * p[(2 * h + 1) * tq:(2 * h + 2) * tq]
             for h in range(N_HEADS)], axis=0)
        o = _head_diag(_mm(a, vals), hv_ref, tq)
        ms = _sel_r(o * o, ones_ref[...], 2) * (1.0 / HEAD_V)
        out_ref[qi * tq:(qi + 1) * tq, :] = o * lax.rsqrt(ms + EPS) * nw_ref[...] * (1.0 - lam_init)


def _att_call(proj, lam_p, nw, cache_k, cache_v, rope, t_len, n_seq, row_blk0, lam_init):
    qm = jnp.asarray(_att_consts(), F32)
    _, hv, _, ones_bd = _head_consts(BRANCH_W)
    hv, ones_bd = jnp.asarray(hv, F32), jnp.asarray(ones_bd, BF)
    lat = cache_k is not None
    consts = (lam_p, nw, qm, hv, ones_bd)
    in_specs = [pl.BlockSpec((t_len, 768), lambda b: (row_blk0 + b, 4))]
    in_specs += [_const_spec(a) for a in consts]
    args = [proj, *consts]
    out_shape = [jax.ShapeDtypeStruct((n_seq * t_len, BRANCH_W), F32)]
    out_specs = [pl.BlockSpec((t_len, BRANCH_W), lambda b: (b, 0))]
    scratch = []
    if lat:
        cos, sin = rope
        in_specs += [_const_spec(cos), _const_spec(sin),
                     pl.BlockSpec((1, PAST_LEN, BRANCH_W), lambda b: (b, 0, 0)),
                     pl.BlockSpec((1, PAST_LEN, BRANCH_W), lambda b: (b, 0, 0))]
        args += [cos, sin, cache_k, cache_v]
        scratch = [pltpu.VMEM((t_len, BRANCH_W), F32),
                   pltpu.VMEM((PAST_LEN + t_len, BRANCH_W), F32),
                   pltpu.VMEM((PAST_LEN + t_len, BRANCH_W), F32)]
    else:
        for _ in range(2):
            out_shape.append(jax.ShapeDtypeStruct((n_seq, N_HEADS, t_len, HEAD_V), F32))
            out_specs.append(pl.BlockSpec((1, N_HEADS, t_len, HEAD_V), lambda b: (b, 0, 0, 0)))
    return pl.pallas_call(
        functools.partial(_att_kernel, t_len=t_len, lat=lat, lam_init=lam_init),
        grid=(n_seq,),
        in_specs=in_specs,
        out_specs=out_specs,
        out_shape=out_shape,
        scratch_shapes=scratch,
        compiler_params=_cparams(("arbitrary",)),
        name="diff_attention",
    )(*args)


def _block_diag_states(st, dk, transposed):
    eye = jnp.eye(N_HEADS, dtype=st.dtype)
    b = st.shape[0]
    if transposed:
        return jnp.einsum('bnhde,hg->bnhegd', st, eye).reshape(b, 2, N_HEADS * HEAD_V, N_HEADS * dk)
    return jnp.einsum('bnhde,hg->bnhdge', st, eye).reshape(b, 2, N_HEADS * dk, N_HEADS * HEAD_V)


def _in_perm():
    offs = np.concatenate([[0], np.cumsum(IN_ORIG)])
    seg = lambda a, b: np.arange(offs[a], offs[b])
    return np.concatenate([seg(0, 4), seg(5, 6), seg(8, 9), seg(9, 16), seg(4, 5), seg(6, 8)])


def kernel(x_prompt, x_sample, cache_diff_k, cache_diff_v, state_gla, state_dn, state_hgrn, c, c_ctx,
           norm_w, w_mod, b_mod, ffn1_in, ffn1_down, ffn2_in, ffn2_down, w_in, gla_w2, gla_b, gla_norm,
           dn_conv, dn_a_log, dn_dt_bias, dn_norm, hg_lb_logits, hg_norm, diff_lambda, diff_norm,
           w_branch, w_mgate, w_out, final_norm):
    xs = (x_prompt.reshape(N_CTX_TOK, D_MODEL), x_sample.reshape(-1, D_MODEL))
    c_rows = jnp.concatenate([c_ctx[None, :], c, jnp.zeros((8 - 1 - N_LAT_SEQ, D_MODEL), F32)], axis=0)
    mod = _mod_call(c_rows, w_mod, b_mod).reshape(DEPTH, 8, N_MOD, D_MODEL)
    rope = _rope_tables()
    perm = _in_perm()
    lb_logits = hg_lb_logits.reshape(DEPTH, 2 * HG_W)
    lat_blk = N_CTX_TOK // T_LAT
    tile4 = lambda a: jnp.tile(a, N_HEADS)[None, :]
    fin = final_norm[None, :]
    new_k, new_v, new_gla, new_dn, new_hg = [], [], [], [], []
    for l in range(DEPTH):
        mod_l = mod[l]
        lam_init = 0.8 - 0.6 * math.exp(-0.3 * l)
        w_in_p = jnp.pad(w_in[l][:, perm], ((0, 0), (0, N_IN_PAD - N_IN))).astype(BF)
        w2bd = jnp.zeros((128, 2 * GLA_KW), F32)
        w2bd = w2bd.at[0:GLA_LOWRANK, 0:GLA_KW].set(gla_w2[l, 0])
        w2bd = w2bd.at[GLA_LOWRANK:2 * GLA_LOWRANK, GLA_KW:].set(gla_w2[l, 1]).astype(BF)
        gbias = gla_b[l].reshape(1, 2 * GLA_KW)
        alog_row = jnp.zeros((1, 128), F32).at[0, SMALL_DNA:SMALL_DNA + 8].set(dn_a_log[l].reshape(-1))
        dtb_row = jnp.zeros((1, 128), F32).at[0, SMALL_DNA:SMALL_DNA + 8].set(dn_dt_bias[l].reshape(-1))

        (x,) = _ffn_call(xs, mod_l, norm_w[l, 0][None, :], ffn1_in[l].astype(BF), ffn1_down[l].astype(BF),
                         fin, 0, False, False)
        proj = _proj_call(x, mod_l, norm_w[l, 1][None, :], w_in_p)

        a_c, st_a = _gla_call(proj, w2bd, gbias, tile4(gla_norm[l]), None, T_CTX, N_CTX_SEQ, 0)
        dn_lat_blk = N_CTX_TOK // (N_PAIR * T_LAT)
        q_c, k_c, dd_c, lw_c, vb_c, kb_c, qkd_c = _dn_build_call(proj, dn_conv[l], alog_row, dtb_row,
                                                                T_CTX, N_CTX_SEQ, 0)
        q_l, k_l, dd_l, lw_l, vb_l, kb_l, qkd_l = _dn_build_call(proj, dn_conv[l], alog_row, dtb_row,
                                                                T_LAT, N_LAT_SEQ, dn_lat_blk)
        u_c, w_c = (_chunks_from_lanes(a) for a in _dn_solve_call(
            _chunks_to_lanes(lw_c), _chunks_to_lanes(vb_c), _chunks_to_lanes(kb_c)))
        u_l, w_l = (_chunks_from_lanes(a) for a in _dn_solve_call(
            _chunks_to_lanes(lw_l), _chunks_to_lanes(vb_l), _chunks_to_lanes(kb_l)))
        b_c, st_b = _dn_scan_call(proj, q_c, k_c, dd_c, u_c, w_c, qkd_c, tile4(dn_norm[l]), None,
                                  T_CTX, N_CTX_SEQ, 0)
        c_c, st_c = _hgrn_call(proj, lb_logits, tile4(hg_norm[l]), None, T_CTX, N_CTX_SEQ, 0, l)
        d_c, new_k_l, new_v_l = _att_call(proj, diff_lambda[l], tile4(diff_norm[l]), None, None, None,
                                  T_CTX, N_CTX_SEQ, 0, lam_init)
        ck = cache_diff_k[:, l].transpose(0, 2, 1, 3).reshape(N_LAT_SEQ, PAST_LEN, BRANCH_W)
        cv = cache_diff_v[:, l].transpose(0, 2, 1, 3).reshape(N_LAT_SEQ, PAST_LEN, BRANCH_W)
        (a_l,) = _gla_call(proj, w2bd, gbias, tile4(gla_norm[l]),
                           _block_diag_states(state_gla[:, l], GLA_DK, True), T_LAT, N_LAT_SEQ, dn_lat_blk)
        (b_l,) = _dn_scan_call(proj, q_l, k_l, dd_l, u_l, w_l, qkd_l, tile4(dn_norm[l]),
                               _block_diag_states(state_dn[:, l], DN_DK, False), T_LAT, N_LAT_SEQ, dn_lat_blk)
        (c_l,) = _hgrn_call(proj, lb_logits, tile4(hg_norm[l]),
                            _block_diag_states(state_hgrn[:, l], HG_DK, True), T_LAT, N_LAT_SEQ, dn_lat_blk, l)
        (d_l,) = _att_call(proj, diff_lambda[l], tile4(diff_norm[l]), ck, cv, rope,
                           T_LAT, N_LAT_SEQ, lat_blk, lam_init)

        x = _merge_call(x, mod_l, norm_w[l, 1][None, :], (a_c, b_c, c_c, d_c), (a_l, b_l, c_l, d_l),
                        w_mgate[l].astype(BF), w_branch[l].astype(BF), w_out[l].astype(BF))
        xs = _ffn_call((x,), mod_l, norm_w[l, 2][None, :], ffn2_in[l].astype(BF), ffn2_down[l].astype(BF),
                       fin, 2, l == DEPTH - 1, l == DEPTH - 1)
        new_k.append(new_k_l)
        new_v.append(new_v_l)
        new_gla.append(st_a)
        new_dn.append(st_b)
        new_hg.append(st_c)
    y_prompt = xs[0].reshape(N_CTX_SEQ, T_CTX, D_MODEL)
    y_sample = xs[1].reshape(N_LAT_SEQ, T_LAT, D_MODEL)
    return (y_prompt, y_sample, jnp.stack(new_k, axis=1), jnp.stack(new_v, axis=1),
            jnp.stack(new_gla, axis=1), jnp.stack(new_dn, axis=1), jnp.stack(new_hg, axis=1))
```

```python
import functools
import math

import numpy as np
import jax
import jax.numpy as jnp
from jax import lax
from jax.experimental import pallas as pl
from jax.experimental.pallas import tpu as pltpu

F32 = jnp.float32
BF = jnp.bfloat16

D_MODEL = 1024
N_CTX_SEQ = 32
T_CTX = 256
DEPTH = 2
N_LAT_SEQ = 2
T_LAT = 1024
PAST_LEN = 512
GRID_W = 64
N_HEADS = 4
BRANCH_W = 256
HEAD_V = 64
GLA_DK = 32
GLA_KW = 128
GLA_LOWRANK = 16
GLA_TAU = 16.0
DN_DK = 64
HG_DK = 64
HG_W = 256
DF_DH = 32
ROPE_BASE = 10000.0
D_FF = 2816
N_MOD = 9
CHUNK = 64
EPS = 1e-6
N_CTX_TOK = N_CTX_SEQ * T_CTX
N_TOK = N_CTX_TOK + N_LAT_SEQ * T_LAT
N_LEVELS = 6
SMALL_LEVEL = 4
N_SCAN_CONSTS = 13

IN_ORIG = (128, 128, 256, 256, 32, 768, 8, 8, 256, 256, 512, 256, 256, 256, 256, 256)
N_IN = sum(IN_ORIG)
N_IN_PAD = 3968
COL_SMALL_BLOCK = 30
SMALL_LR = 0
SMALL_DNB = 32
SMALL_DNA = 40

VMEM_LIMIT = 56 * 1024 * 1024

TM_FFN = 1024
TF_FFN = 704
TM_PROJ = 512
TM_MERGE = 512
TN_MOD = 2304
TQ_ATT = 256


def _silu(x):
    return x * (1.0 / (1.0 + jnp.exp(-x)))


def _sigmoid(x):
    return 1.0 / (1.0 + jnp.exp(-x))


def _softplus(x):
    return jnp.maximum(x, 0.0) + jnp.log(1.0 + jnp.exp(-jnp.abs(x)))


def _log_sigmoid(x):
    return -_softplus(-x)


def _mm(a, b):
    return jnp.dot(a.astype(BF), b.astype(BF), preferred_element_type=F32)


def _mm_nt(a, b):
    return lax.dot_general(a.astype(BF), b.astype(BF), (((1,), (1,)), ((), ())),
                           preferred_element_type=F32)


def _mm_tn(a, b):
    return lax.dot_general(a.astype(BF), b.astype(BF), (((0,), (0,)), ((), ())),
                           preferred_element_type=F32)


def _split(x, n):
    parts = []
    r = x
    for i in range(n):
        p = r.astype(BF)
        parts.append(p)
        if i + 1 < n:
            r = r - p.astype(F32)
    return parts


def _sel_l(m01, x, n=3):
    out = None
    for p in _split(x, n):
        t = jnp.dot(m01, p, preferred_element_type=F32)
        out = t if out is None else out + t
    return out


def _sel_r(x, m01, n=3):
    out = None
    for p in _split(x, n):
        t = jnp.dot(p, m01, preferred_element_type=F32)
        out = t if out is None else out + t
    return out


def _sel_tn(x, m01, n=3):
    out = None
    for p in _split(x, n):
        t = lax.dot_general(p, m01, (((0,), (0,)), ((), ())), preferred_element_type=F32)
        out = t if out is None else out + t
    return out


def _rms(x, w):
    return x * lax.rsqrt(jnp.mean(x * x, axis=-1, keepdims=True) + EPS) * w


def _head_lanes(x, h):
    blk = x[:, (h // 2) * 128:(h // 2 + 1) * 128]
    if h % 2:
        blk = pltpu.roll(blk, 64, 1)
    return blk[:, :HEAD_V]


def _stack_heads(x, hm_ref):
    return jnp.concatenate([x * hm_ref[h:h + 1, :] for h in range(N_HEADS)], axis=0)


def _stack_heads_bf(x, hm_ref):
    xb = x.astype(BF)
    return jnp.concatenate([xb * hm_ref[h:h + 1, :].astype(BF) for h in range(N_HEADS)], axis=0)


def _head_diag(o_full, hv_ref, c):
    out = None
    for h in range(N_HEADS):
        t = o_full[h * c:(h + 1) * c, :] * hv_ref[h:h + 1, :]
        out = t if out is None else out + t
    return out


def _head_norm_gate(o, gate, nw, ones_bd):
    ms = _sel_r(o * o, ones_bd, 2) * (1.0 / HEAD_V)
    return o * lax.rsqrt(ms + EPS) * nw * _silu(gate)


def _mod_row(i, tm):
    return jnp.maximum(i * tm - (N_CTX_TOK - T_LAT), 0) // T_LAT


def _cparams(sem):
    return pltpu.CompilerParams(dimension_semantics=sem, vmem_limit_bytes=VMEM_LIMIT)


@functools.lru_cache(maxsize=None)
def _scan_consts(w, reverse):
    c = CHUNK
    idx = np.arange(c)
    i = idx[:, None]
    m = idx[None, :]
    pm, sg = [], []
    s = c // 2
    while s >= 1:
        par = idx // (2 * s)
        right = (idx % (2 * s)) >= s
        same = par[:, None] == par[None, :]
        query = ~right if reverse else right
        pm.append(same & query[:, None] & (~query)[None, :])
        if s >= SMALL_LEVEL:
            sg.append(np.where(query, 1.0, -1.0))
        s //= 2
    pm.append(i == m)
    tri = (m >= i) if reverse else (m <= i)
    r4, r2 = idx % 4, idx % 2
    if not reverse:
        cf = [r2 == 1, r4 >= 2, r4 == 3, r4 == 0]
    else:
        cf = [r2 == 0, r4 <= 1, r4 == 3, r4 == 0]
    wide = lambda rows: np.repeat(np.stack(rows).astype(np.float32)[:, :, None], w, axis=2)
    pmask = np.stack([np.tile(p, (1, N_HEADS)) for p in pm]).astype(np.float32)
    return tri.astype(np.float32), wide(sg), wide(cf), pmask


@functools.lru_cache(maxsize=None)
def _head_consts(w):
    dk = w // N_HEADS
    hm = np.zeros((N_HEADS, w), np.float32)
    hv = np.zeros((N_HEADS, BRANCH_W), np.float32)
    for h in range(N_HEADS):
        hm[h, h * dk:(h + 1) * dk] = 1.0
        hv[h, h * HEAD_V:(h + 1) * HEAD_V] = 1.0
    bd = hm.T @ hv
    ones_bd = hv.T @ hv
    return hm, hv, bd, ones_bd


@functools.lru_cache(maxsize=None)
def _dn_consts():
    c = CHUNK
    idx = np.arange(c)
    i = idx[:, None]
    j = idx[None, :]
    tri = np.stack([(j <= i), (j >= i)]).astype(np.float32)
    strict = np.stack([np.tile(j < i, (1, N_HEADS)), np.tile(j > i, (1, N_HEADS))]).astype(np.float32)
    eye = np.tile(np.eye(c), (1, N_HEADS)).astype(np.float32)
    blk = np.ones((c, c), np.float32)
    exb = np.zeros((128, 2 * BRANCH_W), np.float32)
    exa = np.zeros((128, 2 * BRANCH_W), np.float32)
    for n in range(2):
        for h in range(N_HEADS):
            lo = n * BRANCH_W + h * HEAD_V
            exb[SMALL_DNB + n * N_HEADS + h, lo:lo + HEAD_V] = 1.0
            exa[SMALL_DNA + n * N_HEADS + h, lo:lo + HEAD_V] = 1.0
    return tri, strict, eye, blk, exb, exa


@functools.lru_cache(maxsize=None)
def _att_consts():
    qm = np.zeros((2 * N_HEADS, BRANCH_W), np.float32)
    for h in range(N_HEADS):
        for mp in range(2):
            lo = h * HEAD_V + mp * DF_DH
            qm[2 * h + mp, lo:lo + DF_DH] = 1.0
    return qm


def _rope_tables():
    rows = T_LAT // GRID_W
    row = jnp.repeat(jnp.arange(rows), GRID_W).astype(F32)
    col = jnp.tile(jnp.arange(GRID_W), rows).astype(F32)
    half = DF_DH // 2
    inv = ROPE_BASE ** (-jnp.arange(0, half, 2, dtype=F32) / half)

    def angles(pos):
        a = pos[:, None] * inv[None, :]
        return jnp.concatenate([a, a], axis=-1)

    ang = jnp.concatenate([angles(row), angles(col)], axis=-1)
    reps = BRANCH_W // DF_DH
    return jnp.tile(jnp.cos(ang), (1, reps)), jnp.tile(jnp.sin(ang), (1, reps))


def _mod_kernel(c_ref, w_ref, b_ref, o_ref):
    a = _silu(c_ref[...])
    w = w_ref[0]
    out = None
    for ap in _split(a, 2):
        for wp in _split(w, 2):
            t = jnp.dot(ap, wp, preferred_element_type=F32)
            out = t if out is None else out + t
    o_ref[0] = out + b_ref[0]


def _mod_call(c_rows, w_mod, b_mod):
    n_t = (N_MOD * D_MODEL) // TN_MOD
    return pl.pallas_call(
        _mod_kernel,
        grid=(DEPTH, n_t),
        in_specs=[pl.BlockSpec((8, D_MODEL), lambda l, j: (0, 0)),
                  pl.BlockSpec((1, D_MODEL, TN_MOD), lambda l, j: (l, 0, j)),
                  pl.BlockSpec((1, 1, TN_MOD), lambda l, j: (l, 0, j))],
        out_specs=pl.BlockSpec((1, 8, TN_MOD), lambda l, j: (l, 0, j)),
        out_shape=jax.ShapeDtypeStruct((DEPTH, 8, N_MOD * D_MODEL), F32),
        compiler_params=_cparams(("arbitrary", "arbitrary")),
        name="mod_vectors",
    )(c_rows, w_mod, b_mod.reshape(DEPTH, 1, N_MOD * D_MODEL))


def _ffn_kernel(*refs, sub, final, split_in, split_out):
    n_x = 2 if split_in else 1
    mod_ref, nw_ref, wup_ref, wd_ref, fn_ref = refs[n_x:n_x + 5]
    outs = refs[n_x + 5:]
    is_lat = pl.program_id(0) >= N_CTX_TOK // TM_FFN
    x = jnp.where(is_lat, refs[1][...], refs[0][...]) if split_in else refs[0][...]
    sh = mod_ref[0, 3 * sub:3 * sub + 1, :]
    sc = mod_ref[0, 3 * sub + 1:3 * sub + 2, :]
    ga = mod_ref[0, 3 * sub + 2:3 * sub + 3, :]
    h = (_rms(x, nw_ref[...]) * (1.0 + sc) + sh).astype(BF)
    acc = None
    for f in range(D_FF // TF_FFN):
        lo = f * TF_FFN
        g = jnp.dot(h, wup_ref[:, lo:lo + TF_FFN], preferred_element_type=F32)
        u = jnp.dot(h, wup_ref[:, D_FF + lo:D_FF + lo + TF_FFN], preferred_element_type=F32)
        t = jnp.dot((_silu(g) * u).astype(BF), wd_ref[lo:lo + TF_FFN, :], preferred_element_type=F32)
        acc = t if acc is None else acc + t
    y = x + 0.5 * ga * acc
    if final:
        y = _rms(y, fn_ref[...])
    if split_out:
        @pl.when(jnp.logical_not(is_lat))
        def _():
            outs[0][...] = y

        @pl.when(is_lat)
        def _():
            outs[1][...] = y
    else:
        outs[0][...] = y


def _ffn_call(xs, mod_l, nw, w_in, w_down, final_w, sub, final, split_out):
    tm = TM_FFN
    n_ctx = N_CTX_TOK // tm
    ctx_map = lambda i: (jnp.minimum(i, n_ctx - 1), 0)
    lat_map = lambda i: (jnp.maximum(i - n_ctx, 0), 0)
    tile = lambda m: pl.BlockSpec((tm, D_MODEL), m)
    resident = lambda a: pl.BlockSpec(a.shape, lambda i: (0, 0), pipeline_mode=pl.Buffered(1))
    split_in = len(xs) == 2
    in_specs = [tile(ctx_map), tile(lat_map)] if split_in else [tile(lambda i: (i, 0))]
    in_specs += [pl.BlockSpec((1, N_MOD, D_MODEL), lambda i: (_mod_row(i, tm), 0, 0)),
                 pl.BlockSpec((1, D_MODEL), lambda i: (0, 0)),
                 resident(w_in), resident(w_down),
                 pl.BlockSpec((1, D_MODEL), lambda i: (0, 0))]
    if split_out:
        out_specs = [tile(ctx_map), tile(lat_map)]
        out_shape = [jax.ShapeDtypeStruct((N_CTX_TOK, D_MODEL), F32),
                     jax.ShapeDtypeStruct((N_TOK - N_CTX_TOK, D_MODEL), F32)]
    else:
        out_specs = [tile(lambda i: (i, 0))]
        out_shape = [jax.ShapeDtypeStruct((N_TOK, D_MODEL), F32)]
    return pl.pallas_call(
        functools.partial(_ffn_kernel, sub=sub, final=final, split_in=split_in, split_out=split_out),
        grid=(N_TOK // tm,),
        in_specs=in_specs,
        out_specs=out_specs,
        out_shape=out_shape,
        compiler_params=_cparams(("arbitrary",)),
        name="swiglu_half_step",
    )(*xs, mod_l, nw, w_in, w_down, final_w)


def _proj_kernel(x_ref, mod_ref, nw_ref, w_ref, o_ref):
    sh = mod_ref[0, 3:4, :]
    sc = mod_ref[0, 4:5, :]
    h = (_rms(x_ref[...], nw_ref[...]) * (1.0 + sc) + sh).astype(BF)
    o_ref[...] = jnp.dot(h, w_ref[...], preferred_element_type=F32)


def _proj_call(x, mod_l, nw, w_in_p):
    tm = TM_PROJ
    return pl.pallas_call(
        _proj_kernel,
        grid=(N_TOK // tm,),
        in_specs=[pl.BlockSpec((tm, D_MODEL), lambda i: (i, 0)),
                  pl.BlockSpec((1, N_MOD, D_MODEL), lambda i: (_mod_row(i, tm), 0, 0)),
                  pl.BlockSpec((1, D_MODEL), lambda i: (0, 0)),
                  pl.BlockSpec((D_MODEL, N_IN_PAD), lambda i: (0, 0))],
        out_specs=pl.BlockSpec((tm, N_IN_PAD), lambda i: (i, 0)),
        out_shape=jax.ShapeDtypeStruct((N_TOK, N_IN_PAD), F32),
        compiler_params=_cparams(("arbitrary",)),
        name="mixer_in_proj",
    )(x, mod_l, nw, w_in_p)


def _merge_kernel(x_ref, mod_ref, nw_ref, *rest):
    ctx_refs, lat_refs = rest[0:4], rest[4:8]
    wg_ref, wb_ref, wo_ref, o_ref = rest[8:12]
    x = x_ref[...]
    sh = mod_ref[0, 3:4, :]
    sc = mod_ref[0, 4:5, :]
    ga = mod_ref[0, 5:6, :]
    h = (_rms(x, nw_ref[...]) * (1.0 + sc) + sh).astype(BF)
    is_lat = pl.program_id(0) >= N_CTX_TOK // TM_MERGE
    mixed = None
    for n in range(4):
        gate = _sigmoid(jnp.dot(h, wg_ref[:, n * D_MODEL:(n + 1) * D_MODEL], preferred_element_type=F32))
        br = jnp.where(is_lat, lat_refs[n][...], ctx_refs[n][...])
        up = jnp.dot(br.astype(BF), wb_ref[n], preferred_element_type=F32)
        mixed = gate * up if mixed is None else mixed + gate * up
    out = jnp.dot(mixed.astype(BF), wo_ref[...], preferred_element_type=F32)
    o_ref[...] = x + ga * out


def _merge_call(x, mod_l, nw, ctx_branches, lat_branches, w_mgate, w_branch, w_out):
    tm = TM_MERGE
    n_ctx = N_CTX_TOK // tm
    cspec = pl.BlockSpec((tm, BRANCH_W), lambda i: (jnp.minimum(i, n_ctx - 1), 0))
    lspec = pl.BlockSpec((tm, BRANCH_W), lambda i: (jnp.maximum(i - n_ctx, 0), 0))
    return pl.pallas_call(
        _merge_kernel,
        grid=(N_TOK // tm,),
        in_specs=[pl.BlockSpec((tm, D_MODEL), lambda i: (i, 0)),
                  pl.BlockSpec((1, N_MOD, D_MODEL), lambda i: (_mod_row(i, tm), 0, 0)),
                  pl.BlockSpec((1, D_MODEL), lambda i: (0, 0)),
                  cspec, cspec, cspec, cspec, lspec, lspec, lspec, lspec,
                  pl.BlockSpec((D_MODEL, 4 * D_MODEL), lambda i: (0, 0)),
                  pl.BlockSpec((4, BRANCH_W, D_MODEL), lambda i: (0, 0, 0)),
                  pl.BlockSpec((D_MODEL, D_MODEL), lambda i: (0, 0))],
        out_specs=pl.BlockSpec((tm, D_MODEL), lambda i: (i, 0)),
        out_shape=jax.ShapeDtypeStruct((N_TOK, D_MODEL), F32),
        compiler_params=_cparams(("arbitrary",)),
        name="gated_merge",
    )(x, mod_l, nw, *ctx_branches, *lat_branches, w_mgate, w_branch, w_out)


def _gated_chunk(qc, kc, vc, gc, st, tri_ref, sg_ref, cf_ref, pm_ref, hm_ref, hv_ref, bdt_ref, reverse):
    c = CHUNK
    w = gc.shape[1]
    cum = _sel_l(tri_ref[...], gc, 2)
    kst = _stack_heads_bf(kc, hm_ref)
    qb = qc.astype(BF)
    a = pm_ref[N_LEVELS] * _mm_nt(qb, kst)
    lv = 0
    s = c // 2
    while s >= 1:
        if s >= SMALL_LEVEL:
            first = s if reverse else s - 1
            bnd = [jnp.broadcast_to(cum[p * 2 * s + first:p * 2 * s + first + 1, :], (2 * s, w))
                   for p in range(c // (2 * s))]
            bnd = jnp.concatenate(bnd, axis=0) if len(bnd) > 1 else bnd[0]
            dl = (cum - bnd) * sg_ref[lv]
        elif s == 2:
            dl = (gc * cf_ref[1] + pltpu.roll(gc, 1, 0) * cf_ref[2] + pltpu.roll(gc, c - 1, 0) * cf_ref[3])
        else:
            dl = gc * cf_ref[0]
        el = jnp.exp(dl).astype(BF)
        a = a + pm_ref[lv] * _mm_nt(qb * el, kst * jnp.concatenate([el] * N_HEADS, axis=0))
        lv += 1
        s //= 2
    tot = cum[0:1, :] if reverse else cum[c - 1:c, :]
    o = _mm(a, _stack_heads_bf(vc, hv_ref)) + _mm_nt(qc * jnp.exp(cum), st)
    st_new = st * jnp.exp(tot) + bdt_ref[...] * _mm_tn(vc, kc * jnp.exp(tot - cum))
    return o, st_new


def _gated_scan(q_ref, kf_ref, kb_ref, v_ref, gf_ref, gb_ref, of_ref, ob_ref, s0_ref, st_ref, cf, cb,
                t_len, w, dk, eye_ref):
    n_chunks = t_len // CHUNK

    def body(t, carry):
        new = []
        for sq in range(N_PAIR):
            rf = pl.ds(pl.multiple_of(sq * t_len + t * CHUNK, CHUNK), CHUNK)
            rb = pl.ds(pl.multiple_of(sq * t_len + (n_chunks - 1 - t) * CHUNK, CHUNK), CHUNK)
            o_f, sf = _gated_chunk(q_ref[rf, :], kf_ref[rf, :], v_ref[rf, :], gf_ref[rf, :],
                                   carry[2 * sq], *cf, False)
            o_b, sb = _gated_chunk(q_ref[rb, :], kb_ref[rb, :], v_ref[rb, :], gb_ref[rb, :],
                                   carry[2 * sq + 1], *cb, True)
            of_ref[rf, :] = o_f
            ob_ref[rb, :] = o_b
            new += [sf, sb]
        return tuple(new)

    if s0_ref is not None:
        init = tuple(s0_ref[sq, d] for sq in range(N_PAIR) for d in range(2))
    else:
        init = tuple(jnp.zeros((BRANCH_W, w), F32) for _ in range(2 * N_PAIR))
    finals = lax.fori_loop(0, n_chunks, body, init)
    if st_ref is not None:
        for sq in range(N_PAIR):
            for d in range(2):
                st = finals[2 * sq + d]
                for h in range(N_HEADS):
                    tr = _sel_tn(st[h * HEAD_V:(h + 1) * HEAD_V, :], eye_ref[...])
                    st_ref[sq, d, h] = tr[h * dk:(h + 1) * dk, :]


def _gla_kernel(*refs, t_len, has_state):
    blk_ref, small_ref, w2_ref, gb_ref, nw_ref = refs[:5]
    pos = 5 + N_SCAN_CONSTS
    cf, cb, ones_ref, eye_ref = _split_scan_consts(refs[5:pos])
    s0_ref = st_ref = None
    if has_state:
        s0_ref = refs[pos]
        pos += 1
    out_ref = refs[pos]
    pos += 1
    if not has_state:
        st_ref = refs[pos]
        pos += 1
    q_s, gf_s, gb_s, of_s, ob_s = refs[pos:pos + 5]

    q_s[...] = blk_ref[:, 0:128] * (GLA_DK ** -0.5)
    z = _mm(small_ref[...], w2_ref[...]) + gb_ref[...]
    g = _log_sigmoid(z) * (1.0 / GLA_TAU)
    gf_s[...] = g[:, 0:128]
    gb_s[...] = g[:, 128:256]
    k_ref = blk_ref.at[:, 128:256]
    v_ref = blk_ref.at[:, 256:512]
    _gated_scan(q_s, k_ref, k_ref, v_ref, gf_s, gb_s, of_s, ob_s, s0_ref, st_ref, cf, cb,
                t_len, GLA_KW, GLA_DK, eye_ref)
    out_ref[...] = _head_norm_gate(of_s[...] + ob_s[...], blk_ref[:, 512:768], nw_ref[...], ones_ref[...])


def _const_spec(a):
    nd = a.ndim
    return pl.BlockSpec(a.shape, lambda b, _n=nd: (0,) * _n)


def _scan_const_arrays(w):
    out = []
    for reverse in (False, True):
        tri, sg, cf, pm = _scan_consts(w, reverse)
        out += [jnp.asarray(tri, BF), jnp.asarray(sg, F32), jnp.asarray(cf, F32), jnp.asarray(pm, F32)]
    hm, hv, bd, ones_bd = _head_consts(w)
    out += [jnp.asarray(hm, F32), jnp.asarray(hv, F32), jnp.asarray(bd.T, F32), jnp.asarray(ones_bd, BF),
            jnp.eye(HEAD_V, dtype=BF)]
    return tuple(out)


def _split_scan_consts(refs):
    shared = tuple(refs[8:11])
    return tuple(refs[0:4]) + shared, tuple(refs[4:8]) + shared, refs[11], refs[12]


def _gla_call(proj, w2bd, gbias, nw, s0, t_len, n_seq, row_blk0):
    has_state = s0 is not None
    consts = (w2bd, gbias, nw) + _scan_const_arrays(GLA_KW)
    n_rows = N_PAIR * t_len
    in_specs = [pl.BlockSpec((n_rows, 768), lambda b: (row_blk0 + b, 0)),
                pl.BlockSpec((n_rows, 128), lambda b: (row_blk0 + b, COL_SMALL_BLOCK))]
    in_specs += [_const_spec(a) for a in consts]
    args = [proj, proj, *consts]
    if has_state:
        in_specs.append(pl.BlockSpec((N_PAIR, 2, BRANCH_W, GLA_KW), lambda b: (b, 0, 0, 0)))
        args.append(s0)
    out_shape = [jax.ShapeDtypeStruct((n_seq * t_len, BRANCH_W), F32)]
    out_specs = [pl.BlockSpec((n_rows, BRANCH_W), lambda b: (b, 0))]
    if not has_state:
        out_shape.append(jax.ShapeDtypeStruct((n_seq, 2, N_HEADS, GLA_DK, HEAD_V), F32))
        out_specs.append(pl.BlockSpec((N_PAIR, 2, N_HEADS, GLA_DK, HEAD_V), lambda b: (b, 0, 0, 0, 0)))
    return pl.pallas_call(
        functools.partial(_gla_kernel, t_len=t_len, has_state=has_state),
        grid=(n_seq // N_PAIR,),
        in_specs=in_specs,
        out_specs=out_specs,
        out_shape=out_shape,
        scratch_shapes=[pltpu.VMEM((n_rows, GLA_KW), F32)] * 3 + [pltpu.VMEM((n_rows, BRANCH_W), F32)] * 2,
        compiler_params=_cparams(("arbitrary",)),
        name="gla_mixer",
    )(*args)


def _hgrn_kernel(*refs, t_len, has_state, layer):
    q_ref, f_ref, v_ref, gate_ref, lbl_ref, nw_ref = refs[:6]
    pos = 6 + N_SCAN_CONSTS
    cf, cb, ones_ref, eye_ref = _split_scan_consts(refs[6:pos])
    s0_ref = st_ref = None
    if has_state:
        s0_ref = refs[pos]
        pos += 1
    out_ref = refs[pos]
    pos += 1
    if not has_state:
        st_ref = refs[pos]
        pos += 1
    q_s, kf_s, kb_s, gf_s, gb_s, of_s, ob_s = refs[pos:pos + 7]

    lg = lbl_ref[...]
    mx = jnp.max(lg, axis=0, keepdims=True)
    ex = jnp.exp(lg - mx)
    p = ex / jnp.sum(ex, axis=0, keepdims=True)
    lb = jnp.sum(p[0:layer + 1], axis=0, keepdims=True) - p[0:1]

    q_s[...] = _silu(q_ref[...]) * (HG_DK ** -0.5)
    f = lb + (1.0 - lb) * _sigmoid(f_ref[...])
    kf_s[...] = 1.0 - f[:, 0:HG_W]
    kb_s[...] = 1.0 - f[:, HG_W:2 * HG_W]
    lf = jnp.log(f)
    gf_s[...] = lf[:, 0:HG_W]
    gb_s[...] = lf[:, HG_W:2 * HG_W]
    _gated_scan(q_s, kf_s, kb_s, v_ref, gf_s, gb_s, of_s, ob_s, s0_ref, st_ref, cf, cb,
                t_len, HG_W, HG_DK, eye_ref)
    out_ref[...] = _head_norm_gate(of_s[...] + ob_s[...], gate_ref[...], nw_ref[...], ones_ref[...])


def _hgrn_call(proj, lb_logits, nw, s0, t_len, n_seq, row_blk0, layer):
    has_state = s0 is not None
    consts = (lb_logits, nw) + _scan_const_arrays(HG_W)
    n_rows = N_PAIR * t_len
    in_specs = [pl.BlockSpec((n_rows, 256), lambda b: (row_blk0 + b, 7)),
                pl.BlockSpec((n_rows, 512), lambda b: (row_blk0 + b, 4)),
                pl.BlockSpec((n_rows, 256), lambda b: (row_blk0 + b, 10)),
                pl.BlockSpec((n_rows, 256), lambda b: (row_blk0 + b, 11))]
    in_specs += [_const_spec(a) for a in consts]
    args = [proj, proj, proj, proj, *consts]
    if has_state:
        in_specs.append(pl.BlockSpec((N_PAIR, 2, BRANCH_W, HG_W), lambda b: (b, 0, 0, 0)))
        args.append(s0)
    out_shape = [jax.ShapeDtypeStruct((n_seq * t_len, BRANCH_W), F32)]
    out_specs = [pl.BlockSpec((n_rows, BRANCH_W), lambda b: (b, 0))]
    if not has_state:
        out_shape.append(jax.ShapeDtypeStruct((n_seq, 2, N_HEADS, HG_DK, HEAD_V), F32))
        out_specs.append(pl.BlockSpec((N_PAIR, 2, N_HEADS, HG_DK, HEAD_V), lambda b: (b, 0, 0, 0, 0)))
    return pl.pallas_call(
        functools.partial(_hgrn_kernel, t_len=t_len, has_state=has_state, layer=layer),
        grid=(n_seq // N_PAIR,),
        in_specs=in_specs,
        out_specs=out_specs,
        out_shape=out_shape,
        scratch_shapes=[pltpu.VMEM((n_rows, HG_W), F32)] * 5 + [pltpu.VMEM((n_rows, BRANCH_W), F32)] * 2,
        compiler_params=_cparams(("arbitrary",)),
        name="hgrn2_mixer",
    )(*args)


N_PAIR = 2
SOLVE_JB = 8
SOLVE_IB = 16


def _dn_solve_kernel(lt_ref, vb_ref, kb_ref, u_ref, w_ref):
    _dn_substitute(pl.program_id(1), lt_ref.at[0], vb_ref.at[0], kb_ref.at[0], u_ref.at[0], w_ref.at[0])


def _dn_substitute(rev, l_ref, vb_ref, kb_ref, u_ref, w_ref):
    c = CHUNK
    u_ref[...] = jnp.zeros_like(u_ref)
    w_ref[...] = jnp.zeros_like(w_ref)

    def outer(t, carry):
        i = t + rev * (c - 1 - 2 * t)

        def inner(jb, acc):
            au, aw = acc
            j0 = pl.multiple_of(jb * SOLVE_JB, SOLVE_JB)
            for r in range(SOLVE_JB):
                coef = l_ref[i, pl.ds(j0 + r, 1), :]
                au = au - coef * u_ref[j0 + r]
                aw = aw - coef * w_ref[j0 + r]
            return au, aw

        blk = lax.shift_right_logical(i, 3)
        lo = rev * blk
        hi = blk + 1 + rev * (c // SOLVE_JB - blk - 1)
        au, aw = lax.fori_loop(lo, hi, inner, (vb_ref[i], kb_ref[i]))
        u_ref[i] = au
        w_ref[i] = aw
        return carry

    lax.fori_loop(0, c, outer, 0)


def _dn_solve_packed_kernel(lt_ref, vb_ref, kb_ref, u_ref, w_ref, lp_s, vp_s, kp_s, up_s, wp_s):
    n_chunk = lt_ref.shape[-1]
    n_ib = CHUNK // SOLVE_IB
    s = pl.program_id(1)

    @pl.when(s < n_ib)
    def _():
        for ii in range(SOLVE_IB):
            for src, dst in ((lt_ref, lp_s), (vb_ref, vp_s), (kb_ref, kp_s)):
                dst[s * SOLVE_IB + ii] = jnp.concatenate(
                    [src[0, ii, h * HEAD_V:(h + 1) * HEAD_V, :] for h in range(N_HEADS)], axis=1)

    @pl.when(s == n_ib - 1)
    def _():
        _dn_substitute(pl.program_id(0), lp_s, vp_s, kp_s, up_s, wp_s)

    @pl.when(s >= n_ib)
    def _():
        for ii in range(SOLVE_IB):
            for src, dst in ((up_s, u_ref), (wp_s, w_ref)):
                row = src[(s - n_ib) * SOLVE_IB + ii]
                for h in range(N_HEADS):
                    dst[0, ii, h * HEAD_V:(h + 1) * HEAD_V, :] = row[:, h * n_chunk:(h + 1) * n_chunk]


def _dn_solve_call(lt, vbt, kbt):
    n_chunk = lt.shape[-1]
    shape = jax.ShapeDtypeStruct((2, CHUNK, BRANCH_W, n_chunk), F32)
    if n_chunk * N_HEADS == 128:
        n_ib = CHUNK // SOLVE_IB
        blk = (1, SOLVE_IB, BRANCH_W, n_chunk)
        spec = pl.BlockSpec(blk, lambda d, s: (d, jnp.minimum(s, n_ib - 1), 0, 0))
        ospec = pl.BlockSpec(blk, lambda d, s: (d, jnp.maximum(s - n_ib, 0), 0, 0))
        return pl.pallas_call(
            _dn_solve_packed_kernel,
            grid=(2, 2 * n_ib),
            in_specs=[spec, spec, spec],
            out_specs=[ospec, ospec],
            out_shape=[shape, shape],
            scratch_shapes=[pltpu.VMEM((CHUNK, HEAD_V, 128), F32)] * 5,
            compiler_params=_cparams(("arbitrary", "arbitrary")),
            name="deltanet_solve_packed",
        )(lt, vbt, kbt)
    spec = pl.BlockSpec((1, CHUNK, HEAD_V, n_chunk), lambda h, d: (d, 0, h, 0))
    return pl.pallas_call(
        _dn_solve_kernel,
        grid=(N_HEADS, 2),
        in_specs=[spec, spec, spec],
        out_specs=[spec, spec],
        out_shape=[shape, shape],
        compiler_params=_cparams(("arbitrary", "arbitrary")),
        name="deltanet_solve",
    )(lt, vbt, kbt)


def _chunks_to_lanes(a):
    return a.reshape(2, a.shape[1] // CHUNK, CHUNK, BRANCH_W).transpose(0, 2, 3, 1)


def _chunks_from_lanes(a):
    return a.transpose(0, 3, 1, 2).reshape(2, -1, BRANCH_W)


def _dn_build_kernel(x_ref, small_ref, cw_ref, alog_ref, dtb_ref,
                     tri_ref, strict_ref, eye_ref, blk_ref, exb_ref, exa_ref, hm_ref, ones_ref,
                     q_ref, k_ref, d_ref, l_ref, vb_ref, kb_ref, qkd_ref, v_ref, be_s, ge_s, *, t_len):
    n_rows = N_PAIR * t_len

    x = x_ref[...]
    row = lax.broadcasted_iota(jnp.int32, (n_rows, 1), 0) % t_len
    x_prev = jnp.where(row == 0, 0.0, pltpu.roll(x, 1, 0))
    x_next = jnp.where(row == t_len - 1, 0.0, pltpu.roll(x, n_rows - 1, 0))
    y = _silu(x_prev * cw_ref[0:1, :] + x * cw_ref[1:2, :] + x_next * cw_ref[2:3, :])
    cq, ck = y[:, 0:256], y[:, 256:512]
    v_ref[...] = y[:, 512:768]
    q_ref[...] = cq * lax.rsqrt(_sel_r(cq * cq, ones_ref[...], 2) + EPS) * (DN_DK ** -0.5)
    k_ref[...] = ck * lax.rsqrt(_sel_r(ck * ck, ones_ref[...], 2) + EPS)
    sm = small_ref[...]
    be_s[...] = _sel_r(_sigmoid(sm), exb_ref[...], 2)
    ge_s[...] = _sel_r(-jnp.exp(alog_ref[...]) * _softplus(sm + dtb_ref[...]), exa_ref[...], 2)

    def body(ci, carry):
        rows = pl.ds(pl.multiple_of(ci * CHUNK, CHUNK), CHUNK)
        qc, kc, vc = q_ref[rows, :], k_ref[rows, :], v_ref[rows, :]
        kst = _stack_heads(kc, hm_ref)
        kk = _mm_nt(kc, kst)
        qk = _mm_nt(qc, kst)
        for d in range(2):
            bexp = be_s[rows, d * BRANCH_W:(d + 1) * BRANCH_W]
            dexp = _sel_l(tri_ref[d], ge_s[rows, d * BRANCH_W:(d + 1) * BRANCH_W], 2)
            drow = _sel_l(blk_ref[...], dexp * eye_ref[...])
            dec_s = jnp.exp(jnp.where(strict_ref[d] > 0.5, dexp - drow, -1e30))
            d_ref[d, rows, :] = dexp
            l_ref[d, rows, :] = bexp * kk * dec_s
            vb_ref[d, rows, :] = vc * bexp
            kb_ref[d, rows, :] = kc * bexp * jnp.exp(dexp)
            qkd_ref[d, rows, :] = (qk * (dec_s + eye_ref[...])).astype(BF)
        return carry

    lax.fori_loop(0, n_rows // CHUNK, body, 0)


def _dn_build_call(proj, conv_w, alog_row, dtb_row, t_len, n_seq, row_blk0):
    tri, strict, eye, blk, exb, exa = _dn_consts()
    tri, blk, exb, exa = (jnp.asarray(a, BF) for a in (tri, blk, exb, exa))
    strict, eye = jnp.asarray(strict, F32), jnp.asarray(eye, F32)
    hm, _, _, ones_bd = _head_consts(BRANCH_W)
    hm, ones_bd = jnp.asarray(hm, F32), jnp.asarray(ones_bd, BF)
    n_rows = N_PAIR * t_len
    n_tok = n_seq * t_len
    consts = (conv_w, alog_row, dtb_row, tri, strict, eye, blk, exb, exa, hm, ones_bd)
    in_specs = [pl.BlockSpec((n_rows, 768), lambda b: (row_blk0 + b, 1)),
                pl.BlockSpec((n_rows, 128), lambda b: (row_blk0 + b, COL_SMALL_BLOCK))]
    in_specs += [_const_spec(a) for a in consts]
    tok_spec = pl.BlockSpec((n_rows, BRANCH_W), lambda b: (b, 0))
    dir_spec = pl.BlockSpec((2, n_rows, BRANCH_W), lambda b: (0, b, 0))
    tok = jax.ShapeDtypeStruct((n_tok, BRANCH_W), F32)
    per_dir = jax.ShapeDtypeStruct((2, n_tok, BRANCH_W), F32)
    return pl.pallas_call(
        functools.partial(_dn_build_kernel, t_len=t_len),
        grid=(n_seq // N_PAIR,),
        in_specs=in_specs,
        out_specs=[tok_spec] * 2 + [dir_spec] * 5,
        out_shape=[tok] * 2 + [per_dir] * 4 + [jax.ShapeDtypeStruct((2, n_tok, BRANCH_W), BF)],
        scratch_shapes=[pltpu.VMEM((n_rows, BRANCH_W), F32)] + [pltpu.VMEM((n_rows, 2 * BRANCH_W), F32)] * 2,
        compiler_params=_cparams(("arbitrary",)),
        name="deltanet_build",
    )(proj, proj, *consts)


def _dn_scan_kernel(*refs, t_len, has_state):
    (q_ref, k_ref, d_ref, u_ref, w_ref, qkd_ref, gate_ref, nw_ref, hv_ref, bd_ref, ones_ref) = refs[:11]
    pos = 11
    if has_state:
        s0_ref = refs[pos]
        pos += 1
    out_ref = refs[pos]
    pos += 1
    if not has_state:
        st_ref = refs[pos]
        pos += 1
    of_s, ob_s = refs[pos:pos + 2]
    c = CHUNK
    n_chunks = t_len // c

    def step(s, rows, d, o_s):
        qc, kc = q_ref[rows, :], k_ref[rows, :]
        dexp = d_ref[d, rows, :]
        v_new = u_ref[d, rows, :] - _mm(w_ref[d, rows, :], s)
        o_s[rows, :] = _mm(qc * jnp.exp(dexp), s) + _mm(qkd_ref[d, rows, :], _stack_heads(v_new, hv_ref))
        dl = dexp[0:1, :] if d == 1 else dexp[c - 1:c, :]
        return s * jnp.exp(dl) + bd_ref[...] * _mm_tn(kc * jnp.exp(dl - dexp), v_new)

    def body(t, carry):
        new = []
        for sq in range(N_PAIR):
            rf = pl.ds(pl.multiple_of(sq * t_len + t * c, c), c)
            rb = pl.ds(pl.multiple_of(sq * t_len + (n_chunks - 1 - t) * c, c), c)
            new.append(step(carry[2 * sq], rf, 0, of_s))
            new.append(step(carry[2 * sq + 1], rb, 1, ob_s))
        return tuple(new)

    if has_state:
        init = tuple(s0_ref[sq, d] for sq in range(N_PAIR) for d in range(2))
    else:
        init = tuple(jnp.zeros((BRANCH_W, BRANCH_W), F32) for _ in range(2 * N_PAIR))
    finals = lax.fori_loop(0, n_chunks, body, init)
    out_ref[...] = _head_norm_gate(of_s[...] + ob_s[...], gate_ref[...], nw_ref[...], ones_ref[...])
    if not has_state:
        for sq in range(N_PAIR):
            for d in range(2):
                for h in range(N_HEADS):
                    st_ref[sq, d, h] = _head_lanes(finals[2 * sq + d][h * DN_DK:(h + 1) * DN_DK, :], h)


def _dn_scan_call(proj, q, k, dd, u, w, qkd, nw, s0, t_len, n_seq, row_blk0):
    _, hv, bd, ones_bd = _head_consts(BRANCH_W)
    hv, bd, ones_bd = jnp.asarray(hv, F32), jnp.asarray(bd, F32), jnp.asarray(ones_bd, BF)
    has_state = s0 is not None
    n_rows = N_PAIR * t_len
    consts = (nw, hv, bd, ones_bd)
    tok_spec = pl.BlockSpec((n_rows, BRANCH_W), lambda b: (b, 0))
    dir_spec = pl.BlockSpec((2, n_rows, BRANCH_W), lambda b: (0, b, 0))
    in_specs = [tok_spec, tok_spec, dir_spec, dir_spec, dir_spec, dir_spec,
                pl.BlockSpec((n_rows, 256), lambda b: (row_blk0 + b, 6))]
    in_specs += [_const_spec(a) for a in consts]
    args = [q, k, dd, u, w, qkd, proj, *consts]
    if has_state:
        in_specs.append(pl.BlockSpec((N_PAIR, 2, BRANCH_W, BRANCH_W), lambda b: (b, 0, 0, 0)))
        args.append(s0)
    out_shape = [jax.ShapeDtypeStruct((n_seq * t_len, BRANCH_W), F32)]
    out_specs = [pl.BlockSpec((n_rows, BRANCH_W), lambda b: (b, 0))]
    if not has_state:
        out_shape.append(jax.ShapeDtypeStruct((n_seq, 2, N_HEADS, DN_DK, HEAD_V), F32))
        out_specs.append(pl.BlockSpec((N_PAIR, 2, N_HEADS, DN_DK, HEAD_V), lambda b: (b, 0, 0, 0, 0)))
    return pl.pallas_call(
        functools.partial(_dn_scan_kernel, t_len=t_len, has_state=has_state),
        grid=(n_seq // N_PAIR,),
        in_specs=in_specs,
        out_specs=out_specs,
        out_shape=out_shape,
        scratch_shapes=[pltpu.VMEM((n_rows, BRANCH_W), F32)] * 2,
        compiler_params=_cparams(("arbitrary",)),
        name="deltanet_scan",
    )(*args)


def _rope(x, cos, sin):
    lane = lax.broadcasted_iota(jnp.int32, x.shape, 1) % 16
    n = x.shape[1]
    xrot = jnp.where(lane < 8, -pltpu.roll(x, n - 8, 1), pltpu.roll(x, 8, 1))
    return x * cos + xrot * sin


def _att_kernel(*refs, t_len, lat, lam_init):
    blk_ref, lam_ref, nw_ref, qm_ref, hv_ref, ones_ref = refs[:6]
    pos = 6
    if lat:
        cos_ref, sin_ref, ck_ref, cv_ref = refs[pos:pos + 4]
        pos += 4
    out_ref = refs[pos]
    pos += 1
    if not lat:
        nk_ref, nv_ref = refs[pos:pos + 2]
        pos += 2
    if lat:
        q_s, k_s, v_s = refs[pos:pos + 3]

    lv = lam_ref[...]
    lam = (jnp.exp(jnp.sum(lv[0:1] * lv[1:2], axis=1, keepdims=True))
           - jnp.exp(jnp.sum(lv[2:3] * lv[3:4], axis=1, keepdims=True)) + lam_init)
    q = blk_ref[:, 0:256]
    k = blk_ref[:, 256:512]
    v = blk_ref[:, 512:768]
    if lat:
        cos, sin = cos_ref[...], sin_ref[...]
        q_s[...] = _rope(q, cos, sin)
        k_s[0:PAST_LEN, :] = ck_ref[0]
        k_s[PAST_LEN:PAST_LEN + t_len, :] = _rope(k, cos, sin)
        v_s[0:PAST_LEN, :] = cv_ref[0]
        v_s[PAST_LEN:PAST_LEN + t_len, :] = v
        keys = k_s[...].astype(BF)
        vals = v_s[...].astype(BF)
    else:
        keys = k.astype(BF)
        vals = v.astype(BF)
        for h in range(N_HEADS):
            nk_ref[0, h] = _head_lanes(k, h)
            nv_ref[0, h] = _head_lanes(v, h)
    tq = TQ_ATT
    scale = DF_DH ** -0.5
    for qi in range(t_len // tq):
        qt = q_s[qi * tq:(qi + 1) * tq, :] if lat else q[qi * tq:(qi + 1) * tq, :]
        qs = jnp.concatenate([qt * qm_ref[r:r + 1, :] for r in range(2 * N_HEADS)], axis=0)
        s = _mm_nt(qs, keys) * scale
        s = jnp.exp(s - jnp.max(s, axis=1, keepdims=True))
        p = s / jnp.sum(s, axis=1, keepdims=True)
        a = jnp.concatenate(
            [p[(2 * h) * tq:(2 * h + 1) * tq] - lam * p[(2 * h + 1) * tq:(2 * h + 2) * tq]
             for h in range(N_HEADS)], axis=0)
        o = _head_diag(_mm(a, vals), hv_ref, tq)
        ms = _sel_r(o * o, ones_ref[...], 2) * (1.0 / HEAD_V)
        out_ref[qi * tq:(qi + 1) * tq, :] = o * lax.rsqrt(ms + EPS) * nw_ref[...] * (1.0 - lam_init)


def _att_call(proj, lam_p, nw, cache_k, cache_v, rope, t_len, n_seq, row_blk0, lam_init):
    qm = jnp.asarray(_att_consts(), F32)
    _, hv, _, ones_bd = _head_consts(BRANCH_W)
    hv, ones_bd = jnp.asarray(hv, F32), jnp.asarray(ones_bd, BF)
    lat = cache_k is not None
    consts = (lam_p, nw, qm, hv, ones_bd)
    in_specs = [pl.BlockSpec((t_len, 768), lambda b: (row_blk0 + b, 4))]
    in_specs += [_const_spec(a) for a in consts]
    args = [proj, *consts]
    out_shape = [jax.ShapeDtypeStruct((n_seq * t_len, BRANCH_W), F32)]
    out_specs = [pl.BlockSpec((t_len, BRANCH_W), lambda b: (b, 0))]
    scratch = []
    if lat:
        cos, sin = rope
        in_specs += [_const_spec(cos), _const_spec(sin),
                     pl.BlockSpec((1, PAST_LEN, BRANCH_W), lambda b: (b, 0, 0)),
                     pl.BlockSpec((1, PAST_LEN, BRANCH_W), lambda b: (b, 0, 0))]
        args += [cos, sin, cache_k, cache_v]
        scratch = [pltpu.VMEM((t_len, BRANCH_W), F32),
                   pltpu.VMEM((PAST_LEN + t_len, BRANCH_W), F32),
                   pltpu.VMEM((PAST_LEN + t_len, BRANCH_W), F32)]
    else:
        for _ in range(2):
            out_shape.append(jax.ShapeDtypeStruct((n_seq, N_HEADS, t_len, HEAD_V), F32))
            out_specs.append(pl.BlockSpec((1, N_HEADS, t_len, HEAD_V), lambda b: (b, 0, 0, 0)))
    return pl.pallas_call(
        functools.partial(_att_kernel, t_len=t_len, lat=lat, lam_init=lam_init),
        grid=(n_seq,),
        in_specs=in_specs,
        out_specs=out_specs,
        out_shape=out_shape,
        scratch_shapes=scratch,
        compiler_params=_cparams(("arbitrary",)),
        name="diff_attention",
    )(*args)


def _block_diag_states(st, dk, transposed):
    eye = jnp.eye(N_HEADS, dtype=st.dtype)
    b = st.shape[0]
    if transposed:
        return jnp.einsum('bnhde,hg->bnhegd', st, eye).reshape(b, 2, N_HEADS * HEAD_V, N_HEADS * dk)
    return jnp.einsum('bnhde,hg->bnhdge', st, eye).reshape(b, 2, N_HEADS * dk, N_HEADS * HEAD_V)


def _in_perm():
    offs = np.concatenate([[0], np.cumsum(IN_ORIG)])
    seg = lambda a, b: np.arange(offs[a], offs[b])
    return np.concatenate([seg(0, 4), seg(5, 6), seg(8, 9), seg(9, 16), seg(4, 5), seg(6, 8)])


def kernel(x_prompt, x_sample, cache_diff_k, cache_diff_v, state_gla, state_dn, state_hgrn, c, c_ctx,
           norm_w, w_mod, b_mod, ffn1_in, ffn1_down, ffn2_in, ffn2_down, w_in, gla_w2, gla_b, gla_norm,
           dn_conv, dn_a_log, dn_dt_bias, dn_norm, hg_lb_logits, hg_norm, diff_lambda, diff_norm,
           w_branch, w_mgate, w_out, final_norm):
    xs = (x_prompt.reshape(N_CTX_TOK, D_MODEL), x_sample.reshape(-1, D_MODEL))
    c_rows = jnp.concatenate([c_ctx[None, :], c, jnp.zeros((8 - 1 - N_LAT_SEQ, D_MODEL), F32)], axis=0)
    mod = _mod_call(c_rows, w_mod, b_mod).reshape(DEPTH, 8, N_MOD, D_MODEL)
    rope = _rope_tables()
    perm = _in_perm()
    lb_logits = hg_lb_logits.reshape(DEPTH, 2 * HG_W)
    lat_blk = N_CTX_TOK // T_LAT
    tile4 = lambda a: jnp.tile(a, N_HEADS)[None, :]
    fin = final_norm[None, :]
    new_k, new_v, new_gla, new_dn, new_hg = [], [], [], [], []
    for l in range(DEPTH):
        mod_l = mod[l]
        lam_init = 0.8 - 0.6 * math.exp(-0.3 * l)
        w_in_p = jnp.pad(w_in[l][:, perm], ((0, 0), (0, N_IN_PAD - N_IN))).astype(BF)
        w2bd = jnp.zeros((128, 2 * GLA_KW), F32)
        w2bd = w2bd.at[0:GLA_LOWRANK, 0:GLA_KW].set(gla_w2[l, 0])
        w2bd = w2bd.at[GLA_LOWRANK:2 * GLA_LOWRANK, GLA_KW:].set(gla_w2[l, 1]).astype(BF)
        gbias = gla_b[l].reshape(1, 2 * GLA_KW)
        alog_row = jnp.zeros((1, 128), F32).at[0, SMALL_DNA:SMALL_DNA + 8].set(dn_a_log[l].reshape(-1))
        dtb_row = jnp.zeros((1, 128), F32).at[0, SMALL_DNA:SMALL_DNA + 8].set(dn_dt_bias[l].reshape(-1))

        (x,) = _ffn_call(xs, mod_l, norm_w[l, 0][None, :], ffn1_in[l].astype(BF), ffn1_down[l].astype(BF),
                         fin, 0, False, False)
        proj = _proj_call(x, mod_l, norm_w[l, 1][None, :], w_in_p)

        a_c, st_a = _gla_call(proj, w2bd, gbias, tile4(gla_norm[l]), None, T_CTX, N_CTX_SEQ, 0)
        dn_lat_blk = N_CTX_TOK // (N_PAIR * T_LAT)
        q_c, k_c, dd_c, lw_c, vb_c, kb_c, qkd_c = _dn_build_call(proj, dn_conv[l], alog_row, dtb_row,
                                                                T_CTX, N_CTX_SEQ, 0)
        q_l, k_l, dd_l, lw_l, vb_l, kb_l, qkd_l = _dn_build_call(proj, dn_conv[l], alog_row, dtb_row,
                                                                T_LAT, N_LAT_SEQ, dn_lat_blk)
        u_c, w_c = (_chunks_from_lanes(a) for a in _dn_solve_call(
            _chunks_to_lanes(lw_c), _chunks_to_lanes(vb_c), _chunks_to_lanes(kb_c)))
        u_l, w_l = (_chunks_from_lanes(a) for a in _dn_solve_call(
            _chunks_to_lanes(lw_l), _chunks_to_lanes(vb_l), _chunks_to_lanes(kb_l)))
        b_c, st_b = _dn_scan_call(proj, q_c, k_c, dd_c, u_c, w_c, qkd_c, tile4(dn_norm[l]), None,
                                  T_CTX, N_CTX_SEQ, 0)
        c_c, st_c = _hgrn_call(proj, lb_logits, tile4(hg_norm[l]), None, T_CTX, N_CTX_SEQ, 0, l)
        d_c, new_k_l, new_v_l = _att_call(proj, diff_lambda[l], tile4(diff_norm[l]), None, None, None,
                                  T_CTX, N_CTX_SEQ, 0, lam_init)
        ck = cache_diff_k[:, l].transpose(0, 2, 1, 3).reshape(N_LAT_SEQ, PAST_LEN, BRANCH_W)
        cv = cache_diff_v[:, l].transpose(0, 2, 1, 3).reshape(N_LAT_SEQ, PAST_LEN, BRANCH_W)
        (a_l,) = _gla_call(proj, w2bd, gbias, tile4(gla_norm[l]),
                           _block_diag_states(state_gla[:, l], GLA_DK, True), T_LAT, N_LAT_SEQ, dn_lat_blk)
        (b_l,) = _dn_scan_call(proj, q_l, k_l, dd_l, u_l, w_l, qkd_l, tile4(dn_norm[l]),
                               _block_diag_states(state_dn[:, l], DN_DK, False), T_LAT, N_LAT_SEQ, dn_lat_blk)
        (c_l,) = _hgrn_call(proj, lb_logits, tile4(hg_norm[l]),
                            _block_diag_states(state_hgrn[:, l], HG_DK, True), T_LAT, N_LAT_SEQ, dn_lat_blk, l)
        (d_l,) = _att_call(proj, diff_lambda[l], tile4(diff_norm[l]), ck, cv, rope,
                           T_LAT, N_LAT_SEQ, lat_blk, lam_init)

        x = _merge_call(x, mod_l, norm_w[l, 1][None, :], (a_c, b_c, c_c, d_c), (a_l, b_l, c_l, d_l),
                        w_mgate[l].astype(BF), w_branch[l].astype(BF), w_out[l].astype(BF))
        xs = _ffn_call((x,), mod_l, norm_w[l, 2][None, :], ffn2_in[l].astype(BF), ffn2_down[l].astype(BF),
                       fin, 2, l == DEPTH - 1, l == DEPTH - 1)
        new_k.append(new_k_l)
        new_v.append(new_v_l)
        new_gla.append(st_a)
        new_dn.append(st_b)
        new_hg.append(st_c)
    y_prompt = xs[0].reshape(N_CTX_SEQ, T_CTX, D_MODEL)
    y_sample = xs[1].reshape(N_LAT_SEQ, T_LAT, D_MODEL)
    return (y_prompt, y_sample, jnp.stack(new_k, axis=1), jnp.stack(new_v, axis=1),
            jnp.stack(new_gla, axis=1), jnp.stack(new_dn, axis=1), jnp.stack(new_hg, axis=1))
```

```python
import functools
import math

import numpy as np
import jax
import jax.numpy as jnp
from jax import lax
from jax.experimental import pallas as pl
from jax.experimental.pallas import tpu as pltpu

F32 = jnp.float32
BF = jnp.bfloat16

D_MODEL = 1024
N_CTX_SEQ = 32
T_CTX = 256
DEPTH = 2
N_LAT_SEQ = 2
T_LAT = 1024
PAST_LEN = 512
GRID_W = 64
N_HEADS = 4
BRANCH_W = 256
HEAD_V = 64
GLA_DK = 32
GLA_KW = 128
GLA_LOWRANK = 16
GLA_TAU = 16.0
DN_DK = 64
HG_DK = 64
HG_W = 256
DF_DH = 32
ROPE_BASE = 10000.0
D_FF = 2816
N_MOD = 9
CHUNK = 64
EPS = 1e-6
N_CTX_TOK = N_CTX_SEQ * T_CTX
N_TOK = N_CTX_TOK + N_LAT_SEQ * T_LAT
N_LEVELS = 6
SMALL_LEVEL = 4
N_SCAN_CONSTS = 13

IN_ORIG = (128, 128, 256, 256, 32, 768, 8, 8, 256, 256, 512, 256, 256, 256, 256, 256)
N_IN = sum(IN_ORIG)
N_IN_PAD = 3968
COL_SMALL_BLOCK = 30
SMALL_LR = 0
SMALL_DNB = 32
SMALL_DNA = 40

VMEM_LIMIT = 56 * 1024 * 1024

TM_FFN = 512
TF_FFN = 1408
TM_PROJ = 512
TM_MERGE = 512
TN_MOD = 2304
TQ_ATT = 256


def _silu(x):
    return x * (1.0 / (1.0 + jnp.exp(-x)))


def _sigmoid(x):
    return 1.0 / (1.0 + jnp.exp(-x))


def _softplus(x):
    return jnp.maximum(x, 0.0) + jnp.log(1.0 + jnp.exp(-jnp.abs(x)))


def _log_sigmoid(x):
    return -_softplus(-x)


def _mm(a, b):
    return jnp.dot(a.astype(BF), b.astype(BF), preferred_element_type=F32)


def _mm_nt(a, b):
    return lax.dot_general(a.astype(BF), b.astype(BF), (((1,), (1,)), ((), ())),
                           preferred_element_type=F32)


def _mm_tn(a, b):
    return lax.dot_general(a.astype(BF), b.astype(BF), (((0,), (0,)), ((), ())),
                           preferred_element_type=F32)


def _split(x, n):
    parts = []
    r = x
    for i in range(n):
        p = r.astype(BF)
        parts.append(p)
        if i + 1 < n:
            r = r - p.astype(F32)
    return parts


def _sel_l(m01, x, n=3):
    out = None
    for p in _split(x, n):
        t = jnp.dot(m01, p, preferred_element_type=F32)
        out = t if out is None else out + t
    return out


def _sel_r(x, m01, n=3):
    out = None
    for p in _split(x, n):
        t = jnp.dot(p, m01, preferred_element_type=F32)
        out = t if out is None else out + t
    return out


def _sel_tn(x, m01, n=3):
    out = None
    for p in _split(x, n):
        t = lax.dot_general(p, m01, (((0,), (0,)), ((), ())), preferred_element_type=F32)
        out = t if out is None else out + t
    return out


def _rms(x, w):
    return x * lax.rsqrt(jnp.mean(x * x, axis=-1, keepdims=True) + EPS) * w


def _head_lanes(x, h):
    blk = x[:, (h // 2) * 128:(h // 2 + 1) * 128]
    if h % 2:
        blk = pltpu.roll(blk, 64, 1)
    return blk[:, :HEAD_V]


def _stack_heads(x, hm_ref):
    return jnp.concatenate([x * hm_ref[h:h + 1, :] for h in range(N_HEADS)], axis=0)


def _stack_heads_bf(x, hm_ref):
    xb = x.astype(BF)
    return jnp.concatenate([xb * hm_ref[h:h + 1, :].astype(BF) for h in range(N_HEADS)], axis=0)


def _head_diag(o_full, hv_ref, c):
    out = None
    for h in range(N_HEADS):
        t = o_full[h * c:(h + 1) * c, :] * hv_ref[h:h + 1, :]
        out = t if out is None else out + t
    return out


def _head_norm_gate(o, gate, nw, ones_bd):
    ms = _sel_r(o * o, ones_bd, 2) * (1.0 / HEAD_V)
    return o * lax.rsqrt(ms + EPS) * nw * _silu(gate)


def _mod_row(i, tm):
    return jnp.maximum(i * tm - (N_CTX_TOK - T_LAT), 0) // T_LAT


def _cparams(sem):
    return pltpu.CompilerParams(dimension_semantics=sem, vmem_limit_bytes=VMEM_LIMIT)


@functools.lru_cache(maxsize=None)
def _scan_consts(w, reverse):
    c = CHUNK
    idx = np.arange(c)
    i = idx[:, None]
    m = idx[None, :]
    pm, sg = [], []
    s = c // 2
    while s >= 1:
        par = idx // (2 * s)
        right = (idx % (2 * s)) >= s
        same = par[:, None] == par[None, :]
        query = ~right if reverse else right
        pm.append(same & query[:, None] & (~query)[None, :])
        if s >= SMALL_LEVEL:
            sg.append(np.where(query, 1.0, -1.0))
        s //= 2
    pm.append(i == m)
    tri = (m >= i) if reverse else (m <= i)
    r4, r2 = idx % 4, idx % 2
    if not reverse:
        cf = [r2 == 1, r4 >= 2, r4 == 3, r4 == 0]
    else:
        cf = [r2 == 0, r4 <= 1, r4 == 3, r4 == 0]
    wide = lambda rows: np.repeat(np.stack(rows).astype(np.float32)[:, :, None], w, axis=2)
    pmask = np.stack([np.tile(p, (1, N_HEADS)) for p in pm]).astype(np.float32)
    return tri.astype(np.float32), wide(sg), wide(cf), pmask


@functools.lru_cache(maxsize=None)
def _head_consts(w):
    dk = w // N_HEADS
    hm = np.zeros((N_HEADS, w), np.float32)
    hv = np.zeros((N_HEADS, BRANCH_W), np.float32)
    for h in range(N_HEADS):
        hm[h, h * dk:(h + 1) * dk] = 1.0
        hv[h, h * HEAD_V:(h + 1) * HEAD_V] = 1.0
    bd = hm.T @ hv
    ones_bd = hv.T @ hv
    return hm, hv, bd, ones_bd


@functools.lru_cache(maxsize=None)
def _dn_consts():
    c = CHUNK
    idx = np.arange(c)
    i = idx[:, None]
    j = idx[None, :]
    tri = np.stack([(j <= i), (j >= i)]).astype(np.float32)
    strict = np.stack([np.tile(j < i, (1, N_HEADS)), np.tile(j > i, (1, N_HEADS))]).astype(np.float32)
    eye = np.tile(np.eye(c), (1, N_HEADS)).astype(np.float32)
    blk = np.ones((c, c), np.float32)
    exb = np.zeros((128, 2 * BRANCH_W), np.float32)
    exa = np.zeros((128, 2 * BRANCH_W), np.float32)
    for n in range(2):
        for h in range(N_HEADS):
            lo = n * BRANCH_W + h * HEAD_V
            exb[SMALL_DNB + n * N_HEADS + h, lo:lo + HEAD_V] = 1.0
            exa[SMALL_DNA + n * N_HEADS + h, lo:lo + HEAD_V] = 1.0
    return tri, strict, eye, blk, exb, exa


@functools.lru_cache(maxsize=None)
def _att_consts():
    qm = np.zeros((2 * N_HEADS, BRANCH_W), np.float32)
    for h in range(N_HEADS):
        for mp in range(2):
            lo = h * HEAD_V + mp * DF_DH
            qm[2 * h + mp, lo:lo + DF_DH] = 1.0
    return qm


def _rope_tables():
    rows = T_LAT // GRID_W
    row = jnp.repeat(jnp.arange(rows), GRID_W).astype(F32)
    col = jnp.tile(jnp.arange(GRID_W), rows).astype(F32)
    half = DF_DH // 2
    inv = ROPE_BASE ** (-jnp.arange(0, half, 2, dtype=F32) / half)

    def angles(pos):
        a = pos[:, None] * inv[None, :]
        return jnp.concatenate([a, a], axis=-1)

    ang = jnp.concatenate([angles(row), angles(col)], axis=-1)
    reps = BRANCH_W // DF_DH
    return jnp.tile(jnp.cos(ang), (1, reps)), jnp.tile(jnp.sin(ang), (1, reps))


def _mod_kernel(c_ref, w_ref, b_ref, o_ref):
    a = _silu(c_ref[...])
    w = w_ref[0]
    out = None
    for ap in _split(a, 2):
        for wp in _split(w, 2):
            t = jnp.dot(ap, wp, preferred_element_type=F32)
            out = t if out is None else out + t
    o_ref[0] = out + b_ref[0]


def _mod_call(c_rows, w_mod, b_mod):
    n_t = (N_MOD * D_MODEL) // TN_MOD
    return pl.pallas_call(
        _mod_kernel,
        grid=(DEPTH, n_t),
        in_specs=[pl.BlockSpec((8, D_MODEL), lambda l, j: (0, 0)),
                  pl.BlockSpec((1, D_MODEL, TN_MOD), lambda l, j: (l, 0, j)),
                  pl.BlockSpec((1, 1, TN_MOD), lambda l, j: (l, 0, j))],
        out_specs=pl.BlockSpec((1, 8, TN_MOD), lambda l, j: (l, 0, j)),
        out_shape=jax.ShapeDtypeStruct((DEPTH, 8, N_MOD * D_MODEL), F32),
        compiler_params=_cparams(("arbitrary", "arbitrary")),
        name="mod_vectors",
    )(c_rows, w_mod, b_mod.reshape(DEPTH, 1, N_MOD * D_MODEL))


def _ffn_kernel(*refs, sub, final, split_in, split_out):
    n_x = 2 if split_in else 1
    mod_ref, nw_ref, wup_ref, wd_ref, fn_ref = refs[n_x:n_x + 5]
    outs = refs[n_x + 5:]
    is_lat = pl.program_id(0) >= N_CTX_TOK // TM_FFN
    x = jnp.where(is_lat, refs[1][...], refs[0][...]) if split_in else refs[0][...]
    sh = mod_ref[0, 3 * sub:3 * sub + 1, :]
    sc = mod_ref[0, 3 * sub + 1:3 * sub + 2, :]
    ga = mod_ref[0, 3 * sub + 2:3 * sub + 3, :]
    h = (_rms(x, nw_ref[...]) * (1.0 + sc) + sh).astype(BF)
    acc = None
    for f in range(D_FF // TF_FFN):
        lo = f * TF_FFN
        g = jnp.dot(h, wup_ref[:, lo:lo + TF_FFN], preferred_element_type=F32)
        u = jnp.dot(h, wup_ref[:, D_FF + lo:D_FF + lo + TF_FFN], preferred_element_type=F32)
        t = jnp.dot((_silu(g) * u).astype(BF), wd_ref[lo:lo + TF_FFN, :], preferred_element_type=F32)
        acc = t if acc is None else acc + t
    y = x + 0.5 * ga * acc
    if final:
        y = _rms(y, fn_ref[...])
    if split_out:
        @pl.when(jnp.logical_not(is_lat))
        def _():
            outs[0][...] = y

        @pl.when(is_lat)
        def _():
            outs[1][...] = y
    else:
        outs[0][...] = y


def _ffn_call(xs, mod_l, nw, w_in, w_down, final_w, sub, final, split_out):
    tm = TM_FFN
    n_ctx = N_CTX_TOK // tm
    ctx_map = lambda i: (jnp.minimum(i, n_ctx - 1), 0)
    lat_map = lambda i: (jnp.maximum(i - n_ctx, 0), 0)
    tile = lambda m: pl.BlockSpec((tm, D_MODEL), m)
    resident = lambda a: pl.BlockSpec(a.shape, lambda i: (0, 0), pipeline_mode=pl.Buffered(1))
    split_in = len(xs) == 2
    in_specs = [tile(ctx_map), tile(lat_map)] if split_in else [tile(lambda i: (i, 0))]
    in_specs += [pl.BlockSpec((1, N_MOD, D_MODEL), lambda i: (_mod_row(i, tm), 0, 0)),
                 pl.BlockSpec((1, D_MODEL), lambda i: (0, 0)),
                 resident(w_in), resident(w_down),
                 pl.BlockSpec((1, D_MODEL), lambda i: (0, 0))]
    if split_out:
        out_specs = [tile(ctx_map), tile(lat_map)]
        out_shape = [jax.ShapeDtypeStruct((N_CTX_TOK, D_MODEL), F32),
                     jax.ShapeDtypeStruct((N_TOK - N_CTX_TOK, D_MODEL), F32)]
    else:
        out_specs = [tile(lambda i: (i, 0))]
        out_shape = [jax.ShapeDtypeStruct((N_TOK, D_MODEL), F32)]
    return pl.pallas_call(
        functools.partial(_ffn_kernel, sub=sub, final=final, split_in=split_in, split_out=split_out),
        grid=(N_TOK // tm,),
        in_specs=in_specs,
        out_specs=out_specs,
        out_shape=out_shape,
        compiler_params=_cparams(("arbitrary",)),
        name="swiglu_half_step",
    )(*xs, mod_l, nw, w_in, w_down, final_w)


def _proj_kernel(x_ref, mod_ref, nw_ref, w_ref, o_ref):
    sh = mod_ref[0, 3:4, :]
    sc = mod_ref[0, 4:5, :]
    h = (_rms(x_ref[...], nw_ref[...]) * (1.0 + sc) + sh).astype(BF)
    o_ref[...] = jnp.dot(h, w_ref[...], preferred_element_type=F32)


def _proj_call(x, mod_l, nw, w_in_p):
    tm = TM_PROJ
    return pl.pallas_call(
        _proj_kernel,
        grid=(N_TOK // tm,),
        in_specs=[pl.BlockSpec((tm, D_MODEL), lambda i: (i, 0)),
                  pl.BlockSpec((1, N_MOD, D_MODEL), lambda i: (_mod_row(i, tm), 0, 0)),
                  pl.BlockSpec((1, D_MODEL), lambda i: (0, 0)),
                  pl.BlockSpec((D_MODEL, N_IN_PAD), lambda i: (0, 0))],
        out_specs=pl.BlockSpec((tm, N_IN_PAD), lambda i: (i, 0)),
        out_shape=jax.ShapeDtypeStruct((N_TOK, N_IN_PAD), F32),
        compiler_params=_cparams(("arbitrary",)),
        name="mixer_in_proj",
    )(x, mod_l, nw, w_in_p)


def _merge_kernel(x_ref, mod_ref, nw_ref, *rest):
    ctx_refs, lat_refs = rest[0:4], rest[4:8]
    wg_ref, wb_ref, wo_ref, o_ref = rest[8:12]
    x = x_ref[...]
    sh = mod_ref[0, 3:4, :]
    sc = mod_ref[0, 4:5, :]
    ga = mod_ref[0, 5:6, :]
    h = (_rms(x, nw_ref[...]) * (1.0 + sc) + sh).astype(BF)
    is_lat = pl.program_id(0) >= N_CTX_TOK // TM_MERGE
    mixed = None
    for n in range(4):
        gate = _sigmoid(jnp.dot(h, wg_ref[:, n * D_MODEL:(n + 1) * D_MODEL], preferred_element_type=F32))
        br = jnp.where(is_lat, lat_refs[n][...], ctx_refs[n][...])
        up = jnp.dot(br.astype(BF), wb_ref[n], preferred_element_type=F32)
        mixed = gate * up if mixed is None else mixed + gate * up
    out = jnp.dot(mixed.astype(BF), wo_ref[...], preferred_element_type=F32)
    o_ref[...] = x + ga * out


def _merge_call(x, mod_l, nw, ctx_branches, lat_branches, w_mgate, w_branch, w_out):
    tm = TM_MERGE
    n_ctx = N_CTX_TOK // tm
    cspec = pl.BlockSpec((tm, BRANCH_W), lambda i: (jnp.minimum(i, n_ctx - 1), 0))
    lspec = pl.BlockSpec((tm, BRANCH_W), lambda i: (jnp.maximum(i - n_ctx, 0), 0))
    return pl.pallas_call(
        _merge_kernel,
        grid=(N_TOK // tm,),
        in_specs=[pl.BlockSpec((tm, D_MODEL), lambda i: (i, 0)),
                  pl.BlockSpec((1, N_MOD, D_MODEL), lambda i: (_mod_row(i, tm), 0, 0)),
                  pl.BlockSpec((1, D_MODEL), lambda i: (0, 0)),
                  cspec, cspec, cspec, cspec, lspec, lspec, lspec, lspec,
                  pl.BlockSpec((D_MODEL, 4 * D_MODEL), lambda i: (0, 0)),
                  pl.BlockSpec((4, BRANCH_W, D_MODEL), lambda i: (0, 0, 0)),
                  pl.BlockSpec((D_MODEL, D_MODEL), lambda i: (0, 0))],
        out_specs=pl.BlockSpec((tm, D_MODEL), lambda i: (i, 0)),
        out_shape=jax.ShapeDtypeStruct((N_TOK, D_MODEL), F32),
        compiler_params=_cparams(("arbitrary",)),
        name="gated_merge",
    )(x, mod_l, nw, *ctx_branches, *lat_branches, w_mgate, w_branch, w_out)


def _gated_chunk(qc, kc, vc, gc, st, tri_ref, sg_ref, cf_ref, pm_ref, hm_ref, hv_ref, bdt_ref, reverse):
    c = CHUNK
    w = gc.shape[1]
    cum = _sel_l(tri_ref[...], gc, 2)
    kst = _stack_heads_bf(kc, hm_ref)
    qb = qc.astype(BF)
    a = pm_ref[N_LEVELS] * _mm_nt(qb, kst)
    lv = 0
    s = c // 2
    while s >= 1:
        if s >= SMALL_LEVEL:
            first = s if reverse else s - 1
            bnd = [jnp.broadcast_to(cum[p * 2 * s + first:p * 2 * s + first + 1, :], (2 * s, w))
                   for p in range(c // (2 * s))]
            bnd = jnp.concatenate(bnd, axis=0) if len(bnd) > 1 else bnd[0]
            dl = (cum - bnd) * sg_ref[lv]
        elif s == 2:
            dl = (gc * cf_ref[1] + pltpu.roll(gc, 1, 0) * cf_ref[2] + pltpu.roll(gc, c - 1, 0) * cf_ref[3])
        else:
            dl = gc * cf_ref[0]
        el = jnp.exp(dl).astype(BF)
        a = a + pm_ref[lv] * _mm_nt(qb * el, kst * jnp.concatenate([el] * N_HEADS, axis=0))
        lv += 1
        s //= 2
    tot = cum[0:1, :] if reverse else cum[c - 1:c, :]
    o = _mm(a, _stack_heads_bf(vc, hv_ref)) + _mm_nt(qc * jnp.exp(cum), st)
    st_new = st * jnp.exp(tot) + bdt_ref[...] * _mm_tn(vc, kc * jnp.exp(tot - cum))
    return o, st_new


def _gated_scan(q_ref, kf_ref, kb_ref, v_ref, gf_ref, gb_ref, of_ref, ob_ref, s0_ref, st_ref, cf, cb,
                t_len, w, dk, eye_ref):
    n_chunks = t_len // CHUNK

    def body(t, carry):
        new = []
        for sq in range(N_PAIR):
            rf = pl.ds(pl.multiple_of(sq * t_len + t * CHUNK, CHUNK), CHUNK)
            rb = pl.ds(pl.multiple_of(sq * t_len + (n_chunks - 1 - t) * CHUNK, CHUNK), CHUNK)
            o_f, sf = _gated_chunk(q_ref[rf, :], kf_ref[rf, :], v_ref[rf, :], gf_ref[rf, :],
                                   carry[2 * sq], *cf, False)
            o_b, sb = _gated_chunk(q_ref[rb, :], kb_ref[rb, :], v_ref[rb, :], gb_ref[rb, :],
                                   carry[2 * sq + 1], *cb, True)
            of_ref[rf, :] = o_f
            ob_ref[rb, :] = o_b
            new += [sf, sb]
        return tuple(new)

    if s0_ref is not None:
        init = tuple(s0_ref[sq, d] for sq in range(N_PAIR) for d in range(2))
    else:
        init = tuple(jnp.zeros((BRANCH_W, w), F32) for _ in range(2 * N_PAIR))
    finals = lax.fori_loop(0, n_chunks, body, init)
    if st_ref is not None:
        _zero_later_layers(st_ref)
        for sq in range(N_PAIR):
            for d in range(2):
                st = finals[2 * sq + d]
                for h in range(N_HEADS):
                    tr = _sel_tn(st[h * HEAD_V:(h + 1) * HEAD_V, :], eye_ref[...])
                    st_ref[sq, 0, d, h] = tr[h * dk:(h + 1) * dk, :]


def _gla_kernel(*refs, t_len, has_state, n_prev):
    blk_ref, small_ref, w2_ref, gb_ref, nw_ref = refs[:5]
    pos = 5 + N_SCAN_CONSTS
    cf, cb, ones_ref, eye_ref = _split_scan_consts(refs[5:pos])
    s0_ref = st_ref = None
    if has_state:
        s0_ref = refs[pos]
        pos += 1
    pos += n_prev
    out_ref = refs[pos]
    pos += 1
    if not has_state:
        st_ref = refs[pos]
        pos += 1
    q_s, gf_s, gb_s, of_s, ob_s = refs[pos:pos + 5]

    q_s[...] = blk_ref[:, 0:128] * (GLA_DK ** -0.5)
    z = _mm(small_ref[...], w2_ref[...]) + gb_ref[...]
    g = _log_sigmoid(z) * (1.0 / GLA_TAU)
    gf_s[...] = g[:, 0:128]
    gb_s[...] = g[:, 128:256]
    k_ref = blk_ref.at[:, 128:256]
    v_ref = blk_ref.at[:, 256:512]
    _gated_scan(q_s, k_ref, k_ref, v_ref, gf_s, gb_s, of_s, ob_s, s0_ref, st_ref, cf, cb,
                t_len, GLA_KW, GLA_DK, eye_ref)
    out_ref[...] = _head_norm_gate(of_s[...] + ob_s[...], blk_ref[:, 512:768], nw_ref[...], ones_ref[...])


def _const_spec(a):
    nd = a.ndim
    return pl.BlockSpec(a.shape, lambda b, _n=nd: (0,) * _n)


def _scan_const_arrays(w):
    out = []
    for reverse in (False, True):
        tri, sg, cf, pm = _scan_consts(w, reverse)
        out += [jnp.asarray(tri, BF), jnp.asarray(sg, F32), jnp.asarray(cf, F32), jnp.asarray(pm, F32)]
    hm, hv, bd, ones_bd = _head_consts(w)
    out += [jnp.asarray(hm, F32), jnp.asarray(hv, F32), jnp.asarray(bd.T, F32), jnp.asarray(ones_bd, BF),
            jnp.eye(HEAD_V, dtype=BF)]
    return tuple(out)


def _split_scan_consts(refs):
    shared = tuple(refs[8:11])
    return tuple(refs[0:4]) + shared, tuple(refs[4:8]) + shared, refs[11], refs[12]


def _zero_later_layers(ref):
    if ref.shape[1] > 1:
        ref[:, 1:] = jnp.zeros((ref.shape[0], ref.shape[1] - 1) + tuple(ref.shape[2:]), ref.dtype)


def _add_layer_outputs(tails, n_lead, lead_block, layer, prevs, in_specs, args, out_shape, out_specs):
    first_out = len(out_shape)
    for tail in tails:
        out_shape.append(jax.ShapeDtypeStruct((n_lead, DEPTH) + tuple(tail), F32))
        if prevs is None:
            assert layer == 0
            out_specs.append(pl.BlockSpec((lead_block, DEPTH) + tuple(tail),
                                          lambda b, _nz=len(tail): (b,) + (0,) * (_nz + 1)))
        else:
            out_specs.append(pl.BlockSpec((lead_block, 1) + tuple(tail),
                                          lambda b, _nz=len(tail): (b, layer) + (0,) * _nz))
    aliases = {}
    for k, p in enumerate(prevs or ()):
        in_specs.append(pl.BlockSpec(memory_space=pl.ANY))
        aliases[len(args)] = first_out + k
        args.append(p)
    return aliases


def _gla_call(proj, w2bd, gbias, nw, s0, t_len, n_seq, row_blk0, layer=0, prev=None):
    has_state = s0 is not None
    consts = (w2bd, gbias, nw) + _scan_const_arrays(GLA_KW)
    n_rows = N_PAIR * t_len
    in_specs = [pl.BlockSpec((n_rows, 768), lambda b: (row_blk0 + b, 0)),
                pl.BlockSpec((n_rows, 128), lambda b: (row_blk0 + b, COL_SMALL_BLOCK))]
    in_specs += [_const_spec(a) for a in consts]
    args = [proj, proj, *consts]
    if has_state:
        in_specs.append(pl.BlockSpec((N_PAIR, 2, BRANCH_W, GLA_KW), lambda b: (b, 0, 0, 0)))
        args.append(s0)
    out_shape = [jax.ShapeDtypeStruct((n_seq * t_len, BRANCH_W), F32)]
    out_specs = [pl.BlockSpec((n_rows, BRANCH_W), lambda b: (b, 0))]
    aliases = {}
    if not has_state:
        aliases = _add_layer_outputs([(2, N_HEADS, GLA_DK, HEAD_V)], n_seq, N_PAIR, layer, prev,
                                     in_specs, args, out_shape, out_specs)
    return pl.pallas_call(
        functools.partial(_gla_kernel, t_len=t_len, has_state=has_state, n_prev=len(aliases)),
        grid=(n_seq // N_PAIR,),
        in_specs=in_specs,
        out_specs=out_specs,
        out_shape=out_shape,
        input_output_aliases=aliases,
        scratch_shapes=[pltpu.VMEM((n_rows, GLA_KW), F32)] * 3 + [pltpu.VMEM((n_rows, BRANCH_W), F32)] * 2,
        compiler_params=_cparams(("arbitrary",)),
        name="gla_mixer",
    )(*args)


def _hgrn_kernel(*refs, t_len, has_state, layer, n_prev):
    q_ref, f_ref, v_ref, gate_ref, lbl_ref, nw_ref = refs[:6]
    pos = 6 + N_SCAN_CONSTS
    cf, cb, ones_ref, eye_ref = _split_scan_consts(refs[6:pos])
    s0_ref = st_ref = None
    if has_state:
        s0_ref = refs[pos]
        pos += 1
    pos += n_prev
    out_ref = refs[pos]
    pos += 1
    if not has_state:
        st_ref = refs[pos]
        pos += 1
    q_s, kf_s, kb_s, gf_s, gb_s, of_s, ob_s = refs[pos:pos + 7]

    lg = lbl_ref[...]
    mx = jnp.max(lg, axis=0, keepdims=True)
    ex = jnp.exp(lg - mx)
    p = ex / jnp.sum(ex, axis=0, keepdims=True)
    lb = jnp.sum(p[0:layer + 1], axis=0, keepdims=True) - p[0:1]

    q_s[...] = _silu(q_ref[...]) * (HG_DK ** -0.5)
    f = lb + (1.0 - lb) * _sigmoid(f_ref[...])
    kf_s[...] = 1.0 - f[:, 0:HG_W]
    kb_s[...] = 1.0 - f[:, HG_W:2 * HG_W]
    lf = jnp.log(f)
    gf_s[...] = lf[:, 0:HG_W]
    gb_s[...] = lf[:, HG_W:2 * HG_W]
    _gated_scan(q_s, kf_s, kb_s, v_ref, gf_s, gb_s, of_s, ob_s, s0_ref, st_ref, cf, cb,
                t_len, HG_W, HG_DK, eye_ref)
    out_ref[...] = _head_norm_gate(of_s[...] + ob_s[...], gate_ref[...], nw_ref[...], ones_ref[...])


def _hgrn_call(proj, lb_logits, nw, s0, t_len, n_seq, row_blk0, layer, prev=None):
    has_state = s0 is not None
    consts = (lb_logits, nw) + _scan_const_arrays(HG_W)
    n_rows = N_PAIR * t_len
    in_specs = [pl.BlockSpec((n_rows, 256), lambda b: (row_blk0 + b, 7)),
                pl.BlockSpec((n_rows, 512), lambda b: (row_blk0 + b, 4)),
                pl.BlockSpec((n_rows, 256), lambda b: (row_blk0 + b, 10)),
                pl.BlockSpec((n_rows, 256), lambda b: (row_blk0 + b, 11))]
    in_specs += [_const_spec(a) for a in consts]
    args = [proj, proj, proj, proj, *consts]
    if has_state:
        in_specs.append(pl.BlockSpec((N_PAIR, 2, BRANCH_W, HG_W), lambda b: (b, 0, 0, 0)))
        args.append(s0)
    out_shape = [jax.ShapeDtypeStruct((n_seq * t_len, BRANCH_W), F32)]
    out_specs = [pl.BlockSpec((n_rows, BRANCH_W), lambda b: (b, 0))]
    aliases = {}
    if not has_state:
        aliases = _add_layer_outputs([(2, N_HEADS, HG_DK, HEAD_V)], n_seq, N_PAIR, layer, prev,
                                     in_specs, args, out_shape, out_specs)
    return pl.pallas_call(
        functools.partial(_hgrn_kernel, t_len=t_len, has_state=has_state, layer=layer, n_prev=len(aliases)),
        grid=(n_seq // N_PAIR,),
        in_specs=in_specs,
        out_specs=out_specs,
        out_shape=out_shape,
        input_output_aliases=aliases,
        scratch_shapes=[pltpu.VMEM((n_rows, HG_W), F32)] * 5 + [pltpu.VMEM((n_rows, BRANCH_W), F32)] * 2,
        compiler_params=_cparams(("arbitrary",)),
        name="hgrn2_mixer",
    )(*args)


N_PAIR = 2
SOLVE_JB = 8
SOLVE_IB = 16


def _dn_solve_kernel(lt_ref, vb_ref, kb_ref, u_ref, w_ref):
    _dn_substitute(pl.program_id(1), lt_ref.at[0], vb_ref.at[0], kb_ref.at[0], u_ref.at[0], w_ref.at[0])


def _dn_substitute(rev, l_ref, vb_ref, kb_ref, u_ref, w_ref):
    c = CHUNK
    u_ref[...] = jnp.zeros_like(u_ref)
    w_ref[...] = jnp.zeros_like(w_ref)

    def outer(t, carry):
        i = t + rev * (c - 1 - 2 * t)

        def inner(jb, acc):
            au, aw = acc
            j0 = pl.multiple_of(jb * SOLVE_JB, SOLVE_JB)
            for r in range(SOLVE_JB):
                coef = l_ref[i, pl.ds(j0 + r, 1), :]
                au = au - coef * u_ref[j0 + r]
                aw = aw - coef * w_ref[j0 + r]
            return au, aw

        blk = lax.shift_right_logical(i, 3)
        lo = rev * blk
        hi = blk + 1 + rev * (c // SOLVE_JB - blk - 1)
        au, aw = lax.fori_loop(lo, hi, inner, (vb_ref[i], kb_ref[i]))
        u_ref[i] = au
        w_ref[i] = aw
        return carry

    lax.fori_loop(0, c, outer, 0)


def _dn_solve_packed_kernel(lt_ref, vb_ref, kb_ref, u_ref, w_ref, lp_s, vp_s, kp_s, up_s, wp_s):
    n_chunk = lt_ref.shape[-1]
    n_ib = CHUNK // SOLVE_IB
    s = pl.program_id(1)

    @pl.when(s < n_ib)
    def _():
        for ii in range(SOLVE_IB):
            for src, dst in ((lt_ref, lp_s), (vb_ref, vp_s), (kb_ref, kp_s)):
                dst[s * SOLVE_IB + ii] = jnp.concatenate(
                    [src[0, ii, h * HEAD_V:(h + 1) * HEAD_V, :] for h in range(N_HEADS)], axis=1)

    @pl.when(s == n_ib - 1)
    def _():
        _dn_substitute(pl.program_id(0), lp_s, vp_s, kp_s, up_s, wp_s)

    @pl.when(s >= n_ib)
    def _():
        for ii in range(SOLVE_IB):
            for src, dst in ((up_s, u_ref), (wp_s, w_ref)):
                row = src[(s - n_ib) * SOLVE_IB + ii]
                for h in range(N_HEADS):
                    dst[0, ii, h * HEAD_V:(h + 1) * HEAD_V, :] = row[:, h * n_chunk:(h + 1) * n_chunk]


def _dn_solve_call(lt, vbt, kbt):
    n_chunk = lt.shape[-1]
    shape = jax.ShapeDtypeStruct((2, CHUNK, BRANCH_W, n_chunk), F32)
    if n_chunk * N_HEADS == 128:
        n_ib = CHUNK // SOLVE_IB
        blk = (1, SOLVE_IB, BRANCH_W, n_chunk)
        spec = pl.BlockSpec(blk, lambda d, s: (d, jnp.minimum(s, n_ib - 1), 0, 0))
        ospec = pl.BlockSpec(blk, lambda d, s: (d, jnp.maximum(s - n_ib, 0), 0, 0))
        return pl.pallas_call(
            _dn_solve_packed_kernel,
            grid=(2, 2 * n_ib),
            in_specs=[spec, spec, spec],
            out_specs=[ospec, ospec],
            out_shape=[shape, shape],
            scratch_shapes=[pltpu.VMEM((CHUNK, HEAD_V, 128), F32)] * 5,
            compiler_params=_cparams(("arbitrary", "arbitrary")),
            name="deltanet_solve_packed",
        )(lt, vbt, kbt)
    spec = pl.BlockSpec((1, CHUNK, HEAD_V, n_chunk), lambda h, d: (d, 0, h, 0))
    return pl.pallas_call(
        _dn_solve_kernel,
        grid=(N_HEADS, 2),
        in_specs=[spec, spec, spec],
        out_specs=[spec, spec],
        out_shape=[shape, shape],
        compiler_params=_cparams(("arbitrary", "arbitrary")),
        name="deltanet_solve",
    )(lt, vbt, kbt)


def _chunks_to_lanes(a):
    return a.reshape(2, a.shape[1] // CHUNK, CHUNK, BRANCH_W).transpose(0, 2, 3, 1)


def _chunks_from_lanes(a):
    return a.transpose(0, 3, 1, 2).reshape(2, -1, BRANCH_W)


def _dn_build_kernel(x_ref, small_ref, cw_ref, alog_ref, dtb_ref,
                     tri_ref, strict_ref, eye_ref, blk_ref, exb_ref, exa_ref, hm_ref, ones_ref,
                     q_ref, k_ref, d_ref, l_ref, vb_ref, kb_ref, qkd_ref, v_ref, be_s, ge_s, *, t_len):
    n_rows = N_PAIR * t_len

    x = x_ref[...]
    row = lax.broadcasted_iota(jnp.int32, (n_rows, 1), 0) % t_len
    x_prev = jnp.where(row == 0, 0.0, pltpu.roll(x, 1, 0))
    x_next = jnp.where(row == t_len - 1, 0.0, pltpu.roll(x, n_rows - 1, 0))
    y = _silu(x_prev * cw_ref[0:1, :] + x * cw_ref[1:2, :] + x_next * cw_ref[2:3, :])
    cq, ck = y[:, 0:256], y[:, 256:512]
    v_ref[...] = y[:, 512:768]
    q_ref[...] = cq * lax.rsqrt(_sel_r(cq * cq, ones_ref[...], 2) + EPS) * (DN_DK ** -0.5)
    k_ref[...] = ck * lax.rsqrt(_sel_r(ck * ck, ones_ref[...], 2) + EPS)
    sm = small_ref[...]
    be_s[...] = _sel_r(_sigmoid(sm), exb_ref[...], 2)
    ge_s[...] = _sel_r(-jnp.exp(alog_ref[...]) * _softplus(sm + dtb_ref[...]), exa_ref[...], 2)

    def body(ci, carry):
        rows = pl.ds(pl.multiple_of(ci * CHUNK, CHUNK), CHUNK)
        qc, kc, vc = q_ref[rows, :], k_ref[rows, :], v_ref[rows, :]
        kst = _stack_heads(kc, hm_ref)
        kk = _mm_nt(kc, kst)
        qk = _mm_nt(qc, kst)
        for d in range(2):
            bexp = be_s[rows, d * BRANCH_W:(d + 1) * BRANCH_W]
            dexp = _sel_l(tri_ref[d], ge_s[rows, d * BRANCH_W:(d + 1) * BRANCH_W], 2)
            drow = _sel_l(blk_ref[...], dexp * eye_ref[...])
            dec_s = jnp.exp(jnp.where(strict_ref[d] > 0.5, dexp - drow, -1e30))
            d_ref[d, rows, :] = dexp
            l_ref[d, rows, :] = bexp * kk * dec_s
            vb_ref[d, rows, :] = vc * bexp
            kb_ref[d, rows, :] = kc * bexp * jnp.exp(dexp)
            qkd_ref[d, rows, :] = (qk * (dec_s + eye_ref[...])).astype(BF)
        return carry

    lax.fori_loop(0, n_rows // CHUNK, body, 0)


def _dn_build_call(proj, conv_w, alog_row, dtb_row, t_len, n_seq, row_blk0):
    tri, strict, eye, blk, exb, exa = _dn_consts()
    tri, blk, exb, exa = (jnp.asarray(a, BF) for a in (tri, blk, exb, exa))
    strict, eye = jnp.asarray(strict, F32), jnp.asarray(eye, F32)
    hm, _, _, ones_bd = _head_consts(BRANCH_W)
    hm, ones_bd = jnp.asarray(hm, F32), jnp.asarray(ones_bd, BF)
    n_rows = N_PAIR * t_len
    n_tok = n_seq * t_len
    consts = (conv_w, alog_row, dtb_row, tri, strict, eye, blk, exb, exa, hm, ones_bd)
    in_specs = [pl.BlockSpec((n_rows, 768), lambda b: (row_blk0 + b, 1)),
                pl.BlockSpec((n_rows, 128), lambda b: (row_blk0 + b, COL_SMALL_BLOCK))]
    in_specs += [_const_spec(a) for a in consts]
    tok_spec = pl.BlockSpec((n_rows, BRANCH_W), lambda b: (b, 0))
    dir_spec = pl.BlockSpec((2, n_rows, BRANCH_W), lambda b: (0, b, 0))
    tok = jax.ShapeDtypeStruct((n_tok, BRANCH_W), F32)
    per_dir = jax.ShapeDtypeStruct((2, n_tok, BRANCH_W), F32)
    return pl.pallas_call(
        functools.partial(_dn_build_kernel, t_len=t_len),
        grid=(n_seq // N_PAIR,),
        in_specs=in_specs,
        out_specs=[tok_spec] * 2 + [dir_spec] * 5,
        out_shape=[tok] * 2 + [per_dir] * 4 + [jax.ShapeDtypeStruct((2, n_tok, BRANCH_W), BF)],
        scratch_shapes=[pltpu.VMEM((n_rows, BRANCH_W), F32)] + [pltpu.VMEM((n_rows, 2 * BRANCH_W), F32)] * 2,
        compiler_params=_cparams(("arbitrary",)),
        name="deltanet_build",
    )(proj, proj, *consts)


def _dn_scan_kernel(*refs, t_len, has_state, n_prev):
    (q_ref, k_ref, d_ref, u_ref, w_ref, qkd_ref, gate_ref, nw_ref, hv_ref, bd_ref, ones_ref) = refs[:11]
    pos = 11
    if has_state:
        s0_ref = refs[pos]
        pos += 1
    pos += n_prev
    out_ref = refs[pos]
    pos += 1
    if not has_state:
        st_ref = refs[pos]
        pos += 1
    of_s, ob_s = refs[pos:pos + 2]
    c = CHUNK
    n_chunks = t_len // c

    def step(s, rows, d, o_s):
        qc, kc = q_ref[rows, :], k_ref[rows, :]
        dexp = d_ref[d, rows, :]
        v_new = u_ref[d, rows, :] - _mm(w_ref[d, rows, :], s)
        o_s[rows, :] = _mm(qc * jnp.exp(dexp), s) + _mm(qkd_ref[d, rows, :], _stack_heads(v_new, hv_ref))
        dl = dexp[0:1, :] if d == 1 else dexp[c - 1:c, :]
        return s * jnp.exp(dl) + bd_ref[...] * _mm_tn(kc * jnp.exp(dl - dexp), v_new)

    def body(t, carry):
        new = []
        for sq in range(N_PAIR):
            rf = pl.ds(pl.multiple_of(sq * t_len + t * c, c), c)
            rb = pl.ds(pl.multiple_of(sq * t_len + (n_chunks - 1 - t) * c, c), c)
            new.append(step(carry[2 * sq], rf, 0, of_s))
            new.append(step(carry[2 * sq + 1], rb, 1, ob_s))
        return tuple(new)

    if has_state:
        init = tuple(s0_ref[sq, d] for sq in range(N_PAIR) for d in range(2))
    else:
        init = tuple(jnp.zeros((BRANCH_W, BRANCH_W), F32) for _ in range(2 * N_PAIR))
    finals = lax.fori_loop(0, n_chunks, body, init)
    out_ref[...] = _head_norm_gate(of_s[...] + ob_s[...], gate_ref[...], nw_ref[...], ones_ref[...])
    if not has_state:
        _zero_later_layers(st_ref)
        for sq in range(N_PAIR):
            for d in range(2):
                for h in range(N_HEADS):
                    st_ref[sq, 0, d, h] = _head_lanes(finals[2 * sq + d][h * DN_DK:(h + 1) * DN_DK, :], h)


def _dn_scan_call(proj, q, k, dd, u, w, qkd, nw, s0, t_len, n_seq, row_blk0, layer=0, prev=None):
    _, hv, bd, ones_bd = _head_consts(BRANCH_W)
    hv, bd, ones_bd = jnp.asarray(hv, F32), jnp.asarray(bd, F32), jnp.asarray(ones_bd, BF)
    has_state = s0 is not None
    n_rows = N_PAIR * t_len
    consts = (nw, hv, bd, ones_bd)
    tok_spec = pl.BlockSpec((n_rows, BRANCH_W), lambda b: (b, 0))
    dir_spec = pl.BlockSpec((2, n_rows, BRANCH_W), lambda b: (0, b, 0))
    in_specs = [tok_spec, tok_spec, dir_spec, dir_spec, dir_spec, dir_spec,
                pl.BlockSpec((n_rows, 256), lambda b: (row_blk0 + b, 6))]
    in_specs += [_const_spec(a) for a in consts]
    args = [q, k, dd, u, w, qkd, proj, *consts]
    if has_state:
        in_specs.append(pl.BlockSpec((N_PAIR, 2, BRANCH_W, BRANCH_W), lambda b: (b, 0, 0, 0)))
        args.append(s0)
    out_shape = [jax.ShapeDtypeStruct((n_seq * t_len, BRANCH_W), F32)]
    out_specs = [pl.BlockSpec((n_rows, BRANCH_W), lambda b: (b, 0))]
    aliases = {}
    if not has_state:
        aliases = _add_layer_outputs([(2, N_HEADS, DN_DK, HEAD_V)], n_seq, N_PAIR, layer, prev,
                                     in_specs, args, out_shape, out_specs)
    return pl.pallas_call(
        functools.partial(_dn_scan_kernel, t_len=t_len, has_state=has_state, n_prev=len(aliases)),
        grid=(n_seq // N_PAIR,),
        in_specs=in_specs,
        out_specs=out_specs,
        out_shape=out_shape,
        input_output_aliases=aliases,
        scratch_shapes=[pltpu.VMEM((n_rows, BRANCH_W), F32)] * 2,
        compiler_params=_cparams(("arbitrary",)),
        name="deltanet_scan",
    )(*args)


def _rope(x, cos, sin):
    lane = lax.broadcasted_iota(jnp.int32, x.shape, 1) % 16
    n = x.shape[1]
    xrot = jnp.where(lane < 8, -pltpu.roll(x, n - 8, 1), pltpu.roll(x, 8, 1))
    return x * cos + xrot * sin


def _att_kernel(*refs, t_len, lat, lam_init, n_prev):
    blk_ref, lam_ref, nw_ref, qm_ref, hv_ref, ones_ref = refs[:6]
    pos = 6
    if lat:
        cos_ref, sin_ref, ck_ref, cv_ref = refs[pos:pos + 4]
        pos += 4
    pos += n_prev
    out_ref = refs[pos]
    pos += 1
    if not lat:
        nk_ref, nv_ref = refs[pos:pos + 2]
        pos += 2
    if lat:
        q_s, k_s, v_s = refs[pos:pos + 3]

    lv = lam_ref[...]
    lam = (jnp.exp(jnp.sum(lv[0:1] * lv[1:2], axis=1, keepdims=True))
           - jnp.exp(jnp.sum(lv[2:3] * lv[3:4], axis=1, keepdims=True)) + lam_init)
    q = blk_ref[:, 0:256]
    k = blk_ref[:, 256:512]
    v = blk_ref[:, 512:768]
    if lat:
        cos, sin = cos_ref[...], sin_ref[...]
        q_s[...] = _rope(q, cos, sin)
        k_s[0:PAST_LEN, :] = ck_ref[0]
        k_s[PAST_LEN:PAST_LEN + t_len, :] = _rope(k, cos, sin)
        v_s[0:PAST_LEN, :] = cv_ref[0]
        v_s[PAST_LEN:PAST_LEN + t_len, :] = v
        keys = k_s[...].astype(BF)
        vals = v_s[...].astype(BF)
    else:
        keys = k.astype(BF)
        vals = v.astype(BF)
        _zero_later_layers(nk_ref)
        _zero_later_layers(nv_ref)
        for h in range(N_HEADS):
            nk_ref[0, 0, h] = _head_lanes(k, h)
            nv_ref[0, 0, h] = _head_lanes(v, h)
    tq = TQ_ATT
    scale = DF_DH ** -0.5
    for qi in range(t_len // tq):
        qt = q_s[qi * tq:(qi + 1) * tq, :] if lat else q[qi * tq:(qi + 1) * tq, :]
        qs = jnp.concatenate([qt * qm_ref[r:r + 1, :] for r in range(2 * N_HEADS)], axis=0)
        s = _mm_nt(qs, keys) * scale
        s = jnp.exp(s - jnp.max(s, axis=1, keepdims=True))
        p = s / jnp.sum(s, axis=1, keepdims=True)
        a = jnp.concatenate(
            [p[(2 * h) * tq:(2 * h + 1) * tq] - lam * p[(2 * h + 1) * tq:(2 * h + 2) * tq]
             for h in range(N_HEADS)], axis=0)
        o = _head_diag(_mm(a, vals), hv_ref, tq)
        ms = _sel_r(o * o, ones_ref[...], 2) * (1.0 / HEAD_V)
        out_ref[qi * tq:(qi + 1) * tq, :] = o * lax.rsqrt(ms + EPS) * nw_ref[...] * (1.0 - lam_init)


def _att_call(proj, lam_p, nw, cache_k, cache_v, rope, t_len, n_seq, row_blk0, lam_init, layer=0, prev=None):
    qm = jnp.asarray(_att_consts(), F32)
    _, hv, _, ones_bd = _head_consts(BRANCH_W)
    hv, ones_bd = jnp.asarray(hv, F32), jnp.asarray(ones_bd, BF)
    lat = cache_k is not None
    consts = (lam_p, nw, qm, hv, ones_bd)
    in_specs = [pl.BlockSpec((t_len, 768), lambda b: (row_blk0 + b, 4))]
    in_specs += [_const_spec(a) for a in consts]
    args = [proj, *consts]
    out_shape = [jax.ShapeDtypeStruct((n_seq * t_len, BRANCH_W), F32)]
    out_specs = [pl.BlockSpec((t_len, BRANCH_W), lambda b: (b, 0))]
    scratch = []
    if lat:
        cos, sin = rope
        in_specs += [_const_spec(cos), _const_spec(sin),
                     pl.BlockSpec((1, PAST_LEN, BRANCH_W), lambda b: (b, 0, 0)),
                     pl.BlockSpec((1, PAST_LEN, BRANCH_W), lambda b: (b, 0, 0))]
        args += [cos, sin, cache_k, cache_v]
        scratch = [pltpu.VMEM((t_len, BRANCH_W), F32),
                   pltpu.VMEM((PAST_LEN + t_len, BRANCH_W), F32),
                   pltpu.VMEM((PAST_LEN + t_len, BRANCH_W), F32)]
    aliases = {}
    if not lat:
        aliases = _add_layer_outputs([(N_HEADS, t_len, HEAD_V)] * 2, n_seq, 1, layer, prev,
                                     in_specs, args, out_shape, out_specs)
    return pl.pallas_call(
        functools.partial(_att_kernel, t_len=t_len, lat=lat, lam_init=lam_init, n_prev=len(aliases)),
        grid=(n_seq,),
        in_specs=in_specs,
        out_specs=out_specs,
        out_shape=out_shape,
        input_output_aliases=aliases,
        scratch_shapes=scratch,
        compiler_params=_cparams(("arbitrary",)),
        name="diff_attention",
    )(*args)


def _block_diag_states(st, dk, transposed):
    eye = jnp.eye(N_HEADS, dtype=st.dtype)
    b = st.shape[0]
    if transposed:
        return jnp.einsum('bnhde,hg->bnhegd', st, eye).reshape(b, 2, N_HEADS * HEAD_V, N_HEADS * dk)
    return jnp.einsum('bnhde,hg->bnhdge', st, eye).reshape(b, 2, N_HEADS * dk, N_HEADS * HEAD_V)


def _in_perm():
    offs = np.concatenate([[0], np.cumsum(IN_ORIG)])
    seg = lambda a, b: np.arange(offs[a], offs[b])
    return np.concatenate([seg(0, 4), seg(5, 6), seg(8, 9), seg(9, 16), seg(4, 5), seg(6, 8)])


def kernel(x_prompt, x_sample, cache_diff_k, cache_diff_v, state_gla, state_dn, state_hgrn, c, c_ctx,
           norm_w, w_mod, b_mod, ffn1_in, ffn1_down, ffn2_in, ffn2_down, w_in, gla_w2, gla_b, gla_norm,
           dn_conv, dn_a_log, dn_dt_bias, dn_norm, hg_lb_logits, hg_norm, diff_lambda, diff_norm,
           w_branch, w_mgate, w_out, final_norm):
    xs = (x_prompt.reshape(N_CTX_TOK, D_MODEL), x_sample.reshape(-1, D_MODEL))
    c_rows = jnp.concatenate([c_ctx[None, :], c, jnp.zeros((8 - 1 - N_LAT_SEQ, D_MODEL), F32)], axis=0)
    mod = _mod_call(c_rows, w_mod, b_mod).reshape(DEPTH, 8, N_MOD, D_MODEL)
    rope = _rope_tables()
    perm = _in_perm()
    lb_logits = hg_lb_logits.reshape(DEPTH, 2 * HG_W)
    lat_blk = N_CTX_TOK // T_LAT
    tile4 = lambda a: jnp.tile(a, N_HEADS)[None, :]
    fin = final_norm[None, :]
    st_a = st_b = st_c = new_kv = None
    prev1 = lambda a: None if a is None else (a,)
    for l in range(DEPTH):
        mod_l = mod[l]
        lam_init = 0.8 - 0.6 * math.exp(-0.3 * l)
        w_in_p = jnp.pad(w_in[l][:, perm], ((0, 0), (0, N_IN_PAD - N_IN))).astype(BF)
        w2bd = jnp.zeros((128, 2 * GLA_KW), F32)
        w2bd = w2bd.at[0:GLA_LOWRANK, 0:GLA_KW].set(gla_w2[l, 0])
        w2bd = w2bd.at[GLA_LOWRANK:2 * GLA_LOWRANK, GLA_KW:].set(gla_w2[l, 1]).astype(BF)
        gbias = gla_b[l].reshape(1, 2 * GLA_KW)
        alog_row = jnp.zeros((1, 128), F32).at[0, SMALL_DNA:SMALL_DNA + 8].set(dn_a_log[l].reshape(-1))
        dtb_row = jnp.zeros((1, 128), F32).at[0, SMALL_DNA:SMALL_DNA + 8].set(dn_dt_bias[l].reshape(-1))

        (x,) = _ffn_call(xs, mod_l, norm_w[l, 0][None, :], ffn1_in[l].astype(BF), ffn1_down[l].astype(BF),
                         fin, 0, False, False)
        proj = _proj_call(x, mod_l, norm_w[l, 1][None, :], w_in_p)

        a_c, st_a = _gla_call(proj, w2bd, gbias, tile4(gla_norm[l]), None, T_CTX, N_CTX_SEQ, 0, l, prev1(st_a))
        dn_lat_blk = N_CTX_TOK // (N_PAIR * T_LAT)
        q_c, k_c, dd_c, lw_c, vb_c, kb_c, qkd_c = _dn_build_call(proj, dn_conv[l], alog_row, dtb_row,
                                                                T_CTX, N_CTX_SEQ, 0)
        q_l, k_l, dd_l, lw_l, vb_l, kb_l, qkd_l = _dn_build_call(proj, dn_conv[l], alog_row, dtb_row,
                                                                T_LAT, N_LAT_SEQ, dn_lat_blk)
        u_c, w_c = (_chunks_from_lanes(a) for a in _dn_solve_call(
            _chunks_to_lanes(lw_c), _chunks_to_lanes(vb_c), _chunks_to_lanes(kb_c)))
        u_l, w_l = (_chunks_from_lanes(a) for a in _dn_solve_call(
            _chunks_to_lanes(lw_l), _chunks_to_lanes(vb_l), _chunks_to_lanes(kb_l)))
        b_c, st_b = _dn_scan_call(proj, q_c, k_c, dd_c, u_c, w_c, qkd_c, tile4(dn_norm[l]), None,
                                  T_CTX, N_CTX_SEQ, 0, l, prev1(st_b))
        c_c, st_c = _hgrn_call(proj, lb_logits, tile4(hg_norm[l]), None, T_CTX, N_CTX_SEQ, 0, l, prev1(st_c))
        d_c, *new_kv = _att_call(proj, diff_lambda[l], tile4(diff_norm[l]), None, None, None,
                                 T_CTX, N_CTX_SEQ, 0, lam_init, l, new_kv)
        ck = cache_diff_k[:, l].transpose(0, 2, 1, 3).reshape(N_LAT_SEQ, PAST_LEN, BRANCH_W)
        cv = cache_diff_v[:, l].transpose(0, 2, 1, 3).reshape(N_LAT_SEQ, PAST_LEN, BRANCH_W)
        (a_l,) = _gla_call(proj, w2bd, gbias, tile4(gla_norm[l]),
                           _block_diag_states(state_gla[:, l], GLA_DK, True), T_LAT, N_LAT_SEQ, dn_lat_blk)
        (b_l,) = _dn_scan_call(proj, q_l, k_l, dd_l, u_l, w_l, qkd_l, tile4(dn_norm[l]),
                               _block_diag_states(state_dn[:, l], DN_DK, False), T_LAT, N_LAT_SEQ, dn_lat_blk)
        (c_l,) = _hgrn_call(proj, lb_logits, tile4(hg_norm[l]),
                            _block_diag_states(state_hgrn[:, l], HG_DK, True), T_LAT, N_LAT_SEQ, dn_lat_blk, l)
        (d_l,) = _att_call(proj, diff_lambda[l], tile4(diff_norm[l]), ck, cv, rope,
                           T_LAT, N_LAT_SEQ, lat_blk, lam_init)

        x = _merge_call(x, mod_l, norm_w[l, 1][None, :], (a_c, b_c, c_c, d_c), (a_l, b_l, c_l, d_l),
                        w_mgate[l].astype(BF), w_branch[l].astype(BF), w_out[l].astype(BF))
        xs = _ffn_call((x,), mod_l, norm_w[l, 2][None, :], ffn2_in[l].astype(BF), ffn2_down[l].astype(BF),
                       fin, 2, l == DEPTH - 1, l == DEPTH - 1)
    y_prompt = xs[0].reshape(N_CTX_SEQ, T_CTX, D_MODEL)
    y_sample = xs[1].reshape(N_LAT_SEQ, T_LAT, D_MODEL)
    return (y_prompt, y_sample, new_kv[0], new_kv[1], st_a, st_b, st_c)
```

```python
import functools
import math

import numpy as np
import jax
import jax.numpy as jnp
from jax import lax
from jax.experimental import pallas as pl
from jax.experimental.pallas import tpu as pltpu

F32 = jnp.float32
BF = jnp.bfloat16

D_MODEL = 1024
N_CTX_SEQ = 32
T_CTX = 256
DEPTH = 2
N_LAT_SEQ = 2
T_LAT = 1024
PAST_LEN = 512
GRID_W = 64
N_HEADS = 4
BRANCH_W = 256
HEAD_V = 64
GLA_DK = 32
GLA_KW = 128
GLA_LOWRANK = 16
GLA_TAU = 16.0
DN_DK = 64
HG_DK = 64
HG_W = 256
DF_DH = 32
ROPE_BASE = 10000.0
D_FF = 2816
N_MOD = 9
CHUNK = 64
EPS = 1e-6
N_CTX_TOK = N_CTX_SEQ * T_CTX
N_TOK = N_CTX_TOK + N_LAT_SEQ * T_LAT
N_LEVELS = 6
SMALL_LEVEL = 4
N_SCAN_CONSTS = 13

IN_ORIG = (128, 128, 256, 256, 32, 768, 8, 8, 256, 256, 512, 256, 256, 256, 256, 256)
N_IN = sum(IN_ORIG)
N_IN_PAD = 3968
COL_SMALL_BLOCK = 30
SMALL_LR = 0
SMALL_DNB = 32
SMALL_DNA = 40

VMEM_LIMIT = 56 * 1024 * 1024

TM_FFN = 512
TF_FFN = 1408
TM_PROJ = 512
TM_MERGE = 512
TN_MOD = 2304
TQ_ATT = 256


def _silu(x):
    return x * (1.0 / (1.0 + jnp.exp(-x)))


def _sigmoid(x):
    return 1.0 / (1.0 + jnp.exp(-x))


def _softplus(x):
    return jnp.maximum(x, 0.0) + jnp.log(1.0 + jnp.exp(-jnp.abs(x)))


def _log_sigmoid(x):
    return -_softplus(-x)


def _mm(a, b):
    return jnp.dot(a.astype(BF), b.astype(BF), preferred_element_type=F32)


def _mm_nt(a, b):
    return lax.dot_general(a.astype(BF), b.astype(BF), (((1,), (1,)), ((), ())),
                           preferred_element_type=F32)


def _mm_tn(a, b):
    return lax.dot_general(a.astype(BF), b.astype(BF), (((0,), (0,)), ((), ())),
                           preferred_element_type=F32)


def _split(x, n):
    parts = []
    r = x
    for i in range(n):
        p = r.astype(BF)
        parts.append(p)
        if i + 1 < n:
            r = r - p.astype(F32)
    return parts


def _sel_l(m01, x, n=3):
    out = None
    for p in _split(x, n):
        t = jnp.dot(m01, p, preferred_element_type=F32)
        out = t if out is None else out + t
    return out


def _sel_r(x, m01, n=3):
    out = None
    for p in _split(x, n):
        t = jnp.dot(p, m01, preferred_element_type=F32)
        out = t if out is None else out + t
    return out


def _sel_tn(x, m01, n=3):
    out = None
    for p in _split(x, n):
        t = lax.dot_general(p, m01, (((0,), (0,)), ((), ())), preferred_element_type=F32)
        out = t if out is None else out + t
    return out


def _rms(x, w):
    return x * lax.rsqrt(jnp.mean(x * x, axis=-1, keepdims=True) + EPS) * w


def _head_lanes(x, h):
    blk = x[:, (h // 2) * 128:(h // 2 + 1) * 128]
    if h % 2:
        blk = pltpu.roll(blk, 64, 1)
    return blk[:, :HEAD_V]


def _stack_heads(x, hm_ref):
    return jnp.concatenate([x * hm_ref[h:h + 1, :] for h in range(N_HEADS)], axis=0)


def _stack_heads_bf(x, hm_ref):
    xb = x.astype(BF)
    return jnp.concatenate([xb * hm_ref[h:h + 1, :].astype(BF) for h in range(N_HEADS)], axis=0)


def _head_diag(o_full, hv_ref, c):
    out = None
    for h in range(N_HEADS):
        t = o_full[h * c:(h + 1) * c, :] * hv_ref[h:h + 1, :]
        out = t if out is None else out + t
    return out


def _head_norm_gate(o, gate, nw, ones_bd):
    ms = _sel_r(o * o, ones_bd, 2) * (1.0 / HEAD_V)
    return o * lax.rsqrt(ms + EPS) * nw * _silu(gate)


def _mod_row(i, tm):
    return jnp.maximum(i * tm - (N_CTX_TOK - T_LAT), 0) // T_LAT


def _cparams(sem):
    return pltpu.CompilerParams(dimension_semantics=sem, vmem_limit_bytes=VMEM_LIMIT)


@functools.lru_cache(maxsize=None)
def _scan_consts(w, reverse):
    c = CHUNK
    idx = np.arange(c)
    i = idx[:, None]
    m = idx[None, :]
    pm, sg = [], []
    s = c // 2
    while s >= 1:
        par = idx // (2 * s)
        right = (idx % (2 * s)) >= s
        same = par[:, None] == par[None, :]
        query = ~right if reverse else right
        pm.append(same & query[:, None] & (~query)[None, :])
        if s >= SMALL_LEVEL:
            sg.append(np.where(query, 1.0, -1.0))
        s //= 2
    pm.append(i == m)
    tri = (m >= i) if reverse else (m <= i)
    r4, r2 = idx % 4, idx % 2
    if not reverse:
        cf = [r2 == 1, r4 >= 2, r4 == 3, r4 == 0]
    else:
        cf = [r2 == 0, r4 <= 1, r4 == 3, r4 == 0]
    wide = lambda rows: np.repeat(np.stack(rows).astype(np.float32)[:, :, None], w, axis=2)
    pmask = np.stack([np.tile(p, (1, N_HEADS)) for p in pm]).astype(np.float32)
    return tri.astype(np.float32), wide(sg), wide(cf), pmask


@functools.lru_cache(maxsize=None)
def _head_consts(w):
    dk = w // N_HEADS
    hm = np.zeros((N_HEADS, w), np.float32)
    hv = np.zeros((N_HEADS, BRANCH_W), np.float32)
    for h in range(N_HEADS):
        hm[h, h * dk:(h + 1) * dk] = 1.0
        hv[h, h * HEAD_V:(h + 1) * HEAD_V] = 1.0
    bd = hm.T @ hv
    ones_bd = hv.T @ hv
    return hm, hv, bd, ones_bd


@functools.lru_cache(maxsize=None)
def _dn_consts():
    c = CHUNK
    idx = np.arange(c)
    i = idx[:, None]
    j = idx[None, :]
    tri = np.stack([(j <= i), (j >= i)]).astype(np.float32)
    strict = np.stack([np.tile(j < i, (1, N_HEADS)), np.tile(j > i, (1, N_HEADS))]).astype(np.float32)
    eye = np.tile(np.eye(c), (1, N_HEADS)).astype(np.float32)
    blk = np.ones((c, c), np.float32)
    exb = np.zeros((128, 2 * BRANCH_W), np.float32)
    exa = np.zeros((128, 2 * BRANCH_W), np.float32)
    for n in range(2):
        for h in range(N_HEADS):
            lo = n * BRANCH_W + h * HEAD_V
            exb[SMALL_DNB + n * N_HEADS + h, lo:lo + HEAD_V] = 1.0
            exa[SMALL_DNA + n * N_HEADS + h, lo:lo + HEAD_V] = 1.0
    return tri, strict, eye, blk, exb, exa


@functools.lru_cache(maxsize=None)
def _att_consts():
    qm = np.zeros((2 * N_HEADS, BRANCH_W), np.float32)
    for h in range(N_HEADS):
        for mp in range(2):
            lo = h * HEAD_V + mp * DF_DH
            qm[2 * h + mp, lo:lo + DF_DH] = 1.0
    return qm


def _rope_tables():
    rows = T_LAT // GRID_W
    row = jnp.repeat(jnp.arange(rows), GRID_W).astype(F32)
    col = jnp.tile(jnp.arange(GRID_W), rows).astype(F32)
    half = DF_DH // 2
    inv = ROPE_BASE ** (-jnp.arange(0, half, 2, dtype=F32) / half)

    def angles(pos):
        a = pos[:, None] * inv[None, :]
        return jnp.concatenate([a, a], axis=-1)

    ang = jnp.concatenate([angles(row), angles(col)], axis=-1)
    reps = BRANCH_W // DF_DH
    return jnp.tile(jnp.cos(ang), (1, reps)), jnp.tile(jnp.sin(ang), (1, reps))


def _mod_kernel(c_ref, w_ref, b_ref, o_ref):
    a = _silu(c_ref[...])
    w = w_ref[0]
    out = None
    for ap in _split(a, 2):
        for wp in _split(w, 2):
            t = jnp.dot(ap, wp, preferred_element_type=F32)
            out = t if out is None else out + t
    o_ref[0] = out + b_ref[0]


def _mod_call(c_rows, w_mod, b_mod):
    n_t = (N_MOD * D_MODEL) // TN_MOD
    return pl.pallas_call(
        _mod_kernel,
        grid=(DEPTH, n_t),
        in_specs=[pl.BlockSpec((8, D_MODEL), lambda l, j: (0, 0)),
                  pl.BlockSpec((1, D_MODEL, TN_MOD), lambda l, j: (l, 0, j)),
                  pl.BlockSpec((1, 1, TN_MOD), lambda l, j: (l, 0, j))],
        out_specs=pl.BlockSpec((1, 8, TN_MOD), lambda l, j: (l, 0, j)),
        out_shape=jax.ShapeDtypeStruct((DEPTH, 8, N_MOD * D_MODEL), F32),
        compiler_params=_cparams(("arbitrary", "arbitrary")),
        name="mod_vectors",
    )(c_rows, w_mod, b_mod.reshape(DEPTH, 1, N_MOD * D_MODEL))


def _ffn_kernel(*refs, sub, final, split_in, split_out):
    n_x = 2 if split_in else 1
    mod_ref, nw_ref, wup_ref, wd_ref, fn_ref = refs[n_x:n_x + 5]
    outs = refs[n_x + 5:]
    is_lat = pl.program_id(0) >= N_CTX_TOK // TM_FFN
    x = jnp.where(is_lat, refs[1][...], refs[0][...]) if split_in else refs[0][...]
    sh = mod_ref[0, 3 * sub:3 * sub + 1, :]
    sc = mod_ref[0, 3 * sub + 1:3 * sub + 2, :]
    ga = mod_ref[0, 3 * sub + 2:3 * sub + 3, :]
    h = (_rms(x, nw_ref[...]) * (1.0 + sc) + sh).astype(BF)
    acc = None
    for f in range(D_FF // TF_FFN):
        lo = f * TF_FFN
        g = jnp.dot(h, wup_ref[:, lo:lo + TF_FFN], preferred_element_type=F32)
        u = jnp.dot(h, wup_ref[:, D_FF + lo:D_FF + lo + TF_FFN], preferred_element_type=F32)
        t = jnp.dot((_silu(g) * u).astype(BF), wd_ref[lo:lo + TF_FFN, :], preferred_element_type=F32)
        acc = t if acc is None else acc + t
    y = x + 0.5 * ga * acc
    if final:
        y = _rms(y, fn_ref[...])
    if split_out:
        @pl.when(jnp.logical_not(is_lat))
        def _():
            outs[0][...] = y

        @pl.when(is_lat)
        def _():
            outs[1][...] = y
    else:
        outs[0][...] = y


def _ffn_call(xs, mod_l, nw, w_in, w_down, final_w, sub, final, split_out):
    tm = TM_FFN
    n_ctx = N_CTX_TOK // tm
    ctx_map = lambda i: (jnp.minimum(i, n_ctx - 1), 0)
    lat_map = lambda i: (jnp.maximum(i - n_ctx, 0), 0)
    tile = lambda m: pl.BlockSpec((tm, D_MODEL), m)
    resident = lambda a: pl.BlockSpec(a.shape, lambda i: (0, 0), pipeline_mode=pl.Buffered(1))
    split_in = len(xs) == 2
    in_specs = [tile(ctx_map), tile(lat_map)] if split_in else [tile(lambda i: (i, 0))]
    in_specs += [pl.BlockSpec((1, N_MOD, D_MODEL), lambda i: (_mod_row(i, tm), 0, 0)),
                 pl.BlockSpec((1, D_MODEL), lambda i: (0, 0)),
                 resident(w_in), resident(w_down),
                 pl.BlockSpec((1, D_MODEL), lambda i: (0, 0))]
    if split_out:
        out_specs = [tile(ctx_map), tile(lat_map)]
        out_shape = [jax.ShapeDtypeStruct((N_CTX_TOK, D_MODEL), F32),
                     jax.ShapeDtypeStruct((N_TOK - N_CTX_TOK, D_MODEL), F32)]
    else:
        out_specs = [tile(lambda i: (i, 0))]
        out_shape = [jax.ShapeDtypeStruct((N_TOK, D_MODEL), F32)]
    return pl.pallas_call(
        functools.partial(_ffn_kernel, sub=sub, final=final, split_in=split_in, split_out=split_out),
        grid=(N_TOK // tm,),
        in_specs=in_specs,
        out_specs=out_specs,
        out_shape=out_shape,
        compiler_params=_cparams(("arbitrary",)),
        name="swiglu_half_step",
    )(*xs, mod_l, nw, w_in, w_down, final_w)


def _proj_kernel(x_ref, mod_ref, nw_ref, w_ref, o_ref):
    sh = mod_ref[0, 3:4, :]
    sc = mod_ref[0, 4:5, :]
    h = (_rms(x_ref[...], nw_ref[...]) * (1.0 + sc) + sh).astype(BF)
    o_ref[...] = jnp.dot(h, w_ref[...], preferred_element_type=F32)


def _proj_call(x, mod_l, nw, w_in_p):
    tm = TM_PROJ
    return pl.pallas_call(
        _proj_kernel,
        grid=(N_TOK // tm,),
        in_specs=[pl.BlockSpec((tm, D_MODEL), lambda i: (i, 0)),
                  pl.BlockSpec((1, N_MOD, D_MODEL), lambda i: (_mod_row(i, tm), 0, 0)),
                  pl.BlockSpec((1, D_MODEL), lambda i: (0, 0)),
                  pl.BlockSpec((D_MODEL, N_IN_PAD), lambda i: (0, 0))],
        out_specs=pl.BlockSpec((tm, N_IN_PAD), lambda i: (i, 0)),
        out_shape=jax.ShapeDtypeStruct((N_TOK, N_IN_PAD), F32),
        compiler_params=_cparams(("arbitrary",)),
        name="mixer_in_proj",
    )(x, mod_l, nw, w_in_p)


def _merge_kernel(x_ref, mod_ref, nw_ref, *rest):
    ctx_refs, lat_refs = rest[0:4], rest[4:8]
    wg_ref, wb_ref, wo_ref, o_ref = rest[8:12]
    x = x_ref[...]
    sh = mod_ref[0, 3:4, :]
    sc = mod_ref[0, 4:5, :]
    ga = mod_ref[0, 5:6, :]
    h = (_rms(x, nw_ref[...]) * (1.0 + sc) + sh).astype(BF)
    is_lat = pl.program_id(0) >= N_CTX_TOK // TM_MERGE
    mixed = None
    for n in range(4):
        gate = _sigmoid(jnp.dot(h, wg_ref[:, n * D_MODEL:(n + 1) * D_MODEL], preferred_element_type=F32))
        br = jnp.where(is_lat, lat_refs[n][...], ctx_refs[n][...])
        up = jnp.dot(br.astype(BF), wb_ref[n], preferred_element_type=F32)
        mixed = gate * up if mixed is None else mixed + gate * up
    out = jnp.dot(mixed.astype(BF), wo_ref[...], preferred_element_type=F32)
    o_ref[...] = x + ga * out


def _merge_call(x, mod_l, nw, ctx_branches, lat_branches, w_mgate, w_branch, w_out):
    tm = TM_MERGE
    n_ctx = N_CTX_TOK // tm
    cspec = pl.BlockSpec((tm, BRANCH_W), lambda i: (jnp.minimum(i, n_ctx - 1), 0))
    lspec = pl.BlockSpec((tm, BRANCH_W), lambda i: (jnp.maximum(i - n_ctx, 0), 0))
    return pl.pallas_call(
        _merge_kernel,
        grid=(N_TOK // tm,),
        in_specs=[pl.BlockSpec((tm, D_MODEL), lambda i: (i, 0)),
                  pl.BlockSpec((1, N_MOD, D_MODEL), lambda i: (_mod_row(i, tm), 0, 0)),
                  pl.BlockSpec((1, D_MODEL), lambda i: (0, 0)),
                  cspec, cspec, cspec, cspec, lspec, lspec, lspec, lspec,
                  pl.BlockSpec((D_MODEL, 4 * D_MODEL), lambda i: (0, 0)),
                  pl.BlockSpec((4, BRANCH_W, D_MODEL), lambda i: (0, 0, 0)),
                  pl.BlockSpec((D_MODEL, D_MODEL), lambda i: (0, 0))],
        out_specs=pl.BlockSpec((tm, D_MODEL), lambda i: (i, 0)),
        out_shape=jax.ShapeDtypeStruct((N_TOK, D_MODEL), F32),
        compiler_params=_cparams(("arbitrary",)),
        name="gated_merge",
    )(x, mod_l, nw, *ctx_branches, *lat_branches, w_mgate, w_branch, w_out)


def _gated_chunk(qc, kc, vc, gc, st, tri_ref, sg_ref, cf_ref, pm_ref, hm_ref, hv_ref, bdt_ref, reverse):
    c = CHUNK
    w = gc.shape[1]
    cum = _sel_l(tri_ref[...], gc, 2)
    kst = _stack_heads_bf(kc, hm_ref)
    qb = qc.astype(BF)
    a = pm_ref[N_LEVELS] * _mm_nt(qb, kst)
    lv = 0
    s = c // 2
    while s >= 1:
        if s >= SMALL_LEVEL:
            first = s if reverse else s - 1
            bnd = [jnp.broadcast_to(cum[p * 2 * s + first:p * 2 * s + first + 1, :], (2 * s, w))
                   for p in range(c // (2 * s))]
            bnd = jnp.concatenate(bnd, axis=0) if len(bnd) > 1 else bnd[0]
            dl = (cum - bnd) * sg_ref[lv]
        elif s == 2:
            dl = (gc * cf_ref[1] + pltpu.roll(gc, 1, 0) * cf_ref[2] + pltpu.roll(gc, c - 1, 0) * cf_ref[3])
        else:
            dl = gc * cf_ref[0]
        el = jnp.exp(dl).astype(BF)
        a = a + pm_ref[lv] * _mm_nt(qb * el, kst * jnp.concatenate([el] * N_HEADS, axis=0))
        lv += 1
        s //= 2
    tot = cum[0:1, :] if reverse else cum[c - 1:c, :]
    o = _mm(a, _stack_heads_bf(vc, hv_ref)) + _mm_nt(qc * jnp.exp(cum), st)
    st_new = st * jnp.exp(tot) + bdt_ref[...] * _mm_tn(vc, kc * jnp.exp(tot - cum))
    return o, st_new


def _gated_scan(q_ref, kf_ref, kb_ref, v_ref, gf_ref, gb_ref, of_ref, ob_ref, s0_ref, st_ref, cf, cb,
                t_len, w, dk, eye_ref):
    n_chunks = t_len // CHUNK

    def body(t, carry):
        new = []
        for sq in range(N_PAIR):
            rf = pl.ds(pl.multiple_of(sq * t_len + t * CHUNK, CHUNK), CHUNK)
            rb = pl.ds(pl.multiple_of(sq * t_len + (n_chunks - 1 - t) * CHUNK, CHUNK), CHUNK)
            o_f, sf = _gated_chunk(q_ref[rf, :], kf_ref[rf, :], v_ref[rf, :], gf_ref[rf, :],
                                   carry[2 * sq], *cf, False)
            o_b, sb = _gated_chunk(q_ref[rb, :], kb_ref[rb, :], v_ref[rb, :], gb_ref[rb, :],
                                   carry[2 * sq + 1], *cb, True)
            of_ref[rf, :] = o_f
            ob_ref[rb, :] = o_b
            new += [sf, sb]
        return tuple(new)

    if s0_ref is not None:
        init = tuple(s0_ref[sq, d] for sq in range(N_PAIR) for d in range(2))
    else:
        init = tuple(jnp.zeros((BRANCH_W, w), F32) for _ in range(2 * N_PAIR))
    finals = lax.fori_loop(0, n_chunks, body, init)
    if st_ref is not None:
        _zero_later_layers(st_ref)
        for sq in range(N_PAIR):
            for d in range(2):
                st = finals[2 * sq + d]
                for h in range(N_HEADS):
                    tr = _sel_tn(st[h * HEAD_V:(h + 1) * HEAD_V, :], eye_ref[...])
                    st_ref[sq, 0, d, h] = tr[h * dk:(h + 1) * dk, :]


def _gla_kernel(*refs, t_len, has_state, n_prev):
    blk_ref, small_ref, w2_ref, gb_ref, nw_ref = refs[:5]
    pos = 5 + N_SCAN_CONSTS
    cf, cb, ones_ref, eye_ref = _split_scan_consts(refs[5:pos])
    s0_ref = st_ref = None
    if has_state:
        s0_ref = refs[pos]
        pos += 1
    pos += n_prev
    out_ref = refs[pos]
    pos += 1
    if not has_state:
        st_ref = refs[pos]
        pos += 1
    q_s, gf_s, gb_s, of_s, ob_s = refs[pos:pos + 5]

    q_s[...] = blk_ref[:, 0:128] * (GLA_DK ** -0.5)
    z = _mm(small_ref[...], w2_ref[...]) + gb_ref[...]
    g = _log_sigmoid(z) * (1.0 / GLA_TAU)
    gf_s[...] = g[:, 0:128]
    gb_s[...] = g[:, 128:256]
    k_ref = blk_ref.at[:, 128:256]
    v_ref = blk_ref.at[:, 256:512]
    _gated_scan(q_s, k_ref, k_ref, v_ref, gf_s, gb_s, of_s, ob_s, s0_ref, st_ref, cf, cb,
                t_len, GLA_KW, GLA_DK, eye_ref)
    out_ref[...] = _head_norm_gate(of_s[...] + ob_s[...], blk_ref[:, 512:768], nw_ref[...], ones_ref[...])


def _const_spec(a):
    nd = a.ndim
    return pl.BlockSpec(a.shape, lambda b, _n=nd: (0,) * _n)


def _scan_const_arrays(w):
    out = []
    for reverse in (False, True):
        tri, sg, cf, pm = _scan_consts(w, reverse)
        out += [jnp.asarray(tri, BF), jnp.asarray(sg, F32), jnp.asarray(cf, F32), jnp.asarray(pm, F32)]
    hm, hv, bd, ones_bd = _head_consts(w)
    out += [jnp.asarray(hm, F32), jnp.asarray(hv, F32), jnp.asarray(bd.T, F32), jnp.asarray(ones_bd, BF),
            jnp.eye(HEAD_V, dtype=BF)]
    return tuple(out)


def _split_scan_consts(refs):
    shared = tuple(refs[8:11])
    return tuple(refs[0:4]) + shared, tuple(refs[4:8]) + shared, refs[11], refs[12]


def _zero_later_layers(ref):
    if ref.shape[1] > 1:
        ref[:, 1:] = jnp.zeros((ref.shape[0], ref.shape[1] - 1) + tuple(ref.shape[2:]), ref.dtype)


def _add_layer_outputs(tails, n_lead, lead_block, layer, prevs, in_specs, args, out_shape, out_specs):
    first_out = len(out_shape)
    for tail in tails:
        out_shape.append(jax.ShapeDtypeStruct((n_lead, DEPTH) + tuple(tail), F32))
        if prevs is None:
            assert layer == 0
            out_specs.append(pl.BlockSpec((lead_block, DEPTH) + tuple(tail),
                                          lambda b, _nz=len(tail): (b,) + (0,) * (_nz + 1)))
        else:
            out_specs.append(pl.BlockSpec((lead_block, 1) + tuple(tail),
                                          lambda b, _nz=len(tail): (b, layer) + (0,) * _nz))
    aliases = {}
    for k, p in enumerate(prevs or ()):
        in_specs.append(pl.BlockSpec(memory_space=pl.ANY))
        aliases[len(args)] = first_out + k
        args.append(p)
    return aliases


def _gla_call(proj, w2bd, gbias, nw, s0, t_len, n_seq, row_blk0, layer=0, prev=None):
    has_state = s0 is not None
    consts = (w2bd, gbias, nw) + _scan_const_arrays(GLA_KW)
    n_rows = N_PAIR * t_len
    in_specs = [pl.BlockSpec((n_rows, 768), lambda b: (row_blk0 + b, 0)),
                pl.BlockSpec((n_rows, 128), lambda b: (row_blk0 + b, COL_SMALL_BLOCK))]
    in_specs += [_const_spec(a) for a in consts]
    args = [proj, proj, *consts]
    if has_state:
        in_specs.append(pl.BlockSpec((N_PAIR, 2, BRANCH_W, GLA_KW), lambda b: (b, 0, 0, 0)))
        args.append(s0)
    out_shape = [jax.ShapeDtypeStruct((n_seq * t_len, BRANCH_W), F32)]
    out_specs = [pl.BlockSpec((n_rows, BRANCH_W), lambda b: (b, 0))]
    aliases = {}
    if not has_state:
        aliases = _add_layer_outputs([(2, N_HEADS, GLA_DK, HEAD_V)], n_seq, N_PAIR, layer, prev,
                                     in_specs, args, out_shape, out_specs)
    return pl.pallas_call(
        functools.partial(_gla_kernel, t_len=t_len, has_state=has_state, n_prev=len(aliases)),
        grid=(n_seq // N_PAIR,),
        in_specs=in_specs,
        out_specs=out_specs,
        out_shape=out_shape,
        input_output_aliases=aliases,
        scratch_shapes=[pltpu.VMEM((n_rows, GLA_KW), F32)] * 3 + [pltpu.VMEM((n_rows, BRANCH_W), F32)] * 2,
        compiler_params=_cparams(("arbitrary",)),
        name="gla_mixer",
    )(*args)


def _hgrn_kernel(*refs, t_len, has_state, layer, n_prev):
    q_ref, f_ref, v_ref, gate_ref, lbl_ref, nw_ref = refs[:6]
    pos = 6 + N_SCAN_CONSTS
    cf, cb, ones_ref, eye_ref = _split_scan_consts(refs[6:pos])
    s0_ref = st_ref = None
    if has_state:
        s0_ref = refs[pos]
        pos += 1
    pos += n_prev
    out_ref = refs[pos]
    pos += 1
    if not has_state:
        st_ref = refs[pos]
        pos += 1
    q_s, kf_s, kb_s, gf_s, gb_s, of_s, ob_s = refs[pos:pos + 7]

    lg = lbl_ref[...]
    mx = jnp.max(lg, axis=0, keepdims=True)
    ex = jnp.exp(lg - mx)
    p = ex / jnp.sum(ex, axis=0, keepdims=True)
    lb = jnp.sum(p[0:layer + 1], axis=0, keepdims=True) - p[0:1]

    q_s[...] = _silu(q_ref[...]) * (HG_DK ** -0.5)
    f = lb + (1.0 - lb) * _sigmoid(f_ref[...])
    kf_s[...] = 1.0 - f[:, 0:HG_W]
    kb_s[...] = 1.0 - f[:, HG_W:2 * HG_W]
    lf = jnp.log(f)
    gf_s[...] = lf[:, 0:HG_W]
    gb_s[...] = lf[:, HG_W:2 * HG_W]
    _gated_scan(q_s, kf_s, kb_s, v_ref, gf_s, gb_s, of_s, ob_s, s0_ref, st_ref, cf, cb,
                t_len, HG_W, HG_DK, eye_ref)
    out_ref[...] = _head_norm_gate(of_s[...] + ob_s[...], gate_ref[...], nw_ref[...], ones_ref[...])


def _hgrn_call(proj, lb_logits, nw, s0, t_len, n_seq, row_blk0, layer, prev=None):
    has_state = s0 is not None
    consts = (lb_logits, nw) + _scan_const_arrays(HG_W)
    n_rows = N_PAIR * t_len
    in_specs = [pl.BlockSpec((n_rows, 256), lambda b: (row_blk0 + b, 7)),
                pl.BlockSpec((n_rows, 512), lambda b: (row_blk0 + b, 4)),
                pl.BlockSpec((n_rows, 256), lambda b: (row_blk0 + b, 10)),
                pl.BlockSpec((n_rows, 256), lambda b: (row_blk0 + b, 11))]
    in_specs += [_const_spec(a) for a in consts]
    args = [proj, proj, proj, proj, *consts]
    if has_state:
        in_specs.append(pl.BlockSpec((N_PAIR, 2, BRANCH_W, HG_W), lambda b: (b, 0, 0, 0)))
        args.append(s0)
    out_shape = [jax.ShapeDtypeStruct((n_seq * t_len, BRANCH_W), F32)]
    out_specs = [pl.BlockSpec((n_rows, BRANCH_W), lambda b: (b, 0))]
    aliases = {}
    if not has_state:
        aliases = _add_layer_outputs([(2, N_HEADS, HG_DK, HEAD_V)], n_seq, N_PAIR, layer, prev,
                                     in_specs, args, out_shape, out_specs)
    return pl.pallas_call(
        functools.partial(_hgrn_kernel, t_len=t_len, has_state=has_state, layer=layer, n_prev=len(aliases)),
        grid=(n_seq // N_PAIR,),
        in_specs=in_specs,
        out_specs=out_specs,
        out_shape=out_shape,
        input_output_aliases=aliases,
        scratch_shapes=[pltpu.VMEM((n_rows, HG_W), F32)] * 5 + [pltpu.VMEM((n_rows, BRANCH_W), F32)] * 2,
        compiler_params=_cparams(("arbitrary",)),
        name="hgrn2_mixer",
    )(*args)


N_PAIR = 2
SOLVE_JB = 8
SOLVE_IB = 16


def _dn_solve_kernel(lt_ref, vb_ref, kb_ref, u_ref, w_ref):
    _dn_substitute(pl.program_id(1), lt_ref.at[0], vb_ref.at[0], kb_ref.at[0], u_ref.at[0], w_ref.at[0])


def _dn_substitute(rev, l_ref, vb_ref, kb_ref, u_ref, w_ref):
    c = CHUNK
    u_ref[...] = jnp.zeros_like(u_ref)
    w_ref[...] = jnp.zeros_like(w_ref)

    def outer(t, carry):
        i = t + rev * (c - 1 - 2 * t)

        def inner(jb, acc):
            au, aw = acc
            j0 = pl.multiple_of(jb * SOLVE_JB, SOLVE_JB)
            for r in range(SOLVE_JB):
                coef = l_ref[i, pl.ds(j0 + r, 1), :]
                au = au - coef * u_ref[j0 + r]
                aw = aw - coef * w_ref[j0 + r]
            return au, aw

        blk = lax.shift_right_logical(i, 3)
        lo = rev * blk
        hi = blk + 1 + rev * (c // SOLVE_JB - blk - 1)
        au, aw = lax.fori_loop(lo, hi, inner, (vb_ref[i], kb_ref[i]))
        u_ref[i] = au
        w_ref[i] = aw
        return carry

    lax.fori_loop(0, c, outer, 0)


def _dn_solve_packed_kernel(lt_ref, vb_ref, kb_ref, u_ref, w_ref, lp_s, vp_s, kp_s, up_s, wp_s):
    n_chunk = lt_ref.shape[-1]
    n_ib = CHUNK // SOLVE_IB
    s = pl.program_id(1)

    @pl.when(s < n_ib)
    def _():
        for ii in range(SOLVE_IB):
            for src, dst in ((lt_ref, lp_s), (vb_ref, vp_s), (kb_ref, kp_s)):
                dst[s * SOLVE_IB + ii] = jnp.concatenate(
                    [src[0, ii, h * HEAD_V:(h + 1) * HEAD_V, :] for h in range(N_HEADS)], axis=1)

    @pl.when(s == n_ib - 1)
    def _():
        _dn_substitute(pl.program_id(0), lp_s, vp_s, kp_s, up_s, wp_s)

    @pl.when(s >= n_ib)
    def _():
        for ii in range(SOLVE_IB):
            for src, dst in ((up_s, u_ref), (wp_s, w_ref)):
                row = src[(s - n_ib) * SOLVE_IB + ii]
                for h in range(N_HEADS):
                    dst[0, ii, h * HEAD_V:(h + 1) * HEAD_V, :] = row[:, h * n_chunk:(h + 1) * n_chunk]


def _dn_solve_call(lt, vbt, kbt):
    n_chunk = lt.shape[-1]
    shape = jax.ShapeDtypeStruct((2, CHUNK, BRANCH_W, n_chunk), F32)
    if n_chunk * N_HEADS == 128:
        n_ib = CHUNK // SOLVE_IB
        blk = (1, SOLVE_IB, BRANCH_W, n_chunk)
        spec = pl.BlockSpec(blk, lambda d, s: (d, jnp.minimum(s, n_ib - 1), 0, 0))
        ospec = pl.BlockSpec(blk, lambda d, s: (d, jnp.maximum(s - n_ib, 0), 0, 0))
        return pl.pallas_call(
            _dn_solve_packed_kernel,
            grid=(2, 2 * n_ib),
            in_specs=[spec, spec, spec],
            out_specs=[ospec, ospec],
            out_shape=[shape, shape],
            scratch_shapes=[pltpu.VMEM((CHUNK, HEAD_V, 128), F32)] * 5,
            compiler_params=_cparams(("arbitrary", "arbitrary")),
            name="deltanet_solve_packed",
        )(lt, vbt, kbt)
    spec = pl.BlockSpec((1, CHUNK, HEAD_V, n_chunk), lambda h, d: (d, 0, h, 0))
    return pl.pallas_call(
        _dn_solve_kernel,
        grid=(N_HEADS, 2),
        in_specs=[spec, spec, spec],
        out_specs=[spec, spec],
        out_shape=[shape, shape],
        compiler_params=_cparams(("arbitrary", "arbitrary")),
        name="deltanet_solve",
    )(lt, vbt, kbt)


def _chunks_to_lanes(a):
    return a.reshape(2, a.shape[1] // CHUNK, CHUNK, BRANCH_W).transpose(0, 2, 3, 1)


def _chunks_from_lanes(a):
    return a.transpose(0, 3, 1, 2).reshape(2, -1, BRANCH_W)


def _dn_build_kernel(x_ref, small_ref, cw_ref, alog_ref, dtb_ref,
                     tri_ref, strict_ref, eye_ref, blk_ref, exb_ref, exa_ref, hm_ref, ones_ref,
                     q_ref, k_ref, d_ref, l_ref, vb_ref, kb_ref, qkd_ref, v_ref, be_s, ge_s, *, t_len):
    n_rows = N_PAIR * t_len

    x = x_ref[...]
    row = lax.broadcasted_iota(jnp.int32, (n_rows, 1), 0) % t_len
    x_prev = jnp.where(row == 0, 0.0, pltpu.roll(x, 1, 0))
    x_next = jnp.where(row == t_len - 1, 0.0, pltpu.roll(x, n_rows - 1, 0))
    y = _silu(x_prev * cw_ref[0:1, :] + x * cw_ref[1:2, :] + x_next * cw_ref[2:3, :])
    cq, ck = y[:, 0:256], y[:, 256:512]
    v_ref[...] = y[:, 512:768]
    q_ref[...] = cq * lax.rsqrt(_sel_r(cq * cq, ones_ref[...], 2) + EPS) * (DN_DK ** -0.5)
    k_ref[...] = ck * lax.rsqrt(_sel_r(ck * ck, ones_ref[...], 2) + EPS)
    sm = small_ref[...]
    be_s[...] = _sel_r(_sigmoid(sm), exb_ref[...])
    ge_s[...] = _sel_r(-jnp.exp(alog_ref[...]) * _softplus(sm + dtb_ref[...]), exa_ref[...])

    def body(ci, carry):
        rows = pl.ds(pl.multiple_of(ci * CHUNK, CHUNK), CHUNK)
        qc, kc, vc = q_ref[rows, :], k_ref[rows, :], v_ref[rows, :]
        kst = _stack_heads(kc, hm_ref)
        kk = _mm_nt(kc, kst)
        qk = _mm_nt(qc, kst)
        for d in range(2):
            bexp = be_s[rows, d * BRANCH_W:(d + 1) * BRANCH_W]
            dexp = _sel_l(tri_ref[d], ge_s[rows, d * BRANCH_W:(d + 1) * BRANCH_W])
            drow = _sel_l(blk_ref[...], dexp * eye_ref[...])
            dec_s = jnp.exp(jnp.where(strict_ref[d] > 0.5, dexp - drow, -1e30))
            d_ref[d, rows, :] = dexp
            l_ref[d, rows, :] = bexp * kk * dec_s
            vb_ref[d, rows, :] = vc * bexp
            kb_ref[d, rows, :] = kc * bexp * jnp.exp(dexp)
            qkd_ref[d, rows, :] = (qk * (dec_s + eye_ref[...])).astype(BF)
        return carry

    lax.fori_loop(0, n_rows // CHUNK, body, 0)


def _dn_build_call(proj, conv_w, alog_row, dtb_row, t_len, n_seq, row_blk0):
    tri, strict, eye, blk, exb, exa = _dn_consts()
    tri, blk, exb, exa = (jnp.asarray(a, BF) for a in (tri, blk, exb, exa))
    strict, eye = jnp.asarray(strict, F32), jnp.asarray(eye, F32)
    hm, _, _, ones_bd = _head_consts(BRANCH_W)
    hm, ones_bd = jnp.asarray(hm, F32), jnp.asarray(ones_bd, BF)
    n_rows = N_PAIR * t_len
    n_tok = n_seq * t_len
    consts = (conv_w, alog_row, dtb_row, tri, strict, eye, blk, exb, exa, hm, ones_bd)
    in_specs = [pl.BlockSpec((n_rows, 768), lambda b: (row_blk0 + b, 1)),
                pl.BlockSpec((n_rows, 128), lambda b: (row_blk0 + b, COL_SMALL_BLOCK))]
    in_specs += [_const_spec(a) for a in consts]
    tok_spec = pl.BlockSpec((n_rows, BRANCH_W), lambda b: (b, 0))
    dir_spec = pl.BlockSpec((2, n_rows, BRANCH_W), lambda b: (0, b, 0))
    tok = jax.ShapeDtypeStruct((n_tok, BRANCH_W), F32)
    per_dir = jax.ShapeDtypeStruct((2, n_tok, BRANCH_W), F32)
    return pl.pallas_call(
        functools.partial(_dn_build_kernel, t_len=t_len),
        grid=(n_seq // N_PAIR,),
        in_specs=in_specs,
        out_specs=[tok_spec] * 2 + [dir_spec] * 5,
        out_shape=[tok] * 2 + [per_dir] * 4 + [jax.ShapeDtypeStruct((2, n_tok, BRANCH_W), BF)],
        scratch_shapes=[pltpu.VMEM((n_rows, BRANCH_W), F32)] + [pltpu.VMEM((n_rows, 2 * BRANCH_W), F32)] * 2,
        compiler_params=_cparams(("arbitrary",)),
        name="deltanet_build",
    )(proj, proj, *consts)


def _dn_scan_kernel(*refs, t_len, has_state, n_prev):
    (q_ref, k_ref, d_ref, u_ref, w_ref, qkd_ref, gate_ref, nw_ref, hv_ref, bd_ref, ones_ref) = refs[:11]
    pos = 11
    if has_state:
        s0_ref = refs[pos]
        pos += 1
    pos += n_prev
    out_ref = refs[pos]
    pos += 1
    if not has_state:
        st_ref = refs[pos]
        pos += 1
    of_s, ob_s = refs[pos:pos + 2]
    c = CHUNK
    n_chunks = t_len // c

    def step(s, rows, d, o_s):
        qc, kc = q_ref[rows, :], k_ref[rows, :]
        dexp = d_ref[d, rows, :]
        v_new = u_ref[d, rows, :] - _mm(w_ref[d, rows, :], s)
        o_s[rows, :] = _mm(qc * jnp.exp(dexp), s) + _mm(qkd_ref[d, rows, :], _stack_heads(v_new, hv_ref))
        dl = dexp[0:1, :] if d == 1 else dexp[c - 1:c, :]
        return s * jnp.exp(dl) + bd_ref[...] * _mm_tn(kc * jnp.exp(dl - dexp), v_new)

    def body(t, carry):
        new = []
        for sq in range(N_PAIR):
            rf = pl.ds(pl.multiple_of(sq * t_len + t * c, c), c)
            rb = pl.ds(pl.multiple_of(sq * t_len + (n_chunks - 1 - t) * c, c), c)
            new.append(step(carry[2 * sq], rf, 0, of_s))
            new.append(step(carry[2 * sq + 1], rb, 1, ob_s))
        return tuple(new)

    if has_state:
        init = tuple(s0_ref[sq, d] for sq in range(N_PAIR) for d in range(2))
    else:
        init = tuple(jnp.zeros((BRANCH_W, BRANCH_W), F32) for _ in range(2 * N_PAIR))
    finals = lax.fori_loop(0, n_chunks, body, init)
    out_ref[...] = _head_norm_gate(of_s[...] + ob_s[...], gate_ref[...], nw_ref[...], ones_ref[...])
    if not has_state:
        _zero_later_layers(st_ref)
        for sq in range(N_PAIR):
            for d in range(2):
                for h in range(N_HEADS):
                    st_ref[sq, 0, d, h] = _head_lanes(finals[2 * sq + d][h * DN_DK:(h + 1) * DN_DK, :], h)


def _dn_scan_call(proj, q, k, dd, u, w, qkd, nw, s0, t_len, n_seq, row_blk0, layer=0, prev=None):
    _, hv, bd, ones_bd = _head_consts(BRANCH_W)
    hv, bd, ones_bd = jnp.asarray(hv, F32), jnp.asarray(bd, F32), jnp.asarray(ones_bd, BF)
    has_state = s0 is not None
    n_rows = N_PAIR * t_len
    consts = (nw, hv, bd, ones_bd)
    tok_spec = pl.BlockSpec((n_rows, BRANCH_W), lambda b: (b, 0))
    dir_spec = pl.BlockSpec((2, n_rows, BRANCH_W), lambda b: (0, b, 0))
    in_specs = [tok_spec, tok_spec, dir_spec, dir_spec, dir_spec, dir_spec,
                pl.BlockSpec((n_rows, 256), lambda b: (row_blk0 + b, 6))]
    in_specs += [_const_spec(a) for a in consts]
    args = [q, k, dd, u, w, qkd, proj, *consts]
    if has_state:
        in_specs.append(pl.BlockSpec((N_PAIR, 2, BRANCH_W, BRANCH_W), lambda b: (b, 0, 0, 0)))
        args.append(s0)
    out_shape = [jax.ShapeDtypeStruct((n_seq * t_len, BRANCH_W), F32)]
    out_specs = [pl.BlockSpec((n_rows, BRANCH_W), lambda b: (b, 0))]
    aliases = {}
    if not has_state:
        aliases = _add_layer_outputs([(2, N_HEADS, DN_DK, HEAD_V)], n_seq, N_PAIR, layer, prev,
                                     in_specs, args, out_shape, out_specs)
    return pl.pallas_call(
        functools.partial(_dn_scan_kernel, t_len=t_len, has_state=has_state, n_prev=len(aliases)),
        grid=(n_seq // N_PAIR,),
        in_specs=in_specs,
        out_specs=out_specs,
        out_shape=out_shape,
        input_output_aliases=aliases,
        scratch_shapes=[pltpu.VMEM((n_rows, BRANCH_W), F32)] * 2,
        compiler_params=_cparams(("arbitrary",)),
        name="deltanet_scan",
    )(*args)


def _rope(x, cos, sin):
    lane = lax.broadcasted_iota(jnp.int32, x.shape, 1) % 16
    n = x.shape[1]
    xrot = jnp.where(lane < 8, -pltpu.roll(x, n - 8, 1), pltpu.roll(x, 8, 1))
    return x * cos + xrot * sin


def _att_kernel(*refs, t_len, lat, lam_init, n_prev):
    blk_ref, lam_ref, nw_ref, qm_ref, hv_ref, ones_ref = refs[:6]
    pos = 6
    if lat:
        cos_ref, sin_ref, ck_ref, cv_ref = refs[pos:pos + 4]
        pos += 4
    pos += n_prev
    out_ref = refs[pos]
    pos += 1
    if not lat:
        nk_ref, nv_ref = refs[pos:pos + 2]
        pos += 2
    if lat:
        q_s, k_s, v_s = refs[pos:pos + 3]

    lv = lam_ref[...]
    lam = (jnp.exp(jnp.sum(lv[0:1] * lv[1:2], axis=1, keepdims=True))
           - jnp.exp(jnp.sum(lv[2:3] * lv[3:4], axis=1, keepdims=True)) + lam_init)
    q = blk_ref[:, 0:256]
    k = blk_ref[:, 256:512]
    v = blk_ref[:, 512:768]
    if lat:
        cos, sin = cos_ref[...], sin_ref[...]
        q_s[...] = _rope(q, cos, sin)
        k_s[0:PAST_LEN, :] = ck_ref[0]
        k_s[PAST_LEN:PAST_LEN + t_len, :] = _rope(k, cos, sin)
        v_s[0:PAST_LEN, :] = cv_ref[0]
        v_s[PAST_LEN:PAST_LEN + t_len, :] = v
        keys = k_s[...].astype(BF)
        vals = v_s[...].astype(BF)
    else:
        keys = k.astype(BF)
        vals = v.astype(BF)
        _zero_later_layers(nk_ref)
        _zero_later_layers(nv_ref)
        for h in range(N_HEADS):
            nk_ref[0, 0, h] = _head_lanes(k, h)
            nv_ref[0, 0, h] = _head_lanes(v, h)
    tq = TQ_ATT
    scale = DF_DH ** -0.5
    for qi in range(t_len // tq):
        qt = q_s[qi * tq:(qi + 1) * tq, :] if lat else q[qi * tq:(qi + 1) * tq, :]
        qs = jnp.concatenate([qt * qm_ref[r:r + 1, :] for r in range(2 * N_HEADS)], axis=0)
        s = _mm_nt(qs, keys) * scale
        s = jnp.exp(s - jnp.max(s, axis=1, keepdims=True))
        p = s / jnp.sum(s, axis=1, keepdims=True)
        a = jnp.concatenate(
            [p[(2 * h) * tq:(2 * h + 1) * tq] - lam * p[(2 * h + 1) * tq:(2 * h + 2) * tq]
             for h in range(N_HEADS)], axis=0)
        o = _head_diag(_mm(a, vals), hv_ref, tq)
        ms = _sel_r(o * o, ones_ref[...], 2) * (1.0 / HEAD_V)
        out_ref[qi * tq:(qi + 1) * tq, :] = o * lax.rsqrt(ms + EPS) * nw_ref[...] * (1.0 - lam_init)


def _att_call(proj, lam_p, nw, cache_k, cache_v, rope, t_len, n_seq, row_blk0, lam_init, layer=0, prev=None):
    qm = jnp.asarray(_att_consts(), F32)
    _, hv, _, ones_bd = _head_consts(BRANCH_W)
    hv, ones_bd = jnp.asarray(hv, F32), jnp.asarray(ones_bd, BF)
    lat = cache_k is not None
    consts = (lam_p, nw, qm, hv, ones_bd)
    in_specs = [pl.BlockSpec((t_len, 768), lambda b: (row_blk0 + b, 4))]
    in_specs += [_const_spec(a) for a in consts]
    args = [proj, *consts]
    out_shape = [jax.ShapeDtypeStruct((n_seq * t_len, BRANCH_W), F32)]
    out_specs = [pl.BlockSpec((t_len, BRANCH_W), lambda b: (b, 0))]
    scratch = []
    if lat:
        cos, sin = rope
        in_specs += [_const_spec(cos), _const_spec(sin),
                     pl.BlockSpec((1, PAST_LEN, BRANCH_W), lambda b: (b, 0, 0)),
                     pl.BlockSpec((1, PAST_LEN, BRANCH_W), lambda b: (b, 0, 0))]
        args += [cos, sin, cache_k, cache_v]
        scratch = [pltpu.VMEM((t_len, BRANCH_W), F32),
                   pltpu.VMEM((PAST_LEN + t_len, BRANCH_W), F32),
                   pltpu.VMEM((PAST_LEN + t_len, BRANCH_W), F32)]
    aliases = {}
    if not lat:
        aliases = _add_layer_outputs([(N_HEADS, t_len, HEAD_V)] * 2, n_seq, 1, layer, prev,
                                     in_specs, args, out_shape, out_specs)
    return pl.pallas_call(
        functools.partial(_att_kernel, t_len=t_len, lat=lat, lam_init=lam_init, n_prev=len(aliases)),
        grid=(n_seq,),
        in_specs=in_specs,
        out_specs=out_specs,
        out_shape=out_shape,
        input_output_aliases=aliases,
        scratch_shapes=scratch,
        compiler_params=_cparams(("arbitrary",)),
        name="diff_attention",
    )(*args)


def _block_diag_states(st, dk, transposed):
    eye = jnp.eye(N_HEADS, dtype=st.dtype)
    b = st.shape[0]
    if transposed:
        return jnp.einsum('bnhde,hg->bnhegd', st, eye).reshape(b, 2, N_HEADS * HEAD_V, N_HEADS * dk)
    return jnp.einsum('bnhde,hg->bnhdge', st, eye).reshape(b, 2, N_HEADS * dk, N_HEADS * HEAD_V)


def _in_perm():
    offs = np.concatenate([[0], np.cumsum(IN_ORIG)])
    seg = lambda a, b: np.arange(offs[a], offs[b])
    return np.concatenate([seg(0, 4), seg(5, 6), seg(8, 9), seg(9, 16), seg(4, 5), seg(6, 8)])


def kernel(x_prompt, x_sample, cache_diff_k, cache_diff_v, state_gla, state_dn, state_hgrn, c, c_ctx,
           norm_w, w_mod, b_mod, ffn1_in, ffn1_down, ffn2_in, ffn2_down, w_in, gla_w2, gla_b, gla_norm,
           dn_conv, dn_a_log, dn_dt_bias, dn_norm, hg_lb_logits, hg_norm, diff_lambda, diff_norm,
           w_branch, w_mgate, w_out, final_norm):
    xs = (x_prompt.reshape(N_CTX_TOK, D_MODEL), x_sample.reshape(-1, D_MODEL))
    c_rows = jnp.concatenate([c_ctx[None, :], c, jnp.zeros((8 - 1 - N_LAT_SEQ, D_MODEL), F32)], axis=0)
    mod = _mod_call(c_rows, w_mod, b_mod).reshape(DEPTH, 8, N_MOD, D_MODEL)
    rope = _rope_tables()
    perm = _in_perm()
    lb_logits = hg_lb_logits.reshape(DEPTH, 2 * HG_W)
    lat_blk = N_CTX_TOK // T_LAT
    tile4 = lambda a: jnp.tile(a, N_HEADS)[None, :]
    fin = final_norm[None, :]
    st_a = st_b = st_c = new_kv = None
    prev1 = lambda a: None if a is None else (a,)
    for l in range(DEPTH):
        mod_l = mod[l]
        lam_init = 0.8 - 0.6 * math.exp(-0.3 * l)
        w_in_p = jnp.pad(w_in[l][:, perm], ((0, 0), (0, N_IN_PAD - N_IN))).astype(BF)
        w2bd = jnp.zeros((128, 2 * GLA_KW), F32)
        w2bd = w2bd.at[0:GLA_LOWRANK, 0:GLA_KW].set(gla_w2[l, 0])
        w2bd = w2bd.at[GLA_LOWRANK:2 * GLA_LOWRANK, GLA_KW:].set(gla_w2[l, 1]).astype(BF)
        gbias = gla_b[l].reshape(1, 2 * GLA_KW)
        alog_row = jnp.zeros((1, 128), F32).at[0, SMALL_DNA:SMALL_DNA + 8].set(dn_a_log[l].reshape(-1))
        dtb_row = jnp.zeros((1, 128), F32).at[0, SMALL_DNA:SMALL_DNA + 8].set(dn_dt_bias[l].reshape(-1))

        (x,) = _ffn_call(xs, mod_l, norm_w[l, 0][None, :], ffn1_in[l].astype(BF), ffn1_down[l].astype(BF),
                         fin, 0, False, False)
        proj = _proj_call(x, mod_l, norm_w[l, 1][None, :], w_in_p)

        a_c, st_a = _gla_call(proj, w2bd, gbias, tile4(gla_norm[l]), None, T_CTX, N_CTX_SEQ, 0, l, prev1(st_a))
        dn_lat_blk = N_CTX_TOK // (N_PAIR * T_LAT)
        q_c, k_c, dd_c, lw_c, vb_c, kb_c, qkd_c = _dn_build_call(proj, dn_conv[l], alog_row, dtb_row,
                                                                T_CTX, N_CTX_SEQ, 0)
        q_l, k_l, dd_l, lw_l, vb_l, kb_l, qkd_l = _dn_build_call(proj, dn_conv[l], alog_row, dtb_row,
                                                                T_LAT, N_LAT_SEQ, dn_lat_blk)
        u_c, w_c = (_chunks_from_lanes(a) for a in _dn_solve_call(
            _chunks_to_lanes(lw_c), _chunks_to_lanes(vb_c), _chunks_to_lanes(kb_c)))
        u_l, w_l = (_chunks_from_lanes(a) for a in _dn_solve_call(
            _chunks_to_lanes(lw_l), _chunks_to_lanes(vb_l), _chunks_to_lanes(kb_l)))
        b_c, st_b = _dn_scan_call(proj, q_c, k_c, dd_c, u_c, w_c, qkd_c, tile4(dn_norm[l]), None,
                                  T_CTX, N_CTX_SEQ, 0, l, prev1(st_b))
        c_c, st_c = _hgrn_call(proj, lb_logits, tile4(hg_norm[l]), None, T_CTX, N_CTX_SEQ, 0, l, prev1(st_c))
        d_c, *new_kv = _att_call(proj, diff_lambda[l], tile4(diff_norm[l]), None, None, None,
                                 T_CTX, N_CTX_SEQ, 0, lam_init, l, new_kv)
        ck = cache_diff_k[:, l].transpose(0, 2, 1, 3).reshape(N_LAT_SEQ, PAST_LEN, BRANCH_W)
        cv = cache_diff_v[:, l].transpose(0, 2, 1, 3).reshape(N_LAT_SEQ, PAST_LEN, BRANCH_W)
        (a_l,) = _gla_call(proj, w2bd, gbias, tile4(gla_norm[l]),
                           _block_diag_states(state_gla[:, l], GLA_DK, True), T_LAT, N_LAT_SEQ, dn_lat_blk)
        (b_l,) = _dn_scan_call(proj, q_l, k_l, dd_l, u_l, w_l, qkd_l, tile4(dn_norm[l]),
                               _block_diag_states(state_dn[:, l], DN_DK, False), T_LAT, N_LAT_SEQ, dn_lat_blk)
        (c_l,) = _hgrn_call(proj, lb_logits, tile4(hg_norm[l]),
                            _block_diag_states(state_hgrn[:, l], HG_DK, True), T_LAT, N_LAT_SEQ, dn_lat_blk, l)
        (d_l,) = _att_call(proj, diff_lambda[l], tile4(diff_norm[l]), ck, cv, rope,
                           T_LAT, N_LAT_SEQ, lat_blk, lam_init)

        x = _merge_call(x, mod_l, norm_w[l, 1][None, :], (a_c, b_c, c_c, d_c), (a_l, b_l, c_l, d_l),
                        w_mgate[l].astype(BF), w_branch[l].astype(BF), w_out[l].astype(BF))
        xs = _ffn_call((x,), mod_l, norm_w[l, 2][None, :], ffn2_in[l].astype(BF), ffn2_down[l].astype(BF),
                       fin, 2, l == DEPTH - 1, l == DEPTH - 1)
    y_prompt = xs[0].reshape(N_CTX_SEQ, T_CTX, D_MODEL)
    y_sample = xs[1].reshape(N_LAT_SEQ, T_LAT, D_MODEL)
    return (y_prompt, y_sample, new_kv[0], new_kv[1], st_a, st_b, st_c)
```

```python
import functools
import math

import numpy as np
import jax
import jax.numpy as jnp
from jax import lax
from jax.experimental import pallas as pl
from jax.experimental.pallas import tpu as pltpu

F32 = jnp.float32
BF = jnp.bfloat16

D_MODEL = 1024
N_CTX_SEQ = 32
T_CTX = 256
DEPTH = 2
N_LAT_SEQ = 2
T_LAT = 1024
PAST_LEN = 512
GRID_W = 64
N_HEADS = 4
BRANCH_W = 256
HEAD_V = 64
GLA_DK = 32
GLA_KW = 128
GLA_LOWRANK = 16
GLA_TAU = 16.0
DN_DK = 64
HG_DK = 64
HG_W = 256
DF_DH = 32
ROPE_BASE = 10000.0
D_FF = 2816
N_MOD = 9
CHUNK = 64
EPS = 1e-6
N_CTX_TOK = N_CTX_SEQ * T_CTX
N_TOK = N_CTX_TOK + N_LAT_SEQ * T_LAT
N_LEVELS = 6
SMALL_LEVEL = 4
N_SCAN_CONSTS = 13

IN_ORIG = (128, 128, 256, 256, 32, 768, 8, 8, 256, 256, 512, 256, 256, 256, 256, 256)
N_IN = sum(IN_ORIG)
N_IN_PAD = 3968
COL_SMALL_BLOCK = 30
SMALL_LR = 0
SMALL_DNB = 32
SMALL_DNA = 40

VMEM_LIMIT = 56 * 1024 * 1024

TM_FFN = 512
TF_FFN = 1408
TM_PROJ = 512
TM_MERGE = 512
TN_MOD = 2304
TQ_ATT = 256


def _silu(x):
    return x * (1.0 / (1.0 + jnp.exp(-x)))


def _sigmoid(x):
    return 1.0 / (1.0 + jnp.exp(-x))


def _softplus(x):
    return jnp.maximum(x, 0.0) + jnp.log(1.0 + jnp.exp(-jnp.abs(x)))


def _log_sigmoid(x):
    return -_softplus(-x)


def _mm(a, b):
    return jnp.dot(a.astype(BF), b.astype(BF), preferred_element_type=F32)


def _mm_nt(a, b):
    return lax.dot_general(a.astype(BF), b.astype(BF), (((1,), (1,)), ((), ())),
                           preferred_element_type=F32)


def _mm_tn(a, b):
    return lax.dot_general(a.astype(BF), b.astype(BF), (((0,), (0,)), ((), ())),
                           preferred_element_type=F32)


def _split(x, n):
    parts = []
    r = x
    for i in range(n):
        p = r.astype(BF)
        parts.append(p)
        if i + 1 < n:
            r = r - p.astype(F32)
    return parts


def _sel_l(m01, x, n=3):
    out = None
    for p in _split(x, n):
        t = jnp.dot(m01, p, preferred_element_type=F32)
        out = t if out is None else out + t
    return out


def _sel_r(x, m01, n=3):
    out = None
    for p in _split(x, n):
        t = jnp.dot(p, m01, preferred_element_type=F32)
        out = t if out is None else out + t
    return out


def _sel_tn(x, m01, n=3):
    out = None
    for p in _split(x, n):
        t = lax.dot_general(p, m01, (((0,), (0,)), ((), ())), preferred_element_type=F32)
        out = t if out is None else out + t
    return out


def _rms(x, w):
    return x * lax.rsqrt(jnp.mean(x * x, axis=-1, keepdims=True) + EPS) * w


def _head_lanes(x, h):
    blk = x[:, (h // 2) * 128:(h // 2 + 1) * 128]
    if h % 2:
        blk = pltpu.roll(blk, 64, 1)
    return blk[:, :HEAD_V]


def _stack_heads(x, hm_ref):
    return jnp.concatenate([x * hm_ref[h:h + 1, :] for h in range(N_HEADS)], axis=0)


def _stack_heads_bf(x, hm_ref):
    xb = x.astype(BF)
    return jnp.concatenate([xb * hm_ref[h:h + 1, :].astype(BF) for h in range(N_HEADS)], axis=0)


def _head_diag(o_full, hv_ref, c):
    out = None
    for h in range(N_HEADS):
        t = o_full[h * c:(h + 1) * c, :] * hv_ref[h:h + 1, :]
        out = t if out is None else out + t
    return out


def _head_norm_gate(o, gate, nw, ones_bd):
    ms = _sel_r(o * o, ones_bd, 2) * (1.0 / HEAD_V)
    return o * lax.rsqrt(ms + EPS) * nw * _silu(gate)


def _mod_row(i, tm):
    return jnp.maximum(i * tm - (N_CTX_TOK - T_LAT), 0) // T_LAT


def _cparams(sem):
    return pltpu.CompilerParams(dimension_semantics=sem, vmem_limit_bytes=VMEM_LIMIT)


@functools.lru_cache(maxsize=None)
def _scan_consts(w, reverse):
    c = CHUNK
    idx = np.arange(c)
    i = idx[:, None]
    m = idx[None, :]
    pm, sg = [], []
    s = c // 2
    while s >= 1:
        par = idx // (2 * s)
        right = (idx % (2 * s)) >= s
        same = par[:, None] == par[None, :]
        query = ~right if reverse else right
        pm.append(same & query[:, None] & (~query)[None, :])
        if s >= SMALL_LEVEL:
            sg.append(np.where(query, 1.0, -1.0))
        s //= 2
    pm.append(i == m)
    tri = (m >= i) if reverse else (m <= i)
    r4, r2 = idx % 4, idx % 2
    if not reverse:
        cf = [r2 == 1, r4 >= 2, r4 == 3, r4 == 0]
    else:
        cf = [r2 == 0, r4 <= 1, r4 == 3, r4 == 0]
    wide = lambda rows: np.repeat(np.stack(rows).astype(np.float32)[:, :, None], w, axis=2)
    pmask = np.stack([np.tile(p, (1, N_HEADS)) for p in pm]).astype(np.float32)
    return tri.astype(np.float32), wide(sg), wide(cf), pmask


@functools.lru_cache(maxsize=None)
def _head_consts(w):
    dk = w // N_HEADS
    hm = np.zeros((N_HEADS, w), np.float32)
    hv = np.zeros((N_HEADS, BRANCH_W), np.float32)
    for h in range(N_HEADS):
        hm[h, h * dk:(h + 1) * dk] = 1.0
        hv[h, h * HEAD_V:(h + 1) * HEAD_V] = 1.0
    bd = hm.T @ hv
    ones_bd = hv.T @ hv
    return hm, hv, bd, ones_bd


@functools.lru_cache(maxsize=None)
def _dn_consts():
    c = CHUNK
    idx = np.arange(c)
    i = idx[:, None]
    j = idx[None, :]
    tri = np.stack([(j <= i), (j >= i)]).astype(np.float32)
    strict = np.stack([np.tile(j < i, (1, N_HEADS)), np.tile(j > i, (1, N_HEADS))]).astype(np.float32)
    eye = np.tile(np.eye(c), (1, N_HEADS)).astype(np.float32)
    blk = np.ones((c, c), np.float32)
    exb = np.zeros((128, 2 * BRANCH_W), np.float32)
    exa = np.zeros((128, 2 * BRANCH_W), np.float32)
    for n in range(2):
        for h in range(N_HEADS):
            lo = n * BRANCH_W + h * HEAD_V
            exb[SMALL_DNB + n * N_HEADS + h, lo:lo + HEAD_V] = 1.0
            exa[SMALL_DNA + n * N_HEADS + h, lo:lo + HEAD_V] = 1.0
    return tri, strict, eye, blk, exb, exa


@functools.lru_cache(maxsize=None)
def _att_consts():
    qm = np.zeros((2 * N_HEADS, BRANCH_W), np.float32)
    for h in range(N_HEADS):
        for mp in range(2):
            lo = h * HEAD_V + mp * DF_DH
            qm[2 * h + mp, lo:lo + DF_DH] = 1.0
    return qm


def _rope_tables():
    rows = T_LAT // GRID_W
    row = jnp.repeat(jnp.arange(rows), GRID_W).astype(F32)
    col = jnp.tile(jnp.arange(GRID_W), rows).astype(F32)
    half = DF_DH // 2
    inv = ROPE_BASE ** (-jnp.arange(0, half, 2, dtype=F32) / half)

    def angles(pos):
        a = pos[:, None] * inv[None, :]
        return jnp.concatenate([a, a], axis=-1)

    ang = jnp.concatenate([angles(row), angles(col)], axis=-1)
    reps = BRANCH_W // DF_DH
    return jnp.tile(jnp.cos(ang), (1, reps)), jnp.tile(jnp.sin(ang), (1, reps))


def _mod_kernel(c_ref, w_ref, b_ref, o_ref):
    a = _silu(c_ref[...])
    w = w_ref[0]
    out = None
    for ap in _split(a, 2):
        for wp in _split(w, 2):
            t = jnp.dot(ap, wp, preferred_element_type=F32)
            out = t if out is None else out + t
    o_ref[0] = out + b_ref[0]


def _mod_call(c_rows, w_mod, b_mod):
    n_t = (N_MOD * D_MODEL) // TN_MOD
    return pl.pallas_call(
        _mod_kernel,
        grid=(DEPTH, n_t),
        in_specs=[pl.BlockSpec((8, D_MODEL), lambda l, j: (0, 0)),
                  pl.BlockSpec((1, D_MODEL, TN_MOD), lambda l, j: (l, 0, j)),
                  pl.BlockSpec((1, 1, TN_MOD), lambda l, j: (l, 0, j))],
        out_specs=pl.BlockSpec((1, 8, TN_MOD), lambda l, j: (l, 0, j)),
        out_shape=jax.ShapeDtypeStruct((DEPTH, 8, N_MOD * D_MODEL), F32),
        compiler_params=_cparams(("arbitrary", "arbitrary")),
        name="mod_vectors",
    )(c_rows, w_mod, b_mod.reshape(DEPTH, 1, N_MOD * D_MODEL))


def _ffn_kernel(*refs, sub, final, split_in, split_out):
    n_x = 2 if split_in else 1
    mod_ref, nw_ref, wup_ref, wd_ref, fn_ref = refs[n_x:n_x + 5]
    outs = refs[n_x + 5:]
    is_lat = pl.program_id(0) >= N_CTX_TOK // TM_FFN
    x = jnp.where(is_lat, refs[1][...], refs[0][...]) if split_in else refs[0][...]
    sh = mod_ref[0, 3 * sub:3 * sub + 1, :]
    sc = mod_ref[0, 3 * sub + 1:3 * sub + 2, :]
    ga = mod_ref[0, 3 * sub + 2:3 * sub + 3, :]
    h = (_rms(x, nw_ref[...]) * (1.0 + sc) + sh).astype(BF)
    acc = None
    for f in range(D_FF // TF_FFN):
        lo = f * TF_FFN
        g = jnp.dot(h, wup_ref[:, lo:lo + TF_FFN], preferred_element_type=F32)
        u = jnp.dot(h, wup_ref[:, D_FF + lo:D_FF + lo + TF_FFN], preferred_element_type=F32)
        t = jnp.dot((_silu(g) * u).astype(BF), wd_ref[lo:lo + TF_FFN, :], preferred_element_type=F32)
        acc = t if acc is None else acc + t
    y = x + 0.5 * ga * acc
    if final:
        y = _rms(y, fn_ref[...])
    if split_out:
        @pl.when(jnp.logical_not(is_lat))
        def _():
            outs[0][...] = y

        @pl.when(is_lat)
        def _():
            outs[1][...] = y
    else:
        outs[0][...] = y


def _ffn_call(xs, mod_l, nw, w_in, w_down, final_w, sub, final, split_out):
    tm = TM_FFN
    n_ctx = N_CTX_TOK // tm
    ctx_map = lambda i: (jnp.minimum(i, n_ctx - 1), 0)
    lat_map = lambda i: (jnp.maximum(i - n_ctx, 0), 0)
    tile = lambda m: pl.BlockSpec((tm, D_MODEL), m)
    resident = lambda a: pl.BlockSpec(a.shape, lambda i: (0, 0), pipeline_mode=pl.Buffered(1))
    split_in = len(xs) == 2
    in_specs = [tile(ctx_map), tile(lat_map)] if split_in else [tile(lambda i: (i, 0))]
    in_specs += [pl.BlockSpec((1, N_MOD, D_MODEL), lambda i: (_mod_row(i, tm), 0, 0)),
                 pl.BlockSpec((1, D_MODEL), lambda i: (0, 0)),
                 resident(w_in), resident(w_down),
                 pl.BlockSpec((1, D_MODEL), lambda i: (0, 0))]
    if split_out:
        out_specs = [tile(ctx_map), tile(lat_map)]
        out_shape = [jax.ShapeDtypeStruct((N_CTX_TOK, D_MODEL), F32),
                     jax.ShapeDtypeStruct((N_TOK - N_CTX_TOK, D_MODEL), F32)]
    else:
        out_specs = [tile(lambda i: (i, 0))]
        out_shape = [jax.ShapeDtypeStruct((N_TOK, D_MODEL), F32)]
    return pl.pallas_call(
        functools.partial(_ffn_kernel, sub=sub, final=final, split_in=split_in, split_out=split_out),
        grid=(N_TOK // tm,),
        in_specs=in_specs,
        out_specs=out_specs,
        out_shape=out_shape,
        compiler_params=_cparams(("arbitrary",)),
        name="swiglu_half_step",
    )(*xs, mod_l, nw, w_in, w_down, final_w)


def _proj_kernel(x_ref, mod_ref, nw_ref, w_ref, o_ref):
    sh = mod_ref[0, 3:4, :]
    sc = mod_ref[0, 4:5, :]
    h = (_rms(x_ref[...], nw_ref[...]) * (1.0 + sc) + sh).astype(BF)
    o_ref[...] = jnp.dot(h, w_ref[...], preferred_element_type=F32)


def _proj_call(x, mod_l, nw, w_in_p):
    tm = TM_PROJ
    return pl.pallas_call(
        _proj_kernel,
        grid=(N_TOK // tm,),
        in_specs=[pl.BlockSpec((tm, D_MODEL), lambda i: (i, 0)),
                  pl.BlockSpec((1, N_MOD, D_MODEL), lambda i: (_mod_row(i, tm), 0, 0)),
                  pl.BlockSpec((1, D_MODEL), lambda i: (0, 0)),
                  pl.BlockSpec((D_MODEL, N_IN_PAD), lambda i: (0, 0))],
        out_specs=pl.BlockSpec((tm, N_IN_PAD), lambda i: (i, 0)),
        out_shape=jax.ShapeDtypeStruct((N_TOK, N_IN_PAD), F32),
        compiler_params=_cparams(("arbitrary",)),
        name="mixer_in_proj",
    )(x, mod_l, nw, w_in_p)


def _merge_kernel(x_ref, mod_ref, nw_ref, *rest):
    ctx_refs, lat_refs = rest[0:4], rest[4:8]
    wg_ref, wb_ref, wo_ref, o_ref = rest[8:12]
    x = x_ref[...]
    sh = mod_ref[0, 3:4, :]
    sc = mod_ref[0, 4:5, :]
    ga = mod_ref[0, 5:6, :]
    h = (_rms(x, nw_ref[...]) * (1.0 + sc) + sh).astype(BF)
    is_lat = pl.program_id(0) >= N_CTX_TOK // TM_MERGE
    mixed = None
    for n in range(4):
        gate = _sigmoid(jnp.dot(h, wg_ref[:, n * D_MODEL:(n + 1) * D_MODEL], preferred_element_type=F32))
        br = jnp.where(is_lat, lat_refs[n][...], ctx_refs[n][...])
        up = jnp.dot(br.astype(BF), wb_ref[n], preferred_element_type=F32)
        mixed = gate * up if mixed is None else mixed + gate * up
    out = jnp.dot(mixed.astype(BF), wo_ref[...], preferred_element_type=F32)
    o_ref[...] = x + ga * out


def _merge_call(x, mod_l, nw, ctx_branches, lat_branches, w_mgate, w_branch, w_out):
    tm = TM_MERGE
    n_ctx = N_CTX_TOK // tm
    cspec = pl.BlockSpec((tm, BRANCH_W), lambda i: (jnp.minimum(i, n_ctx - 1), 0))
    lspec = pl.BlockSpec((tm, BRANCH_W), lambda i: (jnp.maximum(i - n_ctx, 0), 0))
    return pl.pallas_call(
        _merge_kernel,
        grid=(N_TOK // tm,),
        in_specs=[pl.BlockSpec((tm, D_MODEL), lambda i: (i, 0)),
                  pl.BlockSpec((1, N_MOD, D_MODEL), lambda i: (_mod_row(i, tm), 0, 0)),
                  pl.BlockSpec((1, D_MODEL), lambda i: (0, 0)),
                  cspec, cspec, cspec, cspec, lspec, lspec, lspec, lspec,
                  pl.BlockSpec((D_MODEL, 4 * D_MODEL), lambda i: (0, 0)),
                  pl.BlockSpec((4, BRANCH_W, D_MODEL), lambda i: (0, 0, 0)),
                  pl.BlockSpec((D_MODEL, D_MODEL), lambda i: (0, 0))],
        out_specs=pl.BlockSpec((tm, D_MODEL), lambda i: (i, 0)),
        out_shape=jax.ShapeDtypeStruct((N_TOK, D_MODEL), F32),
        compiler_params=_cparams(("arbitrary",)),
        name="gated_merge",
    )(x, mod_l, nw, *ctx_branches, *lat_branches, w_mgate, w_branch, w_out)


def _gated_chunk(qc, kc, vc, gc, st, tri_ref, sg_ref, cf_ref, pm_ref, hm_ref, hv_ref, bdt_ref, reverse):
    c = CHUNK
    w = gc.shape[1]
    cum = _sel_l(tri_ref[...], gc, 2)
    kst = _stack_heads_bf(kc, hm_ref)
    qb = qc.astype(BF)
    a = pm_ref[N_LEVELS] * _mm_nt(qb, kst)
    lv = 0
    s = c // 2
    while s >= 1:
        if s >= SMALL_LEVEL:
            first = s if reverse else s - 1
            bnd = [jnp.broadcast_to(cum[p * 2 * s + first:p * 2 * s + first + 1, :], (2 * s, w))
                   for p in range(c // (2 * s))]
            bnd = jnp.concatenate(bnd, axis=0) if len(bnd) > 1 else bnd[0]
            dl = (cum - bnd) * sg_ref[lv]
        elif s == 2:
            dl = (gc * cf_ref[1] + pltpu.roll(gc, 1, 0) * cf_ref[2] + pltpu.roll(gc, c - 1, 0) * cf_ref[3])
        else:
            dl = gc * cf_ref[0]
        el = jnp.exp(dl).astype(BF)
        a = a + pm_ref[lv] * _mm_nt(qb * el, kst * jnp.concatenate([el] * N_HEADS, axis=0))
        lv += 1
        s //= 2
    tot = cum[0:1, :] if reverse else cum[c - 1:c, :]
    o = _mm(a, _stack_heads_bf(vc, hv_ref)) + _mm_nt(qc * jnp.exp(cum), st)
    st_new = st * jnp.exp(tot) + bdt_ref[...] * _mm_tn(vc, kc * jnp.exp(tot - cum))
    return o, st_new


def _gated_scan(q_ref, kf_ref, kb_ref, v_ref, gf_ref, gb_ref, of_ref, ob_ref, s0_ref, st_ref, cf, cb,
                t_len, w, dk, eye_ref):
    n_chunks = t_len // CHUNK

    def body(t, carry):
        new = []
        for sq in range(N_PAIR):
            rf = pl.ds(pl.multiple_of(sq * t_len + t * CHUNK, CHUNK), CHUNK)
            rb = pl.ds(pl.multiple_of(sq * t_len + (n_chunks - 1 - t) * CHUNK, CHUNK), CHUNK)
            o_f, sf = _gated_chunk(q_ref[rf, :], kf_ref[rf, :], v_ref[rf, :], gf_ref[rf, :],
                                   carry[2 * sq], *cf, False)
            o_b, sb = _gated_chunk(q_ref[rb, :], kb_ref[rb, :], v_ref[rb, :], gb_ref[rb, :],
                                   carry[2 * sq + 1], *cb, True)
            of_ref[rf, :] = o_f
            ob_ref[rb, :] = o_b
            new += [sf, sb]
        return tuple(new)

    if s0_ref is not None:
        init = tuple(s0_ref[sq, d] for sq in range(N_PAIR) for d in range(2))
    else:
        init = tuple(jnp.zeros((BRANCH_W, w), F32) for _ in range(2 * N_PAIR))
    finals = lax.fori_loop(0, n_chunks, body, init)
    if st_ref is not None:
        _zero_later_layers(st_ref)
        for sq in range(N_PAIR):
            for d in range(2):
                st = finals[2 * sq + d]
                for h in range(N_HEADS):
                    tr = _sel_tn(st[h * HEAD_V:(h + 1) * HEAD_V, :], eye_ref[...])
                    st_ref[sq, 0, d, h] = tr[h * dk:(h + 1) * dk, :]


def _gla_kernel(*refs, t_len, has_state, n_prev):
    blk_ref, small_ref, w2_ref, gb_ref, nw_ref = refs[:5]
    pos = 5 + N_SCAN_CONSTS
    cf, cb, ones_ref, eye_ref = _split_scan_consts(refs[5:pos])
    s0_ref = st_ref = None
    if has_state:
        s0_ref = refs[pos]
        pos += 1
    pos += n_prev
    out_ref = refs[pos]
    pos += 1
    if not has_state:
        st_ref = refs[pos]
        pos += 1
    q_s, gf_s, gb_s, of_s, ob_s = refs[pos:pos + 5]

    q_s[...] = blk_ref[:, 0:128] * (GLA_DK ** -0.5)
    z = _mm(small_ref[...], w2_ref[...]) + gb_ref[...]
    g = _log_sigmoid(z) * (1.0 / GLA_TAU)
    gf_s[...] = g[:, 0:128]
    gb_s[...] = g[:, 128:256]
    k_ref = blk_ref.at[:, 128:256]
    v_ref = blk_ref.at[:, 256:512]
    _gated_scan(q_s, k_ref, k_ref, v_ref, gf_s, gb_s, of_s, ob_s, s0_ref, st_ref, cf, cb,
                t_len, GLA_KW, GLA_DK, eye_ref)
    out_ref[...] = _head_norm_gate(of_s[...] + ob_s[...], blk_ref[:, 512:768], nw_ref[...], ones_ref[...])


def _const_spec(a):
    nd = a.ndim
    return pl.BlockSpec(a.shape, lambda b, _n=nd: (0,) * _n)


def _scan_const_arrays(w):
    out = []
    for reverse in (False, True):
        tri, sg, cf, pm = _scan_consts(w, reverse)
        out += [jnp.asarray(tri, BF), jnp.asarray(sg, F32), jnp.asarray(cf, F32), jnp.asarray(pm, F32)]
    hm, hv, bd, ones_bd = _head_consts(w)
    out += [jnp.asarray(hm, F32), jnp.asarray(hv, F32), jnp.asarray(bd.T, F32), jnp.asarray(ones_bd, BF),
            jnp.eye(HEAD_V, dtype=BF)]
    return tuple(out)


def _split_scan_consts(refs):
    shared = tuple(refs[8:11])
    return tuple(refs[0:4]) + shared, tuple(refs[4:8]) + shared, refs[11], refs[12]


def _zero_later_layers(ref):
    if ref.shape[1] > 1:
        ref[:, 1:] = jnp.zeros((ref.shape[0], ref.shape[1] - 1) + tuple(ref.shape[2:]), ref.dtype)


def _add_layer_outputs(tails, n_lead, lead_block, layer, prevs, in_specs, args, out_shape, out_specs):
    first_out = len(out_shape)
    for tail in tails:
        out_shape.append(jax.ShapeDtypeStruct((n_lead, DEPTH) + tuple(tail), F32))
        if prevs is None:
            assert layer == 0
            out_specs.append(pl.BlockSpec((lead_block, DEPTH) + tuple(tail),
                                          lambda b, _nz=len(tail): (b,) + (0,) * (_nz + 1)))
        else:
            out_specs.append(pl.BlockSpec((lead_block, 1) + tuple(tail),
                                          lambda b, _nz=len(tail): (b, layer) + (0,) * _nz))
    aliases = {}
    for k, p in enumerate(prevs or ()):
        in_specs.append(pl.BlockSpec(memory_space=pl.ANY))
        aliases[len(args)] = first_out + k
        args.append(p)
    return aliases


def _gla_call(proj, w2bd, gbias, nw, s0, t_len, n_seq, row_blk0, layer=0, prev=None):
    has_state = s0 is not None
    consts = (w2bd, gbias, nw) + _scan_const_arrays(GLA_KW)
    n_rows = N_PAIR * t_len
    in_specs = [pl.BlockSpec((n_rows, 768), lambda b: (row_blk0 + b, 0)),
                pl.BlockSpec((n_rows, 128), lambda b: (row_blk0 + b, COL_SMALL_BLOCK))]
    in_specs += [_const_spec(a) for a in consts]
    args = [proj, proj, *consts]
    if has_state:
        in_specs.append(pl.BlockSpec((N_PAIR, 2, BRANCH_W, GLA_KW), lambda b: (b, 0, 0, 0)))
        args.append(s0)
    out_shape = [jax.ShapeDtypeStruct((n_seq * t_len, BRANCH_W), F32)]
    out_specs = [pl.BlockSpec((n_rows, BRANCH_W), lambda b: (b, 0))]
    aliases = {}
    if not has_state:
        aliases = _add_layer_outputs([(2, N_HEADS, GLA_DK, HEAD_V)], n_seq, N_PAIR, layer, prev,
                                     in_specs, args, out_shape, out_specs)
    return pl.pallas_call(
        functools.partial(_gla_kernel, t_len=t_len, has_state=has_state, n_prev=len(aliases)),
        grid=(n_seq // N_PAIR,),
        in_specs=in_specs,
        out_specs=out_specs,
        out_shape=out_shape,
        input_output_aliases=aliases,
        scratch_shapes=[pltpu.VMEM((n_rows, GLA_KW), F32)] * 3 + [pltpu.VMEM((n_rows, BRANCH_W), F32)] * 2,
        compiler_params=_cparams(("arbitrary",)),
        name="gla_mixer",
    )(*args)


def _hgrn_kernel(*refs, t_len, has_state, layer, n_prev):
    q_ref, f_ref, v_ref, gate_ref, lbl_ref, nw_ref = refs[:6]
    pos = 6 + N_SCAN_CONSTS
    cf, cb, ones_ref, eye_ref = _split_scan_consts(refs[6:pos])
    s0_ref = st_ref = None
    if has_state:
        s0_ref = refs[pos]
        pos += 1
    pos += n_prev
    out_ref = refs[pos]
    pos += 1
    if not has_state:
        st_ref = refs[pos]
        pos += 1
    q_s, kf_s, kb_s, gf_s, gb_s, of_s, ob_s = refs[pos:pos + 7]

    lg = lbl_ref[...]
    mx = jnp.max(lg, axis=0, keepdims=True)
    ex = jnp.exp(lg - mx)
    p = ex / jnp.sum(ex, axis=0, keepdims=True)
    lb = jnp.sum(p[0:layer + 1], axis=0, keepdims=True) - p[0:1]

    q_s[...] = _silu(q_ref[...]) * (HG_DK ** -0.5)
    f = lb + (1.0 - lb) * _sigmoid(f_ref[...])
    kf_s[...] = 1.0 - f[:, 0:HG_W]
    kb_s[...] = 1.0 - f[:, HG_W:2 * HG_W]
    lf = jnp.log(f)
    gf_s[...] = lf[:, 0:HG_W]
    gb_s[...] = lf[:, HG_W:2 * HG_W]
    _gated_scan(q_s, kf_s, kb_s, v_ref, gf_s, gb_s, of_s, ob_s, s0_ref, st_ref, cf, cb,
                t_len, HG_W, HG_DK, eye_ref)
    out_ref[...] = _head_norm_gate(of_s[...] + ob_s[...], gate_ref[...], nw_ref[...], ones_ref[...])


def _hgrn_call(proj, lb_logits, nw, s0, t_len, n_seq, row_blk0, layer, prev=None):
    has_state = s0 is not None
    consts = (lb_logits, nw) + _scan_const_arrays(HG_W)
    n_rows = N_PAIR * t_len
    in_specs = [pl.BlockSpec((n_rows, 256), lambda b: (row_blk0 + b, 7)),
                pl.BlockSpec((n_rows, 512), lambda b: (row_blk0 + b, 4)),
                pl.BlockSpec((n_rows, 256), lambda b: (row_blk0 + b, 10)),
                pl.BlockSpec((n_rows, 256), lambda b: (row_blk0 + b, 11))]
    in_specs += [_const_spec(a) for a in consts]
    args = [proj, proj, proj, proj, *consts]
    if has_state:
        in_specs.append(pl.BlockSpec((N_PAIR, 2, BRANCH_W, HG_W), lambda b: (b, 0, 0, 0)))
        args.append(s0)
    out_shape = [jax.ShapeDtypeStruct((n_seq * t_len, BRANCH_W), F32)]
    out_specs = [pl.BlockSpec((n_rows, BRANCH_W), lambda b: (b, 0))]
    aliases = {}
    if not has_state:
        aliases = _add_layer_outputs([(2, N_HEADS, HG_DK, HEAD_V)], n_seq, N_PAIR, layer, prev,
                                     in_specs, args, out_shape, out_specs)
    return pl.pallas_call(
        functools.partial(_hgrn_kernel, t_len=t_len, has_state=has_state, layer=layer, n_prev=len(aliases)),
        grid=(n_seq // N_PAIR,),
        in_specs=in_specs,
        out_specs=out_specs,
        out_shape=out_shape,
        input_output_aliases=aliases,
        scratch_shapes=[pltpu.VMEM((n_rows, HG_W), F32)] * 5 + [pltpu.VMEM((n_rows, BRANCH_W), F32)] * 2,
        compiler_params=_cparams(("arbitrary",)),
        name="hgrn2_mixer",
    )(*args)


N_PAIR = 2
SOLVE_JB = 8
SOLVE_IB = 16


def _dn_solve_kernel(lt_ref, vb_ref, kb_ref, u_ref, w_ref):
    _dn_substitute(pl.program_id(1), lt_ref.at[0], vb_ref.at[0], kb_ref.at[0], u_ref.at[0], w_ref.at[0])


def _dn_substitute(rev, l_ref, vb_ref, kb_ref, u_ref, w_ref):
    c = CHUNK
    u_ref[...] = jnp.zeros_like(u_ref)
    w_ref[...] = jnp.zeros_like(w_ref)

    def outer(t, carry):
        i = t + rev * (c - 1 - 2 * t)

        def inner(jb, acc):
            au, aw = acc
            j0 = pl.multiple_of(jb * SOLVE_JB, SOLVE_JB)
            for r in range(SOLVE_JB):
                coef = l_ref[i, pl.ds(j0 + r, 1), :]
                au = au - coef * u_ref[j0 + r]
                aw = aw - coef * w_ref[j0 + r]
            return au, aw

        blk = lax.shift_right_logical(i, 3)
        lo = rev * blk
        hi = blk + 1 + rev * (c // SOLVE_JB - blk - 1)
        au, aw = lax.fori_loop(lo, hi, inner, (vb_ref[i], kb_ref[i]))
        u_ref[i] = au
        w_ref[i] = aw
        return carry

    lax.fori_loop(0, c, outer, 0)


def _dn_solve_packed_kernel(lt_ref, vb_ref, kb_ref, u_ref, w_ref, lp_s, vp_s, kp_s, up_s, wp_s):
    n_chunk = lt_ref.shape[-1]
    n_ib = CHUNK // SOLVE_IB
    s = pl.program_id(1)

    @pl.when(s < n_ib)
    def _():
        for ii in range(SOLVE_IB):
            for src, dst in ((lt_ref, lp_s), (vb_ref, vp_s), (kb_ref, kp_s)):
                dst[s * SOLVE_IB + ii] = jnp.concatenate(
                    [src[0, ii, h * HEAD_V:(h + 1) * HEAD_V, :] for h in range(N_HEADS)], axis=1)

    @pl.when(s == n_ib - 1)
    def _():
        _dn_substitute(pl.program_id(0), lp_s, vp_s, kp_s, up_s, wp_s)

    @pl.when(s >= n_ib)
    def _():
        for ii in range(SOLVE_IB):
            for src, dst in ((up_s, u_ref), (wp_s, w_ref)):
                row = src[(s - n_ib) * SOLVE_IB + ii]
                for h in range(N_HEADS):
                    dst[0, ii, h * HEAD_V:(h + 1) * HEAD_V, :] = row[:, h * n_chunk:(h + 1) * n_chunk]


def _dn_solve_call(lt, vbt, kbt):
    n_chunk = lt.shape[-1]
    shape = jax.ShapeDtypeStruct((2, CHUNK, BRANCH_W, n_chunk), F32)
    if n_chunk * N_HEADS == 128:
        n_ib = CHUNK // SOLVE_IB
        blk = (1, SOLVE_IB, BRANCH_W, n_chunk)
        spec = pl.BlockSpec(blk, lambda d, s: (d, jnp.minimum(s, n_ib - 1), 0, 0))
        ospec = pl.BlockSpec(blk, lambda d, s: (d, jnp.maximum(s - n_ib, 0), 0, 0))
        return pl.pallas_call(
            _dn_solve_packed_kernel,
            grid=(2, 2 * n_ib),
            in_specs=[spec, spec, spec],
            out_specs=[ospec, ospec],
            out_shape=[shape, shape],
            scratch_shapes=[pltpu.VMEM((CHUNK, HEAD_V, 128), F32)] * 5,
            compiler_params=_cparams(("arbitrary", "arbitrary")),
            name="deltanet_solve_packed",
        )(lt, vbt, kbt)
    spec = pl.BlockSpec((1, CHUNK, HEAD_V, n_chunk), lambda h, d: (d, 0, h, 0))
    return pl.pallas_call(
        _dn_solve_kernel,
        grid=(N_HEADS, 2),
        in_specs=[spec, spec, spec],
        out_specs=[spec, spec],
        out_shape=[shape, shape],
        compiler_params=_cparams(("arbitrary", "arbitrary")),
        name="deltanet_solve",
    )(lt, vbt, kbt)


def _chunks_to_lanes(a):
    return a.reshape(2, a.shape[1] // CHUNK, CHUNK, BRANCH_W).transpose(0, 2, 3, 1)


def _chunks_from_lanes(a):
    return a.transpose(0, 3, 1, 2).reshape(2, -1, BRANCH_W)


def _dn_build_kernel(x_ref, small_ref, cw_ref, alog_ref, dtb_ref,
                     tri_ref, strict_ref, eye_ref, blk_ref, exb_ref, exa_ref, hm_ref, ones_ref,
                     q_ref, k_ref, d_ref, l_ref, vb_ref, kb_ref, qkd_ref, v_ref, be_s, ge_s, *, t_len):
    n_rows = N_PAIR * t_len

    x = x_ref[...]
    row = lax.broadcasted_iota(jnp.int32, (n_rows, 1), 0) % t_len
    x_prev = jnp.where(row == 0, 0.0, pltpu.roll(x, 1, 0))
    x_next = jnp.where(row == t_len - 1, 0.0, pltpu.roll(x, n_rows - 1, 0))
    y = _silu(x_prev * cw_ref[0:1, :] + x * cw_ref[1:2, :] + x_next * cw_ref[2:3, :])
    cq, ck = y[:, 0:256], y[:, 256:512]
    v_ref[...] = y[:, 512:768]
    q_ref[...] = cq * lax.rsqrt(_sel_r(cq * cq, ones_ref[...], 2) + EPS) * (DN_DK ** -0.5)
    k_ref[...] = ck * lax.rsqrt(_sel_r(ck * ck, ones_ref[...], 2) + EPS)
    sm = small_ref[...]
    be_s[...] = _sel_r(_sigmoid(sm), exb_ref[...])
    ge_s[...] = _sel_r(-jnp.exp(alog_ref[...]) * _softplus(sm + dtb_ref[...]), exa_ref[...])

    def body(ci, carry):
        rows = pl.ds(pl.multiple_of(ci * CHUNK, CHUNK), CHUNK)
        qc, kc, vc = q_ref[rows, :], k_ref[rows, :], v_ref[rows, :]
        kst = _stack_heads(kc, hm_ref)
        kk = _mm_nt(kc, kst)
        qk = _mm_nt(qc, kst)
        for d in range(2):
            bexp = be_s[rows, d * BRANCH_W:(d + 1) * BRANCH_W]
            dexp = _sel_l(tri_ref[d], ge_s[rows, d * BRANCH_W:(d + 1) * BRANCH_W])
            drow = _sel_l(blk_ref[...], dexp * eye_ref[...])
            dec_s = jnp.exp(jnp.where(strict_ref[d] > 0.5, dexp - drow, -1e30))
            d_ref[d, rows, :] = dexp
            l_ref[d, rows, :] = bexp * kk * dec_s
            vb_ref[d, rows, :] = vc * bexp
            kb_ref[d, rows, :] = kc * bexp * jnp.exp(dexp)
            qkd_ref[d, rows, :] = (qk * (dec_s + eye_ref[...])).astype(BF)
        return carry

    lax.fori_loop(0, n_rows // CHUNK, body, 0)


def _dn_build_call(proj, conv_w, alog_row, dtb_row, t_len, n_seq, row_blk0):
    tri, strict, eye, blk, exb, exa = _dn_consts()
    tri, blk, exb, exa = (jnp.asarray(a, BF) for a in (tri, blk, exb, exa))
    strict, eye = jnp.asarray(strict, F32), jnp.asarray(eye, F32)
    hm, _, _, ones_bd = _head_consts(BRANCH_W)
    hm, ones_bd = jnp.asarray(hm, F32), jnp.asarray(ones_bd, BF)
    n_rows = N_PAIR * t_len
    n_tok = n_seq * t_len
    consts = (conv_w, alog_row, dtb_row, tri, strict, eye, blk, exb, exa, hm, ones_bd)
    in_specs = [pl.BlockSpec((n_rows, 768), lambda b: (row_blk0 + b, 1)),
                pl.BlockSpec((n_rows, 128), lambda b: (row_blk0 + b, COL_SMALL_BLOCK))]
    in_specs += [_const_spec(a) for a in consts]
    tok_spec = pl.BlockSpec((n_rows, BRANCH_W), lambda b: (b, 0))
    dir_spec = pl.BlockSpec((2, n_rows, BRANCH_W), lambda b: (0, b, 0))
    tok = jax.ShapeDtypeStruct((n_tok, BRANCH_W), F32)
    per_dir = jax.ShapeDtypeStruct((2, n_tok, BRANCH_W), F32)
    return pl.pallas_call(
        functools.partial(_dn_build_kernel, t_len=t_len),
        grid=(n_seq // N_PAIR,),
        in_specs=in_specs,
        out_specs=[tok_spec] * 2 + [dir_spec] * 5,
        out_shape=[tok] * 2 + [per_dir] * 4 + [jax.ShapeDtypeStruct((2, n_tok, BRANCH_W), BF)],
        scratch_shapes=[pltpu.VMEM((n_rows, BRANCH_W), F32)] + [pltpu.VMEM((n_rows, 2 * BRANCH_W), F32)] * 2,
        compiler_params=_cparams(("arbitrary",)),
        name="deltanet_build",
    )(proj, proj, *consts)


def _dn_scan_kernel(*refs, t_len, has_state, n_prev):
    (q_ref, k_ref, d_ref, u_ref, w_ref, qkd_ref, gate_ref, nw_ref, hv_ref, bd_ref, ones_ref) = refs[:11]
    pos = 11
    if has_state:
        s0_ref = refs[pos]
        pos += 1
    pos += n_prev
    out_ref = refs[pos]
    pos += 1
    if not has_state:
        st_ref = refs[pos]
        pos += 1
    of_s, ob_s = refs[pos:pos + 2]
    c = CHUNK
    n_chunks = t_len // c

    def step(s, rows, d, o_s):
        qc, kc = q_ref[rows, :], k_ref[rows, :]
        dexp = d_ref[d, rows, :]
        v_new = u_ref[d, rows, :] - _mm(w_ref[d, rows, :], s)
        o_s[rows, :] = _mm(qc * jnp.exp(dexp), s) + _mm(qkd_ref[d, rows, :], _stack_heads(v_new, hv_ref))
        dl = dexp[0:1, :] if d == 1 else dexp[c - 1:c, :]
        return s * jnp.exp(dl) + bd_ref[...] * _mm_tn(kc * jnp.exp(dl - dexp), v_new)

    def body(t, carry):
        new = []
        for sq in range(N_PAIR):
            rf = pl.ds(pl.multiple_of(sq * t_len + t * c, c), c)
            rb = pl.ds(pl.multiple_of(sq * t_len + (n_chunks - 1 - t) * c, c), c)
            new.append(step(carry[2 * sq], rf, 0, of_s))
            new.append(step(carry[2 * sq + 1], rb, 1, ob_s))
        return tuple(new)

    if has_state:
        init = tuple(s0_ref[sq, d] for sq in range(N_PAIR) for d in range(2))
    else:
        init = tuple(jnp.zeros((BRANCH_W, BRANCH_W), F32) for _ in range(2 * N_PAIR))
    finals = lax.fori_loop(0, n_chunks, body, init)
    out_ref[...] = _head_norm_gate(of_s[...] + ob_s[...], gate_ref[...], nw_ref[...], ones_ref[...])
    if not has_state:
        _zero_later_layers(st_ref)
        for sq in range(N_PAIR):
            for d in range(2):
                for h in range(N_HEADS):
                    st_ref[sq, 0, d, h] = _head_lanes(finals[2 * sq + d][h * DN_DK:(h + 1) * DN_DK, :], h)


def _dn_scan_call(proj, q, k, dd, u, w, qkd, nw, s0, t_len, n_seq, row_blk0, layer=0, prev=None):
    _, hv, bd, ones_bd = _head_consts(BRANCH_W)
    hv, bd, ones_bd = jnp.asarray(hv, F32), jnp.asarray(bd, F32), jnp.asarray(ones_bd, BF)
    has_state = s0 is not None
    n_rows = N_PAIR * t_len
    consts = (nw, hv, bd, ones_bd)
    tok_spec = pl.BlockSpec((n_rows, BRANCH_W), lambda b: (b, 0))
    dir_spec = pl.BlockSpec((2, n_rows, BRANCH_W), lambda b: (0, b, 0))
    in_specs = [tok_spec, tok_spec, dir_spec, dir_spec, dir_spec, dir_spec,
                pl.BlockSpec((n_rows, 256), lambda b: (row_blk0 + b, 6))]
    in_specs += [_const_spec(a) for a in consts]
    args = [q, k, dd, u, w, qkd, proj, *consts]
    if has_state:
        in_specs.append(pl.BlockSpec((N_PAIR, 2, BRANCH_W, BRANCH_W), lambda b: (b, 0, 0, 0)))
        args.append(s0)
    out_shape = [jax.ShapeDtypeStruct((n_seq * t_len, BRANCH_W), F32)]
    out_specs = [pl.BlockSpec((n_rows, BRANCH_W), lambda b: (b, 0))]
    aliases = {}
    if not has_state:
        aliases = _add_layer_outputs([(2, N_HEADS, DN_DK, HEAD_V)], n_seq, N_PAIR, layer, prev,
                                     in_specs, args, out_shape, out_specs)
    return pl.pallas_call(
        functools.partial(_dn_scan_kernel, t_len=t_len, has_state=has_state, n_prev=len(aliases)),
        grid=(n_seq // N_PAIR,),
        in_specs=in_specs,
        out_specs=out_specs,
        out_shape=out_shape,
        input_output_aliases=aliases,
        scratch_shapes=[pltpu.VMEM((n_rows, BRANCH_W), F32)] * 2,
        compiler_params=_cparams(("arbitrary",)),
        name="deltanet_scan",
    )(*args)


def _rope(x, cos, sin):
    lane = lax.broadcasted_iota(jnp.int32, x.shape, 1) % 16
    n = x.shape[1]
    xrot = jnp.where(lane < 8, -pltpu.roll(x, n - 8, 1), pltpu.roll(x, 8, 1))
    return x * cos + xrot * sin


def _att_kernel(*refs, t_len, lat, lam_init, n_prev):
    blk_ref, lam_ref, nw_ref, qm_ref, hv_ref, ones_ref = refs[:6]
    pos = 6
    if lat:
        cos_ref, sin_ref, ck_ref, cv_ref = refs[pos:pos + 4]
        pos += 4
    pos += n_prev
    out_ref = refs[pos]
    pos += 1
    if not lat:
        nk_ref, nv_ref = refs[pos:pos + 2]
        pos += 2
    if lat:
        q_s, k_s, v_s = refs[pos:pos + 3]

    lv = lam_ref[...]
    lam = (jnp.exp(jnp.sum(lv[0:1] * lv[1:2], axis=1, keepdims=True))
           - jnp.exp(jnp.sum(lv[2:3] * lv[3:4], axis=1, keepdims=True)) + lam_init)
    q = blk_ref[:, 0:256]
    k = blk_ref[:, 256:512]
    v = blk_ref[:, 512:768]
    if lat:
        cos, sin = cos_ref[...], sin_ref[...]
        q_s[...] = _rope(q, cos, sin)
        k_s[0:PAST_LEN, :] = ck_ref[0]
        k_s[PAST_LEN:PAST_LEN + t_len, :] = _rope(k, cos, sin)
        v_s[0:PAST_LEN, :] = cv_ref[0]
        v_s[PAST_LEN:PAST_LEN + t_len, :] = v
        keys = k_s[...].astype(BF)
        vals = v_s[...].astype(BF)
    else:
        keys = k.astype(BF)
        vals = v.astype(BF)
        _zero_later_layers(nk_ref)
        _zero_later_layers(nv_ref)
        for h in range(N_HEADS):
            nk_ref[0, 0, h] = _head_lanes(k, h)
            nv_ref[0, 0, h] = _head_lanes(v, h)
    tq = TQ_ATT
    scale = DF_DH ** -0.5
    for qi in range(t_len // tq):
        qt = q_s[qi * tq:(qi + 1) * tq, :] if lat else q[qi * tq:(qi + 1) * tq, :]
        qs = jnp.concatenate([qt * qm_ref[r:r + 1, :] for r in range(2 * N_HEADS)], axis=0)
        s = _mm_nt(qs, keys) * scale
        s = jnp.exp(s - jnp.max(s, axis=1, keepdims=True))
        p = s / jnp.sum(s, axis=1, keepdims=True)
        a = jnp.concatenate(
            [p[(2 * h) * tq:(2 * h + 1) * tq] - lam * p[(2 * h + 1) * tq:(2 * h + 2) * tq]
             for h in range(N_HEADS)], axis=0)
        o = _head_diag(_mm(a, vals), hv_ref, tq)
        ms = _sel_r(o * o, ones_ref[...], 2) * (1.0 / HEAD_V)
        out_ref[qi * tq:(qi + 1) * tq, :] = o * lax.rsqrt(ms + EPS) * nw_ref[...] * (1.0 - lam_init)


def _att_call(proj, lam_p, nw, cache_k, cache_v, rope, t_len, n_seq, row_blk0, lam_init, layer=0, prev=None):
    qm = jnp.asarray(_att_consts(), F32)
    _, hv, _, ones_bd = _head_consts(BRANCH_W)
    hv, ones_bd = jnp.asarray(hv, F32), jnp.asarray(ones_bd, BF)
    lat = cache_k is not None
    consts = (lam_p, nw, qm, hv, ones_bd)
    in_specs = [pl.BlockSpec((t_len, 768), lambda b: (row_blk0 + b, 4))]
    in_specs += [_const_spec(a) for a in consts]
    args = [proj, *consts]
    out_shape = [jax.ShapeDtypeStruct((n_seq * t_len, BRANCH_W), F32)]
    out_specs = [pl.BlockSpec((t_len, BRANCH_W), lambda b: (b, 0))]
    scratch = []
    if lat:
        cos, sin = rope
        in_specs += [_const_spec(cos), _const_spec(sin),
                     pl.BlockSpec((1, PAST_LEN, BRANCH_W), lambda b: (b, 0, 0)),
                     pl.BlockSpec((1, PAST_LEN, BRANCH_W), lambda b: (b, 0, 0))]
        args += [cos, sin, cache_k, cache_v]
        scratch = [pltpu.VMEM((t_len, BRANCH_W), F32),
                   pltpu.VMEM((PAST_LEN + t_len, BRANCH_W), F32),
                   pltpu.VMEM((PAST_LEN + t_len, BRANCH_W), F32)]
    aliases = {}
    if not lat:
        aliases = _add_layer_outputs([(N_HEADS, t_len, HEAD_V)] * 2, n_seq, 1, layer, prev,
                                     in_specs, args, out_shape, out_specs)
    return pl.pallas_call(
        functools.partial(_att_kernel, t_len=t_len, lat=lat, lam_init=lam_init, n_prev=len(aliases)),
        grid=(n_seq,),
        in_specs=in_specs,
        out_specs=out_specs,
        out_shape=out_shape,
        input_output_aliases=aliases,
        scratch_shapes=scratch,
        compiler_params=_cparams(("arbitrary",)),
        name="diff_attention",
    )(*args)


def _block_diag_states(st, dk, transposed):
    eye = jnp.eye(N_HEADS, dtype=st.dtype)
    b = st.shape[0]
    if transposed:
        return jnp.einsum('bnhde,hg->bnhegd', st, eye).reshape(b, 2, N_HEADS * HEAD_V, N_HEADS * dk)
    return jnp.einsum('bnhde,hg->bnhdge', st, eye).reshape(b, 2, N_HEADS * dk, N_HEADS * HEAD_V)


def _permute_in_cols(w):
    offs = [0] + [int(v) for v in np.cumsum(IN_ORIG)]
    pieces = [w[:, offs[a]:offs[b]] for a, b in ((0, 4), (5, 6), (8, 9), (9, 16), (4, 5), (6, 8))]
    pieces.append(jnp.zeros((w.shape[0], N_IN_PAD - N_IN), w.dtype))
    return jnp.concatenate(pieces, axis=1).astype(BF)


def kernel(x_prompt, x_sample, cache_diff_k, cache_diff_v, state_gla, state_dn, state_hgrn, c, c_ctx,
           norm_w, w_mod, b_mod, ffn1_in, ffn1_down, ffn2_in, ffn2_down, w_in, gla_w2, gla_b, gla_norm,
           dn_conv, dn_a_log, dn_dt_bias, dn_norm, hg_lb_logits, hg_norm, diff_lambda, diff_norm,
           w_branch, w_mgate, w_out, final_norm):
    xs = (x_prompt.reshape(N_CTX_TOK, D_MODEL), x_sample.reshape(-1, D_MODEL))
    c_rows = jnp.concatenate([c_ctx[None, :], c, jnp.zeros((8 - 1 - N_LAT_SEQ, D_MODEL), F32)], axis=0)
    mod = _mod_call(c_rows, w_mod, b_mod).reshape(DEPTH, 8, N_MOD, D_MODEL)
    rope = _rope_tables()
    lb_logits = hg_lb_logits.reshape(DEPTH, 2 * HG_W)
    lat_blk = N_CTX_TOK // T_LAT
    tile4 = lambda a: jnp.tile(a, N_HEADS)[None, :]
    fin = final_norm[None, :]
    st_a = st_b = st_c = new_kv = None
    prev1 = lambda a: None if a is None else (a,)
    for l in range(DEPTH):
        mod_l = mod[l]
        lam_init = 0.8 - 0.6 * math.exp(-0.3 * l)
        w_in_p = _permute_in_cols(w_in[l])
        w2bd = jnp.zeros((128, 2 * GLA_KW), F32)
        w2bd = w2bd.at[0:GLA_LOWRANK, 0:GLA_KW].set(gla_w2[l, 0])
        w2bd = w2bd.at[GLA_LOWRANK:2 * GLA_LOWRANK, GLA_KW:].set(gla_w2[l, 1]).astype(BF)
        gbias = gla_b[l].reshape(1, 2 * GLA_KW)
        alog_row = jnp.zeros((1, 128), F32).at[0, SMALL_DNA:SMALL_DNA + 8].set(dn_a_log[l].reshape(-1))
        dtb_row = jnp.zeros((1, 128), F32).at[0, SMALL_DNA:SMALL_DNA + 8].set(dn_dt_bias[l].reshape(-1))

        (x,) = _ffn_call(xs, mod_l, norm_w[l, 0][None, :], ffn1_in[l].astype(BF), ffn1_down[l].astype(BF),
                         fin, 0, False, False)
        proj = _proj_call(x, mod_l, norm_w[l, 1][None, :], w_in_p)

        a_c, st_a = _gla_call(proj, w2bd, gbias, tile4(gla_norm[l]), None, T_CTX, N_CTX_SEQ, 0, l, prev1(st_a))
        dn_lat_blk = N_CTX_TOK // (N_PAIR * T_LAT)
        q_c, k_c, dd_c, lw_c, vb_c, kb_c, qkd_c = _dn_build_call(proj, dn_conv[l], alog_row, dtb_row,
                                                                T_CTX, N_CTX_SEQ, 0)
        q_l, k_l, dd_l, lw_l, vb_l, kb_l, qkd_l = _dn_build_call(proj, dn_conv[l], alog_row, dtb_row,
                                                                T_LAT, N_LAT_SEQ, dn_lat_blk)
        u_c, w_c = (_chunks_from_lanes(a) for a in _dn_solve_call(
            _chunks_to_lanes(lw_c), _chunks_to_lanes(vb_c), _chunks_to_lanes(kb_c)))
        u_l, w_l = (_chunks_from_lanes(a) for a in _dn_solve_call(
            _chunks_to_lanes(lw_l), _chunks_to_lanes(vb_l), _chunks_to_lanes(kb_l)))
        b_c, st_b = _dn_scan_call(proj, q_c, k_c, dd_c, u_c, w_c, qkd_c, tile4(dn_norm[l]), None,
                                  T_CTX, N_CTX_SEQ, 0, l, prev1(st_b))
        c_c, st_c = _hgrn_call(proj, lb_logits, tile4(hg_norm[l]), None, T_CTX, N_CTX_SEQ, 0, l, prev1(st_c))
        d_c, *new_kv = _att_call(proj, diff_lambda[l], tile4(diff_norm[l]), None, None, None,
                                 T_CTX, N_CTX_SEQ, 0, lam_init, l, new_kv)
        ck = cache_diff_k[:, l].transpose(0, 2, 1, 3).reshape(N_LAT_SEQ, PAST_LEN, BRANCH_W)
        cv = cache_diff_v[:, l].transpose(0, 2, 1, 3).reshape(N_LAT_SEQ, PAST_LEN, BRANCH_W)
        (a_l,) = _gla_call(proj, w2bd, gbias, tile4(gla_norm[l]),
                           _block_diag_states(state_gla[:, l], GLA_DK, True), T_LAT, N_LAT_SEQ, dn_lat_blk)
        (b_l,) = _dn_scan_call(proj, q_l, k_l, dd_l, u_l, w_l, qkd_l, tile4(dn_norm[l]),
                               _block_diag_states(state_dn[:, l], DN_DK, False), T_LAT, N_LAT_SEQ, dn_lat_blk)
        (c_l,) = _hgrn_call(proj, lb_logits, tile4(hg_norm[l]),
                            _block_diag_states(state_hgrn[:, l], HG_DK, True), T_LAT, N_LAT_SEQ, dn_lat_blk, l)
        (d_l,) = _att_call(proj, diff_lambda[l], tile4(diff_norm[l]), ck, cv, rope,
                           T_LAT, N_LAT_SEQ, lat_blk, lam_init)

        x = _merge_call(x, mod_l, norm_w[l, 1][None, :], (a_c, b_c, c_c, d_c), (a_l, b_l, c_l, d_l),
                        w_mgate[l].astype(BF), w_branch[l].astype(BF), w_out[l].astype(BF))
        xs = _ffn_call((x,), mod_l, norm_w[l, 2][None, :], ffn2_in[l].astype(BF), ffn2_down[l].astype(BF),
                       fin, 2, l == DEPTH - 1, l == DEPTH - 1)
    y_prompt = xs[0].reshape(N_CTX_SEQ, T_CTX, D_MODEL)
    y_sample = xs[1].reshape(N_LAT_SEQ, T_LAT, D_MODEL)
    return (y_prompt, y_sample, new_kv[0], new_kv[1], st_a, st_b, st_c)
```

```python
import functools
import math

import numpy as np
import jax
import jax.numpy as jnp
from jax import lax
from jax.experimental import pallas as pl
from jax.experimental.pallas import tpu as pltpu

F32 = jnp.float32
BF = jnp.bfloat16

D_MODEL = 1024
N_CTX_SEQ = 32
T_CTX = 256
DEPTH = 2
N_LAT_SEQ = 2
T_LAT = 1024
PAST_LEN = 512
GRID_W = 64
N_HEADS = 4
BRANCH_W = 256
HEAD_V = 64
GLA_DK = 32
GLA_KW = 128
GLA_LOWRANK = 16
GLA_TAU = 16.0
DN_DK = 64
HG_DK = 64
HG_W = 256
DF_DH = 32
ROPE_BASE = 10000.0
D_FF = 2816
N_MOD = 9
CHUNK = 64
EPS = 1e-6
N_CTX_TOK = N_CTX_SEQ * T_CTX
N_TOK = N_CTX_TOK + N_LAT_SEQ * T_LAT
N_LEVELS = 6
SMALL_LEVEL = 4
N_SCAN_CONSTS = 13

IN_ORIG = (128, 128, 256, 256, 32, 768, 8, 8, 256, 256, 512, 256, 256, 256, 256, 256)
N_IN = sum(IN_ORIG)
N_IN_PAD = 3968
COL_SMALL_BLOCK = 30
SMALL_LR = 0
SMALL_DNB = 32
SMALL_DNA = 40

VMEM_LIMIT = 56 * 1024 * 1024

TM_FFN = 512
TF_FFN = 1408
TM_PROJ = 512
TM_MERGE = 512
TN_MOD = 2304
TQ_ATT = 256


def _silu(x):
    return x * (1.0 / (1.0 + jnp.exp(-x)))


def _sigmoid(x):
    return 1.0 / (1.0 + jnp.exp(-x))


def _softplus(x):
    return jnp.maximum(x, 0.0) + jnp.log(1.0 + jnp.exp(-jnp.abs(x)))


def _log_sigmoid(x):
    return -_softplus(-x)


def _mm(a, b):
    return jnp.dot(a.astype(BF), b.astype(BF), preferred_element_type=F32)


def _mm_nt(a, b):
    return lax.dot_general(a.astype(BF), b.astype(BF), (((1,), (1,)), ((), ())),
                           preferred_element_type=F32)


def _mm_tn(a, b):
    return lax.dot_general(a.astype(BF), b.astype(BF), (((0,), (0,)), ((), ())),
                           preferred_element_type=F32)


def _split(x, n):
    parts = []
    r = x
    for i in range(n):
        p = r.astype(BF)
        parts.append(p)
        if i + 1 < n:
            r = r - p.astype(F32)
    return parts


def _sel_l(m01, x, n=3):
    out = None
    for p in _split(x, n):
        t = jnp.dot(m01, p, preferred_element_type=F32)
        out = t if out is None else out + t
    return out


def _sel_r(x, m01, n=3):
    out = None
    for p in _split(x, n):
        t = jnp.dot(p, m01, preferred_element_type=F32)
        out = t if out is None else out + t
    return out


def _sel_tn(x, m01, n=3):
    out = None
    for p in _split(x, n):
        t = lax.dot_general(p, m01, (((0,), (0,)), ((), ())), preferred_element_type=F32)
        out = t if out is None else out + t
    return out


def _rms(x, w):
    return x * lax.rsqrt(jnp.mean(x * x, axis=-1, keepdims=True) + EPS) * w


def _head_lanes(x, h):
    blk = x[:, (h // 2) * 128:(h // 2 + 1) * 128]
    if h % 2:
        blk = pltpu.roll(blk, 64, 1)
    return blk[:, :HEAD_V]


def _stack_heads(x, hm_ref):
    return jnp.concatenate([x * hm_ref[h:h + 1, :] for h in range(N_HEADS)], axis=0)


def _stack_heads_bf(x, hm_ref):
    xb = x.astype(BF)
    return jnp.concatenate([xb * hm_ref[h:h + 1, :].astype(BF) for h in range(N_HEADS)], axis=0)


def _head_diag(o_full, hv_ref, c):
    out = None
    for h in range(N_HEADS):
        t = o_full[h * c:(h + 1) * c, :] * hv_ref[h:h + 1, :]
        out = t if out is None else out + t
    return out


def _head_norm_gate(o, gate, nw, ones_bd):
    ms = _sel_r(o * o, ones_bd, 2) * (1.0 / HEAD_V)
    return o * lax.rsqrt(ms + EPS) * nw * _silu(gate)


def _mod_row(i, tm):
    return jnp.maximum(i * tm - (N_CTX_TOK - T_LAT), 0) // T_LAT


def _cparams(sem):
    return pltpu.CompilerParams(dimension_semantics=sem, vmem_limit_bytes=VMEM_LIMIT)


@functools.lru_cache(maxsize=None)
def _scan_consts(w, reverse):
    c = CHUNK
    idx = np.arange(c)
    i = idx[:, None]
    m = idx[None, :]
    pm, sg = [], []
    s = c // 2
    while s >= 1:
        par = idx // (2 * s)
        right = (idx % (2 * s)) >= s
        same = par[:, None] == par[None, :]
        query = ~right if reverse else right
        pm.append(same & query[:, None] & (~query)[None, :])
        if s >= SMALL_LEVEL:
            sg.append(np.where(query, 1.0, -1.0))
        s //= 2
    pm.append(i == m)
    tri = (m >= i) if reverse else (m <= i)
    r4, r2 = idx % 4, idx % 2
    if not reverse:
        cf = [r2 == 1, r4 >= 2, r4 == 3, r4 == 0]
    else:
        cf = [r2 == 0, r4 <= 1, r4 == 3, r4 == 0]
    wide = lambda rows: np.repeat(np.stack(rows).astype(np.float32)[:, :, None], w, axis=2)
    pmask = np.stack([np.tile(p, (1, N_HEADS)) for p in pm]).astype(np.float32)
    return tri.astype(np.float32), wide(sg), wide(cf), pmask


@functools.lru_cache(maxsize=None)
def _head_consts(w):
    dk = w // N_HEADS
    hm = np.zeros((N_HEADS, w), np.float32)
    hv = np.zeros((N_HEADS, BRANCH_W), np.float32)
    for h in range(N_HEADS):
        hm[h, h * dk:(h + 1) * dk] = 1.0
        hv[h, h * HEAD_V:(h + 1) * HEAD_V] = 1.0
    bd = hm.T @ hv
    ones_bd = hv.T @ hv
    return hm, hv, bd, ones_bd


@functools.lru_cache(maxsize=None)
def _dn_consts():
    c = CHUNK
    idx = np.arange(c)
    i = idx[:, None]
    j = idx[None, :]
    tri = np.stack([(j <= i), (j >= i)]).astype(np.float32)
    strict = np.stack([np.tile(j < i, (1, N_HEADS)), np.tile(j > i, (1, N_HEADS))]).astype(np.float32)
    eye = np.tile(np.eye(c), (1, N_HEADS)).astype(np.float32)
    blk = np.ones((c, c), np.float32)
    exb = np.zeros((128, 2 * BRANCH_W), np.float32)
    exa = np.zeros((128, 2 * BRANCH_W), np.float32)
    for n in range(2):
        for h in range(N_HEADS):
            lo = n * BRANCH_W + h * HEAD_V
            exb[SMALL_DNB + n * N_HEADS + h, lo:lo + HEAD_V] = 1.0
            exa[SMALL_DNA + n * N_HEADS + h, lo:lo + HEAD_V] = 1.0
    return tri, strict, eye, blk, exb, exa


@functools.lru_cache(maxsize=None)
def _att_consts():
    qm = np.zeros((2 * N_HEADS, BRANCH_W), np.float32)
    for h in range(N_HEADS):
        for mp in range(2):
            lo = h * HEAD_V + mp * DF_DH
            qm[2 * h + mp, lo:lo + DF_DH] = 1.0
    return qm


def _rope_tables():
    rows = T_LAT // GRID_W
    row = jnp.repeat(jnp.arange(rows), GRID_W).astype(F32)
    col = jnp.tile(jnp.arange(GRID_W), rows).astype(F32)
    half = DF_DH // 2
    inv = ROPE_BASE ** (-jnp.arange(0, half, 2, dtype=F32) / half)

    def angles(pos):
        a = pos[:, None] * inv[None, :]
        return jnp.concatenate([a, a], axis=-1)

    ang = jnp.concatenate([angles(row), angles(col)], axis=-1)
    reps = BRANCH_W // DF_DH
    return jnp.tile(jnp.cos(ang), (1, reps)), jnp.tile(jnp.sin(ang), (1, reps))


N_BUF_MOD = 3


def _mod_kernel(c_ref, b_ref, w_hbm, o_ref, buf, sem):
    n_t = (N_MOD * D_MODEL) // TN_MOD
    n_tiles = DEPTH * n_t

    def tile_copy(k):
        l, j = divmod(k, n_t)
        slot = k % N_BUF_MOD
        return pltpu.make_async_copy(w_hbm.at[l, :, j * TN_MOD:(j + 1) * TN_MOD], buf.at[slot], sem.at[slot])

    for k in range(min(N_BUF_MOD, n_tiles)):
        tile_copy(k).start()
    a_parts = _split(_silu(c_ref[...]), 2)
    for k in range(n_tiles):
        l, j = divmod(k, n_t)
        tile_copy(k).wait()
        w_parts = _split(buf[k % N_BUF_MOD], 2)
        out = None
        for ap in a_parts:
            for wp in w_parts:
                t = jnp.dot(ap, wp, preferred_element_type=F32)
                out = t if out is None else out + t
        cols = slice(j * TN_MOD, (j + 1) * TN_MOD)
        o_ref[l, :, cols] = out + b_ref[l, :, cols]
        if k + N_BUF_MOD < n_tiles:
            tile_copy(k + N_BUF_MOD).start()


def _mod_call(c_rows, w_mod, b_mod):
    vmem = pl.BlockSpec(memory_space=pltpu.VMEM)
    return pl.pallas_call(
        _mod_kernel,
        in_specs=[vmem, vmem, pl.BlockSpec(memory_space=pl.ANY)],
        out_specs=vmem,
        out_shape=jax.ShapeDtypeStruct((DEPTH, 8, N_MOD * D_MODEL), F32),
        scratch_shapes=[pltpu.VMEM((N_BUF_MOD, D_MODEL, TN_MOD), F32), pltpu.SemaphoreType.DMA((N_BUF_MOD,))],
        compiler_params=pltpu.CompilerParams(vmem_limit_bytes=VMEM_LIMIT),
        name="mod_vectors",
    )(c_rows, b_mod.reshape(DEPTH, 1, N_MOD * D_MODEL), w_mod)


def _ffn_kernel(*refs, sub, final, split_in, split_out):
    n_x = 2 if split_in else 1
    mod_ref, nw_ref, wup_ref, wd_ref, fn_ref = refs[n_x:n_x + 5]
    outs = refs[n_x + 5:]
    is_lat = pl.program_id(0) >= N_CTX_TOK // TM_FFN
    x = jnp.where(is_lat, refs[1][...], refs[0][...]) if split_in else refs[0][...]
    sh = mod_ref[0, 3 * sub:3 * sub + 1, :]
    sc = mod_ref[0, 3 * sub + 1:3 * sub + 2, :]
    ga = mod_ref[0, 3 * sub + 2:3 * sub + 3, :]
    h = (_rms(x, nw_ref[...]) * (1.0 + sc) + sh).astype(BF)
    acc = None
    for f in range(D_FF // TF_FFN):
        lo = f * TF_FFN
        g = jnp.dot(h, wup_ref[:, lo:lo + TF_FFN], preferred_element_type=F32)
        u = jnp.dot(h, wup_ref[:, D_FF + lo:D_FF + lo + TF_FFN], preferred_element_type=F32)
        t = jnp.dot((_silu(g) * u).astype(BF), wd_ref[lo:lo + TF_FFN, :], preferred_element_type=F32)
        acc = t if acc is None else acc + t
    y = x + 0.5 * ga * acc
    if final:
        y = _rms(y, fn_ref[...])
    if split_out:
        @pl.when(jnp.logical_not(is_lat))
        def _():
            outs[0][...] = y

        @pl.when(is_lat)
        def _():
            outs[1][...] = y
    else:
        outs[0][...] = y


def _ffn_call(xs, mod_l, nw, w_in, w_down, final_w, sub, final, split_out):
    tm = TM_FFN
    n_ctx = N_CTX_TOK // tm
    ctx_map = lambda i: (jnp.minimum(i, n_ctx - 1), 0)
    lat_map = lambda i: (jnp.maximum(i - n_ctx, 0), 0)
    tile = lambda m: pl.BlockSpec((tm, D_MODEL), m)
    resident = lambda a: pl.BlockSpec(a.shape, lambda i: (0, 0), pipeline_mode=pl.Buffered(1))
    split_in = len(xs) == 2
    in_specs = [tile(ctx_map), tile(lat_map)] if split_in else [tile(lambda i: (i, 0))]
    in_specs += [pl.BlockSpec((1, N_MOD, D_MODEL), lambda i: (_mod_row(i, tm), 0, 0)),
                 pl.BlockSpec((1, D_MODEL), lambda i: (0, 0)),
                 resident(w_in), resident(w_down),
                 pl.BlockSpec((1, D_MODEL), lambda i: (0, 0))]
    if split_out:
        out_specs = [tile(ctx_map), tile(lat_map)]
        out_shape = [jax.ShapeDtypeStruct((N_CTX_TOK, D_MODEL), F32),
                     jax.ShapeDtypeStruct((N_TOK - N_CTX_TOK, D_MODEL), F32)]
    else:
        out_specs = [tile(lambda i: (i, 0))]
        out_shape = [jax.ShapeDtypeStruct((N_TOK, D_MODEL), F32)]
    return pl.pallas_call(
        functools.partial(_ffn_kernel, sub=sub, final=final, split_in=split_in, split_out=split_out),
        grid=(N_TOK // tm,),
        in_specs=in_specs,
        out_specs=out_specs,
        out_shape=out_shape,
        compiler_params=_cparams(("arbitrary",)),
        name="swiglu_half_step",
    )(*xs, mod_l, nw, w_in, w_down, final_w)


def _proj_kernel(x_ref, mod_ref, nw_ref, w_ref, o_ref):
    sh = mod_ref[0, 3:4, :]
    sc = mod_ref[0, 4:5, :]
    h = (_rms(x_ref[...], nw_ref[...]) * (1.0 + sc) + sh).astype(BF)
    o_ref[...] = jnp.dot(h, w_ref[...], preferred_element_type=F32)


def _proj_call(x, mod_l, nw, w_in_p):
    tm = TM_PROJ
    return pl.pallas_call(
        _proj_kernel,
        grid=(N_TOK // tm,),
        in_specs=[pl.BlockSpec((tm, D_MODEL), lambda i: (i, 0)),
                  pl.BlockSpec((1, N_MOD, D_MODEL), lambda i: (_mod_row(i, tm), 0, 0)),
                  pl.BlockSpec((1, D_MODEL), lambda i: (0, 0)),
                  pl.BlockSpec((D_MODEL, N_IN_PAD), lambda i: (0, 0))],
        out_specs=pl.BlockSpec((tm, N_IN_PAD), lambda i: (i, 0)),
        out_shape=jax.ShapeDtypeStruct((N_TOK, N_IN_PAD), F32),
        compiler_params=_cparams(("arbitrary",)),
        name="mixer_in_proj",
    )(x, mod_l, nw, w_in_p)


def _merge_kernel(x_ref, mod_ref, nw_ref, *rest):
    ctx_refs, lat_refs = rest[0:4], rest[4:8]
    wg_ref, wb_ref, wo_ref, o_ref = rest[8:12]
    x = x_ref[...]
    sh = mod_ref[0, 3:4, :]
    sc = mod_ref[0, 4:5, :]
    ga = mod_ref[0, 5:6, :]
    h = (_rms(x, nw_ref[...]) * (1.0 + sc) + sh).astype(BF)
    is_lat = pl.program_id(0) >= N_CTX_TOK // TM_MERGE
    mixed = None
    for n in range(4):
        gate = _sigmoid(jnp.dot(h, wg_ref[:, n * D_MODEL:(n + 1) * D_MODEL], preferred_element_type=F32))
        br = jnp.where(is_lat, lat_refs[n][...], ctx_refs[n][...])
        up = jnp.dot(br.astype(BF), wb_ref[n], preferred_element_type=F32)
        mixed = gate * up if mixed is None else mixed + gate * up
    out = jnp.dot(mixed.astype(BF), wo_ref[...], preferred_element_type=F32)
    o_ref[...] = x + ga * out


def _merge_call(x, mod_l, nw, ctx_branches, lat_branches, w_mgate, w_branch, w_out):
    tm = TM_MERGE
    n_ctx = N_CTX_TOK // tm
    cspec = pl.BlockSpec((tm, BRANCH_W), lambda i: (jnp.minimum(i, n_ctx - 1), 0))
    lspec = pl.BlockSpec((tm, BRANCH_W), lambda i: (jnp.maximum(i - n_ctx, 0), 0))
    return pl.pallas_call(
        _merge_kernel,
        grid=(N_TOK // tm,),
        in_specs=[pl.BlockSpec((tm, D_MODEL), lambda i: (i, 0)),
                  pl.BlockSpec((1, N_MOD, D_MODEL), lambda i: (_mod_row(i, tm), 0, 0)),
                  pl.BlockSpec((1, D_MODEL), lambda i: (0, 0)),
                  cspec, cspec, cspec, cspec, lspec, lspec, lspec, lspec,
                  pl.BlockSpec((D_MODEL, 4 * D_MODEL), lambda i: (0, 0)),
                  pl.BlockSpec((4, BRANCH_W, D_MODEL), lambda i: (0, 0, 0)),
                  pl.BlockSpec((D_MODEL, D_MODEL), lambda i: (0, 0))],
        out_specs=pl.BlockSpec((tm, D_MODEL), lambda i: (i, 0)),
        out_shape=jax.ShapeDtypeStruct((N_TOK, D_MODEL), F32),
        compiler_params=_cparams(("arbitrary",)),
        name="gated_merge",
    )(x, mod_l, nw, *ctx_branches, *lat_branches, w_mgate, w_branch, w_out)


def _gated_chunk(qc, kc, vc, gc, st, tri_ref, sg_ref, cf_ref, pm_ref, hm_ref, hv_ref, bdt_ref, reverse):
    c = CHUNK
    w = gc.shape[1]
    cum = _sel_l(tri_ref[...], gc, 2)
    kst = _stack_heads_bf(kc, hm_ref)
    qb = qc.astype(BF)
    a = pm_ref[N_LEVELS] * _mm_nt(qb, kst)
    lv = 0
    s = c // 2
    while s >= 1:
        if s >= SMALL_LEVEL:
            first = s if reverse else s - 1
            bnd = [jnp.broadcast_to(cum[p * 2 * s + first:p * 2 * s + first + 1, :], (2 * s, w))
                   for p in range(c // (2 * s))]
            bnd = jnp.concatenate(bnd, axis=0) if len(bnd) > 1 else bnd[0]
            dl = (cum - bnd) * sg_ref[lv]
        elif s == 2:
            dl = (gc * cf_ref[1] + pltpu.roll(gc, 1, 0) * cf_ref[2] + pltpu.roll(gc, c - 1, 0) * cf_ref[3])
        else:
            dl = gc * cf_ref[0]
        el = jnp.exp(dl).astype(BF)
        a = a + pm_ref[lv] * _mm_nt(qb * el, kst * jnp.concatenate([el] * N_HEADS, axis=0))
        lv += 1
        s //= 2
    tot = cum[0:1, :] if reverse else cum[c - 1:c, :]
    o = _mm(a, _stack_heads_bf(vc, hv_ref)) + _mm_nt(qc * jnp.exp(cum), st)
    st_new = st * jnp.exp(tot) + bdt_ref[...] * _mm_tn(vc, kc * jnp.exp(tot - cum))
    return o, st_new


def _gated_scan(q_ref, kf_ref, kb_ref, v_ref, gf_ref, gb_ref, of_ref, ob_ref, s0_ref, st_ref, cf, cb,
                t_len, w, dk, eye_ref):
    n_chunks = t_len // CHUNK

    def body(t, carry):
        new = []
        for sq in range(N_PAIR):
            rf = pl.ds(pl.multiple_of(sq * t_len + t * CHUNK, CHUNK), CHUNK)
            rb = pl.ds(pl.multiple_of(sq * t_len + (n_chunks - 1 - t) * CHUNK, CHUNK), CHUNK)
            o_f, sf = _gated_chunk(q_ref[rf, :], kf_ref[rf, :], v_ref[rf, :], gf_ref[rf, :],
                                   carry[2 * sq], *cf, False)
            o_b, sb = _gated_chunk(q_ref[rb, :], kb_ref[rb, :], v_ref[rb, :], gb_ref[rb, :],
                                   carry[2 * sq + 1], *cb, True)
            of_ref[rf, :] = o_f
            ob_ref[rb, :] = o_b
            new += [sf, sb]
        return tuple(new)

    if s0_ref is not None:
        init = tuple(s0_ref[sq, d] for sq in range(N_PAIR) for d in range(2))
    else:
        init = tuple(jnp.zeros((BRANCH_W, w), F32) for _ in range(2 * N_PAIR))
    finals = lax.fori_loop(0, n_chunks, body, init)
    if st_ref is not None:
        _zero_later_layers(st_ref)
        for sq in range(N_PAIR):
            for d in range(2):
                st = finals[2 * sq + d]
                for h in range(N_HEADS):
                    tr = _sel_tn(st[h * HEAD_V:(h + 1) * HEAD_V, :], eye_ref[...])
                    st_ref[sq, 0, d, h] = tr[h * dk:(h + 1) * dk, :]


def _gla_kernel(*refs, t_len, has_state, n_prev):
    blk_ref, small_ref, w2_ref, gb_ref, nw_ref = refs[:5]
    pos = 5 + N_SCAN_CONSTS
    cf, cb, ones_ref, eye_ref = _split_scan_consts(refs[5:pos])
    s0_ref = st_ref = None
    if has_state:
        s0_ref = refs[pos]
        pos += 1
    pos += n_prev
    out_ref = refs[pos]
    pos += 1
    if not has_state:
        st_ref = refs[pos]
        pos += 1
    q_s, gf_s, gb_s, of_s, ob_s = refs[pos:pos + 5]

    q_s[...] = blk_ref[:, 0:128] * (GLA_DK ** -0.5)
    z = _mm(small_ref[...], w2_ref[...]) + gb_ref[...]
    g = _log_sigmoid(z) * (1.0 / GLA_TAU)
    gf_s[...] = g[:, 0:128]
    gb_s[...] = g[:, 128:256]
    k_ref = blk_ref.at[:, 128:256]
    v_ref = blk_ref.at[:, 256:512]
    _gated_scan(q_s, k_ref, k_ref, v_ref, gf_s, gb_s, of_s, ob_s, s0_ref, st_ref, cf, cb,
                t_len, GLA_KW, GLA_DK, eye_ref)
    out_ref[...] = _head_norm_gate(of_s[...] + ob_s[...], blk_ref[:, 512:768], nw_ref[...], ones_ref[...])


def _const_spec(a):
    nd = a.ndim
    return pl.BlockSpec(a.shape, lambda b, _n=nd: (0,) * _n)


def _scan_const_arrays(w):
    out = []
    for reverse in (False, True):
        tri, sg, cf, pm = _scan_consts(w, reverse)
        out += [jnp.asarray(tri, BF), jnp.asarray(sg, F32), jnp.asarray(cf, F32), jnp.asarray(pm, F32)]
    hm, hv, bd, ones_bd = _head_consts(w)
    out += [jnp.asarray(hm, F32), jnp.asarray(hv, F32), jnp.asarray(bd.T, F32), jnp.asarray(ones_bd, BF),
            jnp.eye(HEAD_V, dtype=BF)]
    return tuple(out)


def _split_scan_consts(refs):
    shared = tuple(refs[8:11])
    return tuple(refs[0:4]) + shared, tuple(refs[4:8]) + shared, refs[11], refs[12]


def _zero_later_layers(ref):
    if ref.shape[1] > 1:
        ref[:, 1:] = jnp.zeros((ref.shape[0], ref.shape[1] - 1) + tuple(ref.shape[2:]), ref.dtype)


def _add_layer_outputs(tails, n_lead, lead_block, layer, prevs, in_specs, args, out_shape, out_specs):
    first_out = len(out_shape)
    for tail in tails:
        out_shape.append(jax.ShapeDtypeStruct((n_lead, DEPTH) + tuple(tail), F32))
        if prevs is None:
            assert layer == 0
            out_specs.append(pl.BlockSpec((lead_block, DEPTH) + tuple(tail),
                                          lambda b, _nz=len(tail): (b,) + (0,) * (_nz + 1)))
        else:
            out_specs.append(pl.BlockSpec((lead_block, 1) + tuple(tail),
                                          lambda b, _nz=len(tail): (b, layer) + (0,) * _nz))
    aliases = {}
    for k, p in enumerate(prevs or ()):
        in_specs.append(pl.BlockSpec(memory_space=pl.ANY))
        aliases[len(args)] = first_out + k
        args.append(p)
    return aliases


def _gla_call(proj, w2bd, gbias, nw, s0, t_len, n_seq, row_blk0, layer=0, prev=None):
    has_state = s0 is not None
    consts = (w2bd, gbias, nw) + _scan_const_arrays(GLA_KW)
    n_rows = N_PAIR * t_len
    in_specs = [pl.BlockSpec((n_rows, 768), lambda b: (row_blk0 + b, 0)),
                pl.BlockSpec((n_rows, 128), lambda b: (row_blk0 + b, COL_SMALL_BLOCK))]
    in_specs += [_const_spec(a) for a in consts]
    args = [proj, proj, *consts]
    if has_state:
        in_specs.append(pl.BlockSpec((N_PAIR, 2, BRANCH_W, GLA_KW), lambda b: (b, 0, 0, 0)))
        args.append(s0)
    out_shape = [jax.ShapeDtypeStruct((n_seq * t_len, BRANCH_W), F32)]
    out_specs = [pl.BlockSpec((n_rows, BRANCH_W), lambda b: (b, 0))]
    aliases = {}
    if not has_state:
        aliases = _add_layer_outputs([(2, N_HEADS, GLA_DK, HEAD_V)], n_seq, N_PAIR, layer, prev,
                                     in_specs, args, out_shape, out_specs)
    return pl.pallas_call(
        functools.partial(_gla_kernel, t_len=t_len, has_state=has_state, n_prev=len(aliases)),
        grid=(n_seq // N_PAIR,),
        in_specs=in_specs,
        out_specs=out_specs,
        out_shape=out_shape,
        input_output_aliases=aliases,
        scratch_shapes=[pltpu.VMEM((n_rows, GLA_KW), F32)] * 3 + [pltpu.VMEM((n_rows, BRANCH_W), F32)] * 2,
        compiler_params=_cparams(("arbitrary",)),
        name="gla_mixer",
    )(*args)


def _hgrn_kernel(*refs, t_len, has_state, layer, n_prev):
    q_ref, f_ref, v_ref, gate_ref, lbl_ref, nw_ref = refs[:6]
    pos = 6 + N_SCAN_CONSTS
    cf, cb, ones_ref, eye_ref = _split_scan_consts(refs[6:pos])
    s0_ref = st_ref = None
    if has_state:
        s0_ref = refs[pos]
        pos += 1
    pos += n_prev
    out_ref = refs[pos]
    pos += 1
    if not has_state:
        st_ref = refs[pos]
        pos += 1
    q_s, kf_s, kb_s, gf_s, gb_s, of_s, ob_s = refs[pos:pos + 7]

    lg = lbl_ref[...]
    mx = jnp.max(lg, axis=0, keepdims=True)
    ex = jnp.exp(lg - mx)
    p = ex / jnp.sum(ex, axis=0, keepdims=True)
    lb = jnp.sum(p[0:layer + 1], axis=0, keepdims=True) - p[0:1]

    q_s[...] = _silu(q_ref[...]) * (HG_DK ** -0.5)
    f = lb + (1.0 - lb) * _sigmoid(f_ref[...])
    kf_s[...] = 1.0 - f[:, 0:HG_W]
    kb_s[...] = 1.0 - f[:, HG_W:2 * HG_W]
    lf = jnp.log(f)
    gf_s[...] = lf[:, 0:HG_W]
    gb_s[...] = lf[:, HG_W:2 * HG_W]
    _gated_scan(q_s, kf_s, kb_s, v_ref, gf_s, gb_s, of_s, ob_s, s0_ref, st_ref, cf, cb,
                t_len, HG_W, HG_DK, eye_ref)
    out_ref[...] = _head_norm_gate(of_s[...] + ob_s[...], gate_ref[...], nw_ref[...], ones_ref[...])


def _hgrn_call(proj, lb_logits, nw, s0, t_len, n_seq, row_blk0, layer, prev=None):
    has_state = s0 is not None
    consts = (lb_logits, nw) + _scan_const_arrays(HG_W)
    n_rows = N_PAIR * t_len
    in_specs = [pl.BlockSpec((n_rows, 256), lambda b: (row_blk0 + b, 7)),
                pl.BlockSpec((n_rows, 512), lambda b: (row_blk0 + b, 4)),
                pl.BlockSpec((n_rows, 256), lambda b: (row_blk0 + b, 10)),
                pl.BlockSpec((n_rows, 256), lambda b: (row_blk0 + b, 11))]
    in_specs += [_const_spec(a) for a in consts]
    args = [proj, proj, proj, proj, *consts]
    if has_state:
        in_specs.append(pl.BlockSpec((N_PAIR, 2, BRANCH_W, HG_W), lambda b: (b, 0, 0, 0)))
        args.append(s0)
    out_shape = [jax.ShapeDtypeStruct((n_seq * t_len, BRANCH_W), F32)]
    out_specs = [pl.BlockSpec((n_rows, BRANCH_W), lambda b: (b, 0))]
    aliases = {}
    if not has_state:
        aliases = _add_layer_outputs([(2, N_HEADS, HG_DK, HEAD_V)], n_seq, N_PAIR, layer, prev,
                                     in_specs, args, out_shape, out_specs)
    return pl.pallas_call(
        functools.partial(_hgrn_kernel, t_len=t_len, has_state=has_state, layer=layer, n_prev=len(aliases)),
        grid=(n_seq // N_PAIR,),
        in_specs=in_specs,
        out_specs=out_specs,
        out_shape=out_shape,
        input_output_aliases=aliases,
        scratch_shapes=[pltpu.VMEM((n_rows, HG_W), F32)] * 5 + [pltpu.VMEM((n_rows, BRANCH_W), F32)] * 2,
        compiler_params=_cparams(("arbitrary",)),
        name="hgrn2_mixer",
    )(*args)


N_PAIR = 2
SOLVE_JB = 8
SOLVE_IB = 16


def _dn_solve_kernel(lt_ref, vb_ref, kb_ref, u_ref, w_ref):
    _dn_substitute(pl.program_id(1), lt_ref.at[0], vb_ref.at[0], kb_ref.at[0], u_ref.at[0], w_ref.at[0])


def _dn_substitute(rev, l_ref, vb_ref, kb_ref, u_ref, w_ref):
    c = CHUNK
    u_ref[...] = jnp.zeros_like(u_ref)
    w_ref[...] = jnp.zeros_like(w_ref)

    def outer(t, carry):
        i = t + rev * (c - 1 - 2 * t)

        def inner(jb, acc):
            au, aw = acc
            j0 = pl.multiple_of(jb * SOLVE_JB, SOLVE_JB)
            for r in range(SOLVE_JB):
                coef = l_ref[i, pl.ds(j0 + r, 1), :]
                au = au - coef * u_ref[j0 + r]
                aw = aw - coef * w_ref[j0 + r]
            return au, aw

        blk = lax.shift_right_logical(i, 3)
        lo = rev * blk
        hi = blk + 1 + rev * (c // SOLVE_JB - blk - 1)
        au, aw = lax.fori_loop(lo, hi, inner, (vb_ref[i], kb_ref[i]))
        u_ref[i] = au
        w_ref[i] = aw
        return carry

    lax.fori_loop(0, c, outer, 0)


def _dn_solve_packed_kernel(lt_ref, vb_ref, kb_ref, u_ref, w_ref, lp_s, vp_s, kp_s, up_s, wp_s):
    n_chunk = lt_ref.shape[-1]
    n_ib = CHUNK // SOLVE_IB
    s = pl.program_id(1)

    @pl.when(s < n_ib)
    def _():
        for ii in range(SOLVE_IB):
            for src, dst in ((lt_ref, lp_s), (vb_ref, vp_s), (kb_ref, kp_s)):
                dst[s * SOLVE_IB + ii] = jnp.concatenate(
                    [src[0, ii, h * HEAD_V:(h + 1) * HEAD_V, :] for h in range(N_HEADS)], axis=1)

    @pl.when(s == n_ib - 1)
    def _():
        _dn_substitute(pl.program_id(0), lp_s, vp_s, kp_s, up_s, wp_s)

    @pl.when(s >= n_ib)
    def _():
        for ii in range(SOLVE_IB):
            for src, dst in ((up_s, u_ref), (wp_s, w_ref)):
                row = src[(s - n_ib) * SOLVE_IB + ii]
                for h in range(N_HEADS):
                    dst[0, ii, h * HEAD_V:(h + 1) * HEAD_V, :] = row[:, h * n_chunk:(h + 1) * n_chunk]


def _dn_solve_call(lt, vbt, kbt):
    n_chunk = lt.shape[-1]
    shape = jax.ShapeDtypeStruct((2, CHUNK, BRANCH_W, n_chunk), F32)
    if n_chunk * N_HEADS == 128:
        n_ib = CHUNK // SOLVE_IB
        blk = (1, SOLVE_IB, BRANCH_W, n_chunk)
        spec = pl.BlockSpec(blk, lambda d, s: (d, jnp.minimum(s, n_ib - 1), 0, 0))
        ospec = pl.BlockSpec(blk, lambda d, s: (d, jnp.maximum(s - n_ib, 0), 0, 0))
        return pl.pallas_call(
            _dn_solve_packed_kernel,
            grid=(2, 2 * n_ib),
            in_specs=[spec, spec, spec],
            out_specs=[ospec, ospec],
            out_shape=[shape, shape],
            scratch_shapes=[pltpu.VMEM((CHUNK, HEAD_V, 128), F32)] * 5,
            compiler_params=_cparams(("arbitrary", "arbitrary")),
            name="deltanet_solve_packed",
        )(lt, vbt, kbt)
    spec = pl.BlockSpec((1, CHUNK, HEAD_V, n_chunk), lambda h, d: (d, 0, h, 0))
    return pl.pallas_call(
        _dn_solve_kernel,
        grid=(N_HEADS, 2),
        in_specs=[spec, spec, spec],
        out_specs=[spec, spec],
        out_shape=[shape, shape],
        compiler_params=_cparams(("arbitrary", "arbitrary")),
        name="deltanet_solve",
    )(lt, vbt, kbt)


def _chunks_to_lanes(a):
    return a.reshape(2, a.shape[1] // CHUNK, CHUNK, BRANCH_W).transpose(0, 2, 3, 1)


def _chunks_from_lanes(a):
    return a.transpose(0, 3, 1, 2).reshape(2, -1, BRANCH_W)


def _dn_build_kernel(x_ref, small_ref, cw_ref, alog_ref, dtb_ref,
                     tri_ref, strict_ref, eye_ref, blk_ref, exb_ref, exa_ref, hm_ref, ones_ref,
                     q_ref, k_ref, d_ref, l_ref, vb_ref, kb_ref, qkd_ref, v_ref, be_s, ge_s, *, t_len):
    n_rows = N_PAIR * t_len

    x = x_ref[...]
    row = lax.broadcasted_iota(jnp.int32, (n_rows, 1), 0) % t_len
    x_prev = jnp.where(row == 0, 0.0, pltpu.roll(x, 1, 0))
    x_next = jnp.where(row == t_len - 1, 0.0, pltpu.roll(x, n_rows - 1, 0))
    y = _silu(x_prev * cw_ref[0:1, :] + x * cw_ref[1:2, :] + x_next * cw_ref[2:3, :])
    cq, ck = y[:, 0:256], y[:, 256:512]
    v_ref[...] = y[:, 512:768]
    q_ref[...] = cq * lax.rsqrt(_sel_r(cq * cq, ones_ref[...], 2) + EPS) * (DN_DK ** -0.5)
    k_ref[...] = ck * lax.rsqrt(_sel_r(ck * ck, ones_ref[...], 2) + EPS)
    sm = small_ref[...]
    be_s[...] = _sel_r(_sigmoid(sm), exb_ref[...])
    ge_s[...] = _sel_r(-jnp.exp(alog_ref[...]) * _softplus(sm + dtb_ref[...]), exa_ref[...])

    def body(ci, carry):
        rows = pl.ds(pl.multiple_of(ci * CHUNK, CHUNK), CHUNK)
        qc, kc, vc = q_ref[rows, :], k_ref[rows, :], v_ref[rows, :]
        kst = _stack_heads(kc, hm_ref)
        kk = _mm_nt(kc, kst)
        qk = _mm_nt(qc, kst)
        for d in range(2):
            bexp = be_s[rows, d * BRANCH_W:(d + 1) * BRANCH_W]
            dexp = _sel_l(tri_ref[d], ge_s[rows, d * BRANCH_W:(d + 1) * BRANCH_W])
            drow = _sel_l(blk_ref[...], dexp * eye_ref[...])
            dec_s = jnp.exp(jnp.where(strict_ref[d] > 0.5, dexp - drow, -1e30))
            d_ref[d, rows, :] = dexp
            l_ref[d, rows, :] = bexp * kk * dec_s
            vb_ref[d, rows, :] = vc * bexp
            kb_ref[d, rows, :] = kc * bexp * jnp.exp(dexp)
            qkd_ref[d, rows, :] = (qk * (dec_s + eye_ref[...])).astype(BF)
        return carry

    lax.fori_loop(0, n_rows // CHUNK, body, 0)


def _dn_build_call(proj, conv_w, alog_row, dtb_row, t_len, n_seq, row_blk0):
    tri, strict, eye, blk, exb, exa = _dn_consts()
    tri, blk, exb, exa = (jnp.asarray(a, BF) for a in (tri, blk, exb, exa))
    strict, eye = jnp.asarray(strict, F32), jnp.asarray(eye, F32)
    hm, _, _, ones_bd = _head_consts(BRANCH_W)
    hm, ones_bd = jnp.asarray(hm, F32), jnp.asarray(ones_bd, BF)
    n_rows = N_PAIR * t_len
    n_tok = n_seq * t_len
    consts = (conv_w, alog_row, dtb_row, tri, strict, eye, blk, exb, exa, hm, ones_bd)
    in_specs = [pl.BlockSpec((n_rows, 768), lambda b: (row_blk0 + b, 1)),
                pl.BlockSpec((n_rows, 128), lambda b: (row_blk0 + b, COL_SMALL_BLOCK))]
    in_specs += [_const_spec(a) for a in consts]
    tok_spec = pl.BlockSpec((n_rows, BRANCH_W), lambda b: (b, 0))
    dir_spec = pl.BlockSpec((2, n_rows, BRANCH_W), lambda b: (0, b, 0))
    tok = jax.ShapeDtypeStruct((n_tok, BRANCH_W), F32)
    per_dir = jax.ShapeDtypeStruct((2, n_tok, BRANCH_W), F32)
    return pl.pallas_call(
        functools.partial(_dn_build_kernel, t_len=t_len),
        grid=(n_seq // N_PAIR,),
        in_specs=in_specs,
        out_specs=[tok_spec] * 2 + [dir_spec] * 5,
        out_shape=[tok] * 2 + [per_dir] * 4 + [jax.ShapeDtypeStruct((2, n_tok, BRANCH_W), BF)],
        scratch_shapes=[pltpu.VMEM((n_rows, BRANCH_W), F32)] + [pltpu.VMEM((n_rows, 2 * BRANCH_W), F32)] * 2,
        compiler_params=_cparams(("arbitrary",)),
        name="deltanet_build",
    )(proj, proj, *consts)


def _dn_scan_kernel(*refs, t_len, has_state, n_prev):
    (q_ref, k_ref, d_ref, u_ref, w_ref, qkd_ref, gate_ref, nw_ref, hv_ref, bd_ref, ones_ref) = refs[:11]
    pos = 11
    if has_state:
        s0_ref = refs[pos]
        pos += 1
    pos += n_prev
    out_ref = refs[pos]
    pos += 1
    if not has_state:
        st_ref = refs[pos]
        pos += 1
    of_s, ob_s = refs[pos:pos + 2]
    c = CHUNK
    n_chunks = t_len // c

    def step(s, rows, d, o_s):
        qc, kc = q_ref[rows, :], k_ref[rows, :]
        dexp = d_ref[d, rows, :]
        v_new = u_ref[d, rows, :] - _mm(w_ref[d, rows, :], s)
        o_s[rows, :] = _mm(qc * jnp.exp(dexp), s) + _mm(qkd_ref[d, rows, :], _stack_heads(v_new, hv_ref))
        dl = dexp[0:1, :] if d == 1 else dexp[c - 1:c, :]
        return s * jnp.exp(dl) + bd_ref[...] * _mm_tn(kc * jnp.exp(dl - dexp), v_new)

    def body(t, carry):
        new = []
        for sq in range(N_PAIR):
            rf = pl.ds(pl.multiple_of(sq * t_len + t * c, c), c)
            rb = pl.ds(pl.multiple_of(sq * t_len + (n_chunks - 1 - t) * c, c), c)
            new.append(step(carry[2 * sq], rf, 0, of_s))
            new.append(step(carry[2 * sq + 1], rb, 1, ob_s))
        return tuple(new)

    if has_state:
        init = tuple(s0_ref[sq, d] for sq in range(N_PAIR) for d in range(2))
    else:
        init = tuple(jnp.zeros((BRANCH_W, BRANCH_W), F32) for _ in range(2 * N_PAIR))
    finals = lax.fori_loop(0, n_chunks, body, init)
    out_ref[...] = _head_norm_gate(of_s[...] + ob_s[...], gate_ref[...], nw_ref[...], ones_ref[...])
    if not has_state:
        _zero_later_layers(st_ref)
        for sq in range(N_PAIR):
            for d in range(2):
                for h in range(N_HEADS):
                    st_ref[sq, 0, d, h] = _head_lanes(finals[2 * sq + d][h * DN_DK:(h + 1) * DN_DK, :], h)


def _dn_scan_call(proj, q, k, dd, u, w, qkd, nw, s0, t_len, n_seq, row_blk0, layer=0, prev=None):
    _, hv, bd, ones_bd = _head_consts(BRANCH_W)
    hv, bd, ones_bd = jnp.asarray(hv, F32), jnp.asarray(bd, F32), jnp.asarray(ones_bd, BF)
    has_state = s0 is not None
    n_rows = N_PAIR * t_len
    consts = (nw, hv, bd, ones_bd)
    tok_spec = pl.BlockSpec((n_rows, BRANCH_W), lambda b: (b, 0))
    dir_spec = pl.BlockSpec((2, n_rows, BRANCH_W), lambda b: (0, b, 0))
    in_specs = [tok_spec, tok_spec, dir_spec, dir_spec, dir_spec, dir_spec,
                pl.BlockSpec((n_rows, 256), lambda b: (row_blk0 + b, 6))]
    in_specs += [_const_spec(a) for a in consts]
    args = [q, k, dd, u, w, qkd, proj, *consts]
    if has_state:
        in_specs.append(pl.BlockSpec((N_PAIR, 2, BRANCH_W, BRANCH_W), lambda b: (b, 0, 0, 0)))
        args.append(s0)
    out_shape = [jax.ShapeDtypeStruct((n_seq * t_len, BRANCH_W), F32)]
    out_specs = [pl.BlockSpec((n_rows, BRANCH_W), lambda b: (b, 0))]
    aliases = {}
    if not has_state:
        aliases = _add_layer_outputs([(2, N_HEADS, DN_DK, HEAD_V)], n_seq, N_PAIR, layer, prev,
                                     in_specs, args, out_shape, out_specs)
    return pl.pallas_call(
        functools.partial(_dn_scan_kernel, t_len=t_len, has_state=has_state, n_prev=len(aliases)),
        grid=(n_seq // N_PAIR,),
        in_specs=in_specs,
        out_specs=out_specs,
        out_shape=out_shape,
        input_output_aliases=aliases,
        scratch_shapes=[pltpu.VMEM((n_rows, BRANCH_W), F32)] * 2,
        compiler_params=_cparams(("arbitrary",)),
        name="deltanet_scan",
    )(*args)


def _rope(x, cos, sin):
    lane = lax.broadcasted_iota(jnp.int32, x.shape, 1) % 16
    n = x.shape[1]
    xrot = jnp.where(lane < 8, -pltpu.roll(x, n - 8, 1), pltpu.roll(x, 8, 1))
    return x * cos + xrot * sin


def _att_kernel(*refs, t_len, lat, lam_init, n_prev):
    blk_ref, lam_ref, nw_ref, qm_ref, hv_ref, ones_ref = refs[:6]
    pos = 6
    if lat:
        cos_ref, sin_ref, ck_ref, cv_ref = refs[pos:pos + 4]
        pos += 4
    pos += n_prev
    out_ref = refs[pos]
    pos += 1
    if not lat:
        nk_ref, nv_ref = refs[pos:pos + 2]
        pos += 2
    if lat:
        q_s, k_s, v_s = refs[pos:pos + 3]

    lv = lam_ref[...]
    lam = (jnp.exp(jnp.sum(lv[0:1] * lv[1:2], axis=1, keepdims=True))
           - jnp.exp(jnp.sum(lv[2:3] * lv[3:4], axis=1, keepdims=True)) + lam_init)
    q = blk_ref[:, 0:256]
    k = blk_ref[:, 256:512]
    v = blk_ref[:, 512:768]
    if lat:
        cos, sin = cos_ref[...], sin_ref[...]
        q_s[...] = _rope(q, cos, sin)
        k_s[0:PAST_LEN, :] = ck_ref[0]
        k_s[PAST_LEN:PAST_LEN + t_len, :] = _rope(k, cos, sin)
        v_s[0:PAST_LEN, :] = cv_ref[0]
        v_s[PAST_LEN:PAST_LEN + t_len, :] = v
        keys = k_s[...].astype(BF)
        vals = v_s[...].astype(BF)
    else:
        keys = k.astype(BF)
        vals = v.astype(BF)
        _zero_later_layers(nk_ref)
        _zero_later_layers(nv_ref)
        for h in range(N_HEADS):
            nk_ref[0, 0, h] = _head_lanes(k, h)
            nv_ref[0, 0, h] = _head_lanes(v, h)
    tq = TQ_ATT
    scale = DF_DH ** -0.5
    for qi in range(t_len // tq):
        qt = q_s[qi * tq:(qi + 1) * tq, :] if lat else q[qi * tq:(qi + 1) * tq, :]
        qs = jnp.concatenate([qt * qm_ref[r:r + 1, :] for r in range(2 * N_HEADS)], axis=0)
        s = _mm_nt(qs, keys) * scale
        s = jnp.exp(s - jnp.max(s, axis=1, keepdims=True))
        p = s / jnp.sum(s, axis=1, keepdims=True)
        a = jnp.concatenate(
            [p[(2 * h) * tq:(2 * h + 1) * tq] - lam * p[(2 * h + 1) * tq:(2 * h + 2) * tq]
             for h in range(N_HEADS)], axis=0)
        o = _head_diag(_mm(a, vals), hv_ref, tq)
        ms = _sel_r(o * o, ones_ref[...], 2) * (1.0 / HEAD_V)
        out_ref[qi * tq:(qi + 1) * tq, :] = o * lax.rsqrt(ms + EPS) * nw_ref[...] * (1.0 - lam_init)


def _att_call(proj, lam_p, nw, cache_k, cache_v, rope, t_len, n_seq, row_blk0, lam_init, layer=0, prev=None):
    qm = jnp.asarray(_att_consts(), F32)
    _, hv, _, ones_bd = _head_consts(BRANCH_W)
    hv, ones_bd = jnp.asarray(hv, F32), jnp.asarray(ones_bd, BF)
    lat = cache_k is not None
    consts = (lam_p, nw, qm, hv, ones_bd)
    in_specs = [pl.BlockSpec((t_len, 768), lambda b: (row_blk0 + b, 4))]
    in_specs += [_const_spec(a) for a in consts]
    args = [proj, *consts]
    out_shape = [jax.ShapeDtypeStruct((n_seq * t_len, BRANCH_W), F32)]
    out_specs = [pl.BlockSpec((t_len, BRANCH_W), lambda b: (b, 0))]
    scratch = []
    if lat:
        cos, sin = rope
        in_specs += [_const_spec(cos), _const_spec(sin),
                     pl.BlockSpec((1, PAST_LEN, BRANCH_W), lambda b: (b, 0, 0)),
                     pl.BlockSpec((1, PAST_LEN, BRANCH_W), lambda b: (b, 0, 0))]
        args += [cos, sin, cache_k, cache_v]
        scratch = [pltpu.VMEM((t_len, BRANCH_W), F32),
                   pltpu.VMEM((PAST_LEN + t_len, BRANCH_W), F32),
                   pltpu.VMEM((PAST_LEN + t_len, BRANCH_W), F32)]
    aliases = {}
    if not lat:
        aliases = _add_layer_outputs([(N_HEADS, t_len, HEAD_V)] * 2, n_seq, 1, layer, prev,
                                     in_specs, args, out_shape, out_specs)
    return pl.pallas_call(
        functools.partial(_att_kernel, t_len=t_len, lat=lat, lam_init=lam_init, n_prev=len(aliases)),
        grid=(n_seq,),
        in_specs=in_specs,
        out_specs=out_specs,
        out_shape=out_shape,
        input_output_aliases=aliases,
        scratch_shapes=scratch,
        compiler_params=_cparams(("arbitrary",)),
        name="diff_attention",
    )(*args)


def _block_diag_states(st, dk, transposed):
    eye = jnp.eye(N_HEADS, dtype=st.dtype)
    b = st.shape[0]
    if transposed:
        return jnp.einsum('bnhde,hg->bnhegd', st, eye).reshape(b, 2, N_HEADS * HEAD_V, N_HEADS * dk)
    return jnp.einsum('bnhde,hg->bnhdge', st, eye).reshape(b, 2, N_HEADS * dk, N_HEADS * HEAD_V)


def _in_perm():
    offs = np.concatenate([[0], np.cumsum(IN_ORIG)])
    seg = lambda a, b: np.arange(offs[a], offs[b])
    return np.concatenate([seg(0, 4), seg(5, 6), seg(8, 9), seg(9, 16), seg(4, 5), seg(6, 8)])


def kernel(x_prompt, x_sample, cache_diff_k, cache_diff_v, state_gla, state_dn, state_hgrn, c, c_ctx,
           norm_w, w_mod, b_mod, ffn1_in, ffn1_down, ffn2_in, ffn2_down, w_in, gla_w2, gla_b, gla_norm,
           dn_conv, dn_a_log, dn_dt_bias, dn_norm, hg_lb_logits, hg_norm, diff_lambda, diff_norm,
           w_branch, w_mgate, w_out, final_norm):
    xs = (x_prompt.reshape(N_CTX_TOK, D_MODEL), x_sample.reshape(-1, D_MODEL))
    c_rows = jnp.concatenate([c_ctx[None, :], c, jnp.zeros((8 - 1 - N_LAT_SEQ, D_MODEL), F32)], axis=0)
    mod = _mod_call(c_rows, w_mod, b_mod).reshape(DEPTH, 8, N_MOD, D_MODEL)
    rope = _rope_tables()
    perm = _in_perm()
    lb_logits = hg_lb_logits.reshape(DEPTH, 2 * HG_W)
    lat_blk = N_CTX_TOK // T_LAT
    tile4 = lambda a: jnp.tile(a, N_HEADS)[None, :]
    fin = final_norm[None, :]
    st_a = st_b = st_c = new_kv = None
    prev1 = lambda a: None if a is None else (a,)
    for l in range(DEPTH):
        mod_l = mod[l]
        lam_init = 0.8 - 0.6 * math.exp(-0.3 * l)
        w_in_p = jnp.pad(w_in[l][:, perm], ((0, 0), (0, N_IN_PAD - N_IN))).astype(BF)
        w2bd = jnp.zeros((128, 2 * GLA_KW), F32)
        w2bd = w2bd.at[0:GLA_LOWRANK, 0:GLA_KW].set(gla_w2[l, 0])
        w2bd = w2bd.at[GLA_LOWRANK:2 * GLA_LOWRANK, GLA_KW:].set(gla_w2[l, 1]).astype(BF)
        gbias = gla_b[l].reshape(1, 2 * GLA_KW)
        alog_row = jnp.zeros((1, 128), F32).at[0, SMALL_DNA:SMALL_DNA + 8].set(dn_a_log[l].reshape(-1))
        dtb_row = jnp.zeros((1, 128), F32).at[0, SMALL_DNA:SMALL_DNA + 8].set(dn_dt_bias[l].reshape(-1))

        (x,) = _ffn_call(xs, mod_l, norm_w[l, 0][None, :], ffn1_in[l].astype(BF), ffn1_down[l].astype(BF),
                         fin, 0, False, False)
        proj = _proj_call(x, mod_l, norm_w[l, 1][None, :], w_in_p)

        a_c, st_a = _gla_call(proj, w2bd, gbias, tile4(gla_norm[l]), None, T_CTX, N_CTX_SEQ, 0, l, prev1(st_a))
        dn_lat_blk = N_CTX_TOK // (N_PAIR * T_LAT)
        q_c, k_c, dd_c, lw_c, vb_c, kb_c, qkd_c = _dn_build_call(proj, dn_conv[l], alog_row, dtb_row,
                                                                T_CTX, N_CTX_SEQ, 0)
        q_l, k_l, dd_l, lw_l, vb_l, kb_l, qkd_l = _dn_build_call(proj, dn_conv[l], alog_row, dtb_row,
                                                                T_LAT, N_LAT_SEQ, dn_lat_blk)
        u_c, w_c = (_chunks_from_lanes(a) for a in _dn_solve_call(
            _chunks_to_lanes(lw_c), _chunks_to_lanes(vb_c), _chunks_to_lanes(kb_c)))
        u_l, w_l = (_chunks_from_lanes(a) for a in _dn_solve_call(
            _chunks_to_lanes(lw_l), _chunks_to_lanes(vb_l), _chunks_to_lanes(kb_l)))
        b_c, st_b = _dn_scan_call(proj, q_c, k_c, dd_c, u_c, w_c, qkd_c, tile4(dn_norm[l]), None,
                                  T_CTX, N_CTX_SEQ, 0, l, prev1(st_b))
        c_c, st_c = _hgrn_call(proj, lb_logits, tile4(hg_norm[l]), None, T_CTX, N_CTX_SEQ, 0, l, prev1(st_c))
        d_c, *new_kv = _att_call(proj, diff_lambda[l], tile4(diff_norm[l]), None, None, None,
                                 T_CTX, N_CTX_SEQ, 0, lam_init, l, new_kv)
        ck = cache_diff_k[:, l].transpose(0, 2, 1, 3).reshape(N_LAT_SEQ, PAST_LEN, BRANCH_W)
        cv = cache_diff_v[:, l].transpose(0, 2, 1, 3).reshape(N_LAT_SEQ, PAST_LEN, BRANCH_W)
        (a_l,) = _gla_call(proj, w2bd, gbias, tile4(gla_norm[l]),
                           _block_diag_states(state_gla[:, l], GLA_DK, True), T_LAT, N_LAT_SEQ, dn_lat_blk)
        (b_l,) = _dn_scan_call(proj, q_l, k_l, dd_l, u_l, w_l, qkd_l, tile4(dn_norm[l]),
                               _block_diag_states(state_dn[:, l], DN_DK, False), T_LAT, N_LAT_SEQ, dn_lat_blk)
        (c_l,) = _hgrn_call(proj, lb_logits, tile4(hg_norm[l]),
                            _block_diag_states(state_hgrn[:, l], HG_DK, True), T_LAT, N_LAT_SEQ, dn_lat_blk, l)
        (d_l,) = _att_call(proj, diff_lambda[l], tile4(diff_norm[l]), ck, cv, rope,
                           T_LAT, N_LAT_SEQ, lat_blk, lam_init)

        x = _merge_call(x, mod_l, norm_w[l, 1][None, :], (a_c, b_c, c_c, d_c), (a_l, b_l, c_l, d_l),
                        w_mgate[l].astype(BF), w_branch[l].astype(BF), w_out[l].astype(BF))
        xs = _ffn_call((x,), mod_l, norm_w[l, 2][None, :], ffn2_in[l].astype(BF), ffn2_down[l].astype(BF),
                       fin, 2, l == DEPTH - 1, l == DEPTH - 1)
    y_prompt = xs[0].reshape(N_CTX_SEQ, T_CTX, D_MODEL)
    y_sample = xs[1].reshape(N_LAT_SEQ, T_LAT, D_MODEL)
    return (y_prompt, y_sample, new_kv[0], new_kv[1], st_a, st_b, st_c)
```

```python
import functools
import math

import numpy as np
import jax
import jax.numpy as jnp
from jax import lax
from jax.experimental import pallas as pl
from jax.experimental.pallas import tpu as pltpu

F32 = jnp.float32
BF = jnp.bfloat16

D_MODEL = 1024
N_CTX_SEQ = 32
T_CTX = 256
DEPTH = 2
N_LAT_SEQ = 2
T_LAT = 1024
PAST_LEN = 512
GRID_W = 64
N_HEADS = 4
BRANCH_W = 256
HEAD_V = 64
GLA_DK = 32
GLA_KW = 128
GLA_LOWRANK = 16
GLA_TAU = 16.0
DN_DK = 64
HG_DK = 64
HG_W = 256
DF_DH = 32
ROPE_BASE = 10000.0
D_FF = 2816
N_MOD = 9
CHUNK = 64
EPS = 1e-6
N_CTX_TOK = N_CTX_SEQ * T_CTX
N_TOK = N_CTX_TOK + N_LAT_SEQ * T_LAT
N_LEVELS = 6
SMALL_LEVEL = 4
N_SCAN_CONSTS = 13

IN_ORIG = (128, 128, 256, 256, 32, 768, 8, 8, 256, 256, 512, 256, 256, 256, 256, 256)
N_IN = sum(IN_ORIG)
N_IN_PAD = 3968
COL_SMALL_BLOCK = 30
SMALL_LR = 0
SMALL_DNB = 32
SMALL_DNA = 40

VMEM_LIMIT = 56 * 1024 * 1024

TM_FFN = 512
TF_FFN = 1408
TM_PROJ = 512
TM_MERGE = 512
TN_MOD = 2304
TQ_ATT = 256


def _silu(x):
    return x * (1.0 / (1.0 + jnp.exp(-x)))


def _sigmoid(x):
    return 1.0 / (1.0 + jnp.exp(-x))


def _softplus(x):
    return jnp.maximum(x, 0.0) + jnp.log(1.0 + jnp.exp(-jnp.abs(x)))


def _log_sigmoid(x):
    return -_softplus(-x)


def _mm(a, b):
    return jnp.dot(a.astype(BF), b.astype(BF), preferred_element_type=F32)


def _mm_nt(a, b):
    return lax.dot_general(a.astype(BF), b.astype(BF), (((1,), (1,)), ((), ())),
                           preferred_element_type=F32)


def _mm_tn(a, b):
    return lax.dot_general(a.astype(BF), b.astype(BF), (((0,), (0,)), ((), ())),
                           preferred_element_type=F32)


def _split(x, n):
    parts = []
    r = x
    for i in range(n):
        p = r.astype(BF)
        parts.append(p)
        if i + 1 < n:
            r = r - p.astype(F32)
    return parts


def _sel_l(m01, x, n=3):
    out = None
    for p in _split(x, n):
        t = jnp.dot(m01, p, preferred_element_type=F32)
        out = t if out is None else out + t
    return out


def _sel_r(x, m01, n=3):
    out = None
    for p in _split(x, n):
        t = jnp.dot(p, m01, preferred_element_type=F32)
        out = t if out is None else out + t
    return out


def _sel_tn(x, m01, n=3):
    out = None
    for p in _split(x, n):
        t = lax.dot_general(p, m01, (((0,), (0,)), ((), ())), preferred_element_type=F32)
        out = t if out is None else out + t
    return out


def _rms(x, w):
    return x * lax.rsqrt(jnp.mean(x * x, axis=-1, keepdims=True) + EPS) * w


def _head_lanes(x, h):
    blk = x[:, (h // 2) * 128:(h // 2 + 1) * 128]
    if h % 2:
        blk = pltpu.roll(blk, 64, 1)
    return blk[:, :HEAD_V]


def _stack_heads(x, hm_ref):
    return jnp.concatenate([x * hm_ref[h:h + 1, :] for h in range(N_HEADS)], axis=0)


def _stack_heads_bf(x, hm_ref):
    xb = x.astype(BF)
    return jnp.concatenate([xb * hm_ref[h:h + 1, :].astype(BF) for h in range(N_HEADS)], axis=0)


def _head_diag(o_full, hv_ref, c):
    out = None
    for h in range(N_HEADS):
        t = o_full[h * c:(h + 1) * c, :] * hv_ref[h:h + 1, :]
        out = t if out is None else out + t
    return out


def _head_norm_gate(o, gate, nw, ones_bd):
    ms = _sel_r(o * o, ones_bd, 2) * (1.0 / HEAD_V)
    return o * lax.rsqrt(ms + EPS) * nw * _silu(gate)


def _mod_row(i, tm):
    return jnp.maximum(i * tm - (N_CTX_TOK - T_LAT), 0) // T_LAT


def _cparams(sem):
    return pltpu.CompilerParams(dimension_semantics=sem, vmem_limit_bytes=VMEM_LIMIT)


@functools.lru_cache(maxsize=None)
def _scan_consts(w, reverse):
    c = CHUNK
    idx = np.arange(c)
    i = idx[:, None]
    m = idx[None, :]
    pm, sg = [], []
    s = c // 2
    while s >= 1:
        par = idx // (2 * s)
        right = (idx % (2 * s)) >= s
        same = par[:, None] == par[None, :]
        query = ~right if reverse else right
        pm.append(same & query[:, None] & (~query)[None, :])
        if s >= SMALL_LEVEL:
            sg.append(np.where(query, 1.0, -1.0))
        s //= 2
    pm.append(i == m)
    tri = (m >= i) if reverse else (m <= i)
    r4, r2 = idx % 4, idx % 2
    if not reverse:
        cf = [r2 == 1, r4 >= 2, r4 == 3, r4 == 0]
    else:
        cf = [r2 == 0, r4 <= 1, r4 == 3, r4 == 0]
    wide = lambda rows: np.repeat(np.stack(rows).astype(np.float32)[:, :, None], w, axis=2)
    pmask = np.stack([np.tile(p, (1, N_HEADS)) for p in pm]).astype(np.float32)
    return tri.astype(np.float32), wide(sg), wide(cf), pmask


@functools.lru_cache(maxsize=None)
def _head_consts(w):
    dk = w // N_HEADS
    hm = np.zeros((N_HEADS, w), np.float32)
    hv = np.zeros((N_HEADS, BRANCH_W), np.float32)
    for h in range(N_HEADS):
        hm[h, h * dk:(h + 1) * dk] = 1.0
        hv[h, h * HEAD_V:(h + 1) * HEAD_V] = 1.0
    bd = hm.T @ hv
    ones_bd = hv.T @ hv
    return hm, hv, bd, ones_bd


@functools.lru_cache(maxsize=None)
def _dn_consts():
    c = CHUNK
    idx = np.arange(c)
    i = idx[:, None]
    j = idx[None, :]
    tri = np.stack([(j <= i), (j >= i)]).astype(np.float32)
    strict = np.stack([np.tile(j < i, (1, N_HEADS)), np.tile(j > i, (1, N_HEADS))]).astype(np.float32)
    eye = np.tile(np.eye(c), (1, N_HEADS)).astype(np.float32)
    blk = np.ones((c, c), np.float32)
    exb = np.zeros((128, 2 * BRANCH_W), np.float32)
    exa = np.zeros((128, 2 * BRANCH_W), np.float32)
    for n in range(2):
        for h in range(N_HEADS):
            lo = n * BRANCH_W + h * HEAD_V
            exb[SMALL_DNB + n * N_HEADS + h, lo:lo + HEAD_V] = 1.0
            exa[SMALL_DNA + n * N_HEADS + h, lo:lo + HEAD_V] = 1.0
    return tri, strict, eye, blk, exb, exa


@functools.lru_cache(maxsize=None)
def _att_consts():
    qm = np.zeros((2 * N_HEADS, BRANCH_W), np.float32)
    for h in range(N_HEADS):
        for mp in range(2):
            lo = h * HEAD_V + mp * DF_DH
            qm[2 * h + mp, lo:lo + DF_DH] = 1.0
    return qm


def _rope_tables():
    rows = T_LAT // GRID_W
    row = jnp.repeat(jnp.arange(rows), GRID_W).astype(F32)
    col = jnp.tile(jnp.arange(GRID_W), rows).astype(F32)
    half = DF_DH // 2
    inv = ROPE_BASE ** (-jnp.arange(0, half, 2, dtype=F32) / half)

    def angles(pos):
        a = pos[:, None] * inv[None, :]
        return jnp.concatenate([a, a], axis=-1)

    ang = jnp.concatenate([angles(row), angles(col)], axis=-1)
    reps = BRANCH_W // DF_DH
    return jnp.tile(jnp.cos(ang), (1, reps)), jnp.tile(jnp.sin(ang), (1, reps))


N_BUF_MOD = 3


def _mod_kernel(c_ref, b_ref, w_hbm, o_ref, buf, sem):
    n_t = (N_MOD * D_MODEL) // TN_MOD
    n_tiles = DEPTH * n_t

    def tile_copy(k):
        l, j = divmod(k, n_t)
        slot = k % N_BUF_MOD
        return pltpu.make_async_copy(w_hbm.at[l, :, j * TN_MOD:(j + 1) * TN_MOD], buf.at[slot], sem.at[slot])

    for k in range(min(N_BUF_MOD, n_tiles)):
        tile_copy(k).start()
    a = _silu(c_ref[...])
    for k in range(n_tiles):
        l, j = divmod(k, n_t)
        tile_copy(k).wait()
        out = jnp.dot(a, buf[k % N_BUF_MOD], preferred_element_type=F32, precision=lax.Precision.HIGHEST)
        cols = slice(j * TN_MOD, (j + 1) * TN_MOD)
        o_ref[l, :, cols] = out + b_ref[l, :, cols]
        if k + N_BUF_MOD < n_tiles:
            tile_copy(k + N_BUF_MOD).start()


def _mod_call(c_rows, w_mod, b_mod):
    vmem = pl.BlockSpec(memory_space=pltpu.VMEM)
    return pl.pallas_call(
        _mod_kernel,
        in_specs=[vmem, vmem, pl.BlockSpec(memory_space=pl.ANY)],
        out_specs=vmem,
        out_shape=jax.ShapeDtypeStruct((DEPTH, 8, N_MOD * D_MODEL), F32),
        scratch_shapes=[pltpu.VMEM((N_BUF_MOD, D_MODEL, TN_MOD), F32), pltpu.SemaphoreType.DMA((N_BUF_MOD,))],
        compiler_params=pltpu.CompilerParams(vmem_limit_bytes=VMEM_LIMIT),
        name="mod_vectors",
    )(c_rows, b_mod.reshape(DEPTH, 1, N_MOD * D_MODEL), w_mod)


def _ffn_kernel(*refs, sub, final, split_in, split_out):
    n_x = 2 if split_in else 1
    mod_ref, nw_ref, wup_ref, wd_ref, fn_ref = refs[n_x:n_x + 5]
    outs = refs[n_x + 5:]
    is_lat = pl.program_id(0) >= N_CTX_TOK // TM_FFN
    x = jnp.where(is_lat, refs[1][...], refs[0][...]) if split_in else refs[0][...]
    sh = mod_ref[0, 3 * sub:3 * sub + 1, :]
    sc = mod_ref[0, 3 * sub + 1:3 * sub + 2, :]
    ga = mod_ref[0, 3 * sub + 2:3 * sub + 3, :]
    h = (_rms(x, nw_ref[...]) * (1.0 + sc) + sh).astype(BF)
    acc = None
    for f in range(D_FF // TF_FFN):
        lo = f * TF_FFN
        g = jnp.dot(h, wup_ref[:, lo:lo + TF_FFN], preferred_element_type=F32)
        u = jnp.dot(h, wup_ref[:, D_FF + lo:D_FF + lo + TF_FFN], preferred_element_type=F32)
        t = jnp.dot((_silu(g) * u).astype(BF), wd_ref[lo:lo + TF_FFN, :], preferred_element_type=F32)
        acc = t if acc is None else acc + t
    y = x + 0.5 * ga * acc
    if final:
        y = _rms(y, fn_ref[...])
    if split_out:
        @pl.when(jnp.logical_not(is_lat))
        def _():
            outs[0][...] = y

        @pl.when(is_lat)
        def _():
            outs[1][...] = y
    else:
        outs[0][...] = y


def _ffn_call(xs, mod_l, nw, w_in, w_down, final_w, sub, final, split_out):
    tm = TM_FFN
    n_ctx = N_CTX_TOK // tm
    ctx_map = lambda i: (jnp.minimum(i, n_ctx - 1), 0)
    lat_map = lambda i: (jnp.maximum(i - n_ctx, 0), 0)
    tile = lambda m: pl.BlockSpec((tm, D_MODEL), m)
    resident = lambda a: pl.BlockSpec(a.shape, lambda i: (0, 0), pipeline_mode=pl.Buffered(1))
    split_in = len(xs) == 2
    in_specs = [tile(ctx_map), tile(lat_map)] if split_in else [tile(lambda i: (i, 0))]
    in_specs += [pl.BlockSpec((1, N_MOD, D_MODEL), lambda i: (_mod_row(i, tm), 0, 0)),
                 pl.BlockSpec((1, D_MODEL), lambda i: (0, 0)),
                 resident(w_in), resident(w_down),
                 pl.BlockSpec((1, D_MODEL), lambda i: (0, 0))]
    if split_out:
        out_specs = [tile(ctx_map), tile(lat_map)]
        out_shape = [jax.ShapeDtypeStruct((N_CTX_TOK, D_MODEL), F32),
                     jax.ShapeDtypeStruct((N_TOK - N_CTX_TOK, D_MODEL), F32)]
    else:
        out_specs = [tile(lambda i: (i, 0))]
        out_shape = [jax.ShapeDtypeStruct((N_TOK, D_MODEL), F32)]
    return pl.pallas_call(
        functools.partial(_ffn_kernel, sub=sub, final=final, split_in=split_in, split_out=split_out),
        grid=(N_TOK // tm,),
        in_specs=in_specs,
        out_specs=out_specs,
        out_shape=out_shape,
        compiler_params=_cparams(("arbitrary",)),
        name="swiglu_half_step",
    )(*xs, mod_l, nw, w_in, w_down, final_w)


def _proj_kernel(x_ref, mod_ref, nw_ref, w_ref, o_ref):
    sh = mod_ref[0, 3:4, :]
    sc = mod_ref[0, 4:5, :]
    h = (_rms(x_ref[...], nw_ref[...]) * (1.0 + sc) + sh).astype(BF)
    o_ref[...] = jnp.dot(h, w_ref[...], preferred_element_type=F32)


def _proj_call(x, mod_l, nw, w_in_p):
    tm = TM_PROJ
    return pl.pallas_call(
        _proj_kernel,
        grid=(N_TOK // tm,),
        in_specs=[pl.BlockSpec((tm, D_MODEL), lambda i: (i, 0)),
                  pl.BlockSpec((1, N_MOD, D_MODEL), lambda i: (_mod_row(i, tm), 0, 0)),
                  pl.BlockSpec((1, D_MODEL), lambda i: (0, 0)),
                  pl.BlockSpec((D_MODEL, N_IN_PAD), lambda i: (0, 0))],
        out_specs=pl.BlockSpec((tm, N_IN_PAD), lambda i: (i, 0)),
        out_shape=jax.ShapeDtypeStruct((N_TOK, N_IN_PAD), F32),
        compiler_params=_cparams(("arbitrary",)),
        name="mixer_in_proj",
    )(x, mod_l, nw, w_in_p)


def _merge_kernel(x_ref, mod_ref, nw_ref, *rest):
    ctx_refs, lat_refs = rest[0:4], rest[4:8]
    wg_ref, wb_ref, wo_ref, o_ref = rest[8:12]
    x = x_ref[...]
    sh = mod_ref[0, 3:4, :]
    sc = mod_ref[0, 4:5, :]
    ga = mod_ref[0, 5:6, :]
    h = (_rms(x, nw_ref[...]) * (1.0 + sc) + sh).astype(BF)
    is_lat = pl.program_id(0) >= N_CTX_TOK // TM_MERGE
    mixed = None
    for n in range(4):
        gate = _sigmoid(jnp.dot(h, wg_ref[:, n * D_MODEL:(n + 1) * D_MODEL], preferred_element_type=F32))
        br = jnp.where(is_lat, lat_refs[n][...], ctx_refs[n][...])
        up = jnp.dot(br.astype(BF), wb_ref[n], preferred_element_type=F32)
        mixed = gate * up if mixed is None else mixed + gate * up
    out = jnp.dot(mixed.astype(BF), wo_ref[...], preferred_element_type=F32)
    o_ref[...] = x + ga * out


def _merge_call(x, mod_l, nw, ctx_branches, lat_branches, w_mgate, w_branch, w_out):
    tm = TM_MERGE
    n_ctx = N_CTX_TOK // tm
    cspec = pl.BlockSpec((tm, BRANCH_W), lambda i: (jnp.minimum(i, n_ctx - 1), 0))
    lspec = pl.BlockSpec((tm, BRANCH_W), lambda i: (jnp.maximum(i - n_ctx, 0), 0))
    return pl.pallas_call(
        _merge_kernel,
        grid=(N_TOK // tm,),
        in_specs=[pl.BlockSpec((tm, D_MODEL), lambda i: (i, 0)),
                  pl.BlockSpec((1, N_MOD, D_MODEL), lambda i: (_mod_row(i, tm), 0, 0)),
                  pl.BlockSpec((1, D_MODEL), lambda i: (0, 0)),
                  cspec, cspec, cspec, cspec, lspec, lspec, lspec, lspec,
                  pl.BlockSpec((D_MODEL, 4 * D_MODEL), lambda i: (0, 0)),
                  pl.BlockSpec((4, BRANCH_W, D_MODEL), lambda i: (0, 0, 0)),
                  pl.BlockSpec((D_MODEL, D_MODEL), lambda i: (0, 0))],
        out_specs=pl.BlockSpec((tm, D_MODEL), lambda i: (i, 0)),
        out_shape=jax.ShapeDtypeStruct((N_TOK, D_MODEL), F32),
        compiler_params=_cparams(("arbitrary",)),
        name="gated_merge",
    )(x, mod_l, nw, *ctx_branches, *lat_branches, w_mgate, w_branch, w_out)


def _gated_chunk(qc, kc, vc, gc, st, tri_ref, sg_ref, cf_ref, pm_ref, hm_ref, hv_ref, bdt_ref, reverse):
    c = CHUNK
    w = gc.shape[1]
    cum = _sel_l(tri_ref[...], gc, 2)
    kst = _stack_heads_bf(kc, hm_ref)
    qb = qc.astype(BF)
    a = pm_ref[N_LEVELS] * _mm_nt(qb, kst)
    lv = 0
    s = c // 2
    while s >= 1:
        if s >= SMALL_LEVEL:
            first = s if reverse else s - 1
            bnd = [jnp.broadcast_to(cum[p * 2 * s + first:p * 2 * s + first + 1, :], (2 * s, w))
                   for p in range(c // (2 * s))]
            bnd = jnp.concatenate(bnd, axis=0) if len(bnd) > 1 else bnd[0]
            dl = (cum - bnd) * sg_ref[lv]
        elif s == 2:
            dl = (gc * cf_ref[1] + pltpu.roll(gc, 1, 0) * cf_ref[2] + pltpu.roll(gc, c - 1, 0) * cf_ref[3])
        else:
            dl = gc * cf_ref[0]
        el = jnp.exp(dl).astype(BF)
        a = a + pm_ref[lv] * _mm_nt(qb * el, kst * jnp.concatenate([el] * N_HEADS, axis=0))
        lv += 1
        s //= 2
    tot = cum[0:1, :] if reverse else cum[c - 1:c, :]
    o = _mm(a, _stack_heads_bf(vc, hv_ref)) + _mm_nt(qc * jnp.exp(cum), st)
    st_new = st * jnp.exp(tot) + bdt_ref[...] * _mm_tn(vc, kc * jnp.exp(tot - cum))
    return o, st_new


def _gated_scan(q_ref, kf_ref, kb_ref, v_ref, gf_ref, gb_ref, of_ref, ob_ref, s0_ref, st_ref, cf, cb,
                t_len, w, dk, eye_ref):
    n_chunks = t_len // CHUNK

    def body(t, carry):
        new = []
        for sq in range(N_PAIR):
            rf = pl.ds(pl.multiple_of(sq * t_len + t * CHUNK, CHUNK), CHUNK)
            rb = pl.ds(pl.multiple_of(sq * t_len + (n_chunks - 1 - t) * CHUNK, CHUNK), CHUNK)
            o_f, sf = _gated_chunk(q_ref[rf, :], kf_ref[rf, :], v_ref[rf, :], gf_ref[rf, :],
                                   carry[2 * sq], *cf, False)
            o_b, sb = _gated_chunk(q_ref[rb, :], kb_ref[rb, :], v_ref[rb, :], gb_ref[rb, :],
                                   carry[2 * sq + 1], *cb, True)
            of_ref[rf, :] = o_f
            ob_ref[rb, :] = o_b
            new += [sf, sb]
        return tuple(new)

    if s0_ref is not None:
        init = tuple(s0_ref[sq, d] for sq in range(N_PAIR) for d in range(2))
    else:
        init = tuple(jnp.zeros((BRANCH_W, w), F32) for _ in range(2 * N_PAIR))
    finals = lax.fori_loop(0, n_chunks, body, init)
    if st_ref is not None:
        _zero_later_layers(st_ref)
        for sq in range(N_PAIR):
            for d in range(2):
                st = finals[2 * sq + d]
                for h in range(N_HEADS):
                    tr = _sel_tn(st[h * HEAD_V:(h + 1) * HEAD_V, :], eye_ref[...])
                    st_ref[sq, 0, d, h] = tr[h * dk:(h + 1) * dk, :]


def _gla_kernel(*refs, t_len, has_state, n_prev):
    blk_ref, small_ref, w2_ref, gb_ref, nw_ref = refs[:5]
    pos = 5 + N_SCAN_CONSTS
    cf, cb, ones_ref, eye_ref = _split_scan_consts(refs[5:pos])
    s0_ref = st_ref = None
    if has_state:
        s0_ref = refs[pos]
        pos += 1
    pos += n_prev
    out_ref = refs[pos]
    pos += 1
    if not has_state:
        st_ref = refs[pos]
        pos += 1
    q_s, gf_s, gb_s, of_s, ob_s = refs[pos:pos + 5]

    q_s[...] = blk_ref[:, 0:128] * (GLA_DK ** -0.5)
    z = _mm(small_ref[...], w2_ref[...]) + gb_ref[...]
    g = _log_sigmoid(z) * (1.0 / GLA_TAU)
    gf_s[...] = g[:, 0:128]
    gb_s[...] = g[:, 128:256]
    k_ref = blk_ref.at[:, 128:256]
    v_ref = blk_ref.at[:, 256:512]
    _gated_scan(q_s, k_ref, k_ref, v_ref, gf_s, gb_s, of_s, ob_s, s0_ref, st_ref, cf, cb,
                t_len, GLA_KW, GLA_DK, eye_ref)
    out_ref[...] = _head_norm_gate(of_s[...] + ob_s[...], blk_ref[:, 512:768], nw_ref[...], ones_ref[...])


def _const_spec(a):
    nd = a.ndim
    return pl.BlockSpec(a.shape, lambda b, _n=nd: (0,) * _n)


def _scan_const_arrays(w):
    out = []
    for reverse in (False, True):
        tri, sg, cf, pm = _scan_consts(w, reverse)
        out += [jnp.asarray(tri, BF), jnp.asarray(sg, F32), jnp.asarray(cf, F32), jnp.asarray(pm, F32)]
    hm, hv, bd, ones_bd = _head_consts(w)
    out += [jnp.asarray(hm, F32), jnp.asarray(hv, F32), jnp.asarray(bd.T, F32), jnp.asarray(ones_bd, BF),
            jnp.eye(HEAD_V, dtype=BF)]
    return tuple(out)


def _split_scan_consts(refs):
    shared = tuple(refs[8:11])
    return tuple(refs[0:4]) + shared, tuple(refs[4:8]) + shared, refs[11], refs[12]


def _zero_later_layers(ref):
    if ref.shape[1] > 1:
        ref[:, 1:] = jnp.zeros((ref.shape[0], ref.shape[1] - 1) + tuple(ref.shape[2:]), ref.dtype)


def _add_layer_outputs(tails, n_lead, lead_block, layer, prevs, in_specs, args, out_shape, out_specs):
    first_out = len(out_shape)
    for tail in tails:
        out_shape.append(jax.ShapeDtypeStruct((n_lead, DEPTH) + tuple(tail), F32))
        if prevs is None:
            assert layer == 0
            out_specs.append(pl.BlockSpec((lead_block, DEPTH) + tuple(tail),
                                          lambda b, _nz=len(tail): (b,) + (0,) * (_nz + 1)))
        else:
            out_specs.append(pl.BlockSpec((lead_block, 1) + tuple(tail),
                                          lambda b, _nz=len(tail): (b, layer) + (0,) * _nz))
    aliases = {}
    for k, p in enumerate(prevs or ()):
        in_specs.append(pl.BlockSpec(memory_space=pl.ANY))
        aliases[len(args)] = first_out + k
        args.append(p)
    return aliases


def _gla_call(proj, w2bd, gbias, nw, s0, t_len, n_seq, row_blk0, layer=0, prev=None):
    has_state = s0 is not None
    consts = (w2bd, gbias, nw) + _scan_const_arrays(GLA_KW)
    n_rows = N_PAIR * t_len
    in_specs = [pl.BlockSpec((n_rows, 768), lambda b: (row_blk0 + b, 0)),
                pl.BlockSpec((n_rows, 128), lambda b: (row_blk0 + b, COL_SMALL_BLOCK))]
    in_specs += [_const_spec(a) for a in consts]
    args = [proj, proj, *consts]
    if has_state:
        in_specs.append(pl.BlockSpec((N_PAIR, 2, BRANCH_W, GLA_KW), lambda b: (b, 0, 0, 0)))
        args.append(s0)
    out_shape = [jax.ShapeDtypeStruct((n_seq * t_len, BRANCH_W), F32)]
    out_specs = [pl.BlockSpec((n_rows, BRANCH_W), lambda b: (b, 0))]
    aliases = {}
    if not has_state:
        aliases = _add_layer_outputs([(2, N_HEADS, GLA_DK, HEAD_V)], n_seq, N_PAIR, layer, prev,
                                     in_specs, args, out_shape, out_specs)
    return pl.pallas_call(
        functools.partial(_gla_kernel, t_len=t_len, has_state=has_state, n_prev=len(aliases)),
        grid=(n_seq // N_PAIR,),
        in_specs=in_specs,
        out_specs=out_specs,
        out_shape=out_shape,
        input_output_aliases=aliases,
        scratch_shapes=[pltpu.VMEM((n_rows, GLA_KW), F32)] * 3 + [pltpu.VMEM((n_rows, BRANCH_W), F32)] * 2,
        compiler_params=_cparams(("arbitrary",)),
        name="gla_mixer",
    )(*args)


def _hgrn_kernel(*refs, t_len, has_state, layer, n_prev):
    q_ref, f_ref, v_ref, gate_ref, lbl_ref, nw_ref = refs[:6]
    pos = 6 + N_SCAN_CONSTS
    cf, cb, ones_ref, eye_ref = _split_scan_consts(refs[6:pos])
    s0_ref = st_ref = None
    if has_state:
        s0_ref = refs[pos]
        pos += 1
    pos += n_prev
    out_ref = refs[pos]
    pos += 1
    if not has_state:
        st_ref = refs[pos]
        pos += 1
    q_s, kf_s, kb_s, gf_s, gb_s, of_s, ob_s = refs[pos:pos + 7]

    lg = lbl_ref[...]
    mx = jnp.max(lg, axis=0, keepdims=True)
    ex = jnp.exp(lg - mx)
    p = ex / jnp.sum(ex, axis=0, keepdims=True)
    lb = jnp.sum(p[0:layer + 1], axis=0, keepdims=True) - p[0:1]

    q_s[...] = _silu(q_ref[...]) * (HG_DK ** -0.5)
    f = lb + (1.0 - lb) * _sigmoid(f_ref[...])
    kf_s[...] = 1.0 - f[:, 0:HG_W]
    kb_s[...] = 1.0 - f[:, HG_W:2 * HG_W]
    lf = jnp.log(f)
    gf_s[...] = lf[:, 0:HG_W]
    gb_s[...] = lf[:, HG_W:2 * HG_W]
    _gated_scan(q_s, kf_s, kb_s, v_ref, gf_s, gb_s, of_s, ob_s, s0_ref, st_ref, cf, cb,
                t_len, HG_W, HG_DK, eye_ref)
    out_ref[...] = _head_norm_gate(of_s[...] + ob_s[...], gate_ref[...], nw_ref[...], ones_ref[...])


def _hgrn_call(proj, lb_logits, nw, s0, t_len, n_seq, row_blk0, layer, prev=None):
    has_state = s0 is not None
    consts = (lb_logits, nw) + _scan_const_arrays(HG_W)
    n_rows = N_PAIR * t_len
    in_specs = [pl.BlockSpec((n_rows, 256), lambda b: (row_blk0 + b, 7)),
                pl.BlockSpec((n_rows, 512), lambda b: (row_blk0 + b, 4)),
                pl.BlockSpec((n_rows, 256), lambda b: (row_blk0 + b, 10)),
                pl.BlockSpec((n_rows, 256), lambda b: (row_blk0 + b, 11))]
    in_specs += [_const_spec(a) for a in consts]
    args = [proj, proj, proj, proj, *consts]
    if has_state:
        in_specs.append(pl.BlockSpec((N_PAIR, 2, BRANCH_W, HG_W), lambda b: (b, 0, 0, 0)))
        args.append(s0)
    out_shape = [jax.ShapeDtypeStruct((n_seq * t_len, BRANCH_W), F32)]
    out_specs = [pl.BlockSpec((n_rows, BRANCH_W), lambda b: (b, 0))]
    aliases = {}
    if not has_state:
        aliases = _add_layer_outputs([(2, N_HEADS, HG_DK, HEAD_V)], n_seq, N_PAIR, layer, prev,
                                     in_specs, args, out_shape, out_specs)
    return pl.pallas_call(
        functools.partial(_hgrn_kernel, t_len=t_len, has_state=has_state, layer=layer, n_prev=len(aliases)),
        grid=(n_seq // N_PAIR,),
        in_specs=in_specs,
        out_specs=out_specs,
        out_shape=out_shape,
        input_output_aliases=aliases,
        scratch_shapes=[pltpu.VMEM((n_rows, HG_W), F32)] * 5 + [pltpu.VMEM((n_rows, BRANCH_W), F32)] * 2,
        compiler_params=_cparams(("arbitrary",)),
        name="hgrn2_mixer",
    )(*args)


N_PAIR = 2
SOLVE_JB = 8
SOLVE_IB = 16


def _dn_solve_kernel(lt_ref, vb_ref, kb_ref, u_ref, w_ref):
    _dn_substitute(pl.program_id(1), lt_ref.at[0], vb_ref.at[0], kb_ref.at[0], u_ref.at[0], w_ref.at[0])


def _dn_substitute(rev, l_ref, vb_ref, kb_ref, u_ref, w_ref):
    c = CHUNK
    u_ref[...] = jnp.zeros_like(u_ref)
    w_ref[...] = jnp.zeros_like(w_ref)

    def outer(t, carry):
        i = t + rev * (c - 1 - 2 * t)

        def inner(jb, acc):
            au, aw = acc
            j0 = pl.multiple_of(jb * SOLVE_JB, SOLVE_JB)
            for r in range(SOLVE_JB):
                coef = l_ref[i, pl.ds(j0 + r, 1), :]
                au = au - coef * u_ref[j0 + r]
                aw = aw - coef * w_ref[j0 + r]
            return au, aw

        blk = lax.shift_right_logical(i, 3)
        lo = rev * blk
        hi = blk + 1 + rev * (c // SOLVE_JB - blk - 1)
        au, aw = lax.fori_loop(lo, hi, inner, (vb_ref[i], kb_ref[i]))
        u_ref[i] = au
        w_ref[i] = aw
        return carry

    lax.fori_loop(0, c, outer, 0)


def _dn_solve_packed_kernel(lt_ref, vb_ref, kb_ref, u_ref, w_ref, lp_s, vp_s, kp_s, up_s, wp_s):
    n_chunk = lt_ref.shape[-1]
    n_ib = CHUNK // SOLVE_IB
    s = pl.program_id(1)

    @pl.when(s < n_ib)
    def _():
        for ii in range(SOLVE_IB):
            for src, dst in ((lt_ref, lp_s), (vb_ref, vp_s), (kb_ref, kp_s)):
                dst[s * SOLVE_IB + ii] = jnp.concatenate(
                    [src[0, ii, h * HEAD_V:(h + 1) * HEAD_V, :] for h in range(N_HEADS)], axis=1)

    @pl.when(s == n_ib - 1)
    def _():
        _dn_substitute(pl.program_id(0), lp_s, vp_s, kp_s, up_s, wp_s)

    @pl.when(s >= n_ib)
    def _():
        for ii in range(SOLVE_IB):
            for src, dst in ((up_s, u_ref), (wp_s, w_ref)):
                row = src[(s - n_ib) * SOLVE_IB + ii]
                for h in range(N_HEADS):
                    dst[0, ii, h * HEAD_V:(h + 1) * HEAD_V, :] = row[:, h * n_chunk:(h + 1) * n_chunk]


def _dn_solve_call(lt, vbt, kbt):
    n_chunk = lt.shape[-1]
    shape = jax.ShapeDtypeStruct((2, CHUNK, BRANCH_W, n_chunk), F32)
    if n_chunk * N_HEADS == 128:
        n_ib = CHUNK // SOLVE_IB
        blk = (1, SOLVE_IB, BRANCH_W, n_chunk)
        spec = pl.BlockSpec(blk, lambda d, s: (d, jnp.minimum(s, n_ib - 1), 0, 0))
        ospec = pl.BlockSpec(blk, lambda d, s: (d, jnp.maximum(s - n_ib, 0), 0, 0))
        return pl.pallas_call(
            _dn_solve_packed_kernel,
            grid=(2, 2 * n_ib),
            in_specs=[spec, spec, spec],
            out_specs=[ospec, ospec],
            out_shape=[shape, shape],
            scratch_shapes=[pltpu.VMEM((CHUNK, HEAD_V, 128), F32)] * 5,
            compiler_params=_cparams(("arbitrary", "arbitrary")),
            name="deltanet_solve_packed",
        )(lt, vbt, kbt)
    spec = pl.BlockSpec((1, CHUNK, HEAD_V, n_chunk), lambda h, d: (d, 0, h, 0))
    return pl.pallas_call(
        _dn_solve_kernel,
        grid=(N_HEADS, 2),
        in_specs=[spec, spec, spec],
        out_specs=[spec, spec],
        out_shape=[shape, shape],
        compiler_params=_cparams(("arbitrary", "arbitrary")),
        name="deltanet_solve",
    )(lt, vbt, kbt)


def _chunks_to_lanes(a):
    return a.reshape(2, a.shape[1] // CHUNK, CHUNK, BRANCH_W).transpose(0, 2, 3, 1)


def _chunks_from_lanes(a):
    return a.transpose(0, 3, 1, 2).reshape(2, -1, BRANCH_W)


def _dn_build_kernel(x_ref, small_ref, cw_ref, alog_ref, dtb_ref,
                     tri_ref, strict_ref, eye_ref, blk_ref, exb_ref, exa_ref, hm_ref, ones_ref,
                     q_ref, k_ref, d_ref, l_ref, vb_ref, kb_ref, qkd_ref, v_ref, be_s, ge_s, *, t_len):
    n_rows = N_PAIR * t_len

    x = x_ref[...]
    row = lax.broadcasted_iota(jnp.int32, (n_rows, 1), 0) % t_len
    x_prev = jnp.where(row == 0, 0.0, pltpu.roll(x, 1, 0))
    x_next = jnp.where(row == t_len - 1, 0.0, pltpu.roll(x, n_rows - 1, 0))
    y = _silu(x_prev * cw_ref[0:1, :] + x * cw_ref[1:2, :] + x_next * cw_ref[2:3, :])
    cq, ck = y[:, 0:256], y[:, 256:512]
    v_ref[...] = y[:, 512:768]
    q_ref[...] = cq * lax.rsqrt(_sel_r(cq * cq, ones_ref[...], 2) + EPS) * (DN_DK ** -0.5)
    k_ref[...] = ck * lax.rsqrt(_sel_r(ck * ck, ones_ref[...], 2) + EPS)
    sm = small_ref[...]
    be_s[...] = _sel_r(_sigmoid(sm), exb_ref[...])
    ge_s[...] = _sel_r(-jnp.exp(alog_ref[...]) * _softplus(sm + dtb_ref[...]), exa_ref[...])

    def body(ci, carry):
        rows = pl.ds(pl.multiple_of(ci * CHUNK, CHUNK), CHUNK)
        qc, kc, vc = q_ref[rows, :], k_ref[rows, :], v_ref[rows, :]
        kst = _stack_heads(kc, hm_ref)
        kk = _mm_nt(kc, kst)
        qk = _mm_nt(qc, kst)
        for d in range(2):
            bexp = be_s[rows, d * BRANCH_W:(d + 1) * BRANCH_W]
            dexp = _sel_l(tri_ref[d], ge_s[rows, d * BRANCH_W:(d + 1) * BRANCH_W])
            drow = _sel_l(blk_ref[...], dexp * eye_ref[...])
            dec_s = jnp.exp(jnp.where(strict_ref[d] > 0.5, dexp - drow, -1e30))
            d_ref[d, rows, :] = dexp
            l_ref[d, rows, :] = bexp * kk * dec_s
            vb_ref[d, rows, :] = vc * bexp
            kb_ref[d, rows, :] = kc * bexp * jnp.exp(dexp)
            qkd_ref[d, rows, :] = (qk * (dec_s + eye_ref[...])).astype(BF)
        return carry

    lax.fori_loop(0, n_rows // CHUNK, body, 0)


def _dn_build_call(proj, conv_w, alog_row, dtb_row, t_len, n_seq, row_blk0):
    tri, strict, eye, blk, exb, exa = _dn_consts()
    tri, blk, exb, exa = (jnp.asarray(a, BF) for a in (tri, blk, exb, exa))
    strict, eye = jnp.asarray(strict, F32), jnp.asarray(eye, F32)
    hm, _, _, ones_bd = _head_consts(BRANCH_W)
    hm, ones_bd = jnp.asarray(hm, F32), jnp.asarray(ones_bd, BF)
    n_rows = N_PAIR * t_len
    n_tok = n_seq * t_len
    consts = (conv_w, alog_row, dtb_row, tri, strict, eye, blk, exb, exa, hm, ones_bd)
    in_specs = [pl.BlockSpec((n_rows, 768), lambda b: (row_blk0 + b, 1)),
                pl.BlockSpec((n_rows, 128), lambda b: (row_blk0 + b, COL_SMALL_BLOCK))]
    in_specs += [_const_spec(a) for a in consts]
    tok_spec = pl.BlockSpec((n_rows, BRANCH_W), lambda b: (b, 0))
    dir_spec = pl.BlockSpec((2, n_rows, BRANCH_W), lambda b: (0, b, 0))
    tok = jax.ShapeDtypeStruct((n_tok, BRANCH_W), F32)
    per_dir = jax.ShapeDtypeStruct((2, n_tok, BRANCH_W), F32)
    return pl.pallas_call(
        functools.partial(_dn_build_kernel, t_len=t_len),
        grid=(n_seq // N_PAIR,),
        in_specs=in_specs,
        out_specs=[tok_spec] * 2 + [dir_spec] * 5,
        out_shape=[tok] * 2 + [per_dir] * 4 + [jax.ShapeDtypeStruct((2, n_tok, BRANCH_W), BF)],
        scratch_shapes=[pltpu.VMEM((n_rows, BRANCH_W), F32)] + [pltpu.VMEM((n_rows, 2 * BRANCH_W), F32)] * 2,
        compiler_params=_cparams(("arbitrary",)),
        name="deltanet_build",
    )(proj, proj, *consts)


def _dn_scan_kernel(*refs, t_len, has_state, n_prev):
    (q_ref, k_ref, d_ref, u_ref, w_ref, qkd_ref, gate_ref, nw_ref, hv_ref, bd_ref, ones_ref) = refs[:11]
    pos = 11
    if has_state:
        s0_ref = refs[pos]
        pos += 1
    pos += n_prev
    out_ref = refs[pos]
    pos += 1
    if not has_state:
        st_ref = refs[pos]
        pos += 1
    of_s, ob_s = refs[pos:pos + 2]
    c = CHUNK
    n_chunks = t_len // c

    def step(s, rows, d, o_s):
        qc, kc = q_ref[rows, :], k_ref[rows, :]
        dexp = d_ref[d, rows, :]
        v_new = u_ref[d, rows, :] - _mm(w_ref[d, rows, :], s)
        o_s[rows, :] = _mm(qc * jnp.exp(dexp), s) + _mm(qkd_ref[d, rows, :], _stack_heads(v_new, hv_ref))
        dl = dexp[0:1, :] if d == 1 else dexp[c - 1:c, :]
        return s * jnp.exp(dl) + bd_ref[...] * _mm_tn(kc * jnp.exp(dl - dexp), v_new)

    def body(t, carry):
        new = []
        for sq in range(N_PAIR):
            rf = pl.ds(pl.multiple_of(sq * t_len + t * c, c), c)
            rb = pl.ds(pl.multiple_of(sq * t_len + (n_chunks - 1 - t) * c, c), c)
            new.append(step(carry[2 * sq], rf, 0, of_s))
            new.append(step(carry[2 * sq + 1], rb, 1, ob_s))
        return tuple(new)

    if has_state:
        init = tuple(s0_ref[sq, d] for sq in range(N_PAIR) for d in range(2))
    else:
        init = tuple(jnp.zeros((BRANCH_W, BRANCH_W), F32) for _ in range(2 * N_PAIR))
    finals = lax.fori_loop(0, n_chunks, body, init)
    out_ref[...] = _head_norm_gate(of_s[...] + ob_s[...], gate_ref[...], nw_ref[...], ones_ref[...])
    if not has_state:
        _zero_later_layers(st_ref)
        for sq in range(N_PAIR):
            for d in range(2):
                for h in range(N_HEADS):
                    st_ref[sq, 0, d, h] = _head_lanes(finals[2 * sq + d][h * DN_DK:(h + 1) * DN_DK, :], h)


def _dn_scan_call(proj, q, k, dd, u, w, qkd, nw, s0, t_len, n_seq, row_blk0, layer=0, prev=None):
    _, hv, bd, ones_bd = _head_consts(BRANCH_W)
    hv, bd, ones_bd = jnp.asarray(hv, F32), jnp.asarray(bd, F32), jnp.asarray(ones_bd, BF)
    has_state = s0 is not None
    n_rows = N_PAIR * t_len
    consts = (nw, hv, bd, ones_bd)
    tok_spec = pl.BlockSpec((n_rows, BRANCH_W), lambda b: (b, 0))
    dir_spec = pl.BlockSpec((2, n_rows, BRANCH_W), lambda b: (0, b, 0))
    in_specs = [tok_spec, tok_spec, dir_spec, dir_spec, dir_spec, dir_spec,
                pl.BlockSpec((n_rows, 256), lambda b: (row_blk0 + b, 6))]
    in_specs += [_const_spec(a) for a in consts]
    args = [q, k, dd, u, w, qkd, proj, *consts]
    if has_state:
        in_specs.append(pl.BlockSpec((N_PAIR, 2, BRANCH_W, BRANCH_W), lambda b: (b, 0, 0, 0)))
        args.append(s0)
    out_shape = [jax.ShapeDtypeStruct((n_seq * t_len, BRANCH_W), F32)]
    out_specs = [pl.BlockSpec((n_rows, BRANCH_W), lambda b: (b, 0))]
    aliases = {}
    if not has_state:
        aliases = _add_layer_outputs([(2, N_HEADS, DN_DK, HEAD_V)], n_seq, N_PAIR, layer, prev,
                                     in_specs, args, out_shape, out_specs)
    return pl.pallas_call(
        functools.partial(_dn_scan_kernel, t_len=t_len, has_state=has_state, n_prev=len(aliases)),
        grid=(n_seq // N_PAIR,),
        in_specs=in_specs,
        out_specs=out_specs,
        out_shape=out_shape,
        input_output_aliases=aliases,
        scratch_shapes=[pltpu.VMEM((n_rows, BRANCH_W), F32)] * 2,
        compiler_params=_cparams(("arbitrary",)),
        name="deltanet_scan",
    )(*args)


def _rope(x, cos, sin):
    lane = lax.broadcasted_iota(jnp.int32, x.shape, 1) % 16
    n = x.shape[1]
    xrot = jnp.where(lane < 8, -pltpu.roll(x, n - 8, 1), pltpu.roll(x, 8, 1))
    return x * cos + xrot * sin


def _att_kernel(*refs, t_len, lat, lam_init, n_prev):
    blk_ref, lam_ref, nw_ref, qm_ref, hv_ref, ones_ref = refs[:6]
    pos = 6
    if lat:
        cos_ref, sin_ref, ck_ref, cv_ref = refs[pos:pos + 4]
        pos += 4
    pos += n_prev
    out_ref = refs[pos]
    pos += 1
    if not lat:
        nk_ref, nv_ref = refs[pos:pos + 2]
        pos += 2
    if lat:
        q_s, k_s, v_s = refs[pos:pos + 3]

    lv = lam_ref[...]
    lam = (jnp.exp(jnp.sum(lv[0:1] * lv[1:2], axis=1, keepdims=True))
           - jnp.exp(jnp.sum(lv[2:3] * lv[3:4], axis=1, keepdims=True)) + lam_init)
    q = blk_ref[:, 0:256]
    k = blk_ref[:, 256:512]
    v = blk_ref[:, 512:768]
    if lat:
        cos, sin = cos_ref[...], sin_ref[...]
        q_s[...] = _rope(q, cos, sin)
        k_s[0:PAST_LEN, :] = ck_ref[0]
        k_s[PAST_LEN:PAST_LEN + t_len, :] = _rope(k, cos, sin)
        v_s[0:PAST_LEN, :] = cv_ref[0]
        v_s[PAST_LEN:PAST_LEN + t_len, :] = v
        keys = k_s[...].astype(BF)
        vals = v_s[...].astype(BF)
    else:
        keys = k.astype(BF)
        vals = v.astype(BF)
        _zero_later_layers(nk_ref)
        _zero_later_layers(nv_ref)
        for h in range(N_HEADS):
            nk_ref[0, 0, h] = _head_lanes(k, h)
            nv_ref[0, 0, h] = _head_lanes(v, h)
    tq = TQ_ATT
    scale = DF_DH ** -0.5
    for qi in range(t_len // tq):
        qt = q_s[qi * tq:(qi + 1) * tq, :] if lat else q[qi * tq:(qi + 1) * tq, :]
        qs = jnp.concatenate([qt * qm_ref[r:r + 1, :] for r in range(2 * N_HEADS)], axis=0)
        s = _mm_nt(qs, keys) * scale
        s = jnp.exp(s - jnp.max(s, axis=1, keepdims=True))
        p = s / jnp.sum(s, axis=1, keepdims=True)
        a = jnp.concatenate(
            [p[(2 * h) * tq:(2 * h + 1) * tq] - lam * p[(2 * h + 1) * tq:(2 * h + 2) * tq]
             for h in range(N_HEADS)], axis=0)
        o = _head_diag(_mm(a, vals), hv_ref, tq)
        ms = _sel_r(o * o, ones_ref[...], 2) * (1.0 / HEAD_V)
        out_ref[qi * tq:(qi + 1) * tq, :] = o * lax.rsqrt(ms + EPS) * nw_ref[...] * (1.0 - lam_init)


def _att_call(proj, lam_p, nw, cache_k, cache_v, rope, t_len, n_seq, row_blk0, lam_init, layer=0, prev=None):
    qm = jnp.asarray(_att_consts(), F32)
    _, hv, _, ones_bd = _head_consts(BRANCH_W)
    hv, ones_bd = jnp.asarray(hv, F32), jnp.asarray(ones_bd, BF)
    lat = cache_k is not None
    consts = (lam_p, nw, qm, hv, ones_bd)
    in_specs = [pl.BlockSpec((t_len, 768), lambda b: (row_blk0 + b, 4))]
    in_specs += [_const_spec(a) for a in consts]
    args = [proj, *consts]
    out_shape = [jax.ShapeDtypeStruct((n_seq * t_len, BRANCH_W), F32)]
    out_specs = [pl.BlockSpec((t_len, BRANCH_W), lambda b: (b, 0))]
    scratch = []
    if lat:
        cos, sin = rope
        in_specs += [_const_spec(cos), _const_spec(sin),
                     pl.BlockSpec((1, PAST_LEN, BRANCH_W), lambda b: (b, 0, 0)),
                     pl.BlockSpec((1, PAST_LEN, BRANCH_W), lambda b: (b, 0, 0))]
        args += [cos, sin, cache_k, cache_v]
        scratch = [pltpu.VMEM((t_len, BRANCH_W), F32),
                   pltpu.VMEM((PAST_LEN + t_len, BRANCH_W), F32),
                   pltpu.VMEM((PAST_LEN + t_len, BRANCH_W), F32)]
    aliases = {}
    if not lat:
        aliases = _add_layer_outputs([(N_HEADS, t_len, HEAD_V)] * 2, n_seq, 1, layer, prev,
                                     in_specs, args, out_shape, out_specs)
    return pl.pallas_call(
        functools.partial(_att_kernel, t_len=t_len, lat=lat, lam_init=lam_init, n_prev=len(aliases)),
        grid=(n_seq,),
        in_specs=in_specs,
        out_specs=out_specs,
        out_shape=out_shape,
        input_output_aliases=aliases,
        scratch_shapes=scratch,
        compiler_params=_cparams(("arbitrary",)),
        name="diff_attention",
    )(*args)


def _block_diag_states(st, dk, transposed):
    eye = jnp.eye(N_HEADS, dtype=st.dtype)
    b = st.shape[0]
    if transposed:
        return jnp.einsum('bnhde,hg->bnhegd', st, eye).reshape(b, 2, N_HEADS * HEAD_V, N_HEADS * dk)
    return jnp.einsum('bnhde,hg->bnhdge', st, eye).reshape(b, 2, N_HEADS * dk, N_HEADS * HEAD_V)


def _in_perm():
    offs = np.concatenate([[0], np.cumsum(IN_ORIG)])
    seg = lambda a, b: np.arange(offs[a], offs[b])
    return np.concatenate([seg(0, 4), seg(5, 6), seg(8, 9), seg(9, 16), seg(4, 5), seg(6, 8)])


def kernel(x_prompt, x_sample, cache_diff_k, cache_diff_v, state_gla, state_dn, state_hgrn, c, c_ctx,
           norm_w, w_mod, b_mod, ffn1_in, ffn1_down, ffn2_in, ffn2_down, w_in, gla_w2, gla_b, gla_norm,
           dn_conv, dn_a_log, dn_dt_bias, dn_norm, hg_lb_logits, hg_norm, diff_lambda, diff_norm,
           w_branch, w_mgate, w_out, final_norm):
    xs = (x_prompt.reshape(N_CTX_TOK, D_MODEL), x_sample.reshape(-1, D_MODEL))
    c_rows = jnp.concatenate([c_ctx[None, :], c, jnp.zeros((8 - 1 - N_LAT_SEQ, D_MODEL), F32)], axis=0)
    mod = _mod_call(c_rows, w_mod, b_mod).reshape(DEPTH, 8, N_MOD, D_MODEL)
    rope = _rope_tables()
    perm = _in_perm()
    lb_logits = hg_lb_logits.reshape(DEPTH, 2 * HG_W)
    lat_blk = N_CTX_TOK // T_LAT
    tile4 = lambda a: jnp.tile(a, N_HEADS)[None, :]
    fin = final_norm[None, :]
    st_a = st_b = st_c = new_kv = None
    prev1 = lambda a: None if a is None else (a,)
    for l in range(DEPTH):
        mod_l = mod[l]
        lam_init = 0.8 - 0.6 * math.exp(-0.3 * l)
        w_in_p = jnp.pad(w_in[l][:, perm], ((0, 0), (0, N_IN_PAD - N_IN))).astype(BF)
        w2bd = jnp.zeros((128, 2 * GLA_KW), F32)
        w2bd = w2bd.at[0:GLA_LOWRANK, 0:GLA_KW].set(gla_w2[l, 0])
        w2bd = w2bd.at[GLA_LOWRANK:2 * GLA_LOWRANK, GLA_KW:].set(gla_w2[l, 1]).astype(BF)
        gbias = gla_b[l].reshape(1, 2 * GLA_KW)
        alog_row = jnp.zeros((1, 128), F32).at[0, SMALL_DNA:SMALL_DNA + 8].set(dn_a_log[l].reshape(-1))
        dtb_row = jnp.zeros((1, 128), F32).at[0, SMALL_DNA:SMALL_DNA + 8].set(dn_dt_bias[l].reshape(-1))

        (x,) = _ffn_call(xs, mod_l, norm_w[l, 0][None, :], ffn1_in[l].astype(BF), ffn1_down[l].astype(BF),
                         fin, 0, False, False)
        proj = _proj_call(x, mod_l, norm_w[l, 1][None, :], w_in_p)

        a_c, st_a = _gla_call(proj, w2bd, gbias, tile4(gla_norm[l]), None, T_CTX, N_CTX_SEQ, 0, l, prev1(st_a))
        dn_lat_blk = N_CTX_TOK // (N_PAIR * T_LAT)
        q_c, k_c, dd_c, lw_c, vb_c, kb_c, qkd_c = _dn_build_call(proj, dn_conv[l], alog_row, dtb_row,
                                                                T_CTX, N_CTX_SEQ, 0)
        q_l, k_l, dd_l, lw_l, vb_l, kb_l, qkd_l = _dn_build_call(proj, dn_conv[l], alog_row, dtb_row,
                                                                T_LAT, N_LAT_SEQ, dn_lat_blk)
        u_c, w_c = (_chunks_from_lanes(a) for a in _dn_solve_call(
            _chunks_to_lanes(lw_c), _chunks_to_lanes(vb_c), _chunks_to_lanes(kb_c)))
        u_l, w_l = (_chunks_from_lanes(a) for a in _dn_solve_call(
            _chunks_to_lanes(lw_l), _chunks_to_lanes(vb_l), _chunks_to_lanes(kb_l)))
        b_c, st_b = _dn_scan_call(proj, q_c, k_c, dd_c, u_c, w_c, qkd_c, tile4(dn_norm[l]), None,
                                  T_CTX, N_CTX_SEQ, 0, l, prev1(st_b))
        c_c, st_c = _hgrn_call(proj, lb_logits, tile4(hg_norm[l]), None, T_CTX, N_CTX_SEQ, 0, l, prev1(st_c))
        d_c, *new_kv = _att_call(proj, diff_lambda[l], tile4(diff_norm[l]), None, None, None,
                                 T_CTX, N_CTX_SEQ, 0, lam_init, l, new_kv)
        ck = cache_diff_k[:, l].transpose(0, 2, 1, 3).reshape(N_LAT_SEQ, PAST_LEN, BRANCH_W)
        cv = cache_diff_v[:, l].transpose(0, 2, 1, 3).reshape(N_LAT_SEQ, PAST_LEN, BRANCH_W)
        (a_l,) = _gla_call(proj, w2bd, gbias, tile4(gla_norm[l]),
                           _block_diag_states(state_gla[:, l], GLA_DK, True), T_LAT, N_LAT_SEQ, dn_lat_blk)
        (b_l,) = _dn_scan_call(proj, q_l, k_l, dd_l, u_l, w_l, qkd_l, tile4(dn_norm[l]),
                               _block_diag_states(state_dn[:, l], DN_DK, False), T_LAT, N_LAT_SEQ, dn_lat_blk)
        (c_l,) = _hgrn_call(proj, lb_logits, tile4(hg_norm[l]),
                            _block_diag_states(state_hgrn[:, l], HG_DK, True), T_LAT, N_LAT_SEQ, dn_lat_blk, l)
        (d_l,) = _att_call(proj, diff_lambda[l], tile4(diff_norm[l]), ck, cv, rope,
                           T_LAT, N_LAT_SEQ, lat_blk, lam_init)

        x = _merge_call(x, mod_l, norm_w[l, 1][None, :], (a_c, b_c, c_c, d_c), (a_l, b_l, c_l, d_l),
                        w_mgate[l].astype(BF), w_branch[l].astype(BF), w_out[l].astype(BF))
        xs = _ffn_call((x,), mod_l, norm_w[l, 2][None, :], ffn2_in[l].astype(BF), ffn2_down[l].astype(BF),
                       fin, 2, l == DEPTH - 1, l == DEPTH - 1)
    y_prompt = xs[0].reshape(N_CTX_SEQ, T_CTX, D_MODEL)
    y_sample = xs[1].reshape(N_LAT_SEQ, T_LAT, D_MODEL)
    return (y_prompt, y_sample, new_kv[0], new_kv[1], st_a, st_b, st_c)
```
